```python
import math
import jax, jax.numpy as jnp
from jax import lax
import numpy as np

D_MODEL = 1024
BATCH = 16
SEQ = 2048
DEPTH = 2

N_A_LAYERS = DEPTH // 2
N_B_LAYERS = DEPTH - N_A_LAYERS
N_DENSE_LAYERS = (DEPTH + 1) // 2
N_MOE_LAYERS = DEPTH // 2

SSM_GROUP_CH = 16
SSM_GROUPS = D_MODEL // SSM_GROUP_CH
SSM_STATE = 64
DT_MIN = 1e-3
DT_MAX = 1e-1

N_HEADS = 16
HEAD_DIM = 64
N_KV_HEADS = 4
Q_PER_KV = N_HEADS // N_KV_HEADS
KV_DIM = N_KV_HEADS * HEAD_DIM
WINDOW = 128
BLOCK = 128
ROT_DIM = HEAD_DIM // 4
ROPE_THETA = 500000.0
MAX_POS_OFFSET = 4096

FFN_DIM = 2816
N_EXPERTS = 8
TOP_K = 2
EXPERT_DIM = 1024

DEEPNORM_ALPHA = (2 * DEPTH) ** 0.25
DEEPNORM_BETA = (8 * DEPTH) ** -0.25
LN_EPS = 1e-5

kernel_name = "yoco_s5_swa_sink_moe_deepnorm"


def layer_norm(x, g, b):
    xf = x.astype(jnp.float32)
    mu = jnp.mean(xf, axis=-1, keepdims=True)
    xc = xf - mu
    var = jnp.mean(xc * xc, axis=-1, keepdims=True)
    return (xc * lax.rsqrt(var + LN_EPS) * g.astype(jnp.float32) + b.astype(jnp.float32)).astype(x.dtype)


def _ssm_combine(e1, e2):
    a1r, a1i, b1r, b1i = e1
    a2r, a2i, b2r, b2i = e2
    ar = a2r * a1r - a2i * a1i
    ai = a2r * a1i + a2i * a1r
    br = a2r * b1r - a2i * b1i + b2r
    bi = a2r * b1i + a2i * b1r + b2i
    return (ar, ai, br, bi)


def s5_mixer(u, lam_re, lam_im, log_step, b_re, b_im, c_re, c_im, d_skip, w_glu):
    bsz, seq, _ = u.shape
    uf = u.astype(jnp.float32)
    ug = uf.reshape(bsz, seq, SSM_GROUPS, SSM_GROUP_CH)
    lr = lam_re.astype(jnp.float32)
    li = lam_im.astype(jnp.float32)
    dt = jnp.exp(log_step.astype(jnp.float32))[:, None]
    mag = jnp.exp(lr * dt)
    ar = mag * jnp.cos(li * dt)
    ai = mag * jnp.sin(li * dt)
    nr = ar - 1.0
    den = lr * lr + li * li
    kr = (nr * lr + ai * li) / den
    ki = (ai * lr - nr * li) / den
    br = b_re.astype(jnp.float32)
    bi = b_im.astype(jnp.float32)
    bbar_r = kr[..., None] * br - ki[..., None] * bi
    bbar_i = kr[..., None] * bi + ki[..., None] * br
    bu_r = jnp.einsum('blgc,gpc->blgp', ug, bbar_r)
    bu_i = jnp.einsum('blgc,gpc->blgp', ug, bbar_i)
    a_r = jnp.broadcast_to(ar[None, None], (1, seq, SSM_GROUPS, SSM_STATE))
    a_i = jnp.broadcast_to(ai[None, None], (1, seq, SSM_GROUPS, SSM_STATE))
    _, _, h_r, h_i = lax.associative_scan(_ssm_combine, (a_r, a_i, bu_r, bu_i), axis=1)
    y = (jnp.einsum('blgp,gcp->blgc', h_r, c_re.astype(jnp.float32))
         - jnp.einsum('blgp,gcp->blgc', h_i, c_im.astype(jnp.float32)))
    y = y.reshape(bsz, seq, D_MODEL) + d_skip.astype(jnp.float32) * uf
    g = jax.nn.gelu(y).astype(u.dtype)
    val, gate = jnp.split(g @ w_glu, 2, axis=-1)
    return val * jax.nn.sigmoid(gate)


def rope_tables(positions):
    inv_freq = ROPE_THETA ** (-jnp.arange(0, ROT_DIM, 2, dtype=jnp.float32) / ROT_DIM)
    ang = positions.astype(jnp.float32)[..., None] * inv_freq
    return jnp.cos(ang)[:, :, None, :], jnp.sin(ang)[:, :, None, :]


def apply_partial_rope(t, cos, sin):
    half = ROT_DIM // 2
    r = t[..., :ROT_DIM].astype(jnp.float32)
    x1, x2 = r[..., :half], r[..., half:]
    rot = jnp.concatenate([x1 * cos - x2 * sin, x2 * cos + x1 * sin], axis=-1).astype(t.dtype)
    return jnp.concatenate([rot, t[..., ROT_DIM:]], axis=-1)


def shared_kv(h, kv_w, cos, sin):
    bsz, seq, _ = h.shape
    k, v = jnp.split(h @ kv_w, 2, axis=-1)
    k = apply_partial_rope(k.reshape(bsz, seq, N_KV_HEADS, HEAD_DIM), cos, sin)
    v = v.reshape(bsz, seq, N_KV_HEADS, HEAD_DIM)
    return k, v


def _band(t):
    prev = jnp.pad(t[:, :-1], ((0, 0), (1, 0), (0, 0), (0, 0), (0, 0)))
    return jnp.concatenate([prev, t], axis=2)


def sliding_window_attention(h, k, v, w_q, sinks, w_out, cos, sin):
    bsz, seq, _ = h.shape
    nb = seq // BLOCK
    q = apply_partial_rope((h @ w_q).reshape(bsz, seq, N_HEADS, HEAD_DIM), cos, sin)
    q = q.reshape(bsz, nb, BLOCK, N_KV_HEADS, Q_PER_KV, HEAD_DIM)
    kb = _band(k.reshape(bsz, nb, BLOCK, N_KV_HEADS, HEAD_DIM))
    vb = _band(v.reshape(bsz, nb, BLOCK, N_KV_HEADS, HEAD_DIM))
    s = jnp.einsum('bnqkgd,bnskd->bnkgqs', q, kb).astype(jnp.float32) * (HEAD_DIM ** -0.5)
    q_idx = jnp.arange(BLOCK)[:, None]
    s_idx = jnp.arange(2 * BLOCK)[None, :]
    rel = q_idx + BLOCK - s_idx
    valid = (rel >= 0) & (rel < WINDOW)
    has_prev = (jnp.arange(nb)[:, None, None] > 0) | (s_idx[None] >= BLOCK)
    mask = valid[None] & has_prev
    s = jnp.where(mask[None, :, None, None], s, -jnp.inf)
    sink = sinks.astype(jnp.float32).reshape(N_KV_HEADS, Q_PER_KV)[None, None, :, :, None, None]
    m = jnp.maximum(jnp.max(s, axis=-1, keepdims=True), sink)
    p = jnp.exp(s - m)
    denom = jnp.sum(p, axis=-1, keepdims=True) + jnp.exp(sink - m)
    p = (p / denom).astype(v.dtype)
    o = jnp.einsum('bnkgqs,bnskd->bnqkgd', p, vb).reshape(bsz, seq, N_HEADS * HEAD_DIM)
    return o @ w_out


def swiglu(t, w_gate, w_up, w_down):
    return (jax.nn.silu(t @ w_gate) * (t @ w_up)) @ w_down


def moe_swiglu(h, w_router, b_router, w_gate, w_up, w_down):
    bsz, seq, d = h.shape
    t = h.reshape(-1, d)
    logits = (t @ w_router).astype(jnp.float32) + b_router.astype(jnp.float32)
    top_vals, top_idx = lax.top_k(logits, TOP_K)
    top_w = jax.nn.softmax(top_vals, axis=-1)
    combine = jnp.sum(jax.nn.one_hot(top_idx, N_EXPERTS, dtype=jnp.float32) * top_w[..., None], axis=1).astype(h.dtype)
    out = jnp.zeros_like(t)
    for e in range(N_EXPERTS):
        out = out + combine[:, e:e + 1] * swiglu(t, w_gate[e], w_up[e], w_down[e])
    return out.reshape(bsz, seq, d)


def setup_inputs(seed: int = 0) -> dict:
    key = jax.random.key(seed)
    ks = iter(jax.random.split(key, 40))
    f32 = jnp.float32

    def nrm(shape, scale):
        return jax.random.normal(next(ks), shape, f32) * scale

    beta = DEEPNORM_BETA
    x = nrm((BATCH, SEQ, D_MODEL), 1.0)
    offs = jax.random.randint(next(ks), (BATCH, 1), 0, MAX_POS_OFFSET, dtype=jnp.int32)
    positions = (offs + jnp.arange(SEQ, dtype=jnp.int32)[None, :]).astype(jnp.int32)
    ln_g = 1.0 + nrm((DEPTH, 2, D_MODEL), 0.02)
    ln_b = nrm((DEPTH, 2, D_MODEL), 0.02)
    ssm_lambda_re = -0.5 + nrm((N_A_LAYERS, SSM_GROUPS, SSM_STATE), 0.01)
    ssm_lambda_im = math.pi * jnp.arange(SSM_STATE, dtype=f32)[None, None, :] + nrm((N_A_LAYERS, SSM_GROUPS, SSM_STATE), 0.01)
    ssm_log_step = jax.random.uniform(next(ks), (N_A_LAYERS, SSM_GROUPS), f32, math.log(DT_MIN), math.log(DT_MAX))
    ssm_b_re = nrm((N_A_LAYERS, SSM_GROUPS, SSM_STATE, SSM_GROUP_CH), (2 * SSM_GROUP_CH) ** -0.5)
    ssm_b_im = nrm((N_A_LAYERS, SSM_GROUPS, SSM_STATE, SSM_GROUP_CH), (2 * SSM_GROUP_CH) ** -0.5)
    ssm_c_re = nrm((N_A_LAYERS, SSM_GROUPS, SSM_GROUP_CH, SSM_STATE), (2 * SSM_STATE) ** -0.5)
    ssm_c_im = nrm((N_A_LAYERS, SSM_GROUPS, SSM_GROUP_CH, SSM_STATE), (2 * SSM_STATE) ** -0.5)
    ssm_d = nrm((N_A_LAYERS, D_MODEL), 1.0)
    ssm_w_glu = jnp.concatenate([nrm((N_A_LAYERS, D_MODEL, D_MODEL), D_MODEL ** -0.5 * beta),
                                 nrm((N_A_LAYERS, D_MODEL, D_MODEL), D_MODEL ** -0.5)], axis=-1)
    kv_w = jnp.concatenate([nrm((D_MODEL, KV_DIM), D_MODEL ** -0.5),
                            nrm((D_MODEL, KV_DIM), D_MODEL ** -0.5 * beta)], axis=-1)
    attn_w_q = nrm((N_B_LAYERS, D_MODEL, N_HEADS * HEAD_DIM), D_MODEL ** -0.5)
    attn_sinks = nrm((N_B_LAYERS, N_HEADS), 0.5)
    attn_w_out = nrm((N_B_LAYERS, N_HEADS * HEAD_DIM, D_MODEL), (N_HEADS * HEAD_DIM) ** -0.5 * beta)
    ffn_w_gate = nrm((N_DENSE_LAYERS, D_MODEL, FFN_DIM), D_MODEL ** -0.5 * beta)
    ffn_w_up = nrm((N_DENSE_LAYERS, D_MODEL, FFN_DIM), D_MODEL ** -0.5 * beta)
    ffn_w_down = nrm((N_DENSE_LAYERS, FFN_DIM, D_MODEL), FFN_DIM ** -0.5 * beta)
    moe_w_router = nrm((N_MOE_LAYERS, D_MODEL, N_EXPERTS), D_MODEL ** -0.5)
    moe_b_router = nrm((N_MOE_LAYERS, N_EXPERTS), 0.01)
    moe_w_gate = nrm((N_MOE_LAYERS, N_EXPERTS, D_MODEL, EXPERT_DIM), D_MODEL ** -0.5 * beta)
    moe_w_up = nrm((N_MOE_LAYERS, N_EXPERTS, D_MODEL, EXPERT_DIM), D_MODEL ** -0.5 * beta)
    moe_w_down = nrm((N_MOE_LAYERS, N_EXPERTS, EXPERT_DIM, D_MODEL), EXPERT_DIM ** -0.5 * beta)
    return {"x": x, "positions": positions, "ln_g": ln_g, "ln_b": ln_b,
            "ssm_lambda_re": ssm_lambda_re, "ssm_lambda_im": ssm_lambda_im, "ssm_log_step": ssm_log_step,
            "ssm_b_re": ssm_b_re, "ssm_b_im": ssm_b_im, "ssm_c_re": ssm_c_re, "ssm_c_im": ssm_c_im,
            "ssm_d": ssm_d, "ssm_w_glu": ssm_w_glu, "kv_w": kv_w,
            "attn_w_q": attn_w_q, "attn_sinks": attn_sinks, "attn_w_out": attn_w_out,
            "ffn_w_gate": ffn_w_gate, "ffn_w_up": ffn_w_up, "ffn_w_down": ffn_w_down,
            "moe_w_router": moe_w_router, "moe_b_router": moe_b_router,
            "moe_w_gate": moe_w_gate, "moe_w_up": moe_w_up, "moe_w_down": moe_w_down}


def reference(x, positions, ln_g, ln_b, ssm_lambda_re, ssm_lambda_im, ssm_log_step,
              ssm_b_re, ssm_b_im, ssm_c_re, ssm_c_im, ssm_d, ssm_w_glu, kv_w,
              attn_w_q, attn_sinks, attn_w_out, ffn_w_gate, ffn_w_up, ffn_w_down,
              moe_w_router, moe_b_router, moe_w_gate, moe_w_up, moe_w_down):
    cos, sin = rope_tables(positions)
    h = x
    k_shared = None
    v_shared = None
    for layer in range(DEPTH):
        if layer < N_A_LAYERS:
            a = layer
            mix = s5_mixer(h, ssm_lambda_re[a], ssm_lambda_im[a], ssm_log_step[a],
                           ssm_b_re[a], ssm_b_im[a], ssm_c_re[a], ssm_c_im[a], ssm_d[a], ssm_w_glu[a])
        else:
            bl = layer - N_A_LAYERS
            mix = sliding_window_attention(h, k_shared, v_shared, attn_w_q[bl], attn_sinks[bl],
                                           attn_w_out[bl], cos, sin)
        h = layer_norm(DEEPNORM_ALPHA * h + mix, ln_g[layer, 0], ln_b[layer, 0])
        c = layer // 2
        if layer % 2 == 0:
            ff = swiglu(h, ffn_w_gate[c], ffn_w_up[c], ffn_w_down[c])
        else:
            ff = moe_swiglu(h, moe_w_router[c], moe_b_router[c], moe_w_gate[c], moe_w_up[c], moe_w_down[c])
        h = layer_norm(DEEPNORM_ALPHA * h + ff, ln_g[layer, 1], ln_b[layer, 1])
        if layer == N_A_LAYERS - 1 and N_B_LAYERS > 0:
            k_shared, v_shared = shared_kv(h, kv_w, cos, sin)
    return h
```

```python
import functools
import math

import jax
import jax.numpy as jnp
from jax import lax
from jax.experimental import pallas as pl
from jax.experimental.pallas import tpu as pltpu

F32 = jnp.float32
BF16 = jnp.bfloat16

D_MODEL = 1024
BATCH = 16
SEQ = 2048
N_TOK = BATCH * SEQ
DEPTH = 2

SSM_GROUP_CH = 16
SSM_GROUPS = D_MODEL // SSM_GROUP_CH
SSM_STATE = 64
SSM_CHUNK = 16
SSM_NCHUNK = SEQ // SSM_CHUNK
SSM_ROWS = SSM_NCHUNK * BATCH
SSM_WIDTH = SSM_CHUNK * SSM_GROUP_CH
SSM_SCAN_STEPS = int(math.log2(SSM_NCHUNK))

N_HEADS = 16
HEAD_DIM = 64
N_KV_HEADS = 4
Q_PER_KV = N_HEADS // N_KV_HEADS
KV_DIM = N_KV_HEADS * HEAD_DIM
WINDOW = 128
BLOCK = 128
ROT_DIM = HEAD_DIM // 4
ROT_HALF = ROT_DIM // 2
ROPE_THETA = 500000.0

FFN_DIM = 2816
N_EXPERTS = 8
EXPERT_DIM = 1024

DEEPNORM_ALPHA = (2 * DEPTH) ** 0.25
LN_EPS = 1e-5

LANES = 128
VMEM_LIMIT = 56 * 1024 * 1024


def _cparams(*sem):
    return pltpu.CompilerParams(dimension_semantics=sem, vmem_limit_bytes=VMEM_LIMIT)


def _layer_norm(r, g, b):
    mu = jnp.mean(r, axis=-1, keepdims=True)
    xc = r - mu
    var = jnp.mean(xc * xc, axis=-1, keepdims=True)
    return xc * lax.rsqrt(var + LN_EPS) * g + b


def _rope_kernel(invf_ref, pos_ref, cos_ref, sin_ref):
    pos = pos_ref[...]
    for f in range(ROT_HALF):
        ang = pos * invf_ref[f]
        cos_ref[f] = jnp.cos(ang)
        sin_ref[f] = jnp.sin(ang)


def _rope_tables(positions):
    inv_freq = ROPE_THETA ** (-jnp.arange(0, ROT_DIM, 2, dtype=F32) / ROT_DIM)
    pos = positions.astype(F32)
    cos8, sin8 = pl.pallas_call(
        _rope_kernel,
        out_shape=[jax.ShapeDtypeStruct((ROT_HALF, BATCH, SEQ), F32)] * 2,
        in_specs=[pl.BlockSpec(memory_space=pltpu.SMEM),
                  pl.BlockSpec(memory_space=pltpu.VMEM)],
        out_specs=[pl.BlockSpec(memory_space=pltpu.VMEM)] * 2,
        name="rope_tables",
    )(inv_freq, pos)
    cos8 = cos8.transpose(1, 2, 0).reshape(N_TOK, ROT_HALF)
    sin8 = sin8.transpose(1, 2, 0).reshape(N_TOK, ROT_HALF)
    rest = HEAD_DIM - ROT_DIM
    cos_head = jnp.concatenate([cos8, cos8, jnp.ones((N_TOK, rest), F32)], axis=1)
    sin_head = jnp.concatenate([-sin8, sin8, jnp.zeros((N_TOK, rest), F32)], axis=1)
    reps = LANES // HEAD_DIM
    return jnp.tile(cos_head, (1, reps)), jnp.tile(sin_head, (1, reps))


def _ssm_params(lam_re, lam_im, log_step, b_re, b_im, c_re, c_im, d_skip):
    hp = lax.Precision.HIGHEST
    lr, li = lam_re.astype(F32), lam_im.astype(F32)
    dt = jnp.exp(log_step.astype(F32))[:, None]
    mag = jnp.exp(lr * dt)
    ar = mag * jnp.cos(li * dt)
    ai = mag * jnp.sin(li * dt)
    nr = ar - 1.0
    den = lr * lr + li * li
    kr = (nr * lr + ai * li) / den
    ki = (ai * lr - nr * li) / den
    br, bi = b_re.astype(F32), b_im.astype(F32)
    bbar_r = kr[..., None] * br - ki[..., None] * bi
    bbar_i = kr[..., None] * bi + ki[..., None] * br
    cr, ci = c_re.astype(F32), c_im.astype(F32)

    def powers(taus):
        t = taus.astype(F32)[:, None, None]
        m = jnp.exp(lr[None] * dt[None] * t)
        ang = li[None] * dt[None] * t
        return m * jnp.cos(ang), m * jnp.sin(ang)

    er, ei = powers(jnp.arange(SSM_CHUNK + 1))
    w_r = er[:, :, :, None] * bbar_r[None] - ei[:, :, :, None] * bbar_i[None]
    w_i = er[:, :, :, None] * bbar_i[None] + ei[:, :, :, None] * bbar_r[None]
    kern = (jnp.einsum('gcp,tgpd->gtcd', cr, w_r[:SSM_CHUNK], precision=hp)
            - jnp.einsum('gcp,tgpd->gtcd', ci, w_i[:SSM_CHUNK], precision=hp))
    s_idx = jnp.arange(SSM_CHUNK)[:, None]
    t_idx = jnp.arange(SSM_CHUNK)[None, :]
    tau = t_idx - s_idx
    toep = kern[:, jnp.clip(tau, 0, SSM_CHUNK - 1)]
    toep = jnp.where((tau >= 0)[None, :, :, None, None], toep, 0.0)
    m_mat = toep.transpose(0, 1, 4, 2, 3).reshape(SSM_GROUPS, SSM_WIDTH, SSM_WIDTH)
    rev = SSM_CHUNK - 1 - jnp.arange(SSM_CHUNK)
    g_r = w_r[rev].transpose(1, 0, 3, 2).reshape(SSM_GROUPS, SSM_WIDTH, SSM_STATE)
    g_i = w_i[rev].transpose(1, 0, 3, 2).reshape(SSM_GROUPS, SSM_WIDTH, SSM_STATE)
    g_mat = jnp.concatenate([g_r, g_i], axis=-1)
    e1r, e1i = er[1:], ei[1:]
    ce_r = cr[None] * e1r[:, :, None, :] - ci[None] * e1i[:, :, None, :]
    ce_i = cr[None] * e1i[:, :, None, :] + ci[None] * e1r[:, :, None, :]
    c_top = ce_r.transpose(1, 3, 0, 2).reshape(SSM_GROUPS, SSM_STATE, SSM_WIDTH)
    c_bot = (-ce_i).transpose(1, 3, 0, 2).reshape(SSM_GROUPS, SSM_STATE, SSM_WIDTH)
    c_mat = jnp.concatenate([c_top, c_bot], axis=1)
    sr, si = powers(SSM_CHUNK * (2 ** jnp.arange(8)))
    a1 = jnp.concatenate([sr, sr], axis=-1).transpose(1, 0, 2)
    a2 = jnp.concatenate([-si, si], axis=-1).transpose(1, 0, 2)
    d_mat = jnp.tile(d_skip.astype(F32).reshape(SSM_GROUPS, 1, SSM_GROUP_CH), (1, 1, SSM_CHUNK))
    return m_mat.astype(BF16), g_mat.astype(BF16), c_mat.astype(BF16), a1, a2, d_mat


SSM_ROW_TILE = 256


def _ssm_kernel(u_ref, m_ref, g_ref, c_ref, a1_ref, a2_ref, d_ref, o_ref, xa_ref, xb_ref):
    rows = SSM_ROWS
    for r0 in range(0, rows, SSM_ROW_TILE):
        ub = u_ref[0, r0:r0 + SSM_ROW_TILE, :].astype(BF16)
        xa_ref[r0:r0 + SSM_ROW_TILE, :] = jnp.dot(ub, g_ref[0], preferred_element_type=F32)
    bufs = (xa_ref, xb_ref)
    for k in range(SSM_SCAN_STEPS):
        d = BATCH << k
        src, dst = bufs[k % 2], bufs[(k + 1) % 2]
        a1 = a1_ref[0, k:k + 1, :]
        a2 = a2_ref[0, k:k + 1, :]
        dst[0:d, :] = src[0:d, :]
        for r0 in range(d, rows, SSM_ROW_TILE):
            r1 = min(r0 + SSM_ROW_TILE, rows)
            lo = src[r0 - d:r1 - d, :]
            dst[r0:r1, :] = src[r0:r1, :] + a1 * lo + a2 * pltpu.roll(lo, SSM_STATE, 1)
    fin = bufs[SSM_SCAN_STEPS % 2]
    dvec = d_ref[0]
    for r0 in range(0, rows, SSM_ROW_TILE):
        u = u_ref[0, r0:r0 + SSM_ROW_TILE, :]
        ub = u.astype(BF16)
        if r0 == 0:
            sprev = jnp.concatenate(
                [jnp.zeros((BATCH, 2 * SSM_STATE), F32), fin[0:SSM_ROW_TILE - BATCH, :]], axis=0)
        else:
            sprev = fin[r0 - BATCH:r0 + SSM_ROW_TILE - BATCH, :]
        y = (jnp.dot(ub, m_ref[0], preferred_element_type=F32)
             + jnp.dot(sprev.astype(BF16), c_ref[0], preferred_element_type=F32)
             + dvec * u)
        o_ref[0, r0:r0 + SSM_ROW_TILE, :] = jax.nn.gelu(y).astype(BF16)


def _ssm_mixer_act(x, params):
    m_mat, g_mat, c_mat, a1, a2, d_mat = params
    ut = (x.reshape(BATCH, SSM_NCHUNK, SSM_CHUNK, SSM_GROUPS, SSM_GROUP_CH)
          .transpose(3, 1, 0, 2, 4).reshape(SSM_GROUPS, SSM_ROWS, SSM_WIDTH))
    gspec = lambda *shape: pl.BlockSpec((1,) + shape, lambda g: (g, 0, 0))
    yt = pl.pallas_call(
        _ssm_kernel,
        out_shape=jax.ShapeDtypeStruct((SSM_GROUPS, SSM_ROWS, SSM_WIDTH), BF16),
        grid=(SSM_GROUPS,),
        in_specs=[gspec(SSM_ROWS, SSM_WIDTH), gspec(SSM_WIDTH, SSM_WIDTH),
                  gspec(SSM_WIDTH, 2 * SSM_STATE), gspec(2 * SSM_STATE, SSM_WIDTH),
                  gspec(8, 2 * SSM_STATE), gspec(8, 2 * SSM_STATE), gspec(1, SSM_WIDTH)],
        out_specs=gspec(SSM_ROWS, SSM_WIDTH),
        scratch_shapes=[pltpu.VMEM((SSM_ROWS, 2 * SSM_STATE), F32)] * 2,
        compiler_params=_cparams("arbitrary"),
        name="ssm_mixer",
    )(ut, m_mat, g_mat, c_mat, a1, a2, d_mat)
    return (yt.reshape(SSM_GROUPS, SSM_NCHUNK, BATCH, SSM_CHUNK, SSM_GROUP_CH)
            .transpose(2, 1, 3, 0, 4).reshape(N_TOK, D_MODEL))


def _glu_ln_kernel(g_ref, x_ref, w_ref, lg_ref, lb_ref, o_ref):
    z = jnp.dot(g_ref[...], w_ref[...], preferred_element_type=F32)
    mix = z[:, :D_MODEL] * jax.nn.sigmoid(z[:, D_MODEL:])
    o_ref[...] = _layer_norm(DEEPNORM_ALPHA * x_ref[...] + mix, lg_ref[...], lb_ref[...])


def _row_spec(tm, width):
    return pl.BlockSpec((tm, width), lambda i: (i, 0))


def _const_spec(*shape):
    return pl.BlockSpec(shape, lambda *_: (0,) * len(shape))


def _glu_ln(gact, x2, w_glu, lg, lb, tm=512):
    return pl.pallas_call(
        _glu_ln_kernel,
        out_shape=jax.ShapeDtypeStruct((N_TOK, D_MODEL), F32),
        grid=(N_TOK // tm,),
        in_specs=[_row_spec(tm, D_MODEL), _row_spec(tm, D_MODEL),
                  _const_spec(D_MODEL, 2 * D_MODEL), _const_spec(1, D_MODEL), _const_spec(1, D_MODEL)],
        out_specs=_row_spec(tm, D_MODEL),
        compiler_params=_cparams("parallel"),
        name="glu_ln",
    )(gact, x2, w_glu, lg, lb)


FFN_SPLIT = 2
FFN_TILE = FFN_DIM // FFN_SPLIT


def _ffn_ln_kernel(h_ref, wg_ref, wu_ref, wd_ref, lg_ref, lb_ref, o_ref, acc_ref):
    f = pl.program_id(1)
    hb = h_ref[...].astype(BF16)
    gate = jnp.dot(hb, wg_ref[...], preferred_element_type=F32)
    up = jnp.dot(hb, wu_ref[...], preferred_element_type=F32)
    act = (jax.nn.silu(gate) * up).astype(BF16)
    part = jnp.dot(act, wd_ref[...], preferred_element_type=F32)

    @pl.when(f == 0)
    def _():
        acc_ref[...] = part

    @pl.when(f > 0)
    def _():
        acc_ref[...] += part

    @pl.when(f == FFN_SPLIT - 1)
    def _():
        o_ref[...] = _layer_norm(DEEPNORM_ALPHA * h_ref[...] + acc_ref[...], lg_ref[...], lb_ref[...])


def _ffn_ln(h, wg, wu, wd, lg, lb, tm=512):
    return pl.pallas_call(
        _ffn_ln_kernel,
        out_shape=jax.ShapeDtypeStruct((N_TOK, D_MODEL), F32),
        grid=(N_TOK // tm, FFN_SPLIT),
        in_specs=[pl.BlockSpec((tm, D_MODEL), lambda i, f: (i, 0)),
                  pl.BlockSpec((D_MODEL, FFN_TILE), lambda i, f: (0, f)),
                  pl.BlockSpec((D_MODEL, FFN_TILE), lambda i, f: (0, f)),
                  pl.BlockSpec((FFN_TILE, D_MODEL), lambda i, f: (f, 0)),
                  _const_spec(1, D_MODEL), _const_spec(1, D_MODEL)],
        out_specs=pl.BlockSpec((tm, D_MODEL), lambda i, f: (i, 0)),
        scratch_shapes=[pltpu.VMEM((tm, D_MODEL), F32)],
        compiler_params=_cparams("parallel", "arbitrary"),
        name="ffn_ln",
    )(h, wg, wu, wd, lg, lb)


Q_DIM = N_HEADS * HEAD_DIM


def _qkv_kernel(h_ref, w_ref, cos_ref, sin_ref, q_ref, k_ref, v_ref):
    hb = h_ref[...].astype(BF16)
    z = jnp.dot(hb, w_ref[...], preferred_element_type=F32)
    cos_t = cos_ref[...]
    sin_t = sin_ref[...]
    lane = lax.broadcasted_iota(jnp.int32, (1, LANES), 1)
    first_half = (lane % HEAD_DIM) < ROT_HALF

    def rope(t):
        partner = jnp.where(first_half, pltpu.roll(t, LANES - ROT_HALF, 1), pltpu.roll(t, ROT_HALF, 1))
        return t * cos_t + partner * sin_t

    scale = HEAD_DIM ** -0.5
    for c in range(Q_DIM // LANES):
        sl = slice(c * LANES, (c + 1) * LANES)
        q_ref[:, sl] = (rope(z[:, sl]) * scale).astype(BF16)
    for c in range(KV_DIM // LANES):
        sl = slice(c * LANES, (c + 1) * LANES)
        k_ref[:, sl] = rope(z[:, Q_DIM + c * LANES:Q_DIM + (c + 1) * LANES]).astype(BF16)
    v_ref[...] = z[:, Q_DIM + KV_DIM:].astype(BF16)


def _qkv(h, w_qkv, cos_t, sin_t, tm=512):
    return pl.pallas_call(
        _qkv_kernel,
        out_shape=[jax.ShapeDtypeStruct((N_TOK, Q_DIM), BF16),
                   jax.ShapeDtypeStruct((N_TOK, KV_DIM), BF16),
                   jax.ShapeDtypeStruct((N_TOK, KV_DIM), BF16)],
        grid=(N_TOK // tm,),
        in_specs=[_row_spec(tm, D_MODEL), _const_spec(D_MODEL, Q_DIM + 2 * KV_DIM),
                  _row_spec(tm, LANES), _row_spec(tm, LANES)],
        out_specs=[_row_spec(tm, Q_DIM), _row_spec(tm, KV_DIM), _row_spec(tm, KV_DIM)],
        compiler_params=_cparams("parallel"),
        name="qkv_rope",
    )(h, w_qkv, cos_t, sin_t)


N_BLOCKS = SEQ // BLOCK


def _attn_kernel(sink_ref, q_ref, kp_ref, kc_ref, vp_ref, vc_ref, o_ref):
    n = pl.program_id(1)
    rows = Q_PER_KV * BLOCK
    qi = lax.broadcasted_iota(jnp.int32, (rows, 2 * BLOCK), 0) % BLOCK
    si = lax.broadcasted_iota(jnp.int32, (rows, 2 * BLOCK), 1)
    rel = qi + BLOCK - si
    first_key = jnp.where(n > 0, 0, BLOCK)
    valid = (rel >= 0) & (rel < WINDOW) & (si >= first_key)
    for kh in range(N_KV_HEADS):
        ks = slice(kh * HEAD_DIM, (kh + 1) * HEAD_DIM)
        kb = jnp.concatenate([kp_ref[:, ks], kc_ref[:, ks]], axis=0)
        vb = jnp.concatenate([vp_ref[:, ks], vc_ref[:, ks]], axis=0)
        heads = [kh * Q_PER_KV + g for g in range(Q_PER_KV)]
        q4 = jnp.concatenate([q_ref[:, h * HEAD_DIM:(h + 1) * HEAD_DIM] for h in heads], axis=0)
        s = lax.dot_general(q4, kb, (((1,), (1,)), ((), ())), preferred_element_type=F32)
        s = jnp.where(valid, s, -jnp.inf)
        sink = jnp.concatenate([jnp.full((BLOCK, 1), sink_ref[h], F32) for h in heads], axis=0)
        m = jnp.maximum(jnp.max(s, axis=-1, keepdims=True), sink)
        p = jnp.exp(s - m)
        denom = jnp.sum(p, axis=-1, keepdims=True) + jnp.exp(sink - m)
        p = (p / denom).astype(BF16)
        o = jnp.dot(p, vb, preferred_element_type=F32)
        for g, h in enumerate(heads):
            o_ref[:, h * HEAD_DIM:(h + 1) * HEAD_DIM] = o[g * BLOCK:(g + 1) * BLOCK].astype(BF16)


def _attention(q, k, v, sinks):
    cur = lambda b, n: (b * N_BLOCKS + n, 0)
    prev = lambda b, n: (b * N_BLOCKS + jnp.maximum(n - 1, 0), 0)
    return pl.pallas_call(
        _attn_kernel,
        out_shape=jax.ShapeDtypeStruct((N_TOK, Q_DIM), BF16),
        grid=(BATCH, N_BLOCKS),
        in_specs=[pl.BlockSpec(memory_space=pltpu.SMEM),
                  pl.BlockSpec((BLOCK, Q_DIM), cur),
                  pl.BlockSpec((BLOCK, KV_DIM), prev), pl.BlockSpec((BLOCK, KV_DIM), cur),
                  pl.BlockSpec((BLOCK, KV_DIM), prev), pl.BlockSpec((BLOCK, KV_DIM), cur)],
        out_specs=pl.BlockSpec((BLOCK, Q_DIM), cur),
        compiler_params=_cparams("parallel", "arbitrary"),
        name="swa_attention",
    )(sinks, q, k, k, v, v)


def _proj_ln_kernel(o_ref, h_ref, w_ref, lg_ref, lb_ref, out_ref):
    mix = jnp.dot(o_ref[...], w_ref[...], preferred_element_type=F32)
    out_ref[...] = _layer_norm(DEEPNORM_ALPHA * h_ref[...] + mix, lg_ref[...], lb_ref[...])


def _proj_ln(o, h, w, lg, lb, tm=512):
    return pl.pallas_call(
        _proj_ln_kernel,
        out_shape=jax.ShapeDtypeStruct((N_TOK, D_MODEL), F32),
        grid=(N_TOK // tm,),
        in_specs=[_row_spec(tm, Q_DIM), _row_spec(tm, D_MODEL), _const_spec(Q_DIM, D_MODEL),
                  _const_spec(1, D_MODEL), _const_spec(1, D_MODEL)],
        out_specs=_row_spec(tm, D_MODEL),
        compiler_params=_cparams("parallel"),
        name="attn_out_ln",
    )(o, h, w, lg, lb)


def _moe_ln_kernel(h_ref, wr_ref, br_ref, wg_ref, wu_ref, wd_ref, lg_ref, lb_ref, o_ref,
                   hb_ref, comb_ref, acc_ref):
    e = pl.program_id(1)
    lane = lax.broadcasted_iota(jnp.int32, (1, LANES), 1)

    @pl.when(e == 0)
    def _():
        h = h_ref[...]
        hb_ref[...] = h.astype(BF16)
        logits = jnp.dot(h, wr_ref[...], preferred_element_type=F32,
                         precision=lax.Precision.HIGHEST) + br_ref[...]
        logits = jnp.where(lane < N_EXPERTS, logits, -jnp.inf)
        lane_f = lane.astype(F32)
        m1 = jnp.max(logits, axis=-1, keepdims=True)
        i1 = jnp.min(jnp.where(logits == m1, lane_f, float(LANES)), axis=-1, keepdims=True)
        rest = jnp.where(lane_f == i1, -jnp.inf, logits)
        m2 = jnp.max(rest, axis=-1, keepdims=True)
        i2 = jnp.min(jnp.where(rest == m2, lane_f, float(LANES)), axis=-1, keepdims=True)
        e2 = jnp.exp(m2 - m1)
        tot = 1.0 + e2
        comb_ref[...] = jnp.where(lane_f == i1, 1.0 / tot, 0.0) + jnp.where(lane_f == i2, e2 / tot, 0.0)

    hb = hb_ref[...]
    gate = jnp.dot(hb, wg_ref[0], preferred_element_type=F32)
    up = jnp.dot(hb, wu_ref[0], preferred_element_type=F32)
    act = (jax.nn.silu(gate) * up).astype(BF16)
    y = jnp.dot(act, wd_ref[0], preferred_element_type=F32)
    w_e = jnp.sum(jnp.where(lane == e, comb_ref[...], 0.0), axis=-1, keepdims=True)

    @pl.when(e == 0)
    def _():
        acc_ref[...] = w_e * y

    @pl.when(e > 0)
    def _():
        acc_ref[...] += w_e * y

    @pl.when(e == N_EXPERTS - 1)
    def _():
        o_ref[...] = _layer_norm(DEEPNORM_ALPHA * h_ref[...] + acc_ref[...], lg_ref[...], lb_ref[...])


def _moe_ln(h, w_router, b_router, wg, wu, wd, lg, lb, tm=512):
    wspec = lambda a, b: pl.BlockSpec((1, a, b), lambda i, e: (e, 0, 0))
    return pl.pallas_call(
        _moe_ln_kernel,
        out_shape=jax.ShapeDtypeStruct((N_TOK, D_MODEL), F32),
        grid=(N_TOK // tm, N_EXPERTS),
        in_specs=[pl.BlockSpec((tm, D_MODEL), lambda i, e: (i, 0)),
                  _const_spec(D_MODEL, LANES), _const_spec(1, LANES),
                  wspec(D_MODEL, EXPERT_DIM), wspec(D_MODEL, EXPERT_DIM), wspec(EXPERT_DIM, D_MODEL),
                  _const_spec(1, D_MODEL), _const_spec(1, D_MODEL)],
        out_specs=pl.BlockSpec((tm, D_MODEL), lambda i, e: (i, 0)),
        scratch_shapes=[pltpu.VMEM((tm, D_MODEL), BF16), pltpu.VMEM((tm, LANES), F32),
                        pltpu.VMEM((tm, D_MODEL), F32)],
        compiler_params=_cparams("parallel", "arbitrary"),
        name="moe_ln",
    )(h, w_router, b_router, wg, wu, wd, lg, lb)


def kernel(x, positions, ln_g, ln_b, ssm_lambda_re, ssm_lambda_im, ssm_log_step, ssm_b_re, ssm_b_im, ssm_c_re, ssm_c_im, ssm_d, ssm_w_glu, kv_w, attn_w_q, attn_sinks, attn_w_out, ffn_w_gate, ffn_w_up, ffn_w_down, moe_w_router, moe_b_router, moe_w_gate, moe_w_up, moe_w_down):
    ln = lambda layer, j: (ln_g[layer, j].reshape(1, D_MODEL).astype(F32),
                           ln_b[layer, j].reshape(1, D_MODEL).astype(F32))
    cos_t, sin_t = _rope_tables(positions)
    x2 = x.reshape(N_TOK, D_MODEL)

    params = _ssm_params(ssm_lambda_re[0], ssm_lambda_im[0], ssm_log_step[0], ssm_b_re[0], ssm_b_im[0],
                         ssm_c_re[0], ssm_c_im[0], ssm_d[0])
    gact = _ssm_mixer_act(x, params)
    h = _glu_ln(gact, x2, ssm_w_glu[0].astype(BF16), *ln(0, 0))
    h = _ffn_ln(h, ffn_w_gate[0].astype(BF16), ffn_w_up[0].astype(BF16), ffn_w_down[0].astype(BF16),
                *ln(0, 1))

    w_qkv = jnp.concatenate([attn_w_q[0], kv_w], axis=1).astype(BF16)
    q, k, v = _qkv(h, w_qkv, cos_t, sin_t)
    o = _attention(q, k, v, attn_sinks[0].astype(F32))
    h = _proj_ln(o, h, attn_w_out[0].astype(BF16), *ln(1, 0))
    w_router = jnp.pad(moe_w_router[0].astype(F32), ((0, 0), (0, LANES - N_EXPERTS)))
    b_router = jnp.pad(moe_b_router[0].astype(F32), (0, LANES - N_EXPERTS)).reshape(1, LANES)
    h = _moe_ln(h, w_router, b_router, moe_w_gate[0].astype(BF16), moe_w_up[0].astype(BF16),
                moe_w_down[0].astype(BF16), *ln(1, 1))
    return h.reshape(BATCH, SEQ, D_MODEL)
```

```python
import functools
import math

import jax
import jax.numpy as jnp
from jax import lax
from jax.experimental import pallas as pl
from jax.experimental.pallas import tpu as pltpu

F32 = jnp.float32
BF16 = jnp.bfloat16

D_MODEL = 1024
BATCH = 16
SEQ = 2048
N_TOK = BATCH * SEQ
DEPTH = 2

SSM_GROUP_CH = 16
SSM_GROUPS = D_MODEL // SSM_GROUP_CH
SSM_STATE = 64
SSM_CHUNK = 16
SSM_NCHUNK = SEQ // SSM_CHUNK
SSM_ROWS = SSM_NCHUNK * BATCH
SSM_WIDTH = SSM_CHUNK * SSM_GROUP_CH
SSM_SCAN_STEPS = int(math.log2(SSM_NCHUNK))

N_HEADS = 16
HEAD_DIM = 64
N_KV_HEADS = 4
Q_PER_KV = N_HEADS // N_KV_HEADS
KV_DIM = N_KV_HEADS * HEAD_DIM
WINDOW = 128
BLOCK = 128
ROT_DIM = HEAD_DIM // 4
ROT_HALF = ROT_DIM // 2
ROPE_THETA = 500000.0

FFN_DIM = 2816
N_EXPERTS = 8
EXPERT_DIM = 1024

DEEPNORM_ALPHA = (2 * DEPTH) ** 0.25
LN_EPS = 1e-5

LANES = 128
VMEM_LIMIT = 56 * 1024 * 1024


def _cparams(*sem):
    return pltpu.CompilerParams(dimension_semantics=sem, vmem_limit_bytes=VMEM_LIMIT)


def _layer_norm(r, g, b):
    mu = jnp.mean(r, axis=-1, keepdims=True)
    xc = r - mu
    var = jnp.mean(xc * xc, axis=-1, keepdims=True)
    return xc * lax.rsqrt(var + LN_EPS) * g + b


def _rope_kernel(invf_ref, pos_ref, cos_ref, sin_ref):
    pos = pos_ref[...]
    for f in range(ROT_HALF):
        ang = pos * invf_ref[f]
        cos_ref[f] = jnp.cos(ang)
        sin_ref[f] = jnp.sin(ang)


def _rope_tables(positions):
    inv_freq = ROPE_THETA ** (-jnp.arange(0, ROT_DIM, 2, dtype=F32) / ROT_DIM)
    pos = positions.astype(F32)
    cos8, sin8 = pl.pallas_call(
        _rope_kernel,
        out_shape=[jax.ShapeDtypeStruct((ROT_HALF, BATCH, SEQ), F32)] * 2,
        in_specs=[pl.BlockSpec(memory_space=pltpu.SMEM),
                  pl.BlockSpec(memory_space=pltpu.VMEM)],
        out_specs=[pl.BlockSpec(memory_space=pltpu.VMEM)] * 2,
        name="rope_tables",
    )(inv_freq, pos)
    cos8 = cos8.transpose(1, 2, 0).reshape(N_TOK, ROT_HALF)
    sin8 = sin8.transpose(1, 2, 0).reshape(N_TOK, ROT_HALF)
    rest = HEAD_DIM - ROT_DIM
    cos_head = jnp.concatenate([cos8, cos8, jnp.ones((N_TOK, rest), F32)], axis=1)
    sin_head = jnp.concatenate([-sin8, sin8, jnp.zeros((N_TOK, rest), F32)], axis=1)
    reps = LANES // HEAD_DIM
    return jnp.tile(cos_head, (1, reps)), jnp.tile(sin_head, (1, reps))


def _ssm_params(lam_re, lam_im, log_step, b_re, b_im, c_re, c_im, d_skip):
    hp = lax.Precision.HIGHEST
    lr, li = lam_re.astype(F32), lam_im.astype(F32)
    dt = jnp.exp(log_step.astype(F32))[:, None]
    mag = jnp.exp(lr * dt)
    ar = mag * jnp.cos(li * dt)
    ai = mag * jnp.sin(li * dt)
    nr = ar - 1.0
    den = lr * lr + li * li
    kr = (nr * lr + ai * li) / den
    ki = (ai * lr - nr * li) / den
    br, bi = b_re.astype(F32), b_im.astype(F32)
    bbar_r = kr[..., None] * br - ki[..., None] * bi
    bbar_i = kr[..., None] * bi + ki[..., None] * br
    cr, ci = c_re.astype(F32), c_im.astype(F32)

    def powers(taus):
        t = taus.astype(F32)[:, None, None]
        m = jnp.exp(lr[None] * dt[None] * t)
        ang = li[None] * dt[None] * t
        return m * jnp.cos(ang), m * jnp.sin(ang)

    er, ei = powers(jnp.arange(SSM_CHUNK + 1))
    w_r = er[:, :, :, None] * bbar_r[None] - ei[:, :, :, None] * bbar_i[None]
    w_i = er[:, :, :, None] * bbar_i[None] + ei[:, :, :, None] * bbar_r[None]
    kern = (jnp.einsum('gcp,tgpd->gtcd', cr, w_r[:SSM_CHUNK], precision=hp)
            - jnp.einsum('gcp,tgpd->gtcd', ci, w_i[:SSM_CHUNK], precision=hp))
    s_idx = jnp.arange(SSM_CHUNK)[:, None]
    t_idx = jnp.arange(SSM_CHUNK)[None, :]
    tau = t_idx - s_idx
    toep = kern[:, jnp.clip(tau, 0, SSM_CHUNK - 1)]
    toep = jnp.where((tau >= 0)[None, :, :, None, None], toep, 0.0)
    m_mat = toep.transpose(0, 1, 4, 2, 3).reshape(SSM_GROUPS, SSM_WIDTH, SSM_WIDTH)
    rev = SSM_CHUNK - 1 - jnp.arange(SSM_CHUNK)
    g_r = w_r[rev].transpose(1, 0, 3, 2).reshape(SSM_GROUPS, SSM_WIDTH, SSM_STATE)
    g_i = w_i[rev].transpose(1, 0, 3, 2).reshape(SSM_GROUPS, SSM_WIDTH, SSM_STATE)
    g_mat = jnp.concatenate([g_r, g_i], axis=-1)
    e1r, e1i = er[1:], ei[1:]
    ce_r = cr[None] * e1r[:, :, None, :] - ci[None] * e1i[:, :, None, :]
    ce_i = cr[None] * e1i[:, :, None, :] + ci[None] * e1r[:, :, None, :]
    c_top = ce_r.transpose(1, 3, 0, 2).reshape(SSM_GROUPS, SSM_STATE, SSM_WIDTH)
    c_bot = (-ce_i).transpose(1, 3, 0, 2).reshape(SSM_GROUPS, SSM_STATE, SSM_WIDTH)
    c_mat = jnp.concatenate([c_top, c_bot], axis=1)
    sr, si = powers(SSM_CHUNK * (2 ** jnp.arange(8)))
    a1 = jnp.concatenate([sr, sr], axis=-1).transpose(1, 0, 2)
    a2 = jnp.concatenate([-si, si], axis=-1).transpose(1, 0, 2)
    d_mat = jnp.tile(d_skip.astype(F32).reshape(SSM_GROUPS, 1, SSM_GROUP_CH), (1, 1, SSM_CHUNK))
    return m_mat.astype(BF16), g_mat.astype(BF16), c_mat.astype(BF16), a1, a2, d_mat


SSM_ROW_TILE = 256


def _ssm_kernel(u_ref, m_ref, g_ref, c_ref, a1_ref, a2_ref, d_ref, o_ref, xa_ref, xb_ref):
    rows = SSM_ROWS
    for r0 in range(0, rows, SSM_ROW_TILE):
        ub = u_ref[0, r0:r0 + SSM_ROW_TILE, :].astype(BF16)
        xa_ref[r0:r0 + SSM_ROW_TILE, :] = jnp.dot(ub, g_ref[0], preferred_element_type=F32)
    bufs = (xa_ref, xb_ref)
    for k in range(SSM_SCAN_STEPS):
        d = BATCH << k
        src, dst = bufs[k % 2], bufs[(k + 1) % 2]
        a1 = a1_ref[0, k:k + 1, :]
        a2 = a2_ref[0, k:k + 1, :]
        dst[0:d, :] = src[0:d, :]
        for r0 in range(d, rows, SSM_ROW_TILE):
            r1 = min(r0 + SSM_ROW_TILE, rows)
            lo = src[r0 - d:r1 - d, :]
            dst[r0:r1, :] = src[r0:r1, :] + a1 * lo + a2 * pltpu.roll(lo, SSM_STATE, 1)
    fin = bufs[SSM_SCAN_STEPS % 2]
    dvec = d_ref[0]
    for r0 in range(0, rows, SSM_ROW_TILE):
        u = u_ref[0, r0:r0 + SSM_ROW_TILE, :]
        ub = u.astype(BF16)
        if r0 == 0:
            sprev = jnp.concatenate(
                [jnp.zeros((BATCH, 2 * SSM_STATE), F32), fin[0:SSM_ROW_TILE - BATCH, :]], axis=0)
        else:
            sprev = fin[r0 - BATCH:r0 + SSM_ROW_TILE - BATCH, :]
        y = (jnp.dot(ub, m_ref[0], preferred_element_type=F32)
             + jnp.dot(sprev.astype(BF16), c_ref[0], preferred_element_type=F32)
             + dvec * u)
        o_ref[0, r0:r0 + SSM_ROW_TILE, :] = jax.nn.gelu(y).astype(BF16)


def _ssm_mixer_act(x, params):
    m_mat, g_mat, c_mat, a1, a2, d_mat = params
    ut = (x.reshape(BATCH, SSM_NCHUNK, SSM_CHUNK, SSM_GROUPS, SSM_GROUP_CH)
          .transpose(3, 1, 0, 2, 4).reshape(SSM_GROUPS, SSM_ROWS, SSM_WIDTH))
    gspec = lambda *shape: pl.BlockSpec((1,) + shape, lambda g: (g, 0, 0))
    yt = pl.pallas_call(
        _ssm_kernel,
        out_shape=jax.ShapeDtypeStruct((SSM_GROUPS, SSM_ROWS, SSM_WIDTH), BF16),
        grid=(SSM_GROUPS,),
        in_specs=[gspec(SSM_ROWS, SSM_WIDTH), gspec(SSM_WIDTH, SSM_WIDTH),
                  gspec(SSM_WIDTH, 2 * SSM_STATE), gspec(2 * SSM_STATE, SSM_WIDTH),
                  gspec(8, 2 * SSM_STATE), gspec(8, 2 * SSM_STATE), gspec(1, SSM_WIDTH)],
        out_specs=gspec(SSM_ROWS, SSM_WIDTH),
        scratch_shapes=[pltpu.VMEM((SSM_ROWS, 2 * SSM_STATE), F32)] * 2,
        compiler_params=_cparams("arbitrary"),
        name="ssm_mixer",
    )(ut, m_mat, g_mat, c_mat, a1, a2, d_mat)
    return (yt.reshape(SSM_GROUPS, SSM_NCHUNK, BATCH, SSM_CHUNK, SSM_GROUP_CH)
            .transpose(2, 1, 3, 0, 4).reshape(N_TOK, D_MODEL))


def _glu_ln_kernel(g_ref, x_ref, w_ref, lg_ref, lb_ref, o_ref):
    z = jnp.dot(g_ref[...], w_ref[...], preferred_element_type=F32)
    mix = z[:, :D_MODEL] * jax.nn.sigmoid(z[:, D_MODEL:])
    o_ref[...] = _layer_norm(DEEPNORM_ALPHA * x_ref[...] + mix, lg_ref[...], lb_ref[...])


def _row_spec(tm, width):
    return pl.BlockSpec((tm, width), lambda i: (i, 0))


def _const_spec(*shape):
    return pl.BlockSpec(shape, lambda *_: (0,) * len(shape))


def _glu_ln(gact, x2, w_glu, lg, lb, tm=512):
    return pl.pallas_call(
        _glu_ln_kernel,
        out_shape=jax.ShapeDtypeStruct((N_TOK, D_MODEL), F32),
        grid=(N_TOK // tm,),
        in_specs=[_row_spec(tm, D_MODEL), _row_spec(tm, D_MODEL),
                  _const_spec(D_MODEL, 2 * D_MODEL), _const_spec(1, D_MODEL), _const_spec(1, D_MODEL)],
        out_specs=_row_spec(tm, D_MODEL),
        compiler_params=_cparams("parallel"),
        name="glu_ln",
    )(gact, x2, w_glu, lg, lb)


FFN_SPLIT = 2
FFN_TILE = FFN_DIM // FFN_SPLIT


def _ffn_ln_kernel(h_ref, wg_ref, wu_ref, wd_ref, lg_ref, lb_ref, o_ref, acc_ref):
    f = pl.program_id(1)
    hb = h_ref[...].astype(BF16)
    gate = jnp.dot(hb, wg_ref[...], preferred_element_type=F32)
    up = jnp.dot(hb, wu_ref[...], preferred_element_type=F32)
    act = (jax.nn.silu(gate) * up).astype(BF16)
    part = jnp.dot(act, wd_ref[...], preferred_element_type=F32)

    @pl.when(f == 0)
    def _():
        acc_ref[...] = part

    @pl.when(f > 0)
    def _():
        acc_ref[...] += part

    @pl.when(f == FFN_SPLIT - 1)
    def _():
        o_ref[...] = _layer_norm(DEEPNORM_ALPHA * h_ref[...] + acc_ref[...], lg_ref[...], lb_ref[...])


def _ffn_ln(h, wg, wu, wd, lg, lb, tm=512):
    return pl.pallas_call(
        _ffn_ln_kernel,
        out_shape=jax.ShapeDtypeStruct((N_TOK, D_MODEL), F32),
        grid=(N_TOK // tm, FFN_SPLIT),
        in_specs=[pl.BlockSpec((tm, D_MODEL), lambda i, f: (i, 0)),
                  pl.BlockSpec((D_MODEL, FFN_TILE), lambda i, f: (0, f)),
                  pl.BlockSpec((D_MODEL, FFN_TILE), lambda i, f: (0, f)),
                  pl.BlockSpec((FFN_TILE, D_MODEL), lambda i, f: (f, 0)),
                  _const_spec(1, D_MODEL), _const_spec(1, D_MODEL)],
        out_specs=pl.BlockSpec((tm, D_MODEL), lambda i, f: (i, 0)),
        scratch_shapes=[pltpu.VMEM((tm, D_MODEL), F32)],
        compiler_params=_cparams("parallel", "arbitrary"),
        name="ffn_ln",
    )(h, wg, wu, wd, lg, lb)


Q_DIM = N_HEADS * HEAD_DIM


def _qkv_kernel(h_ref, w_ref, cos_ref, sin_ref, q_ref, k_ref, v_ref):
    hb = h_ref[...].astype(BF16)
    z = jnp.dot(hb, w_ref[...], preferred_element_type=F32)
    cos_t = cos_ref[...]
    sin_t = sin_ref[...]
    lane = lax.broadcasted_iota(jnp.int32, (1, LANES), 1)
    first_half = (lane % HEAD_DIM) < ROT_HALF

    def rope(t):
        partner = jnp.where(first_half, pltpu.roll(t, LANES - ROT_HALF, 1), pltpu.roll(t, ROT_HALF, 1))
        return t * cos_t + partner * sin_t

    scale = HEAD_DIM ** -0.5
    for c in range(Q_DIM // LANES):
        sl = slice(c * LANES, (c + 1) * LANES)
        q_ref[:, sl] = (rope(z[:, sl]) * scale).astype(BF16)
    for c in range(KV_DIM // LANES):
        sl = slice(c * LANES, (c + 1) * LANES)
        k_ref[:, sl] = rope(z[:, Q_DIM + c * LANES:Q_DIM + (c + 1) * LANES]).astype(BF16)
    v_ref[...] = z[:, Q_DIM + KV_DIM:].astype(BF16)


def _qkv(h, w_qkv, cos_t, sin_t, tm=512):
    return pl.pallas_call(
        _qkv_kernel,
        out_shape=[jax.ShapeDtypeStruct((N_TOK, Q_DIM), BF16),
                   jax.ShapeDtypeStruct((N_TOK, KV_DIM), BF16),
                   jax.ShapeDtypeStruct((N_TOK, KV_DIM), BF16)],
        grid=(N_TOK // tm,),
        in_specs=[_row_spec(tm, D_MODEL), _const_spec(D_MODEL, Q_DIM + 2 * KV_DIM),
                  _row_spec(tm, LANES), _row_spec(tm, LANES)],
        out_specs=[_row_spec(tm, Q_DIM), _row_spec(tm, KV_DIM), _row_spec(tm, KV_DIM)],
        compiler_params=_cparams("parallel"),
        name="qkv_rope",
    )(h, w_qkv, cos_t, sin_t)


N_BLOCKS = SEQ // BLOCK


ATTN_Q_TILE = 512
ATTN_SUB = ATTN_Q_TILE // BLOCK
ATTN_BAND = ATTN_Q_TILE + BLOCK
KV_PAIRS = N_KV_HEADS // 2

HEAD_ORDER = [h for c in range(KV_PAIRS) for g in range(Q_PER_KV)
              for h in (2 * c * Q_PER_KV + g, (2 * c + 1) * Q_PER_KV + g)]


def _band_bias():
    qi = jnp.arange(Q_PER_KV * BLOCK)[:, None] % BLOCK
    si = jnp.arange(2 * BLOCK)[None, :]
    rel = qi + BLOCK - si
    valid = (rel >= 0) & (rel < WINDOW)
    first = valid & (si >= BLOCK)
    neg = jnp.float32(-jnp.inf)
    return jnp.stack([jnp.where(valid, 0.0, neg), jnp.where(first, 0.0, neg)]).astype(F32)


def _attn_kernel(sink_ref, q_ref, kp_ref, kc_ref, vp_ref, vc_ref, bias_ref, h_ref, w_ref, lg_ref, lb_ref,
                 out_ref, ka_ref, kb_ref, va_ref, vb_ref, o_ref):
    i = pl.program_id(1)
    lane = lax.broadcasted_iota(jnp.int32, (1, KV_DIM), 1)
    low = (lane % LANES) < HEAD_DIM
    zero = jnp.zeros((), BF16)
    for src, dst_a, dst_b in ((kp_ref, ka_ref, kb_ref), (vp_ref, va_ref, vb_ref)):
        t = src[...]
        dst_a[0:BLOCK, :] = jnp.where(low, t, zero)
        dst_b[0:BLOCK, :] = jnp.where(low, zero, t)
    for src, dst_a, dst_b in ((kc_ref, ka_ref, kb_ref), (vc_ref, va_ref, vb_ref)):
        t = src[...]
        dst_a[BLOCK:ATTN_BAND, :] = jnp.where(low, t, zero)
        dst_b[BLOCK:ATTN_BAND, :] = jnp.where(low, zero, t)

    lane1 = lax.broadcasted_iota(jnp.int32, (1, LANES), 1)
    row = lax.broadcasted_iota(jnp.int32, (Q_PER_KV * BLOCK, 1), 0)
    contract_last = (((1,), (1,)), ((), ()))
    for blk in range(ATTN_SUB):
        r0 = blk * BLOCK
        if blk == 0:
            bias = bias_ref[jnp.where(i == 0, 1, 0)]
        else:
            bias = bias_ref[0]
        for c in range(KV_PAIRS):
            cs = slice(c * LANES, (c + 1) * LANES)
            chunks = [c * Q_PER_KV + g for g in range(Q_PER_KV)]
            q4 = jnp.concatenate([q_ref[r0:r0 + BLOCK, m * LANES:(m + 1) * LANES] for m in chunks], axis=0)
            outs = []
            for half, (k_ref, v_ref) in enumerate(((ka_ref, va_ref), (kb_ref, vb_ref))):
                kband = k_ref[r0:r0 + 2 * BLOCK, cs]
                s = lax.dot_general(q4, kband, contract_last, preferred_element_type=F32) + bias
                sink = jnp.zeros((Q_PER_KV * BLOCK, 1), F32)
                for g, m in enumerate(chunks):
                    sink = jnp.where(row // BLOCK == g, sink_ref[HEAD_ORDER[2 * m + half]], sink)
                mx = jnp.maximum(jnp.max(s, axis=-1, keepdims=True), sink)
                p = jnp.exp(s - mx)
                denom = jnp.sum(p, axis=-1, keepdims=True) + jnp.exp(sink - mx)
                pv = jnp.dot(p.astype(BF16), v_ref[r0:r0 + 2 * BLOCK, cs], preferred_element_type=F32)
                outs.append((pv, 1.0 / denom))
            (pv_a, r_a), (pv_b, r_b) = outs
            o = (pv_a + pv_b) * jnp.where(lane1 < HEAD_DIM, r_a, r_b)
            for g, m in enumerate(chunks):
                o_ref[r0:r0 + BLOCK, m * LANES:(m + 1) * LANES] = o[g * BLOCK:(g + 1) * BLOCK].astype(BF16)

    mix = jnp.dot(o_ref[...], w_ref[...], preferred_element_type=F32)
    out_ref[...] = _layer_norm(DEEPNORM_ALPHA * h_ref[...] + mix, lg_ref[...], lb_ref[...])


def _attention_ln(q, k, v, sinks, h, w_out, lg, lb):
    tiles = SEQ // ATTN_Q_TILE
    cur = lambda b, i: (b * tiles + i, 0)
    prev = lambda b, i: (jnp.maximum((b * tiles + i) * ATTN_SUB - 1, 0), 0)
    const2 = lambda b, i: (0, 0)
    return pl.pallas_call(
        _attn_kernel,
        out_shape=jax.ShapeDtypeStruct((N_TOK, D_MODEL), F32),
        grid=(BATCH, tiles),
        in_specs=[pl.BlockSpec(memory_space=pltpu.SMEM),
                  pl.BlockSpec((ATTN_Q_TILE, Q_DIM), cur),
                  pl.BlockSpec((BLOCK, KV_DIM), prev), pl.BlockSpec((ATTN_Q_TILE, KV_DIM), cur),
                  pl.BlockSpec((BLOCK, KV_DIM), prev), pl.BlockSpec((ATTN_Q_TILE, KV_DIM), cur),
                  pl.BlockSpec((2, Q_PER_KV * BLOCK, 2 * BLOCK), lambda b, i: (0, 0, 0)),
                  pl.BlockSpec((ATTN_Q_TILE, D_MODEL), cur),
                  pl.BlockSpec((Q_DIM, D_MODEL), const2),
                  pl.BlockSpec((1, D_MODEL), const2), pl.BlockSpec((1, D_MODEL), const2)],
        out_specs=pl.BlockSpec((ATTN_Q_TILE, D_MODEL), cur),
        scratch_shapes=[pltpu.VMEM((ATTN_BAND, KV_DIM), BF16)] * 4
                       + [pltpu.VMEM((ATTN_Q_TILE, Q_DIM), BF16)],
        compiler_params=_cparams("parallel", "parallel"),
        name="swa_attention_ln",
    )(sinks, q, k, k, v, v, _band_bias(), h, w_out, lg, lb)


def _moe_ln_kernel(h_ref, wr_ref, br_ref, wg_ref, wu_ref, wd_ref, lg_ref, lb_ref, o_ref,
                   hb_ref, comb_ref, acc_ref):
    e = pl.program_id(1)
    lane = lax.broadcasted_iota(jnp.int32, (1, LANES), 1)

    @pl.when(e == 0)
    def _():
        h = h_ref[...]
        hb_ref[...] = h.astype(BF16)
        logits = jnp.dot(h, wr_ref[...], preferred_element_type=F32,
                         precision=lax.Precision.HIGHEST) + br_ref[...]
        logits = jnp.where(lane < N_EXPERTS, logits, -jnp.inf)
        lane_f = lane.astype(F32)
        m1 = jnp.max(logits, axis=-1, keepdims=True)
        i1 = jnp.min(jnp.where(logits == m1, lane_f, float(LANES)), axis=-1, keepdims=True)
        rest = jnp.where(lane_f == i1, -jnp.inf, logits)
        m2 = jnp.max(rest, axis=-1, keepdims=True)
        i2 = jnp.min(jnp.where(rest == m2, lane_f, float(LANES)), axis=-1, keepdims=True)
        e2 = jnp.exp(m2 - m1)
        tot = 1.0 + e2
        comb_ref[...] = jnp.where(lane_f == i1, 1.0 / tot, 0.0) + jnp.where(lane_f == i2, e2 / tot, 0.0)

    hb = hb_ref[...]
    gate = jnp.dot(hb, wg_ref[0], preferred_element_type=F32)
    up = jnp.dot(hb, wu_ref[0], preferred_element_type=F32)
    act = (jax.nn.silu(gate) * up).astype(BF16)
    y = jnp.dot(act, wd_ref[0], preferred_element_type=F32)
    w_e = jnp.sum(jnp.where(lane == e, comb_ref[...], 0.0), axis=-1, keepdims=True)

    @pl.when(e == 0)
    def _():
        acc_ref[...] = w_e * y

    @pl.when(e > 0)
    def _():
        acc_ref[...] += w_e * y

    @pl.when(e == N_EXPERTS - 1)
    def _():
        o_ref[...] = _layer_norm(DEEPNORM_ALPHA * h_ref[...] + acc_ref[...], lg_ref[...], lb_ref[...])


def _moe_ln(h, w_router, b_router, wg, wu, wd, lg, lb, tm=512):
    wspec = lambda a, b: pl.BlockSpec((1, a, b), lambda i, e: (e, 0, 0))
    return pl.pallas_call(
        _moe_ln_kernel,
        out_shape=jax.ShapeDtypeStruct((N_TOK, D_MODEL), F32),
        grid=(N_TOK // tm, N_EXPERTS),
        in_specs=[pl.BlockSpec((tm, D_MODEL), lambda i, e: (i, 0)),
                  _const_spec(D_MODEL, LANES), _const_spec(1, LANES),
                  wspec(D_MODEL, EXPERT_DIM), wspec(D_MODEL, EXPERT_DIM), wspec(EXPERT_DIM, D_MODEL),
                  _const_spec(1, D_MODEL), _const_spec(1, D_MODEL)],
        out_specs=pl.BlockSpec((tm, D_MODEL), lambda i, e: (i, 0)),
        scratch_shapes=[pltpu.VMEM((tm, D_MODEL), BF16), pltpu.VMEM((tm, LANES), F32),
                        pltpu.VMEM((tm, D_MODEL), F32)],
        compiler_params=_cparams("parallel", "arbitrary"),
        name="moe_ln",
    )(h, w_router, b_router, wg, wu, wd, lg, lb)


def kernel(x, positions, ln_g, ln_b, ssm_lambda_re, ssm_lambda_im, ssm_log_step, ssm_b_re, ssm_b_im, ssm_c_re, ssm_c_im, ssm_d, ssm_w_glu, kv_w, attn_w_q, attn_sinks, attn_w_out, ffn_w_gate, ffn_w_up, ffn_w_down, moe_w_router, moe_b_router, moe_w_gate, moe_w_up, moe_w_down):
    ln = lambda layer, j: (ln_g[layer, j].reshape(1, D_MODEL).astype(F32),
                           ln_b[layer, j].reshape(1, D_MODEL).astype(F32))
    cos_t, sin_t = _rope_tables(positions)
    x2 = x.reshape(N_TOK, D_MODEL)

    params = _ssm_params(ssm_lambda_re[0], ssm_lambda_im[0], ssm_log_step[0], ssm_b_re[0], ssm_b_im[0],
                         ssm_c_re[0], ssm_c_im[0], ssm_d[0])
    gact = _ssm_mixer_act(x, params)
    h = _glu_ln(gact, x2, ssm_w_glu[0].astype(BF16), *ln(0, 0))
    h = _ffn_ln(h, ffn_w_gate[0].astype(BF16), ffn_w_up[0].astype(BF16), ffn_w_down[0].astype(BF16),
                *ln(0, 1))

    order = jnp.array(HEAD_ORDER)
    w_q = attn_w_q[0].reshape(D_MODEL, N_HEADS, HEAD_DIM)[:, order].reshape(D_MODEL, Q_DIM)
    w_out = attn_w_out[0].reshape(N_HEADS, HEAD_DIM, D_MODEL)[order].reshape(Q_DIM, D_MODEL)
    w_qkv = jnp.concatenate([w_q, kv_w], axis=1).astype(BF16)
    q, k, v = _qkv(h, w_qkv, cos_t, sin_t)
    h = _attention_ln(q, k, v, attn_sinks[0].astype(F32), h, w_out.astype(BF16), *ln(1, 0))
    w_router = jnp.pad(moe_w_router[0].astype(F32), ((0, 0), (0, LANES - N_EXPERTS)))
    b_router = jnp.pad(moe_b_router[0].astype(F32), (0, LANES - N_EXPERTS)).reshape(1, LANES)
    h = _moe_ln(h, w_router, b_router, moe_w_gate[0].astype(BF16), moe_w_up[0].astype(BF16),
                moe_w_down[0].astype(BF16), *ln(1, 1))
    return h.reshape(BATCH, SEQ, D_MODEL)
```

```python
import functools
import math

import jax
import jax.numpy as jnp
from jax import lax
from jax.experimental import pallas as pl
from jax.experimental.pallas import tpu as pltpu

F32 = jnp.float32
BF16 = jnp.bfloat16

D_MODEL = 1024
BATCH = 16
SEQ = 2048
N_TOK = BATCH * SEQ
DEPTH = 2

SSM_GROUP_CH = 16
SSM_GROUPS = D_MODEL // SSM_GROUP_CH
SSM_STATE = 64
SSM_CHUNK = 16
SSM_NCHUNK = SEQ // SSM_CHUNK
SSM_ROWS = SSM_NCHUNK * BATCH
SSM_WIDTH = SSM_CHUNK * SSM_GROUP_CH
SSM_SCAN_STEPS = int(math.log2(SSM_NCHUNK))

N_HEADS = 16
HEAD_DIM = 64
N_KV_HEADS = 4
Q_PER_KV = N_HEADS // N_KV_HEADS
KV_DIM = N_KV_HEADS * HEAD_DIM
WINDOW = 128
BLOCK = 128
ROT_DIM = HEAD_DIM // 4
ROT_HALF = ROT_DIM // 2
ROPE_THETA = 500000.0

FFN_DIM = 2816
N_EXPERTS = 8
EXPERT_DIM = 1024

DEEPNORM_ALPHA = (2 * DEPTH) ** 0.25
LN_EPS = 1e-5

LANES = 128
VMEM_LIMIT = 56 * 1024 * 1024


def _cparams(*sem):
    return pltpu.CompilerParams(dimension_semantics=sem, vmem_limit_bytes=VMEM_LIMIT)


def _layer_norm(r, g, b):
    mu = jnp.mean(r, axis=-1, keepdims=True)
    xc = r - mu
    var = jnp.mean(xc * xc, axis=-1, keepdims=True)
    return xc * lax.rsqrt(var + LN_EPS) * g + b


def _rope_kernel(invf_ref, pos_ref, cos_ref, sin_ref):
    pos = pos_ref[...]
    for f in range(ROT_HALF):
        ang = pos * invf_ref[f]
        cos_ref[f] = jnp.cos(ang)
        sin_ref[f] = jnp.sin(ang)


def _rope_tables(positions):
    inv_freq = ROPE_THETA ** (-jnp.arange(0, ROT_DIM, 2, dtype=F32) / ROT_DIM)
    pos = positions.astype(F32)
    cos8, sin8 = pl.pallas_call(
        _rope_kernel,
        out_shape=[jax.ShapeDtypeStruct((ROT_HALF, BATCH, SEQ), F32)] * 2,
        in_specs=[pl.BlockSpec(memory_space=pltpu.SMEM),
                  pl.BlockSpec(memory_space=pltpu.VMEM)],
        out_specs=[pl.BlockSpec(memory_space=pltpu.VMEM)] * 2,
        name="rope_tables",
    )(inv_freq, pos)
    cos8 = cos8.transpose(1, 2, 0).reshape(N_TOK, ROT_HALF)
    sin8 = sin8.transpose(1, 2, 0).reshape(N_TOK, ROT_HALF)
    rest = HEAD_DIM - ROT_DIM
    cos_head = jnp.concatenate([cos8, cos8, jnp.ones((N_TOK, rest), F32)], axis=1)
    sin_head = jnp.concatenate([-sin8, sin8, jnp.zeros((N_TOK, rest), F32)], axis=1)
    reps = LANES // HEAD_DIM
    return jnp.tile(cos_head, (1, reps)), jnp.tile(sin_head, (1, reps))


def _ssm_params(lam_re, lam_im, log_step, b_re, b_im, c_re, c_im):
    hp = lax.Precision.HIGHEST
    lr, li = lam_re.astype(F32), lam_im.astype(F32)
    dt = jnp.exp(log_step.astype(F32))[:, None]
    mag = jnp.exp(lr * dt)
    ar = mag * jnp.cos(li * dt)
    ai = mag * jnp.sin(li * dt)
    nr = ar - 1.0
    den = lr * lr + li * li
    kr = (nr * lr + ai * li) / den
    ki = (ai * lr - nr * li) / den
    br, bi = b_re.astype(F32), b_im.astype(F32)
    bbar_r = kr[..., None] * br - ki[..., None] * bi
    bbar_i = kr[..., None] * bi + ki[..., None] * br
    cr, ci = c_re.astype(F32), c_im.astype(F32)

    def powers(taus):
        t = taus.astype(F32)[:, None, None]
        m = jnp.exp(lr[None] * dt[None] * t)
        ang = li[None] * dt[None] * t
        return m * jnp.cos(ang), m * jnp.sin(ang)

    er, ei = powers(jnp.arange(SSM_CHUNK + 1))
    w_r = er[:, :, :, None] * bbar_r[None] - ei[:, :, :, None] * bbar_i[None]
    w_i = er[:, :, :, None] * bbar_i[None] + ei[:, :, :, None] * bbar_r[None]
    kern = (jnp.einsum('gcp,tgpd->gtcd', cr, w_r[:SSM_CHUNK], precision=hp)
            - jnp.einsum('gcp,tgpd->gtcd', ci, w_i[:SSM_CHUNK], precision=hp))
    s_idx = jnp.arange(SSM_CHUNK)[:, None, None]
    t_idx = jnp.arange(SSM_CHUNK)[None, :, None]
    lag = (t_idx - s_idx == jnp.arange(SSM_CHUNK)[None, None, :]).astype(F32)
    toep = jnp.einsum('stu,gucd->gtcsd', lag, kern, precision=hp)
    mt = toep.reshape(SSM_GROUPS, SSM_WIDTH, SSM_WIDTH)
    rev = SSM_CHUNK - 1 - jnp.arange(SSM_CHUNK)
    g_r = w_r[rev].transpose(1, 2, 0, 3).reshape(SSM_GROUPS, SSM_STATE, SSM_WIDTH)
    g_i = w_i[rev].transpose(1, 2, 0, 3).reshape(SSM_GROUPS, SSM_STATE, SSM_WIDTH)
    gt = jnp.concatenate([g_r, g_i], axis=1)
    e1r, e1i = er[1:], ei[1:]
    ce_r = cr[None] * e1r[:, :, None, :] - ci[None] * e1i[:, :, None, :]
    ce_i = cr[None] * e1i[:, :, None, :] + ci[None] * e1r[:, :, None, :]
    c_re = ce_r.transpose(1, 0, 2, 3).reshape(SSM_GROUPS, SSM_WIDTH, SSM_STATE)
    c_im = (-ce_i).transpose(1, 0, 2, 3).reshape(SSM_GROUPS, SSM_WIDTH, SSM_STATE)
    ct = jnp.concatenate([c_re, c_im], axis=-1)
    sr, si = powers(SSM_CHUNK * (2 ** jnp.arange(8)))
    return (mt.astype(BF16), gt.astype(BF16), ct.astype(BF16),
            sr.transpose(1, 2, 0), si.transpose(1, 2, 0))


SSM_LANE_GROUPS = LANES // SSM_GROUP_CH
SSM_LANE_CHUNKS = D_MODEL // LANES
SSM_BATCH_TILE = 4
SSM_COLS = SSM_BATCH_TILE * SSM_NCHUNK


def _ssm_kernel(x_ref, mt_ref, gt_ref, ct_ref, cr_ref, ci_ref, y_ref, v_ref, yt_ref):
    for bl in range(SSM_BATCH_TILE):
        cols = slice(bl * SSM_NCHUNK, (bl + 1) * SSM_NCHUNK)
        for s in range(SSM_CHUNK):
            a = x_ref[pl.ds(bl * SEQ + s, SSM_NCHUNK, stride=SSM_CHUNK), :]
            at = a.T.astype(BF16)
            for g in range(SSM_LANE_GROUPS):
                v_ref[g, s * SSM_GROUP_CH:(s + 1) * SSM_GROUP_CH, cols] = (
                    at[g * SSM_GROUP_CH:(g + 1) * SSM_GROUP_CH])

    lane_j = lax.broadcasted_iota(jnp.int32, (1, SSM_NCHUNK), 1)

    def group_body(g, carry):
        ut = v_ref[g]
        xt = jnp.dot(gt_ref[g], ut, preferred_element_type=F32)
        cr_all = cr_ref[g]
        ci_all = ci_ref[g]
        xr = [xt[0:SSM_STATE, b * SSM_NCHUNK:(b + 1) * SSM_NCHUNK] for b in range(SSM_BATCH_TILE)]
        xi = [xt[SSM_STATE:, b * SSM_NCHUNK:(b + 1) * SSM_NCHUNK] for b in range(SSM_BATCH_TILE)]
        for k in range(SSM_SCAN_STEPS):
            d = 1 << k
            keep = lane_j >= d
            cr = jnp.where(keep, cr_all[:, k:k + 1], 0.0)
            ci = jnp.where(keep, ci_all[:, k:k + 1], 0.0)
            for b in range(SSM_BATCH_TILE):
                rr = pltpu.roll(xr[b], d, 1)
                ri = pltpu.roll(xi[b], d, 1)
                xr[b], xi[b] = xr[b] + cr * rr - ci * ri, xi[b] + cr * ri + ci * rr
        keep = lane_j >= 1
        sr = jnp.concatenate([jnp.where(keep, pltpu.roll(t, 1, 1), 0.0) for t in xr], axis=1)
        si = jnp.concatenate([jnp.where(keep, pltpu.roll(t, 1, 1), 0.0) for t in xi], axis=1)
        sprev = jnp.concatenate([sr, si], axis=0).astype(BF16)
        yt_ref[g] = (jnp.dot(mt_ref[g], ut, preferred_element_type=F32)
                     + jnp.dot(ct_ref[g], sprev, preferred_element_type=F32))
        return carry

    lax.fori_loop(0, SSM_LANE_GROUPS, group_body, 0)

    for bl in range(SSM_BATCH_TILE):
        cols = slice(bl * SSM_NCHUNK, (bl + 1) * SSM_NCHUNK)
        for t in range(SSM_CHUNK):
            tile = jnp.concatenate(
                [yt_ref[g, t * SSM_GROUP_CH:(t + 1) * SSM_GROUP_CH, cols] for g in range(SSM_LANE_GROUPS)],
                axis=0)
            y_ref[pl.ds(bl * SEQ + t, SSM_NCHUNK, stride=SSM_CHUNK), :] = tile.T


def _ssm_mixer(x2, params):
    mt, gt, ct, coef_r, coef_i = params
    rows = SSM_BATCH_TILE * SEQ
    pspec = lambda a, b: pl.BlockSpec((SSM_LANE_GROUPS, a, b), lambda k, q: (k, 0, 0))
    return pl.pallas_call(
        _ssm_kernel,
        out_shape=jax.ShapeDtypeStruct((N_TOK, D_MODEL), F32),
        grid=(SSM_LANE_CHUNKS, BATCH // SSM_BATCH_TILE),
        in_specs=[pl.BlockSpec((rows, LANES), lambda k, q: (q, k)),
                  pspec(SSM_WIDTH, SSM_WIDTH), pspec(2 * SSM_STATE, SSM_WIDTH), pspec(SSM_WIDTH, 2 * SSM_STATE),
                  pspec(SSM_STATE, 8), pspec(SSM_STATE, 8)],
        out_specs=pl.BlockSpec((rows, LANES), lambda k, q: (q, k)),
        scratch_shapes=[pltpu.VMEM((SSM_LANE_GROUPS, SSM_WIDTH, SSM_COLS), BF16),
                        pltpu.VMEM((SSM_LANE_GROUPS, SSM_WIDTH, SSM_COLS), F32)],
        compiler_params=_cparams("parallel", "parallel"),
        name="ssm_mixer",
    )(x2, mt, gt, ct, coef_r, coef_i)


def _glu_ln_kernel(y_ref, x_ref, d_ref, w_ref, lg_ref, lb_ref, o_ref):
    x = x_ref[...]
    act = jax.nn.gelu(y_ref[...] + d_ref[...] * x).astype(BF16)
    z = jnp.dot(act, w_ref[...], preferred_element_type=F32)
    mix = z[:, :D_MODEL] * jax.nn.sigmoid(z[:, D_MODEL:])
    o_ref[...] = _layer_norm(DEEPNORM_ALPHA * x + mix, lg_ref[...], lb_ref[...])


def _row_spec(tm, width):
    return pl.BlockSpec((tm, width), lambda i: (i, 0))


def _const_spec(*shape):
    return pl.BlockSpec(shape, lambda *_: (0,) * len(shape))


def _glu_ln(y, x2, d_skip, w_glu, lg, lb, tm=512):
    return pl.pallas_call(
        _glu_ln_kernel,
        out_shape=jax.ShapeDtypeStruct((N_TOK, D_MODEL), F32),
        grid=(N_TOK // tm,),
        in_specs=[_row_spec(tm, D_MODEL), _row_spec(tm, D_MODEL), _const_spec(1, D_MODEL),
                  _const_spec(D_MODEL, 2 * D_MODEL), _const_spec(1, D_MODEL), _const_spec(1, D_MODEL)],
        out_specs=_row_spec(tm, D_MODEL),
        compiler_params=_cparams("parallel"),
        name="glu_ln",
    )(y, x2, d_skip, w_glu, lg, lb)


FFN_SPLIT = 2
FFN_TILE = FFN_DIM // FFN_SPLIT


def _ffn_ln_kernel(h_ref, wg_ref, wu_ref, wd_ref, lg_ref, lb_ref, o_ref, acc_ref):
    f = pl.program_id(1)
    hb = h_ref[...].astype(BF16)
    gate = jnp.dot(hb, wg_ref[...], preferred_element_type=F32)
    up = jnp.dot(hb, wu_ref[...], preferred_element_type=F32)
    act = (jax.nn.silu(gate) * up).astype(BF16)
    part = jnp.dot(act, wd_ref[...], preferred_element_type=F32)

    @pl.when(f == 0)
    def _():
        acc_ref[...] = part

    @pl.when(f > 0)
    def _():
        acc_ref[...] += part

    @pl.when(f == FFN_SPLIT - 1)
    def _():
        o_ref[...] = _layer_norm(DEEPNORM_ALPHA * h_ref[...] + acc_ref[...], lg_ref[...], lb_ref[...])


def _ffn_ln(h, wg, wu, wd, lg, lb, tm=512):
    return pl.pallas_call(
        _ffn_ln_kernel,
        out_shape=jax.ShapeDtypeStruct((N_TOK, D_MODEL), F32),
        grid=(N_TOK // tm, FFN_SPLIT),
        in_specs=[pl.BlockSpec((tm, D_MODEL), lambda i, f: (i, 0)),
                  pl.BlockSpec((D_MODEL, FFN_TILE), lambda i, f: (0, f)),
                  pl.BlockSpec((D_MODEL, FFN_TILE), lambda i, f: (0, f)),
                  pl.BlockSpec((FFN_TILE, D_MODEL), lambda i, f: (f, 0)),
                  _const_spec(1, D_MODEL), _const_spec(1, D_MODEL)],
        out_specs=pl.BlockSpec((tm, D_MODEL), lambda i, f: (i, 0)),
        scratch_shapes=[pltpu.VMEM((tm, D_MODEL), F32)],
        compiler_params=_cparams("parallel", "arbitrary"),
        name="ffn_ln",
    )(h, wg, wu, wd, lg, lb)


Q_DIM = N_HEADS * HEAD_DIM


def _qkv_kernel(h_ref, w_ref, cos_ref, sin_ref, q_ref, k_ref, v_ref):
    hb = h_ref[...].astype(BF16)
    z = jnp.dot(hb, w_ref[...], preferred_element_type=F32)
    cos_t = cos_ref[...]
    sin_t = sin_ref[...]
    lane = lax.broadcasted_iota(jnp.int32, (1, LANES), 1)
    first_half = (lane % HEAD_DIM) < ROT_HALF

    def rope(t):
        partner = jnp.where(first_half, pltpu.roll(t, LANES - ROT_HALF, 1), pltpu.roll(t, ROT_HALF, 1))
        return t * cos_t + partner * sin_t

    scale = HEAD_DIM ** -0.5
    for c in range(Q_DIM // LANES):
        sl = slice(c * LANES, (c + 1) * LANES)
        q_ref[:, sl] = (rope(z[:, sl]) * scale).astype(BF16)
    for c in range(KV_DIM // LANES):
        sl = slice(c * LANES, (c + 1) * LANES)
        k_ref[:, sl] = rope(z[:, Q_DIM + c * LANES:Q_DIM + (c + 1) * LANES]).astype(BF16)
    v_ref[...] = z[:, Q_DIM + KV_DIM:].astype(BF16)


def _qkv(h, w_qkv, cos_t, sin_t, tm=512):
    return pl.pallas_call(
        _qkv_kernel,
        out_shape=[jax.ShapeDtypeStruct((N_TOK, Q_DIM), BF16),
                   jax.ShapeDtypeStruct((N_TOK, KV_DIM), BF16),
                   jax.ShapeDtypeStruct((N_TOK, KV_DIM), BF16)],
        grid=(N_TOK // tm,),
        in_specs=[_row_spec(tm, D_MODEL), _const_spec(D_MODEL, Q_DIM + 2 * KV_DIM),
                  _row_spec(tm, LANES), _row_spec(tm, LANES)],
        out_specs=[_row_spec(tm, Q_DIM), _row_spec(tm, KV_DIM), _row_spec(tm, KV_DIM)],
        compiler_params=_cparams("parallel"),
        name="qkv_rope",
    )(h, w_qkv, cos_t, sin_t)


N_BLOCKS = SEQ // BLOCK


ATTN_Q_TILE = 512
ATTN_SUB = ATTN_Q_TILE // BLOCK
ATTN_BAND = ATTN_Q_TILE + BLOCK
KV_PAIRS = N_KV_HEADS // 2

HEAD_ORDER = [h for c in range(KV_PAIRS) for g in range(Q_PER_KV)
              for h in (2 * c * Q_PER_KV + g, (2 * c + 1) * Q_PER_KV + g)]


def _band_bias():
    qi = jnp.arange(Q_PER_KV * BLOCK)[:, None] % BLOCK
    si = jnp.arange(2 * BLOCK)[None, :]
    rel = qi + BLOCK - si
    valid = (rel >= 0) & (rel < WINDOW)
    first = valid & (si >= BLOCK)
    neg = jnp.float32(-jnp.inf)
    return jnp.stack([jnp.where(valid, 0.0, neg), jnp.where(first, 0.0, neg)]).astype(F32)


def _attn_kernel(sink_ref, q_ref, kp_ref, kc_ref, vp_ref, vc_ref, bias_ref, h_ref, w_ref, lg_ref, lb_ref,
                 out_ref, ka_ref, kb_ref, va_ref, vb_ref, o_ref):
    i = pl.program_id(1)
    lane = lax.broadcasted_iota(jnp.int32, (1, KV_DIM), 1)
    low = (lane % LANES) < HEAD_DIM
    zero = jnp.zeros((), BF16)
    for src, dst_a, dst_b in ((kp_ref, ka_ref, kb_ref), (vp_ref, va_ref, vb_ref)):
        t = src[...]
        dst_a[0:BLOCK, :] = jnp.where(low, t, zero)
        dst_b[0:BLOCK, :] = jnp.where(low, zero, t)
    for src, dst_a, dst_b in ((kc_ref, ka_ref, kb_ref), (vc_ref, va_ref, vb_ref)):
        t = src[...]
        dst_a[BLOCK:ATTN_BAND, :] = jnp.where(low, t, zero)
        dst_b[BLOCK:ATTN_BAND, :] = jnp.where(low, zero, t)

    lane1 = lax.broadcasted_iota(jnp.int32, (1, LANES), 1)
    row = lax.broadcasted_iota(jnp.int32, (Q_PER_KV * BLOCK, 1), 0)
    contract_last = (((1,), (1,)), ((), ()))
    for blk in range(ATTN_SUB):
        r0 = blk * BLOCK
        if blk == 0:
            bias = bias_ref[jnp.where(i == 0, 1, 0)]
        else:
            bias = bias_ref[0]
        for c in range(KV_PAIRS):
            cs = slice(c * LANES, (c + 1) * LANES)
            chunks = [c * Q_PER_KV + g for g in range(Q_PER_KV)]
            q4 = jnp.concatenate([q_ref[r0:r0 + BLOCK, m * LANES:(m + 1) * LANES] for m in chunks], axis=0)
            outs = []
            for half, (k_ref, v_ref) in enumerate(((ka_ref, va_ref), (kb_ref, vb_ref))):
                kband = k_ref[r0:r0 + 2 * BLOCK, cs]
                s = lax.dot_general(q4, kband, contract_last, preferred_element_type=F32) + bias
                sink = jnp.zeros((Q_PER_KV * BLOCK, 1), F32)
                for g, m in enumerate(chunks):
                    sink = jnp.where(row // BLOCK == g, sink_ref[HEAD_ORDER[2 * m + half]], sink)
                mx = jnp.maximum(jnp.max(s, axis=-1, keepdims=True), sink)
                p = jnp.exp(s - mx)
                denom = jnp.sum(p, axis=-1, keepdims=True) + jnp.exp(sink - mx)
                pv = jnp.dot(p.astype(BF16), v_ref[r0:r0 + 2 * BLOCK, cs], preferred_element_type=F32)
                outs.append((pv, 1.0 / denom))
            (pv_a, r_a), (pv_b, r_b) = outs
            o = (pv_a + pv_b) * jnp.where(lane1 < HEAD_DIM, r_a, r_b)
            for g, m in enumerate(chunks):
                o_ref[r0:r0 + BLOCK, m * LANES:(m + 1) * LANES] = o[g * BLOCK:(g + 1) * BLOCK].astype(BF16)

    mix = jnp.dot(o_ref[...], w_ref[...], preferred_element_type=F32)
    out_ref[...] = _layer_norm(DEEPNORM_ALPHA * h_ref[...] + mix, lg_ref[...], lb_ref[...])


def _attention_ln(q, k, v, sinks, h, w_out, lg, lb):
    tiles = SEQ // ATTN_Q_TILE
    cur = lambda b, i: (b * tiles + i, 0)
    prev = lambda b, i: (jnp.maximum((b * tiles + i) * ATTN_SUB - 1, 0), 0)
    const2 = lambda b, i: (0, 0)
    return pl.pallas_call(
        _attn_kernel,
        out_shape=jax.ShapeDtypeStruct((N_TOK, D_MODEL), F32),
        grid=(BATCH, tiles),
        in_specs=[pl.BlockSpec(memory_space=pltpu.SMEM),
                  pl.BlockSpec((ATTN_Q_TILE, Q_DIM), cur),
                  pl.BlockSpec((BLOCK, KV_DIM), prev), pl.BlockSpec((ATTN_Q_TILE, KV_DIM), cur),
                  pl.BlockSpec((BLOCK, KV_DIM), prev), pl.BlockSpec((ATTN_Q_TILE, KV_DIM), cur),
                  pl.BlockSpec((2, Q_PER_KV * BLOCK, 2 * BLOCK), lambda b, i: (0, 0, 0)),
                  pl.BlockSpec((ATTN_Q_TILE, D_MODEL), cur),
                  pl.BlockSpec((Q_DIM, D_MODEL), const2),
                  pl.BlockSpec((1, D_MODEL), const2), pl.BlockSpec((1, D_MODEL), const2)],
        out_specs=pl.BlockSpec((ATTN_Q_TILE, D_MODEL), cur),
        scratch_shapes=[pltpu.VMEM((ATTN_BAND, KV_DIM), BF16)] * 4
                       + [pltpu.VMEM((ATTN_Q_TILE, Q_DIM), BF16)],
        compiler_params=_cparams("parallel", "parallel"),
        name="swa_attention_ln",
    )(sinks, q, k, k, v, v, _band_bias(), h, w_out, lg, lb)


def _moe_ln_kernel(h_ref, wr_ref, br_ref, wg_ref, wu_ref, wd_ref, lg_ref, lb_ref, o_ref,
                   hb_ref, comb_ref, acc_ref):
    e = pl.program_id(1)
    lane = lax.broadcasted_iota(jnp.int32, (1, LANES), 1)

    @pl.when(e == 0)
    def _():
        h = h_ref[...]
        hb_ref[...] = h.astype(BF16)
        logits = jnp.dot(h, wr_ref[...], preferred_element_type=F32,
                         precision=lax.Precision.HIGHEST) + br_ref[...]
        logits = jnp.where(lane < N_EXPERTS, logits, -jnp.inf)
        lane_f = lane.astype(F32)
        m1 = jnp.max(logits, axis=-1, keepdims=True)
        i1 = jnp.min(jnp.where(logits == m1, lane_f, float(LANES)), axis=-1, keepdims=True)
        rest = jnp.where(lane_f == i1, -jnp.inf, logits)
        m2 = jnp.max(rest, axis=-1, keepdims=True)
        i2 = jnp.min(jnp.where(rest == m2, lane_f, float(LANES)), axis=-1, keepdims=True)
        e2 = jnp.exp(m2 - m1)
        tot = 1.0 + e2
        comb_ref[...] = jnp.where(lane_f == i1, 1.0 / tot, 0.0) + jnp.where(lane_f == i2, e2 / tot, 0.0)

    hb = hb_ref[...]
    gate = jnp.dot(hb, wg_ref[0], preferred_element_type=F32)
    up = jnp.dot(hb, wu_ref[0], preferred_element_type=F32)
    act = (jax.nn.silu(gate) * up).astype(BF16)
    y = jnp.dot(act, wd_ref[0], preferred_element_type=F32)
    w_e = jnp.sum(jnp.where(lane == e, comb_ref[...], 0.0), axis=-1, keepdims=True)

    @pl.when(e == 0)
    def _():
        acc_ref[...] = w_e * y

    @pl.when(e > 0)
    def _():
        acc_ref[...] += w_e * y

    @pl.when(e == N_EXPERTS - 1)
    def _():
        o_ref[...] = _layer_norm(DEEPNORM_ALPHA * h_ref[...] + acc_ref[...], lg_ref[...], lb_ref[...])


def _moe_ln(h, w_router, b_router, wg, wu, wd, lg, lb, tm=512):
    wspec = lambda a, b: pl.BlockSpec((1, a, b), lambda i, e: (e, 0, 0))
    return pl.pallas_call(
        _moe_ln_kernel,
        out_shape=jax.ShapeDtypeStruct((N_TOK, D_MODEL), F32),
        grid=(N_TOK // tm, N_EXPERTS),
        in_specs=[pl.BlockSpec((tm, D_MODEL), lambda i, e: (i, 0)),
                  _const_spec(D_MODEL, LANES), _const_spec(1, LANES),
                  wspec(D_MODEL, EXPERT_DIM), wspec(D_MODEL, EXPERT_DIM), wspec(EXPERT_DIM, D_MODEL),
                  _const_spec(1, D_MODEL), _const_spec(1, D_MODEL)],
        out_specs=pl.BlockSpec((tm, D_MODEL), lambda i, e: (i, 0)),
        scratch_shapes=[pltpu.VMEM((tm, D_MODEL), BF16), pltpu.VMEM((tm, LANES), F32),
                        pltpu.VMEM((tm, D_MODEL), F32)],
        compiler_params=_cparams("parallel", "arbitrary"),
        name="moe_ln",
    )(h, w_router, b_router, wg, wu, wd, lg, lb)


def kernel(x, positions, ln_g, ln_b, ssm_lambda_re, ssm_lambda_im, ssm_log_step, ssm_b_re, ssm_b_im, ssm_c_re, ssm_c_im, ssm_d, ssm_w_glu, kv_w, attn_w_q, attn_sinks, attn_w_out, ffn_w_gate, ffn_w_up, ffn_w_down, moe_w_router, moe_b_router, moe_w_gate, moe_w_up, moe_w_down):
    ln = lambda layer, j: (ln_g[layer, j].reshape(1, D_MODEL).astype(F32),
                           ln_b[layer, j].reshape(1, D_MODEL).astype(F32))
    cos_t, sin_t = _rope_tables(positions)
    x2 = x.reshape(N_TOK, D_MODEL)

    params = _ssm_params(ssm_lambda_re[0], ssm_lambda_im[0], ssm_log_step[0], ssm_b_re[0], ssm_b_im[0],
                         ssm_c_re[0], ssm_c_im[0])
    y = _ssm_mixer(x2, params)
    h = _glu_ln(y, x2, ssm_d[0].astype(F32).reshape(1, D_MODEL), ssm_w_glu[0].astype(BF16), *ln(0, 0))
    h = _ffn_ln(h, ffn_w_gate[0].astype(BF16), ffn_w_up[0].astype(BF16), ffn_w_down[0].astype(BF16),
                *ln(0, 1))

    order = jnp.array(HEAD_ORDER)
    w_q = attn_w_q[0].reshape(D_MODEL, N_HEADS, HEAD_DIM)[:, order].reshape(D_MODEL, Q_DIM)
    w_out = attn_w_out[0].reshape(N_HEADS, HEAD_DIM, D_MODEL)[order].reshape(Q_DIM, D_MODEL)
    w_qkv = jnp.concatenate([w_q, kv_w], axis=1).astype(BF16)
    q, k, v = _qkv(h, w_qkv, cos_t, sin_t)
    h = _attention_ln(q, k, v, attn_sinks[0].astype(F32), h, w_out.astype(BF16), *ln(1, 0))
    w_router = jnp.pad(moe_w_router[0].astype(F32), ((0, 0), (0, LANES - N_EXPERTS)))
    b_router = jnp.pad(moe_b_router[0].astype(F32), (0, LANES - N_EXPERTS)).reshape(1, LANES)
    h = _moe_ln(h, w_router, b_router, moe_w_gate[0].astype(BF16), moe_w_up[0].astype(BF16),
                moe_w_down[0].astype(BF16), *ln(1, 1))
    return h.reshape(BATCH, SEQ, D_MODEL)
```

```python
import functools
import math

import jax
import jax.numpy as jnp
from jax import lax
from jax.experimental import pallas as pl
from jax.experimental.pallas import tpu as pltpu

F32 = jnp.float32
BF16 = jnp.bfloat16

D_MODEL = 1024
BATCH = 16
SEQ = 2048
N_TOK = BATCH * SEQ
DEPTH = 2

SSM_GROUP_CH = 16
SSM_GROUPS = D_MODEL // SSM_GROUP_CH
SSM_STATE = 64
SSM_CHUNK = 16
SSM_NCHUNK = SEQ // SSM_CHUNK
SSM_ROWS = SSM_NCHUNK * BATCH
SSM_WIDTH = SSM_CHUNK * SSM_GROUP_CH
SSM_SCAN_STEPS = int(math.log2(SSM_NCHUNK))

N_HEADS = 16
HEAD_DIM = 64
N_KV_HEADS = 4
Q_PER_KV = N_HEADS // N_KV_HEADS
KV_DIM = N_KV_HEADS * HEAD_DIM
WINDOW = 128
BLOCK = 128
ROT_DIM = HEAD_DIM // 4
ROT_HALF = ROT_DIM // 2
ROPE_THETA = 500000.0

FFN_DIM = 2816
N_EXPERTS = 8
EXPERT_DIM = 1024

DEEPNORM_ALPHA = (2 * DEPTH) ** 0.25
LN_EPS = 1e-5

LANES = 128
VMEM_LIMIT = 56 * 1024 * 1024


def _cparams(*sem):
    return pltpu.CompilerParams(dimension_semantics=sem, vmem_limit_bytes=VMEM_LIMIT)


def _layer_norm(r, g, b):
    mu = jnp.mean(r, axis=-1, keepdims=True)
    xc = r - mu
    var = jnp.mean(xc * xc, axis=-1, keepdims=True)
    return xc * lax.rsqrt(var + LN_EPS) * g + b


def _rope_kernel(invf_ref, pos_ref, cos_ref, sin_ref):
    pos = pos_ref[...]
    for f in range(ROT_HALF):
        ang = pos * invf_ref[f]
        cos_ref[f] = jnp.cos(ang)
        sin_ref[f] = jnp.sin(ang)


def _rope_tables(positions):
    inv_freq = ROPE_THETA ** (-jnp.arange(0, ROT_DIM, 2, dtype=F32) / ROT_DIM)
    pos = positions.astype(F32)
    cos8, sin8 = pl.pallas_call(
        _rope_kernel,
        out_shape=[jax.ShapeDtypeStruct((ROT_HALF, BATCH, SEQ), F32)] * 2,
        in_specs=[pl.BlockSpec(memory_space=pltpu.SMEM),
                  pl.BlockSpec(memory_space=pltpu.VMEM)],
        out_specs=[pl.BlockSpec(memory_space=pltpu.VMEM)] * 2,
        name="rope_tables",
    )(inv_freq, pos)
    cos8 = cos8.transpose(1, 2, 0).reshape(N_TOK, ROT_HALF)
    sin8 = sin8.transpose(1, 2, 0).reshape(N_TOK, ROT_HALF)
    rest = HEAD_DIM - ROT_DIM
    cos_head = jnp.concatenate([cos8, cos8, jnp.ones((N_TOK, rest), F32)], axis=1)
    sin_head = jnp.concatenate([-sin8, sin8, jnp.zeros((N_TOK, rest), F32)], axis=1)
    reps = LANES // HEAD_DIM
    return jnp.tile(cos_head, (1, reps)), jnp.tile(sin_head, (1, reps))


def _ssm_params(lam_re, lam_im, log_step, b_re, b_im, c_re, c_im):
    hp = lax.Precision.HIGHEST
    lr, li = lam_re.astype(F32), lam_im.astype(F32)
    dt = jnp.exp(log_step.astype(F32))[:, None]
    mag = jnp.exp(lr * dt)
    ar = mag * jnp.cos(li * dt)
    ai = mag * jnp.sin(li * dt)
    nr = ar - 1.0
    den = lr * lr + li * li
    kr = (nr * lr + ai * li) / den
    ki = (ai * lr - nr * li) / den
    br, bi = b_re.astype(F32), b_im.astype(F32)
    bbar_r = kr[..., None] * br - ki[..., None] * bi
    bbar_i = kr[..., None] * bi + ki[..., None] * br
    cr, ci = c_re.astype(F32), c_im.astype(F32)

    def powers(taus):
        t = taus.astype(F32)[:, None, None]
        m = jnp.exp(lr[None] * dt[None] * t)
        ang = li[None] * dt[None] * t
        return m * jnp.cos(ang), m * jnp.sin(ang)

    er, ei = powers(jnp.arange(SSM_CHUNK + 1))
    w_r = er[:, :, :, None] * bbar_r[None] - ei[:, :, :, None] * bbar_i[None]
    w_i = er[:, :, :, None] * bbar_i[None] + ei[:, :, :, None] * bbar_r[None]
    kern = (jnp.einsum('gcp,tgpd->gtcd', cr, w_r[:SSM_CHUNK], precision=hp)
            - jnp.einsum('gcp,tgpd->gtcd', ci, w_i[:SSM_CHUNK], precision=hp))
    s_idx = jnp.arange(SSM_CHUNK)[:, None, None]
    t_idx = jnp.arange(SSM_CHUNK)[None, :, None]
    lag = (t_idx - s_idx == jnp.arange(SSM_CHUNK)[None, None, :]).astype(F32)
    toep = jnp.einsum('stu,gucd->gtcsd', lag, kern, precision=hp)
    mt = toep.reshape(SSM_GROUPS, SSM_WIDTH, SSM_WIDTH)
    rev = SSM_CHUNK - 1 - jnp.arange(SSM_CHUNK)
    g_r = w_r[rev].transpose(1, 2, 0, 3).reshape(SSM_GROUPS, SSM_STATE, SSM_WIDTH)
    g_i = w_i[rev].transpose(1, 2, 0, 3).reshape(SSM_GROUPS, SSM_STATE, SSM_WIDTH)
    gt = jnp.concatenate([g_r, g_i], axis=1)
    e1r, e1i = er[1:], ei[1:]
    ce_r = cr[None] * e1r[:, :, None, :] - ci[None] * e1i[:, :, None, :]
    ce_i = cr[None] * e1i[:, :, None, :] + ci[None] * e1r[:, :, None, :]
    c_re = ce_r.transpose(1, 0, 2, 3).reshape(SSM_GROUPS, SSM_WIDTH, SSM_STATE)
    c_im = (-ce_i).transpose(1, 0, 2, 3).reshape(SSM_GROUPS, SSM_WIDTH, SSM_STATE)
    ct = jnp.concatenate([c_re, c_im], axis=-1)
    sr, si = powers(SSM_CHUNK * (2 ** jnp.arange(8)))
    return (mt.astype(BF16), gt.astype(BF16), ct.astype(BF16),
            sr.transpose(1, 2, 0), si.transpose(1, 2, 0))


SSM_LANE_GROUPS = LANES // SSM_GROUP_CH
SSM_LANE_CHUNKS = D_MODEL // LANES
SSM_BATCH_TILE = 4
SSM_COLS = SSM_BATCH_TILE * SSM_NCHUNK


def _ssm_kernel(x_ref, mt_ref, gt_ref, ct_ref, cr_ref, ci_ref, y_ref, v_ref, yt_ref):
    for bl in range(SSM_BATCH_TILE):
        cols = slice(bl * SSM_NCHUNK, (bl + 1) * SSM_NCHUNK)
        for s in range(SSM_CHUNK):
            a = x_ref[pl.ds(bl * SEQ + s, SSM_NCHUNK, stride=SSM_CHUNK), :]
            at = a.T.astype(BF16)
            for g in range(SSM_LANE_GROUPS):
                v_ref[g, s * SSM_GROUP_CH:(s + 1) * SSM_GROUP_CH, cols] = (
                    at[g * SSM_GROUP_CH:(g + 1) * SSM_GROUP_CH])

    lane_j = lax.broadcasted_iota(jnp.int32, (1, SSM_NCHUNK), 1)

    def group_body(g, carry):
        ut = v_ref[g]
        xt = jnp.dot(gt_ref[g], ut, preferred_element_type=F32)
        cr_all = cr_ref[g]
        ci_all = ci_ref[g]
        xr = [xt[0:SSM_STATE, b * SSM_NCHUNK:(b + 1) * SSM_NCHUNK] for b in range(SSM_BATCH_TILE)]
        xi = [xt[SSM_STATE:, b * SSM_NCHUNK:(b + 1) * SSM_NCHUNK] for b in range(SSM_BATCH_TILE)]
        for k in range(SSM_SCAN_STEPS):
            d = 1 << k
            keep = lane_j >= d
            cr = jnp.where(keep, cr_all[:, k:k + 1], 0.0)
            ci = jnp.where(keep, ci_all[:, k:k + 1], 0.0)
            for b in range(SSM_BATCH_TILE):
                rr = pltpu.roll(xr[b], d, 1)
                ri = pltpu.roll(xi[b], d, 1)
                xr[b], xi[b] = xr[b] + cr * rr - ci * ri, xi[b] + cr * ri + ci * rr
        keep = lane_j >= 1
        sr = jnp.concatenate([jnp.where(keep, pltpu.roll(t, 1, 1), 0.0) for t in xr], axis=1)
        si = jnp.concatenate([jnp.where(keep, pltpu.roll(t, 1, 1), 0.0) for t in xi], axis=1)
        sprev = jnp.concatenate([sr, si], axis=0).astype(BF16)
        yt_ref[g] = (jnp.dot(mt_ref[g], ut, preferred_element_type=F32)
                     + jnp.dot(ct_ref[g], sprev, preferred_element_type=F32))
        return carry

    lax.fori_loop(0, SSM_LANE_GROUPS, group_body, 0)

    for bl in range(SSM_BATCH_TILE):
        cols = slice(bl * SSM_NCHUNK, (bl + 1) * SSM_NCHUNK)
        for t in range(SSM_CHUNK):
            tile = jnp.concatenate(
                [yt_ref[g, t * SSM_GROUP_CH:(t + 1) * SSM_GROUP_CH, cols] for g in range(SSM_LANE_GROUPS)],
                axis=0)
            y_ref[pl.ds(bl * SEQ + t, SSM_NCHUNK, stride=SSM_CHUNK), :] = tile.T


def _ssm_mixer(x2, params):
    mt, gt, ct, coef_r, coef_i = params
    rows = SSM_BATCH_TILE * SEQ
    pspec = lambda a, b: pl.BlockSpec((SSM_LANE_GROUPS, a, b), lambda k, q: (k, 0, 0))
    return pl.pallas_call(
        _ssm_kernel,
        out_shape=jax.ShapeDtypeStruct((N_TOK, D_MODEL), F32),
        grid=(SSM_LANE_CHUNKS, BATCH // SSM_BATCH_TILE),
        in_specs=[pl.BlockSpec((rows, LANES), lambda k, q: (q, k)),
                  pspec(SSM_WIDTH, SSM_WIDTH), pspec(2 * SSM_STATE, SSM_WIDTH), pspec(SSM_WIDTH, 2 * SSM_STATE),
                  pspec(SSM_STATE, 8), pspec(SSM_STATE, 8)],
        out_specs=pl.BlockSpec((rows, LANES), lambda k, q: (q, k)),
        scratch_shapes=[pltpu.VMEM((SSM_LANE_GROUPS, SSM_WIDTH, SSM_COLS), BF16),
                        pltpu.VMEM((SSM_LANE_GROUPS, SSM_WIDTH, SSM_COLS), F32)],
        compiler_params=_cparams("parallel", "parallel"),
        name="ssm_mixer",
    )(x2, mt, gt, ct, coef_r, coef_i)


def _glu_ln_kernel(y_ref, x_ref, d_ref, w_ref, lg_ref, lb_ref, o_ref):
    x = x_ref[...]
    act = jax.nn.gelu(y_ref[...] + d_ref[...] * x).astype(BF16)
    z = jnp.dot(act, w_ref[...], preferred_element_type=F32)
    mix = z[:, :D_MODEL] * jax.nn.sigmoid(z[:, D_MODEL:])
    o_ref[...] = _layer_norm(DEEPNORM_ALPHA * x + mix, lg_ref[...], lb_ref[...])


def _row_spec(tm, width):
    return pl.BlockSpec((tm, width), lambda i: (i, 0))


def _const_spec(*shape):
    return pl.BlockSpec(shape, lambda *_: (0,) * len(shape))


def _glu_ln(y, x2, d_skip, w_glu, lg, lb, tm=512):
    return pl.pallas_call(
        _glu_ln_kernel,
        out_shape=jax.ShapeDtypeStruct((N_TOK, D_MODEL), F32),
        grid=(N_TOK // tm,),
        in_specs=[_row_spec(tm, D_MODEL), _row_spec(tm, D_MODEL), _const_spec(1, D_MODEL),
                  _const_spec(D_MODEL, 2 * D_MODEL), _const_spec(1, D_MODEL), _const_spec(1, D_MODEL)],
        out_specs=_row_spec(tm, D_MODEL),
        compiler_params=_cparams("parallel"),
        name="glu_ln",
    )(y, x2, d_skip, w_glu, lg, lb)


FFN_SPLIT = 2
FFN_TILE = FFN_DIM // FFN_SPLIT


def _ffn_ln_kernel(h_ref, wg_ref, wu_ref, wd_ref, lg_ref, lb_ref, o_ref, acc_ref):
    f = pl.program_id(1)
    hb = h_ref[...].astype(BF16)
    gate = jnp.dot(hb, wg_ref[...], preferred_element_type=F32)
    up = jnp.dot(hb, wu_ref[...], preferred_element_type=F32)
    act = (jax.nn.silu(gate) * up).astype(BF16)
    part = jnp.dot(act, wd_ref[...], preferred_element_type=F32)

    @pl.when(f == 0)
    def _():
        acc_ref[...] = part

    @pl.when(f > 0)
    def _():
        acc_ref[...] += part

    @pl.when(f == FFN_SPLIT - 1)
    def _():
        o_ref[...] = _layer_norm(DEEPNORM_ALPHA * h_ref[...] + acc_ref[...], lg_ref[...], lb_ref[...])


def _ffn_ln(h, wg, wu, wd, lg, lb, tm=512):
    return pl.pallas_call(
        _ffn_ln_kernel,
        out_shape=jax.ShapeDtypeStruct((N_TOK, D_MODEL), F32),
        grid=(N_TOK // tm, FFN_SPLIT),
        in_specs=[pl.BlockSpec((tm, D_MODEL), lambda i, f: (i, 0)),
                  pl.BlockSpec((D_MODEL, FFN_TILE), lambda i, f: (0, f)),
                  pl.BlockSpec((D_MODEL, FFN_TILE), lambda i, f: (0, f)),
                  pl.BlockSpec((FFN_TILE, D_MODEL), lambda i, f: (f, 0)),
                  _const_spec(1, D_MODEL), _const_spec(1, D_MODEL)],
        out_specs=pl.BlockSpec((tm, D_MODEL), lambda i, f: (i, 0)),
        scratch_shapes=[pltpu.VMEM((tm, D_MODEL), F32)],
        compiler_params=_cparams("parallel", "arbitrary"),
        name="ffn_ln",
    )(h, wg, wu, wd, lg, lb)


Q_DIM = N_HEADS * HEAD_DIM


def _qkv_kernel(h_ref, w_ref, cos_ref, sin_ref, q_ref, k_ref, v_ref):
    hb = h_ref[...].astype(BF16)
    z = jnp.dot(hb, w_ref[...], preferred_element_type=F32)
    cos_t = cos_ref[...]
    sin_t = sin_ref[...]
    lane = lax.broadcasted_iota(jnp.int32, (1, LANES), 1)
    first_half = (lane % HEAD_DIM) < ROT_HALF

    def rope(t):
        partner = jnp.where(first_half, pltpu.roll(t, LANES - ROT_HALF, 1), pltpu.roll(t, ROT_HALF, 1))
        return t * cos_t + partner * sin_t

    scale = HEAD_DIM ** -0.5
    for c in range(Q_DIM // LANES):
        sl = slice(c * LANES, (c + 1) * LANES)
        q_ref[:, sl] = (rope(z[:, sl]) * scale).astype(BF16)
    for c in range(KV_DIM // LANES):
        sl = slice(c * LANES, (c + 1) * LANES)
        k_ref[:, sl] = rope(z[:, Q_DIM + c * LANES:Q_DIM + (c + 1) * LANES]).astype(BF16)
    v_ref[...] = z[:, Q_DIM + KV_DIM:].astype(BF16)


def _qkv(h, w_qkv, cos_t, sin_t, tm=512):
    return pl.pallas_call(
        _qkv_kernel,
        out_shape=[jax.ShapeDtypeStruct((N_TOK, Q_DIM), BF16),
                   jax.ShapeDtypeStruct((N_TOK, KV_DIM), BF16),
                   jax.ShapeDtypeStruct((N_TOK, KV_DIM), BF16)],
        grid=(N_TOK // tm,),
        in_specs=[_row_spec(tm, D_MODEL), _const_spec(D_MODEL, Q_DIM + 2 * KV_DIM),
                  _row_spec(tm, LANES), _row_spec(tm, LANES)],
        out_specs=[_row_spec(tm, Q_DIM), _row_spec(tm, KV_DIM), _row_spec(tm, KV_DIM)],
        compiler_params=_cparams("parallel"),
        name="qkv_rope",
    )(h, w_qkv, cos_t, sin_t)


N_BLOCKS = SEQ // BLOCK


ATTN_Q_TILE = 512
ATTN_SUB = ATTN_Q_TILE // BLOCK
ATTN_BAND = ATTN_Q_TILE + BLOCK
KV_PAIRS = N_KV_HEADS // 2

HEAD_ORDER = [h for c in range(KV_PAIRS) for g in range(Q_PER_KV)
              for h in (2 * c * Q_PER_KV + g, (2 * c + 1) * Q_PER_KV + g)]


def _band_bias():
    qi = jnp.arange(Q_PER_KV * BLOCK)[:, None] % BLOCK
    si = jnp.arange(2 * BLOCK)[None, :]
    rel = qi + BLOCK - si
    valid = (rel >= 0) & (rel < WINDOW)
    first = valid & (si >= BLOCK)
    neg = jnp.float32(-jnp.inf)
    return jnp.stack([jnp.where(valid, 0.0, neg), jnp.where(first, 0.0, neg)]).astype(F32)


def _attn_kernel(sink_ref, q_ref, kp_ref, kc_ref, vp_ref, vc_ref, bias_ref, h_ref, w_ref, lg_ref, lb_ref,
                 out_ref, ka_ref, kb_ref, va_ref, vb_ref, o_ref):
    i = pl.program_id(1)
    lane = lax.broadcasted_iota(jnp.int32, (1, KV_DIM), 1)
    low = (lane % LANES) < HEAD_DIM
    zero = jnp.zeros((), BF16)
    for src, dst_a, dst_b in ((kp_ref, ka_ref, kb_ref), (vp_ref, va_ref, vb_ref)):
        t = src[...]
        dst_a[0:BLOCK, :] = jnp.where(low, t, zero)
        dst_b[0:BLOCK, :] = jnp.where(low, zero, t)
    for src, dst_a, dst_b in ((kc_ref, ka_ref, kb_ref), (vc_ref, va_ref, vb_ref)):
        t = src[...]
        dst_a[BLOCK:ATTN_BAND, :] = jnp.where(low, t, zero)
        dst_b[BLOCK:ATTN_BAND, :] = jnp.where(low, zero, t)

    lane1 = lax.broadcasted_iota(jnp.int32, (1, LANES), 1)
    row = lax.broadcasted_iota(jnp.int32, (Q_PER_KV * BLOCK, 1), 0)
    contract_last = (((1,), (1,)), ((), ()))
    for blk in range(ATTN_SUB):
        r0 = blk * BLOCK
        if blk == 0:
            bias = bias_ref[jnp.where(i == 0, 1, 0)]
        else:
            bias = bias_ref[0]
        for c in range(KV_PAIRS):
            cs = slice(c * LANES, (c + 1) * LANES)
            chunks = [c * Q_PER_KV + g for g in range(Q_PER_KV)]
            q4 = jnp.concatenate([q_ref[r0:r0 + BLOCK, m * LANES:(m + 1) * LANES] for m in chunks], axis=0)
            outs = []
            for half, (k_ref, v_ref) in enumerate(((ka_ref, va_ref), (kb_ref, vb_ref))):
                kband = k_ref[r0:r0 + 2 * BLOCK, cs]
                s = lax.dot_general(q4, kband, contract_last, preferred_element_type=F32) + bias
                sink = jnp.zeros((Q_PER_KV * BLOCK, 1), F32)
                for g, m in enumerate(chunks):
                    sink = jnp.where(row // BLOCK == g, sink_ref[HEAD_ORDER[2 * m + half]], sink)
                mx = jnp.maximum(jnp.max(s, axis=-1, keepdims=True), sink)
                p = jnp.exp(s - mx)
                denom = jnp.sum(p, axis=-1, keepdims=True) + jnp.exp(sink - mx)
                pv = jnp.dot(p.astype(BF16), v_ref[r0:r0 + 2 * BLOCK, cs], preferred_element_type=F32)
                outs.append((pv, 1.0 / denom))
            (pv_a, r_a), (pv_b, r_b) = outs
            o = (pv_a + pv_b) * jnp.where(lane1 < HEAD_DIM, r_a, r_b)
            for g, m in enumerate(chunks):
                o_ref[r0:r0 + BLOCK, m * LANES:(m + 1) * LANES] = o[g * BLOCK:(g + 1) * BLOCK].astype(BF16)

    mix = jnp.dot(o_ref[...], w_ref[...], preferred_element_type=F32)
    out_ref[...] = _layer_norm(DEEPNORM_ALPHA * h_ref[...] + mix, lg_ref[...], lb_ref[...])


def _attention_ln(q, k, v, sinks, h, w_out, lg, lb):
    tiles = SEQ // ATTN_Q_TILE
    cur = lambda b, i: (b * tiles + i, 0)
    prev = lambda b, i: (jnp.maximum((b * tiles + i) * ATTN_SUB - 1, 0), 0)
    const2 = lambda b, i: (0, 0)
    return pl.pallas_call(
        _attn_kernel,
        out_shape=jax.ShapeDtypeStruct((N_TOK, D_MODEL), F32),
        grid=(BATCH, tiles),
        in_specs=[pl.BlockSpec(memory_space=pltpu.SMEM),
                  pl.BlockSpec((ATTN_Q_TILE, Q_DIM), cur),
                  pl.BlockSpec((BLOCK, KV_DIM), prev), pl.BlockSpec((ATTN_Q_TILE, KV_DIM), cur),
                  pl.BlockSpec((BLOCK, KV_DIM), prev), pl.BlockSpec((ATTN_Q_TILE, KV_DIM), cur),
                  pl.BlockSpec((2, Q_PER_KV * BLOCK, 2 * BLOCK), lambda b, i: (0, 0, 0)),
                  pl.BlockSpec((ATTN_Q_TILE, D_MODEL), cur),
                  pl.BlockSpec((Q_DIM, D_MODEL), const2),
                  pl.BlockSpec((1, D_MODEL), const2), pl.BlockSpec((1, D_MODEL), const2)],
        out_specs=pl.BlockSpec((ATTN_Q_TILE, D_MODEL), cur),
        scratch_shapes=[pltpu.VMEM((ATTN_BAND, KV_DIM), BF16)] * 4
                       + [pltpu.VMEM((ATTN_Q_TILE, Q_DIM), BF16)],
        compiler_params=_cparams("parallel", "parallel"),
        name="swa_attention_ln",
    )(sinks, q, k, k, v, v, _band_bias(), h, w_out, lg, lb)


ROUTE_TILE = 512
EXPERT_TILE = 512
SLAB = D_MODEL // LANES
N_SORTED = 2 * N_TOK + N_EXPERTS * EXPERT_TILE
N_EXPERT_TILES = N_SORTED // EXPERT_TILE
META_E1, META_E2, META_R1, META_R2, META_W1, META_W2 = range(6)


def _to_slabs(ref, val, rows):
    for c in range(SLAB):
        ref[pl.ds(c, rows, stride=SLAB), :] = val[:, c * LANES:(c + 1) * LANES]


def _from_slabs(ref, rows):
    return jnp.concatenate([ref[pl.ds(c, rows, stride=SLAB), :] for c in range(SLAB)], axis=1)


def _router_kernel(h_ref, wr_ref, br_ref, tri_ref, meta_ref, hs_ref, counts_ref, carry_ref):
    i = pl.program_id(0)

    @pl.when(i == 0)
    def _():
        carry_ref[...] = jnp.zeros_like(carry_ref)

    lane = lax.broadcasted_iota(jnp.int32, (1, LANES), 1)
    lane_f = lane.astype(F32)
    h = h_ref[...]
    logits = jnp.dot(h, wr_ref[...], preferred_element_type=F32,
                     precision=lax.Precision.HIGHEST) + br_ref[...]
    logits = jnp.where(lane < N_EXPERTS, logits, -jnp.inf)
    m1 = jnp.max(logits, axis=-1, keepdims=True)
    i1 = jnp.min(jnp.where(logits == m1, lane_f, float(LANES)), axis=-1, keepdims=True)
    rest = jnp.where(lane_f == i1, -jnp.inf, logits)
    m2 = jnp.max(rest, axis=-1, keepdims=True)
    i2 = jnp.min(jnp.where(rest == m2, lane_f, float(LANES)), axis=-1, keepdims=True)
    e2 = jnp.exp(m2 - m1)
    tot = 1.0 + e2
    chosen = jnp.where((lane_f == i1) | (lane_f == i2), 1.0, 0.0)
    before = jnp.dot(tri_ref[...], chosen.astype(BF16), preferred_element_type=F32) + carry_ref[...]
    r1 = jnp.sum(jnp.where(lane_f == i1, before, 0.0), axis=-1, keepdims=True)
    r2 = jnp.sum(jnp.where(lane_f == i2, before, 0.0), axis=-1, keepdims=True)
    carry_ref[...] += jnp.sum(chosen, axis=0, keepdims=True)
    counts_ref[...] = carry_ref[...]
    record = jnp.zeros((ROUTE_TILE, LANES), F32)
    for slot, col in ((META_E1, i1), (META_E2, i2), (META_R1, r1), (META_R2, r2),
                      (META_W1, 1.0 / tot), (META_W2, e2 / tot)):
        record = jnp.where(lane == slot, col, record)
    meta_ref[...] = record
    _to_slabs(hs_ref, h, ROUTE_TILE)


def _router(h, w_router, b_router):
    tri = (jnp.arange(ROUTE_TILE)[:, None] > jnp.arange(ROUTE_TILE)[None, :]).astype(BF16)
    return pl.pallas_call(
        _router_kernel,
        out_shape=[jax.ShapeDtypeStruct((N_TOK, LANES), F32),
                   jax.ShapeDtypeStruct((N_TOK * SLAB, LANES), F32),
                   jax.ShapeDtypeStruct((1, LANES), F32)],
        grid=(N_TOK // ROUTE_TILE,),
        in_specs=[_row_spec(ROUTE_TILE, D_MODEL), _const_spec(D_MODEL, LANES), _const_spec(1, LANES),
                  _const_spec(ROUTE_TILE, ROUTE_TILE)],
        out_specs=[_row_spec(ROUTE_TILE, LANES), _row_spec(ROUTE_TILE * SLAB, LANES), _const_spec(1, LANES)],
        scratch_shapes=[pltpu.VMEM((1, LANES), F32)],
        compiler_params=_cparams("arbitrary"),
        name="moe_router",
    )(h, w_router, b_router, tri)


def _row_slab(ref, row):
    return ref.at[pl.ds(pl.multiple_of(row * SLAB, SLAB), SLAB)]


def _experts_kernel(src_ref, texp_ref, nused_ref, hs_hbm, wg_ref, wu_ref, wd_ref, ys_ref,
                    xa_ref, xb_ref, sem):
    i = pl.program_id(0)
    n_used = nused_ref[0]
    bufs = (xa_ref, xb_ref)

    def start_gather(tile, slot):
        def body(r, carry):
            tok = src_ref[tile * EXPERT_TILE + r]
            pltpu.make_async_copy(_row_slab(hs_hbm, tok), _row_slab(bufs[slot], r), sem.at[slot]).start()
            return carry
        lax.fori_loop(0, EXPERT_TILE, body, 0, unroll=8)

    def wait_gather(slot):
        pltpu.make_async_copy(hs_hbm.at[pl.ds(0, EXPERT_TILE * SLAB)], bufs[slot], sem.at[slot]).wait()

    def compute(slot):
        wait_gather(slot)
        x = _from_slabs(bufs[slot], EXPERT_TILE).astype(BF16)
        gate = jnp.dot(x, wg_ref[0], preferred_element_type=F32)
        up = jnp.dot(x, wu_ref[0], preferred_element_type=F32)
        act = (jax.nn.silu(gate) * up).astype(BF16)
        _to_slabs(ys_ref, jnp.dot(act, wd_ref[0], preferred_element_type=F32), EXPERT_TILE)

    @pl.when(i == 0)
    def _():
        start_gather(0, 0)

    for slot in range(2):
        @pl.when((i + 1 < n_used) & ((i + 1) % 2 == slot))
        def _():
            start_gather(i + 1, slot)

        @pl.when((i < n_used) & (i % 2 == slot))
        def _():
            compute(slot)

    @pl.when(i >= n_used)
    def _():
        ys_ref[...] = jnp.zeros_like(ys_ref)


def _experts(src, tile_expert, n_used, hs, wg, wu, wd):
    wspec = lambda a, b: pl.BlockSpec((1, a, b), lambda i, src, texp, nused: (texp[i], 0, 0))
    return pl.pallas_call(
        _experts_kernel,
        out_shape=jax.ShapeDtypeStruct((N_SORTED * SLAB, LANES), F32),
        grid_spec=pltpu.PrefetchScalarGridSpec(
            num_scalar_prefetch=3,
            grid=(N_EXPERT_TILES,),
            in_specs=[pl.BlockSpec(memory_space=pl.ANY),
                      wspec(D_MODEL, EXPERT_DIM), wspec(D_MODEL, EXPERT_DIM), wspec(EXPERT_DIM, D_MODEL)],
            out_specs=pl.BlockSpec((EXPERT_TILE * SLAB, LANES), lambda i, src, texp, nused: (i, 0)),
            scratch_shapes=[pltpu.VMEM((EXPERT_TILE * SLAB, LANES), F32)] * 2
                           + [pltpu.SemaphoreType.DMA((2,))]),
        compiler_params=_cparams("arbitrary"),
        name="moe_experts",
    )(src, tile_expert, n_used, hs, wg, wu, wd)


def _combine_ln_kernel(dest_ref, meta_ref, h_ref, ys_hbm, lg_ref, lb_ref, o_ref, y1_ref, y2_ref, sem):
    base = pl.program_id(0) * ROUTE_TILE

    def body(t, carry):
        for slot, buf in enumerate((y1_ref, y2_ref)):
            row = dest_ref[slot * N_TOK + base + t]
            pltpu.make_async_copy(_row_slab(ys_hbm, row), _row_slab(buf, t), sem.at[slot]).start()
        return carry
    lax.fori_loop(0, ROUTE_TILE, body, 0, unroll=8)

    lane = lax.broadcasted_iota(jnp.int32, (1, LANES), 1)
    meta = meta_ref[...]
    w1 = jnp.sum(jnp.where(lane == META_W1, meta, 0.0), axis=-1, keepdims=True)
    w2 = jnp.sum(jnp.where(lane == META_W2, meta, 0.0), axis=-1, keepdims=True)
    for slot, buf in enumerate((y1_ref, y2_ref)):
        pltpu.make_async_copy(ys_hbm.at[pl.ds(0, ROUTE_TILE * SLAB)], buf, sem.at[slot]).wait()
    ff = w1 * _from_slabs(y1_ref, ROUTE_TILE) + w2 * _from_slabs(y2_ref, ROUTE_TILE)
    o_ref[...] = _layer_norm(DEEPNORM_ALPHA * h_ref[...] + ff, lg_ref[...], lb_ref[...])


def _combine_ln(dest, meta, h, ys, lg, lb):
    row = lambda width: pl.BlockSpec((ROUTE_TILE, width), lambda i, dest: (i, 0))
    const = pl.BlockSpec((1, D_MODEL), lambda i, dest: (0, 0))
    return pl.pallas_call(
        _combine_ln_kernel,
        out_shape=jax.ShapeDtypeStruct((N_TOK, D_MODEL), F32),
        grid_spec=pltpu.PrefetchScalarGridSpec(
            num_scalar_prefetch=1,
            grid=(N_TOK // ROUTE_TILE,),
            in_specs=[row(LANES), row(D_MODEL), pl.BlockSpec(memory_space=pl.ANY), const, const],
            out_specs=row(D_MODEL),
            scratch_shapes=[pltpu.VMEM((ROUTE_TILE * SLAB, LANES), F32)] * 2
                           + [pltpu.SemaphoreType.DMA((2,))]),
        compiler_params=_cparams("arbitrary"),
        name="moe_combine_ln",
    )(dest, meta, h, ys, lg, lb)


def _moe_ln(h, w_router, b_router, wg, wu, wd, lg, lb):
    meta, hs, counts = _router(h, w_router, b_router)
    counts = counts[0, :N_EXPERTS].astype(jnp.int32)
    sizes = (counts + EXPERT_TILE - 1) // EXPERT_TILE * EXPERT_TILE
    ends = jnp.cumsum(sizes)
    starts = ends - sizes
    ids = meta[:, :META_W1].astype(jnp.int32)
    dest = jnp.concatenate([starts[ids[:, META_E1]] + ids[:, META_R1],
                            starts[ids[:, META_E2]] + ids[:, META_R2]])
    token = jnp.tile(jnp.arange(N_TOK, dtype=jnp.int32), 2)
    src = jnp.zeros((N_SORTED,), jnp.int32).at[dest].set(token)
    tile_ends = ends // EXPERT_TILE
    tile_expert = jnp.minimum(jnp.searchsorted(tile_ends, jnp.arange(N_EXPERT_TILES), side='right'),
                              N_EXPERTS - 1).astype(jnp.int32)
    ys = _experts(src, tile_expert, tile_ends[-1:].astype(jnp.int32), hs, wg, wu, wd)
    return _combine_ln(dest, meta, h, ys, lg, lb)


def kernel(x, positions, ln_g, ln_b, ssm_lambda_re, ssm_lambda_im, ssm_log_step, ssm_b_re, ssm_b_im, ssm_c_re, ssm_c_im, ssm_d, ssm_w_glu, kv_w, attn_w_q, attn_sinks, attn_w_out, ffn_w_gate, ffn_w_up, ffn_w_down, moe_w_router, moe_b_router, moe_w_gate, moe_w_up, moe_w_down):
    ln = lambda layer, j: (ln_g[layer, j].reshape(1, D_MODEL).astype(F32),
                           ln_b[layer, j].reshape(1, D_MODEL).astype(F32))
    cos_t, sin_t = _rope_tables(positions)
    x2 = x.reshape(N_TOK, D_MODEL)

    params = _ssm_params(ssm_lambda_re[0], ssm_lambda_im[0], ssm_log_step[0], ssm_b_re[0], ssm_b_im[0],
                         ssm_c_re[0], ssm_c_im[0])
    y = _ssm_mixer(x2, params)
    h = _glu_ln(y, x2, ssm_d[0].astype(F32).reshape(1, D_MODEL), ssm_w_glu[0].astype(BF16), *ln(0, 0))
    h = _ffn_ln(h, ffn_w_gate[0].astype(BF16), ffn_w_up[0].astype(BF16), ffn_w_down[0].astype(BF16),
                *ln(0, 1))

    order = jnp.array(HEAD_ORDER)
    w_q = attn_w_q[0].reshape(D_MODEL, N_HEADS, HEAD_DIM)[:, order].reshape(D_MODEL, Q_DIM)
    w_out = attn_w_out[0].reshape(N_HEADS, HEAD_DIM, D_MODEL)[order].reshape(Q_DIM, D_MODEL)
    w_qkv = jnp.concatenate([w_q, kv_w], axis=1).astype(BF16)
    q, k, v = _qkv(h, w_qkv, cos_t, sin_t)
    h = _attention_ln(q, k, v, attn_sinks[0].astype(F32), h, w_out.astype(BF16), *ln(1, 0))
    w_router = jnp.pad(moe_w_router[0].astype(F32), ((0, 0), (0, LANES - N_EXPERTS)))
    b_router = jnp.pad(moe_b_router[0].astype(F32), (0, LANES - N_EXPERTS)).reshape(1, LANES)
    h = _moe_ln(h, w_router, b_router, moe_w_gate[0].astype(BF16), moe_w_up[0].astype(BF16),
                moe_w_down[0].astype(BF16), *ln(1, 1))
    return h.reshape(BATCH, SEQ, D_MODEL)
```

```python
import functools
import math

import jax
import jax.numpy as jnp
from jax import lax
from jax.experimental import pallas as pl
from jax.experimental.pallas import tpu as pltpu

F32 = jnp.float32
BF16 = jnp.bfloat16

D_MODEL = 1024
BATCH = 16
SEQ = 2048
N_TOK = BATCH * SEQ
DEPTH = 2

SSM_GROUP_CH = 16
SSM_GROUPS = D_MODEL // SSM_GROUP_CH
SSM_STATE = 64
SSM_CHUNK = 16
SSM_NCHUNK = SEQ // SSM_CHUNK
SSM_ROWS = SSM_NCHUNK * BATCH
SSM_WIDTH = SSM_CHUNK * SSM_GROUP_CH
SSM_SCAN_STEPS = int(math.log2(SSM_NCHUNK))

N_HEADS = 16
HEAD_DIM = 64
N_KV_HEADS = 4
Q_PER_KV = N_HEADS // N_KV_HEADS
KV_DIM = N_KV_HEADS * HEAD_DIM
WINDOW = 128
BLOCK = 128
ROT_DIM = HEAD_DIM // 4
ROT_HALF = ROT_DIM // 2
ROPE_THETA = 500000.0

FFN_DIM = 2816
N_EXPERTS = 8
EXPERT_DIM = 1024

DEEPNORM_ALPHA = (2 * DEPTH) ** 0.25
LN_EPS = 1e-5

LANES = 128
VMEM_LIMIT = 56 * 1024 * 1024


def _cparams(*sem):
    return pltpu.CompilerParams(dimension_semantics=sem, vmem_limit_bytes=VMEM_LIMIT)


def _layer_norm(r, g, b):
    mu = jnp.mean(r, axis=-1, keepdims=True)
    xc = r - mu
    var = jnp.mean(xc * xc, axis=-1, keepdims=True)
    return xc * lax.rsqrt(var + LN_EPS) * g + b


def _rope_kernel(invf_ref, pos_ref, cos_ref, sin_ref):
    pos = pos_ref[...]
    for f in range(ROT_HALF):
        ang = pos * invf_ref[f]
        cos_ref[f] = jnp.cos(ang)
        sin_ref[f] = jnp.sin(ang)


def _rope_tables(positions):
    inv_freq = ROPE_THETA ** (-jnp.arange(0, ROT_DIM, 2, dtype=F32) / ROT_DIM)
    pos = positions.astype(F32)
    cos8, sin8 = pl.pallas_call(
        _rope_kernel,
        out_shape=[jax.ShapeDtypeStruct((ROT_HALF, BATCH, SEQ), F32)] * 2,
        in_specs=[pl.BlockSpec(memory_space=pltpu.SMEM),
                  pl.BlockSpec(memory_space=pltpu.VMEM)],
        out_specs=[pl.BlockSpec(memory_space=pltpu.VMEM)] * 2,
        name="rope_tables",
    )(inv_freq, pos)
    cos8 = cos8.transpose(1, 2, 0).reshape(N_TOK, ROT_HALF)
    sin8 = sin8.transpose(1, 2, 0).reshape(N_TOK, ROT_HALF)
    rest = HEAD_DIM - ROT_DIM
    cos_head = jnp.concatenate([cos8, cos8, jnp.ones((N_TOK, rest), F32)], axis=1)
    sin_head = jnp.concatenate([-sin8, sin8, jnp.zeros((N_TOK, rest), F32)], axis=1)
    reps = LANES // HEAD_DIM
    return jnp.tile(cos_head, (1, reps)), jnp.tile(sin_head, (1, reps))


def _ssm_params(lam_re, lam_im, log_step, b_re, b_im, c_re, c_im):
    hp = lax.Precision.HIGHEST
    lr, li = lam_re.astype(F32), lam_im.astype(F32)
    dt = jnp.exp(log_step.astype(F32))[:, None]
    mag = jnp.exp(lr * dt)
    ar = mag * jnp.cos(li * dt)
    ai = mag * jnp.sin(li * dt)
    nr = ar - 1.0
    den = lr * lr + li * li
    kr = (nr * lr + ai * li) / den
    ki = (ai * lr - nr * li) / den
    br, bi = b_re.astype(F32), b_im.astype(F32)
    bbar_r = kr[..., None] * br - ki[..., None] * bi
    bbar_i = kr[..., None] * bi + ki[..., None] * br
    cr, ci = c_re.astype(F32), c_im.astype(F32)

    def powers(taus):
        t = taus.astype(F32)[:, None, None]
        m = jnp.exp(lr[None] * dt[None] * t)
        ang = li[None] * dt[None] * t
        return m * jnp.cos(ang), m * jnp.sin(ang)

    er, ei = powers(jnp.arange(SSM_CHUNK + 1))
    w_r = er[:, :, :, None] * bbar_r[None] - ei[:, :, :, None] * bbar_i[None]
    w_i = er[:, :, :, None] * bbar_i[None] + ei[:, :, :, None] * bbar_r[None]
    kern = (jnp.einsum('gcp,tgpd->gtcd', cr, w_r[:SSM_CHUNK], precision=hp)
            - jnp.einsum('gcp,tgpd->gtcd', ci, w_i[:SSM_CHUNK], precision=hp))
    s_idx = jnp.arange(SSM_CHUNK)[:, None, None]
    t_idx = jnp.arange(SSM_CHUNK)[None, :, None]
    lag = (t_idx - s_idx == jnp.arange(SSM_CHUNK)[None, None, :]).astype(F32)
    toep = jnp.einsum('stu,gucd->gtcsd', lag, kern, precision=hp)
    mt = toep.reshape(SSM_GROUPS, SSM_WIDTH, SSM_WIDTH)
    rev = SSM_CHUNK - 1 - jnp.arange(SSM_CHUNK)
    g_r = w_r[rev].transpose(1, 2, 0, 3).reshape(SSM_GROUPS, SSM_STATE, SSM_WIDTH)
    g_i = w_i[rev].transpose(1, 2, 0, 3).reshape(SSM_GROUPS, SSM_STATE, SSM_WIDTH)
    gt = jnp.concatenate([g_r, g_i], axis=1)
    e1r, e1i = er[1:], ei[1:]
    ce_r = cr[None] * e1r[:, :, None, :] - ci[None] * e1i[:, :, None, :]
    ce_i = cr[None] * e1i[:, :, None, :] + ci[None] * e1r[:, :, None, :]
    c_re = ce_r.transpose(1, 0, 2, 3).reshape(SSM_GROUPS, SSM_WIDTH, SSM_STATE)
    c_im = (-ce_i).transpose(1, 0, 2, 3).reshape(SSM_GROUPS, SSM_WIDTH, SSM_STATE)
    ct = jnp.concatenate([c_re, c_im], axis=-1)
    sr, si = powers(SSM_CHUNK * (2 ** jnp.arange(8)))
    return (mt.astype(BF16), gt.astype(BF16), ct.astype(BF16),
            sr.transpose(1, 2, 0), si.transpose(1, 2, 0))


SSM_LANE_GROUPS = LANES // SSM_GROUP_CH
SSM_LANE_CHUNKS = D_MODEL // LANES
SSM_BATCH_TILE = 4
SSM_COLS = SSM_BATCH_TILE * SSM_NCHUNK


def _ssm_kernel(x_ref, mt_ref, gt_ref, ct_ref, cr_ref, ci_ref, y_ref, v_ref, yt_ref):
    for bl in range(SSM_BATCH_TILE):
        cols = slice(bl * SSM_NCHUNK, (bl + 1) * SSM_NCHUNK)
        for s in range(SSM_CHUNK):
            a = x_ref[pl.ds(bl * SEQ + s, SSM_NCHUNK, stride=SSM_CHUNK), :]
            at = a.T.astype(BF16)
            for g in range(SSM_LANE_GROUPS):
                v_ref[g, s * SSM_GROUP_CH:(s + 1) * SSM_GROUP_CH, cols] = (
                    at[g * SSM_GROUP_CH:(g + 1) * SSM_GROUP_CH])

    lane_j = lax.broadcasted_iota(jnp.int32, (1, SSM_NCHUNK), 1)

    def group_body(g, carry):
        ut = v_ref[g]
        xt = jnp.dot(gt_ref[g], ut, preferred_element_type=F32)
        cr_all = cr_ref[g]
        ci_all = ci_ref[g]
        xr = [xt[0:SSM_STATE, b * SSM_NCHUNK:(b + 1) * SSM_NCHUNK] for b in range(SSM_BATCH_TILE)]
        xi = [xt[SSM_STATE:, b * SSM_NCHUNK:(b + 1) * SSM_NCHUNK] for b in range(SSM_BATCH_TILE)]
        for k in range(SSM_SCAN_STEPS):
            d = 1 << k
            keep = lane_j >= d
            cr = jnp.where(keep, cr_all[:, k:k + 1], 0.0)
            ci = jnp.where(keep, ci_all[:, k:k + 1], 0.0)
            for b in range(SSM_BATCH_TILE):
                rr = pltpu.roll(xr[b], d, 1)
                ri = pltpu.roll(xi[b], d, 1)
                xr[b], xi[b] = xr[b] + cr * rr - ci * ri, xi[b] + cr * ri + ci * rr
        keep = lane_j >= 1
        sr = jnp.concatenate([jnp.where(keep, pltpu.roll(t, 1, 1), 0.0) for t in xr], axis=1)
        si = jnp.concatenate([jnp.where(keep, pltpu.roll(t, 1, 1), 0.0) for t in xi], axis=1)
        sprev = jnp.concatenate([sr, si], axis=0).astype(BF16)
        yt_ref[g] = (jnp.dot(mt_ref[g], ut, preferred_element_type=F32)
                     + jnp.dot(ct_ref[g], sprev, preferred_element_type=F32))
        return carry

    lax.fori_loop(0, SSM_LANE_GROUPS, group_body, 0)

    for bl in range(SSM_BATCH_TILE):
        cols = slice(bl * SSM_NCHUNK, (bl + 1) * SSM_NCHUNK)
        for t in range(SSM_CHUNK):
            tile = jnp.concatenate(
                [yt_ref[g, t * SSM_GROUP_CH:(t + 1) * SSM_GROUP_CH, cols] for g in range(SSM_LANE_GROUPS)],
                axis=0)
            y_ref[pl.ds(bl * SEQ + t, SSM_NCHUNK, stride=SSM_CHUNK), :] = tile.T


def _ssm_mixer(x2, params):
    mt, gt, ct, coef_r, coef_i = params
    rows = SSM_BATCH_TILE * SEQ
    pspec = lambda a, b: pl.BlockSpec((SSM_LANE_GROUPS, a, b), lambda k, q: (k, 0, 0))
    return pl.pallas_call(
        _ssm_kernel,
        out_shape=jax.ShapeDtypeStruct((N_TOK, D_MODEL), F32),
        grid=(SSM_LANE_CHUNKS, BATCH // SSM_BATCH_TILE),
        in_specs=[pl.BlockSpec((rows, LANES), lambda k, q: (q, k)),
                  pspec(SSM_WIDTH, SSM_WIDTH), pspec(2 * SSM_STATE, SSM_WIDTH), pspec(SSM_WIDTH, 2 * SSM_STATE),
                  pspec(SSM_STATE, 8), pspec(SSM_STATE, 8)],
        out_specs=pl.BlockSpec((rows, LANES), lambda k, q: (q, k)),
        scratch_shapes=[pltpu.VMEM((SSM_LANE_GROUPS, SSM_WIDTH, SSM_COLS), BF16),
                        pltpu.VMEM((SSM_LANE_GROUPS, SSM_WIDTH, SSM_COLS), F32)],
        compiler_params=_cparams("parallel", "parallel"),
        name="ssm_mixer",
    )(x2, mt, gt, ct, coef_r, coef_i)


def _glu_ln_kernel(y_ref, x_ref, d_ref, w_ref, lg_ref, lb_ref, o_ref):
    x = x_ref[...]
    act = jax.nn.gelu(y_ref[...] + d_ref[...] * x).astype(BF16)
    z = jnp.dot(act, w_ref[...], preferred_element_type=F32)
    mix = z[:, :D_MODEL] * jax.nn.sigmoid(z[:, D_MODEL:])
    o_ref[...] = _layer_norm(DEEPNORM_ALPHA * x + mix, lg_ref[...], lb_ref[...])


def _row_spec(tm, width):
    return pl.BlockSpec((tm, width), lambda i: (i, 0))


def _const_spec(*shape):
    return pl.BlockSpec(shape, lambda *_: (0,) * len(shape))


def _glu_ln(y, x2, d_skip, w_glu, lg, lb, tm=512):
    return pl.pallas_call(
        _glu_ln_kernel,
        out_shape=jax.ShapeDtypeStruct((N_TOK, D_MODEL), F32),
        grid=(N_TOK // tm,),
        in_specs=[_row_spec(tm, D_MODEL), _row_spec(tm, D_MODEL), _const_spec(1, D_MODEL),
                  _const_spec(D_MODEL, 2 * D_MODEL), _const_spec(1, D_MODEL), _const_spec(1, D_MODEL)],
        out_specs=_row_spec(tm, D_MODEL),
        compiler_params=_cparams("parallel"),
        name="glu_ln",
    )(y, x2, d_skip, w_glu, lg, lb)


FFN_SPLIT = 2
FFN_TILE = FFN_DIM // FFN_SPLIT


def _ffn_ln_kernel(h_ref, wg_ref, wu_ref, wd_ref, lg_ref, lb_ref, o_ref, acc_ref):
    f = pl.program_id(1)
    hb = h_ref[...].astype(BF16)
    gate = jnp.dot(hb, wg_ref[...], preferred_element_type=F32)
    up = jnp.dot(hb, wu_ref[...], preferred_element_type=F32)
    act = (jax.nn.silu(gate) * up).astype(BF16)
    part = jnp.dot(act, wd_ref[...], preferred_element_type=F32)

    @pl.when(f == 0)
    def _():
        acc_ref[...] = part

    @pl.when(f > 0)
    def _():
        acc_ref[...] += part

    @pl.when(f == FFN_SPLIT - 1)
    def _():
        o_ref[...] = _layer_norm(DEEPNORM_ALPHA * h_ref[...] + acc_ref[...], lg_ref[...], lb_ref[...])


def _ffn_ln(h, wg, wu, wd, lg, lb, tm=512):
    return pl.pallas_call(
        _ffn_ln_kernel,
        out_shape=jax.ShapeDtypeStruct((N_TOK, D_MODEL), F32),
        grid=(N_TOK // tm, FFN_SPLIT),
        in_specs=[pl.BlockSpec((tm, D_MODEL), lambda i, f: (i, 0)),
                  pl.BlockSpec((D_MODEL, FFN_TILE), lambda i, f: (0, f)),
                  pl.BlockSpec((D_MODEL, FFN_TILE), lambda i, f: (0, f)),
                  pl.BlockSpec((FFN_TILE, D_MODEL), lambda i, f: (f, 0)),
                  _const_spec(1, D_MODEL), _const_spec(1, D_MODEL)],
        out_specs=pl.BlockSpec((tm, D_MODEL), lambda i, f: (i, 0)),
        scratch_shapes=[pltpu.VMEM((tm, D_MODEL), F32)],
        compiler_params=_cparams("parallel", "arbitrary"),
        name="ffn_ln",
    )(h, wg, wu, wd, lg, lb)


Q_DIM = N_HEADS * HEAD_DIM


def _qkv_kernel(h_ref, w_ref, cos_ref, sin_ref, q_ref, k_ref, v_ref):
    hb = h_ref[...].astype(BF16)
    z = jnp.dot(hb, w_ref[...], preferred_element_type=F32)
    cos_t = cos_ref[...]
    sin_t = sin_ref[...]
    lane = lax.broadcasted_iota(jnp.int32, (1, LANES), 1)
    first_half = (lane % HEAD_DIM) < ROT_HALF

    def rope(t):
        partner = jnp.where(first_half, pltpu.roll(t, LANES - ROT_HALF, 1), pltpu.roll(t, ROT_HALF, 1))
        return t * cos_t + partner * sin_t

    scale = HEAD_DIM ** -0.5
    for c in range(Q_DIM // LANES):
        sl = slice(c * LANES, (c + 1) * LANES)
        q_ref[:, sl] = (rope(z[:, sl]) * scale).astype(BF16)
    for c in range(KV_DIM // LANES):
        sl = slice(c * LANES, (c + 1) * LANES)
        k_ref[:, sl] = rope(z[:, Q_DIM + c * LANES:Q_DIM + (c + 1) * LANES]).astype(BF16)
    v_ref[...] = z[:, Q_DIM + KV_DIM:].astype(BF16)


def _qkv(h, w_qkv, cos_t, sin_t, tm=512):
    return pl.pallas_call(
        _qkv_kernel,
        out_shape=[jax.ShapeDtypeStruct((N_TOK, Q_DIM), BF16),
                   jax.ShapeDtypeStruct((N_TOK, KV_DIM), BF16),
                   jax.ShapeDtypeStruct((N_TOK, KV_DIM), BF16)],
        grid=(N_TOK // tm,),
        in_specs=[_row_spec(tm, D_MODEL), _const_spec(D_MODEL, Q_DIM + 2 * KV_DIM),
                  _row_spec(tm, LANES), _row_spec(tm, LANES)],
        out_specs=[_row_spec(tm, Q_DIM), _row_spec(tm, KV_DIM), _row_spec(tm, KV_DIM)],
        compiler_params=_cparams("parallel"),
        name="qkv_rope",
    )(h, w_qkv, cos_t, sin_t)


N_BLOCKS = SEQ // BLOCK


ATTN_Q_TILE = 512
ATTN_SUB = ATTN_Q_TILE // BLOCK
ATTN_BAND = ATTN_Q_TILE + BLOCK
KV_PAIRS = N_KV_HEADS // 2

HEAD_ORDER = [h for c in range(KV_PAIRS) for g in range(Q_PER_KV)
              for h in (2 * c * Q_PER_KV + g, (2 * c + 1) * Q_PER_KV + g)]


def _band_bias():
    qi = jnp.arange(Q_PER_KV * BLOCK)[:, None] % BLOCK
    si = jnp.arange(2 * BLOCK)[None, :]
    rel = qi + BLOCK - si
    valid = (rel >= 0) & (rel < WINDOW)
    first = valid & (si >= BLOCK)
    neg = jnp.float32(-jnp.inf)
    return jnp.stack([jnp.where(valid, 0.0, neg), jnp.where(first, 0.0, neg)]).astype(F32)


def _attn_kernel(sink_ref, q_ref, kp_ref, kc_ref, vp_ref, vc_ref, bias_ref, h_ref, w_ref, lg_ref, lb_ref,
                 out_ref, ka_ref, kb_ref, va_ref, vb_ref, o_ref):
    i = pl.program_id(1)
    lane = lax.broadcasted_iota(jnp.int32, (1, KV_DIM), 1)
    low = (lane % LANES) < HEAD_DIM
    zero = jnp.zeros((), BF16)
    for src, dst_a, dst_b in ((kp_ref, ka_ref, kb_ref), (vp_ref, va_ref, vb_ref)):
        t = src[...]
        dst_a[0:BLOCK, :] = jnp.where(low, t, zero)
        dst_b[0:BLOCK, :] = jnp.where(low, zero, t)
    for src, dst_a, dst_b in ((kc_ref, ka_ref, kb_ref), (vc_ref, va_ref, vb_ref)):
        t = src[...]
        dst_a[BLOCK:ATTN_BAND, :] = jnp.where(low, t, zero)
        dst_b[BLOCK:ATTN_BAND, :] = jnp.where(low, zero, t)

    lane1 = lax.broadcasted_iota(jnp.int32, (1, LANES), 1)
    row = lax.broadcasted_iota(jnp.int32, (Q_PER_KV * BLOCK, 1), 0)
    contract_last = (((1,), (1,)), ((), ()))
    for blk in range(ATTN_SUB):
        r0 = blk * BLOCK
        if blk == 0:
            bias = bias_ref[jnp.where(i == 0, 1, 0)]
        else:
            bias = bias_ref[0]
        for c in range(KV_PAIRS):
            cs = slice(c * LANES, (c + 1) * LANES)
            chunks = [c * Q_PER_KV + g for g in range(Q_PER_KV)]
            q4 = jnp.concatenate([q_ref[r0:r0 + BLOCK, m * LANES:(m + 1) * LANES] for m in chunks], axis=0)
            outs = []
            for half, (k_ref, v_ref) in enumerate(((ka_ref, va_ref), (kb_ref, vb_ref))):
                kband = k_ref[r0:r0 + 2 * BLOCK, cs]
                s = lax.dot_general(q4, kband, contract_last, preferred_element_type=F32) + bias
                sink = jnp.zeros((Q_PER_KV * BLOCK, 1), F32)
                for g, m in enumerate(chunks):
                    sink = jnp.where(row // BLOCK == g, sink_ref[HEAD_ORDER[2 * m + half]], sink)
                mx = jnp.maximum(jnp.max(s, axis=-1, keepdims=True), sink)
                p = jnp.exp(s - mx)
                denom = jnp.sum(p, axis=-1, keepdims=True) + jnp.exp(sink - mx)
                pv = jnp.dot(p.astype(BF16), v_ref[r0:r0 + 2 * BLOCK, cs], preferred_element_type=F32)
                outs.append((pv, 1.0 / denom))
            (pv_a, r_a), (pv_b, r_b) = outs
            o = (pv_a + pv_b) * jnp.where(lane1 < HEAD_DIM, r_a, r_b)
            for g, m in enumerate(chunks):
                o_ref[r0:r0 + BLOCK, m * LANES:(m + 1) * LANES] = o[g * BLOCK:(g + 1) * BLOCK].astype(BF16)

    mix = jnp.dot(o_ref[...], w_ref[...], preferred_element_type=F32)
    out_ref[...] = _layer_norm(DEEPNORM_ALPHA * h_ref[...] + mix, lg_ref[...], lb_ref[...])


def _attention_ln(q, k, v, sinks, h, w_out, lg, lb):
    tiles = SEQ // ATTN_Q_TILE
    cur = lambda b, i: (b * tiles + i, 0)
    prev = lambda b, i: (jnp.maximum((b * tiles + i) * ATTN_SUB - 1, 0), 0)
    const2 = lambda b, i: (0, 0)
    return pl.pallas_call(
        _attn_kernel,
        out_shape=jax.ShapeDtypeStruct((N_TOK, D_MODEL), F32),
        grid=(BATCH, tiles),
        in_specs=[pl.BlockSpec(memory_space=pltpu.SMEM),
                  pl.BlockSpec((ATTN_Q_TILE, Q_DIM), cur),
                  pl.BlockSpec((BLOCK, KV_DIM), prev), pl.BlockSpec((ATTN_Q_TILE, KV_DIM), cur),
                  pl.BlockSpec((BLOCK, KV_DIM), prev), pl.BlockSpec((ATTN_Q_TILE, KV_DIM), cur),
                  pl.BlockSpec((2, Q_PER_KV * BLOCK, 2 * BLOCK), lambda b, i: (0, 0, 0)),
                  pl.BlockSpec((ATTN_Q_TILE, D_MODEL), cur),
                  pl.BlockSpec((Q_DIM, D_MODEL), const2),
                  pl.BlockSpec((1, D_MODEL), const2), pl.BlockSpec((1, D_MODEL), const2)],
        out_specs=pl.BlockSpec((ATTN_Q_TILE, D_MODEL), cur),
        scratch_shapes=[pltpu.VMEM((ATTN_BAND, KV_DIM), BF16)] * 4
                       + [pltpu.VMEM((ATTN_Q_TILE, Q_DIM), BF16)],
        compiler_params=_cparams("parallel", "parallel"),
        name="swa_attention_ln",
    )(sinks, q, k, k, v, v, _band_bias(), h, w_out, lg, lb)


ROUTE_TILE = 512
EXPERT_TILE = 512
SLAB = D_MODEL // LANES
EXPERT_REGION = N_TOK
REGION_TILES = EXPERT_REGION // EXPERT_TILE
N_SORTED = N_EXPERTS * EXPERT_REGION
N_EXPERT_TILES = 2 * N_TOK // EXPERT_TILE + N_EXPERTS
META_E1, META_E2, META_R1, META_R2, META_W1, META_W2 = range(6)
META_D1, META_D2 = 8, 9


def _to_slabs(ref, val, rows):
    for c in range(SLAB):
        ref[pl.ds(c, rows, stride=SLAB), :] = val[:, c * LANES:(c + 1) * LANES]


def _from_slabs(ref, rows, first_row=0):
    return jnp.concatenate([ref[pl.ds(first_row * SLAB + c, rows, stride=SLAB), :] for c in range(SLAB)],
                           axis=1)


def _row_slab(ref, row):
    return ref.at[pl.ds(pl.multiple_of(row * SLAB, SLAB), SLAB)]


def _split_bf16(x):
    def top_bits(v):
        bits = lax.bitcast_convert_type(v, jnp.uint32) & jnp.uint32(0xFFFF0000)
        return lax.bitcast_convert_type(bits, F32)
    hi = top_bits(x)
    mid = top_bits(x - hi)
    lo = (x - hi) - mid
    return hi.astype(BF16), mid.astype(BF16), lo.astype(BF16)


def _router_kernel(h_ref, wr_ref, br_ref, tri_ref, meta_ref, counts_ref, xs_hbm,
                   carry_ref, slab_ref, zero_ref, dest_v, dest_s, count_v, count_s, sem, idx_sem):
    i = pl.program_id(0)

    @pl.when(i == 0)
    def _():
        carry_ref[...] = jnp.zeros_like(carry_ref)
        zero_ref[...] = jnp.zeros_like(zero_ref)

    lane = lax.broadcasted_iota(jnp.int32, (1, LANES), 1)
    lane_f = lane.astype(F32)
    h = h_ref[...]
    h0, h1, h2 = _split_bf16(h)
    w0, w1, w2 = wr_ref[0], wr_ref[1], wr_ref[2]
    dot = functools.partial(jnp.dot, preferred_element_type=F32)
    logits = ((dot(h2, w0) + dot(h0, w2) + dot(h1, w1)) + (dot(h1, w0) + dot(h0, w1)) + dot(h0, w0)
              + br_ref[...])
    logits = jnp.where(lane < N_EXPERTS, logits, -jnp.inf)
    m1 = jnp.max(logits, axis=-1, keepdims=True)
    i1 = jnp.min(jnp.where(logits == m1, lane_f, float(LANES)), axis=-1, keepdims=True)
    rest = jnp.where(lane_f == i1, -jnp.inf, logits)
    m2 = jnp.max(rest, axis=-1, keepdims=True)
    i2 = jnp.min(jnp.where(rest == m2, lane_f, float(LANES)), axis=-1, keepdims=True)
    e2 = jnp.exp(m2 - m1)
    tot = 1.0 + e2
    chosen = jnp.where((lane_f == i1) | (lane_f == i2), 1.0, 0.0)
    before = jnp.dot(tri_ref[...], chosen.astype(BF16), preferred_element_type=F32) + carry_ref[...]
    r1 = jnp.sum(jnp.where(lane_f == i1, before, 0.0), axis=-1, keepdims=True)
    r2 = jnp.sum(jnp.where(lane_f == i2, before, 0.0), axis=-1, keepdims=True)
    new_carry = carry_ref[...] + jnp.sum(chosen, axis=0, keepdims=True)
    carry_ref[...] = new_carry
    counts_ref[...] = new_carry.astype(jnp.int32)
    record = jnp.zeros((ROUTE_TILE, LANES), F32)
    for slot, col in ((META_E1, i1), (META_E2, i2), (META_R1, r1), (META_R2, r2),
                      (META_W1, 1.0 / tot), (META_W2, e2 / tot),
                      (META_D1, i1 * EXPERT_REGION + r1), (META_D2, i2 * EXPERT_REGION + r2)):
        record = jnp.where(lane == slot, col, record)
    meta_ref[...] = record

    _to_slabs(slab_ref, h, ROUTE_TILE)
    dest_v[...] = record.T[META_D1:META_D1 + 8, :].astype(jnp.int32)
    to_smem = pltpu.make_async_copy(dest_v, dest_s, idx_sem)
    to_smem.start()
    to_smem.wait()

    def scatter(t, carry):
        for k in range(2):
            pltpu.make_async_copy(_row_slab(slab_ref, t), _row_slab(xs_hbm, dest_s[k, t]), sem).start()
        return carry
    lax.fori_loop(0, ROUTE_TILE, scatter, 0, unroll=8)
    for _ in range(2):
        pltpu.make_async_copy(slab_ref, xs_hbm.at[pl.ds(0, ROUTE_TILE * SLAB)], sem).wait()

    @pl.when(i == pl.num_programs(0) - 1)
    def _():
        count_v[...] = jnp.broadcast_to(new_carry.astype(jnp.int32), count_v.shape)
        cp = pltpu.make_async_copy(count_v, count_s, idx_sem)
        cp.start()
        cp.wait()
        for e in range(N_EXPERTS):
            n = count_s[0, e]
            end = (n + EXPERT_TILE - 1) // EXPERT_TILE * EXPERT_TILE
            pad = lambda r: pltpu.make_async_copy(zero_ref, _row_slab(xs_hbm, e * EXPERT_REGION + r), sem)
            lax.fori_loop(n, end, lambda r, c: (pad(r).start(), c)[1], 0)
            lax.fori_loop(n, end, lambda r, c: (pad(r).wait(), c)[1], 0)


def _router(h, w_router, b_router):
    tri = (jnp.arange(ROUTE_TILE)[:, None] > jnp.arange(ROUTE_TILE)[None, :]).astype(BF16)
    w_split = jnp.stack(_split_bf16(w_router))
    return pl.pallas_call(
        _router_kernel,
        out_shape=[jax.ShapeDtypeStruct((N_TOK, LANES), F32),
                   jax.ShapeDtypeStruct((1, LANES), jnp.int32),
                   jax.ShapeDtypeStruct((N_SORTED * SLAB, LANES), F32)],
        grid=(N_TOK // ROUTE_TILE,),
        in_specs=[_row_spec(ROUTE_TILE, D_MODEL), _const_spec(3, D_MODEL, LANES),
                  _const_spec(1, LANES), _const_spec(ROUTE_TILE, ROUTE_TILE)],
        out_specs=[_row_spec(ROUTE_TILE, LANES), _const_spec(1, LANES), pl.BlockSpec(memory_space=pl.ANY)],
        scratch_shapes=[pltpu.VMEM((1, LANES), F32), pltpu.VMEM((ROUTE_TILE * SLAB, LANES), F32),
                        pltpu.VMEM((SLAB, LANES), F32),
                        pltpu.VMEM((8, ROUTE_TILE), jnp.int32), pltpu.SMEM((8, ROUTE_TILE), jnp.int32),
                        pltpu.VMEM((8, LANES), jnp.int32), pltpu.SMEM((8, LANES), jnp.int32),
                        pltpu.SemaphoreType.DMA, pltpu.SemaphoreType.DMA],
        compiler_params=_cparams("arbitrary"),
        name="moe_router",
    )(h, w_split, b_router, tri)


def _experts_kernel(tblk_ref, texp_ref, xs_ref, wg_ref, wu_ref, wd_ref, ys_ref):
    x = _from_slabs(xs_ref, EXPERT_TILE).astype(BF16)
    gate = jnp.dot(x, wg_ref[0], preferred_element_type=F32)
    up = jnp.dot(x, wu_ref[0], preferred_element_type=F32)
    act = (jax.nn.silu(gate) * up).astype(BF16)
    _to_slabs(ys_ref, jnp.dot(act, wd_ref[0], preferred_element_type=F32), EXPERT_TILE)


def _experts(tile_block, tile_expert, xs, wg, wu, wd):
    wspec = lambda a, b: pl.BlockSpec((1, a, b), lambda i, tblk, texp: (texp[i], 0, 0))
    rows = pl.BlockSpec((EXPERT_TILE * SLAB, LANES), lambda i, tblk, texp: (tblk[i], 0))
    return pl.pallas_call(
        _experts_kernel,
        out_shape=jax.ShapeDtypeStruct((N_SORTED * SLAB, LANES), F32),
        grid_spec=pltpu.PrefetchScalarGridSpec(
            num_scalar_prefetch=2,
            grid=(N_EXPERT_TILES,),
            in_specs=[rows, wspec(D_MODEL, EXPERT_DIM), wspec(D_MODEL, EXPERT_DIM),
                      wspec(EXPERT_DIM, D_MODEL)],
            out_specs=rows),
        compiler_params=_cparams("arbitrary"),
        name="moe_experts",
    )(tile_block, tile_expert, xs, wg, wu, wd)


def _combine_ln_kernel(dest_ref, meta_ref, h_ref, ys_hbm, lg_ref, lb_ref, o_ref, ya_ref, yb_ref, sem):
    i = pl.program_id(0)
    bufs = (ya_ref, yb_ref)

    def start_gather(tile, slot):
        def body(t, carry):
            for k in range(2):
                row = dest_ref[k * N_TOK + tile * ROUTE_TILE + t]
                pltpu.make_async_copy(_row_slab(ys_hbm, row), _row_slab(bufs[slot], k * ROUTE_TILE + t),
                                      sem.at[slot]).start()
            return carry
        lax.fori_loop(0, ROUTE_TILE, body, 0, unroll=8)

    def finish(slot):
        pltpu.make_async_copy(ys_hbm.at[pl.ds(0, 2 * ROUTE_TILE * SLAB)], bufs[slot], sem.at[slot]).wait()
        lane = lax.broadcasted_iota(jnp.int32, (1, LANES), 1)
        meta = meta_ref[...]
        w1 = jnp.sum(jnp.where(lane == META_W1, meta, 0.0), axis=-1, keepdims=True)
        w2 = jnp.sum(jnp.where(lane == META_W2, meta, 0.0), axis=-1, keepdims=True)
        ff = (w1 * _from_slabs(bufs[slot], ROUTE_TILE)
              + w2 * _from_slabs(bufs[slot], ROUTE_TILE, first_row=ROUTE_TILE))
        o_ref[...] = _layer_norm(DEEPNORM_ALPHA * h_ref[...] + ff, lg_ref[...], lb_ref[...])

    @pl.when(i == 0)
    def _():
        start_gather(0, 0)

    for slot in range(2):
        @pl.when((i + 1 < pl.num_programs(0)) & ((i + 1) % 2 == slot))
        def _():
            start_gather(i + 1, slot)

        @pl.when(i % 2 == slot)
        def _():
            finish(slot)


def _combine_ln(dest, meta, h, ys, lg, lb):
    row = lambda width: pl.BlockSpec((ROUTE_TILE, width), lambda i, dest: (i, 0))
    const = pl.BlockSpec((1, D_MODEL), lambda i, dest: (0, 0))
    return pl.pallas_call(
        _combine_ln_kernel,
        out_shape=jax.ShapeDtypeStruct((N_TOK, D_MODEL), F32),
        grid_spec=pltpu.PrefetchScalarGridSpec(
            num_scalar_prefetch=1,
            grid=(N_TOK // ROUTE_TILE,),
            in_specs=[row(LANES), row(D_MODEL), pl.BlockSpec(memory_space=pl.ANY), const, const],
            out_specs=row(D_MODEL),
            scratch_shapes=[pltpu.VMEM((2 * ROUTE_TILE * SLAB, LANES), F32)] * 2
                           + [pltpu.SemaphoreType.DMA((2,))]),
        compiler_params=_cparams("arbitrary"),
        name="moe_combine_ln",
    )(dest, meta, h, ys, lg, lb)


def _moe_ln(h, w_router, b_router, wg, wu, wd, lg, lb):
    meta, counts, xs = _router(h, w_router, b_router)
    expert = jnp.arange(N_EXPERTS)
    tiles = (counts[0, :N_EXPERTS] + EXPERT_TILE - 1) // EXPERT_TILE
    ends = jnp.sum(jnp.where(expert[None, :] <= expert[:, None], tiles[None, :], 0), axis=1)
    step = jnp.minimum(jnp.arange(N_EXPERT_TILES), ends[-1] - 1)
    tile_expert = jnp.sum(step[:, None] >= ends[None, :], axis=1)
    first = jnp.sum(jnp.where(tile_expert[:, None] == expert[None, :], (ends - tiles)[None, :], 0), axis=1)
    tile_block = tile_expert * REGION_TILES + (step - first)
    ys = _experts(tile_block.astype(jnp.int32), tile_expert.astype(jnp.int32), xs, wg, wu, wd)
    dest = jnp.concatenate([meta[:, META_D1], meta[:, META_D2]]).astype(jnp.int32)
    return _combine_ln(dest, meta, h, ys, lg, lb)


def kernel(x, positions, ln_g, ln_b, ssm_lambda_re, ssm_lambda_im, ssm_log_step, ssm_b_re, ssm_b_im, ssm_c_re, ssm_c_im, ssm_d, ssm_w_glu, kv_w, attn_w_q, attn_sinks, attn_w_out, ffn_w_gate, ffn_w_up, ffn_w_down, moe_w_router, moe_b_router, moe_w_gate, moe_w_up, moe_w_down):
    ln = lambda layer, j: (ln_g[layer, j].reshape(1, D_MODEL).astype(F32),
                           ln_b[layer, j].reshape(1, D_MODEL).astype(F32))
    cos_t, sin_t = _rope_tables(positions)
    x2 = x.reshape(N_TOK, D_MODEL)

    params = _ssm_params(ssm_lambda_re[0], ssm_lambda_im[0], ssm_log_step[0], ssm_b_re[0], ssm_b_im[0],
                         ssm_c_re[0], ssm_c_im[0])
    y = _ssm_mixer(x2, params)
    h = _glu_ln(y, x2, ssm_d[0].astype(F32).reshape(1, D_MODEL), ssm_w_glu[0].astype(BF16), *ln(0, 0))
    h = _ffn_ln(h, ffn_w_gate[0].astype(BF16), ffn_w_up[0].astype(BF16), ffn_w_down[0].astype(BF16),
                *ln(0, 1))

    order = jnp.array(HEAD_ORDER)
    w_q = attn_w_q[0].reshape(D_MODEL, N_HEADS, HEAD_DIM)[:, order].reshape(D_MODEL, Q_DIM)
    w_out = attn_w_out[0].reshape(N_HEADS, HEAD_DIM, D_MODEL)[order].reshape(Q_DIM, D_MODEL)
    w_qkv = jnp.concatenate([w_q, kv_w], axis=1).astype(BF16)
    q, k, v = _qkv(h, w_qkv, cos_t, sin_t)
    h = _attention_ln(q, k, v, attn_sinks[0].astype(F32), h, w_out.astype(BF16), *ln(1, 0))
    w_router = jnp.pad(moe_w_router[0].astype(F32), ((0, 0), (0, LANES - N_EXPERTS)))
    b_router = jnp.pad(moe_b_router[0].astype(F32), (0, LANES - N_EXPERTS)).reshape(1, LANES)
    h = _moe_ln(h, w_router, b_router, moe_w_gate[0].astype(BF16), moe_w_up[0].astype(BF16),
                moe_w_down[0].astype(BF16), *ln(1, 1))
    return h.reshape(BATCH, SEQ, D_MODEL)
```

```python
import functools
import math

import jax
import jax.numpy as jnp
from jax import lax
from jax.experimental import pallas as pl
from jax.experimental.pallas import tpu as pltpu

F32 = jnp.float32
BF16 = jnp.bfloat16

D_MODEL = 1024
BATCH = 16
SEQ = 2048
N_TOK = BATCH * SEQ
DEPTH = 2

SSM_GROUP_CH = 16
SSM_GROUPS = D_MODEL // SSM_GROUP_CH
SSM_STATE = 64
SSM_CHUNK = 16
SSM_NCHUNK = SEQ // SSM_CHUNK
SSM_ROWS = SSM_NCHUNK * BATCH
SSM_WIDTH = SSM_CHUNK * SSM_GROUP_CH
SSM_SCAN_STEPS = int(math.log2(SSM_NCHUNK))

N_HEADS = 16
HEAD_DIM = 64
N_KV_HEADS = 4
Q_PER_KV = N_HEADS // N_KV_HEADS
KV_DIM = N_KV_HEADS * HEAD_DIM
WINDOW = 128
BLOCK = 128
ROT_DIM = HEAD_DIM // 4
ROT_HALF = ROT_DIM // 2
ROPE_THETA = 500000.0

FFN_DIM = 2816
N_EXPERTS = 8
EXPERT_DIM = 1024

DEEPNORM_ALPHA = (2 * DEPTH) ** 0.25
LN_EPS = 1e-5

LANES = 128
VMEM_LIMIT = 56 * 1024 * 1024


def _cparams(*sem):
    return pltpu.CompilerParams(dimension_semantics=sem, vmem_limit_bytes=VMEM_LIMIT)


def _layer_norm(r, g, b):
    mu = jnp.mean(r, axis=-1, keepdims=True)
    xc = r - mu
    var = jnp.mean(xc * xc, axis=-1, keepdims=True)
    return xc * lax.rsqrt(var + LN_EPS) * g + b


def _rope_kernel(invf_ref, pos_ref, cos_ref, sin_ref):
    pos = pos_ref[...]
    for f in range(ROT_HALF):
        ang = pos * invf_ref[f]
        cos_ref[f] = jnp.cos(ang)
        sin_ref[f] = jnp.sin(ang)


def _rope_tables(positions):
    inv_freq = ROPE_THETA ** (-jnp.arange(0, ROT_DIM, 2, dtype=F32) / ROT_DIM)
    pos = positions.astype(F32)
    cos8, sin8 = pl.pallas_call(
        _rope_kernel,
        out_shape=[jax.ShapeDtypeStruct((ROT_HALF, BATCH, SEQ), F32)] * 2,
        in_specs=[pl.BlockSpec(memory_space=pltpu.SMEM),
                  pl.BlockSpec(memory_space=pltpu.VMEM)],
        out_specs=[pl.BlockSpec(memory_space=pltpu.VMEM)] * 2,
        name="rope_tables",
    )(inv_freq, pos)
    cos8 = cos8.transpose(1, 2, 0).reshape(N_TOK, ROT_HALF)
    sin8 = sin8.transpose(1, 2, 0).reshape(N_TOK, ROT_HALF)
    rest = HEAD_DIM - ROT_DIM
    cos_head = jnp.concatenate([cos8, cos8, jnp.ones((N_TOK, rest), F32)], axis=1)
    sin_head = jnp.concatenate([-sin8, sin8, jnp.zeros((N_TOK, rest), F32)], axis=1)
    reps = LANES // HEAD_DIM
    return jnp.tile(cos_head, (1, reps)), jnp.tile(sin_head, (1, reps))


def _ssm_params(lam_re, lam_im, log_step, b_re, b_im, c_re, c_im):
    hp = lax.Precision.HIGHEST
    lr, li = lam_re.astype(F32), lam_im.astype(F32)
    dt = jnp.exp(log_step.astype(F32))[:, None]
    mag = jnp.exp(lr * dt)
    ar = mag * jnp.cos(li * dt)
    ai = mag * jnp.sin(li * dt)
    nr = ar - 1.0
    den = lr * lr + li * li
    kr = (nr * lr + ai * li) / den
    ki = (ai * lr - nr * li) / den
    br, bi = b_re.astype(F32), b_im.astype(F32)
    bbar_r = kr[..., None] * br - ki[..., None] * bi
    bbar_i = kr[..., None] * bi + ki[..., None] * br
    cr, ci = c_re.astype(F32), c_im.astype(F32)

    def powers(taus):
        t = taus.astype(F32)[:, None, None]
        m = jnp.exp(lr[None] * dt[None] * t)
        ang = li[None] * dt[None] * t
        return m * jnp.cos(ang), m * jnp.sin(ang)

    er, ei = powers(jnp.arange(SSM_CHUNK + 1))
    w_r = er[:, :, :, None] * bbar_r[None] - ei[:, :, :, None] * bbar_i[None]
    w_i = er[:, :, :, None] * bbar_i[None] + ei[:, :, :, None] * bbar_r[None]
    kern = (jnp.einsum('gcp,tgpd->gtcd', cr, w_r[:SSM_CHUNK], precision=hp)
            - jnp.einsum('gcp,tgpd->gtcd', ci, w_i[:SSM_CHUNK], precision=hp))
    s_idx = jnp.arange(SSM_CHUNK)[:, None, None]
    t_idx = jnp.arange(SSM_CHUNK)[None, :, None]
    lag = (t_idx - s_idx == jnp.arange(SSM_CHUNK)[None, None, :]).astype(F32)
    toep = jnp.einsum('stu,gucd->gtcsd', lag, kern, precision=hp)
    mt = toep.reshape(SSM_GROUPS, SSM_WIDTH, SSM_WIDTH)
    rev = SSM_CHUNK - 1 - jnp.arange(SSM_CHUNK)
    g_r = w_r[rev].transpose(1, 2, 0, 3).reshape(SSM_GROUPS, SSM_STATE, SSM_WIDTH)
    g_i = w_i[rev].transpose(1, 2, 0, 3).reshape(SSM_GROUPS, SSM_STATE, SSM_WIDTH)
    gt = jnp.concatenate([g_r, g_i], axis=1)
    e1r, e1i = er[1:], ei[1:]
    ce_r = cr[None] * e1r[:, :, None, :] - ci[None] * e1i[:, :, None, :]
    ce_i = cr[None] * e1i[:, :, None, :] + ci[None] * e1r[:, :, None, :]
    c_re = ce_r.transpose(1, 0, 2, 3).reshape(SSM_GROUPS, SSM_WIDTH, SSM_STATE)
    c_im = (-ce_i).transpose(1, 0, 2, 3).reshape(SSM_GROUPS, SSM_WIDTH, SSM_STATE)
    ct = jnp.concatenate([c_re, c_im], axis=-1)
    sr, si = powers(SSM_CHUNK * (2 ** jnp.arange(8)))
    return (mt.astype(BF16), gt.astype(BF16), ct.astype(BF16),
            sr.transpose(1, 2, 0), si.transpose(1, 2, 0))


SSM_LANE_GROUPS = LANES // SSM_GROUP_CH
SSM_LANE_CHUNKS = D_MODEL // LANES
SSM_BATCH_TILE = 4
SSM_COLS = SSM_BATCH_TILE * SSM_NCHUNK


def _ssm_kernel(x_ref, mt_ref, gt_ref, ct_ref, cr_ref, ci_ref, y_ref, v_ref, yt_ref):
    for bl in range(SSM_BATCH_TILE):
        cols = slice(bl * SSM_NCHUNK, (bl + 1) * SSM_NCHUNK)
        for s in range(SSM_CHUNK):
            a = x_ref[pl.ds(bl * SEQ + s, SSM_NCHUNK, stride=SSM_CHUNK), :]
            at = a.T.astype(BF16)
            for g in range(SSM_LANE_GROUPS):
                v_ref[g, s * SSM_GROUP_CH:(s + 1) * SSM_GROUP_CH, cols] = (
                    at[g * SSM_GROUP_CH:(g + 1) * SSM_GROUP_CH])

    lane_j = lax.broadcasted_iota(jnp.int32, (1, SSM_NCHUNK), 1)

    def group_body(g, carry):
        ut = v_ref[g]
        xt = jnp.dot(gt_ref[g], ut, preferred_element_type=F32)
        cr_all = cr_ref[g]
        ci_all = ci_ref[g]
        xr = [xt[0:SSM_STATE, b * SSM_NCHUNK:(b + 1) * SSM_NCHUNK] for b in range(SSM_BATCH_TILE)]
        xi = [xt[SSM_STATE:, b * SSM_NCHUNK:(b + 1) * SSM_NCHUNK] for b in range(SSM_BATCH_TILE)]
        for k in range(SSM_SCAN_STEPS):
            d = 1 << k
            keep = lane_j >= d
            cr = jnp.where(keep, cr_all[:, k:k + 1], 0.0)
            ci = jnp.where(keep, ci_all[:, k:k + 1], 0.0)
            for b in range(SSM_BATCH_TILE):
                rr = pltpu.roll(xr[b], d, 1)
                ri = pltpu.roll(xi[b], d, 1)
                xr[b], xi[b] = xr[b] + cr * rr - ci * ri, xi[b] + cr * ri + ci * rr
        keep = lane_j >= 1
        sr = jnp.concatenate([jnp.where(keep, pltpu.roll(t, 1, 1), 0.0) for t in xr], axis=1)
        si = jnp.concatenate([jnp.where(keep, pltpu.roll(t, 1, 1), 0.0) for t in xi], axis=1)
        sprev = jnp.concatenate([sr, si], axis=0).astype(BF16)
        yt_ref[g] = (jnp.dot(mt_ref[g], ut, preferred_element_type=F32)
                     + jnp.dot(ct_ref[g], sprev, preferred_element_type=F32))
        return carry

    lax.fori_loop(0, SSM_LANE_GROUPS, group_body, 0)

    for bl in range(SSM_BATCH_TILE):
        cols = slice(bl * SSM_NCHUNK, (bl + 1) * SSM_NCHUNK)
        for t in range(SSM_CHUNK):
            tile = jnp.concatenate(
                [yt_ref[g, t * SSM_GROUP_CH:(t + 1) * SSM_GROUP_CH, cols] for g in range(SSM_LANE_GROUPS)],
                axis=0)
            y_ref[pl.ds(bl * SEQ + t, SSM_NCHUNK, stride=SSM_CHUNK), :] = tile.T


def _ssm_mixer(x2, params):
    mt, gt, ct, coef_r, coef_i = params
    rows = SSM_BATCH_TILE * SEQ
    pspec = lambda a, b: pl.BlockSpec((SSM_LANE_GROUPS, a, b), lambda k, q: (k, 0, 0))
    return pl.pallas_call(
        _ssm_kernel,
        out_shape=jax.ShapeDtypeStruct((N_TOK, D_MODEL), F32),
        grid=(SSM_LANE_CHUNKS, BATCH // SSM_BATCH_TILE),
        in_specs=[pl.BlockSpec((rows, LANES), lambda k, q: (q, k)),
                  pspec(SSM_WIDTH, SSM_WIDTH), pspec(2 * SSM_STATE, SSM_WIDTH), pspec(SSM_WIDTH, 2 * SSM_STATE),
                  pspec(SSM_STATE, 8), pspec(SSM_STATE, 8)],
        out_specs=pl.BlockSpec((rows, LANES), lambda k, q: (q, k)),
        scratch_shapes=[pltpu.VMEM((SSM_LANE_GROUPS, SSM_WIDTH, SSM_COLS), BF16),
                        pltpu.VMEM((SSM_LANE_GROUPS, SSM_WIDTH, SSM_COLS), F32)],
        compiler_params=_cparams("parallel", "parallel"),
        name="ssm_mixer",
    )(x2, mt, gt, ct, coef_r, coef_i)


def _glu_ln_kernel(y_ref, x_ref, d_ref, w_ref, lg_ref, lb_ref, o_ref):
    x = x_ref[...]
    act = jax.nn.gelu(y_ref[...] + d_ref[...] * x).astype(BF16)
    z = jnp.dot(act, w_ref[...], preferred_element_type=F32)
    mix = z[:, :D_MODEL] * jax.nn.sigmoid(z[:, D_MODEL:])
    o_ref[...] = _layer_norm(DEEPNORM_ALPHA * x + mix, lg_ref[...], lb_ref[...])


def _row_spec(tm, width):
    return pl.BlockSpec((tm, width), lambda i: (i, 0))


def _const_spec(*shape):
    return pl.BlockSpec(shape, lambda *_: (0,) * len(shape))


def _glu_ln(y, x2, d_skip, w_glu, lg, lb, tm=512):
    return pl.pallas_call(
        _glu_ln_kernel,
        out_shape=jax.ShapeDtypeStruct((N_TOK, D_MODEL), F32),
        grid=(N_TOK // tm,),
        in_specs=[_row_spec(tm, D_MODEL), _row_spec(tm, D_MODEL), _const_spec(1, D_MODEL),
                  _const_spec(D_MODEL, 2 * D_MODEL), _const_spec(1, D_MODEL), _const_spec(1, D_MODEL)],
        out_specs=_row_spec(tm, D_MODEL),
        compiler_params=_cparams("parallel"),
        name="glu_ln",
    )(y, x2, d_skip, w_glu, lg, lb)


FFN_SPLIT = 2
FFN_TILE = FFN_DIM // FFN_SPLIT


def _ffn_ln_kernel(h_ref, wg_ref, wu_ref, wd_ref, lg_ref, lb_ref, o_ref, acc_ref):
    f = pl.program_id(1)
    hb = h_ref[...].astype(BF16)
    gate = jnp.dot(hb, wg_ref[...], preferred_element_type=F32)
    up = jnp.dot(hb, wu_ref[...], preferred_element_type=F32)
    act = (jax.nn.silu(gate) * up).astype(BF16)
    part = jnp.dot(act, wd_ref[...], preferred_element_type=F32)

    @pl.when(f == 0)
    def _():
        acc_ref[...] = part

    @pl.when(f > 0)
    def _():
        acc_ref[...] += part

    @pl.when(f == FFN_SPLIT - 1)
    def _():
        o_ref[...] = _layer_norm(DEEPNORM_ALPHA * h_ref[...] + acc_ref[...], lg_ref[...], lb_ref[...])


def _ffn_ln(h, wg, wu, wd, lg, lb, tm=512):
    return pl.pallas_call(
        _ffn_ln_kernel,
        out_shape=jax.ShapeDtypeStruct((N_TOK, D_MODEL), F32),
        grid=(N_TOK // tm, FFN_SPLIT),
        in_specs=[pl.BlockSpec((tm, D_MODEL), lambda i, f: (i, 0)),
                  pl.BlockSpec((D_MODEL, FFN_TILE), lambda i, f: (0, f)),
                  pl.BlockSpec((D_MODEL, FFN_TILE), lambda i, f: (0, f)),
                  pl.BlockSpec((FFN_TILE, D_MODEL), lambda i, f: (f, 0)),
                  _const_spec(1, D_MODEL), _const_spec(1, D_MODEL)],
        out_specs=pl.BlockSpec((tm, D_MODEL), lambda i, f: (i, 0)),
        scratch_shapes=[pltpu.VMEM((tm, D_MODEL), F32)],
        compiler_params=_cparams("parallel", "arbitrary"),
        name="ffn_ln",
    )(h, wg, wu, wd, lg, lb)


Q_DIM = N_HEADS * HEAD_DIM


def _qkv_kernel(h_ref, w_ref, cos_ref, sin_ref, q_ref, k_ref, v_ref):
    hb = h_ref[...].astype(BF16)
    z = jnp.dot(hb, w_ref[...], preferred_element_type=F32)
    cos_t = cos_ref[...]
    sin_t = sin_ref[...]
    lane = lax.broadcasted_iota(jnp.int32, (1, LANES), 1)
    first_half = (lane % HEAD_DIM) < ROT_HALF

    def rope(t):
        partner = jnp.where(first_half, pltpu.roll(t, LANES - ROT_HALF, 1), pltpu.roll(t, ROT_HALF, 1))
        return t * cos_t + partner * sin_t

    scale = HEAD_DIM ** -0.5
    for c in range(Q_DIM // LANES):
        sl = slice(c * LANES, (c + 1) * LANES)
        q_ref[:, sl] = (rope(z[:, sl]) * scale).astype(BF16)
    for c in range(KV_DIM // LANES):
        sl = slice(c * LANES, (c + 1) * LANES)
        k_ref[:, sl] = rope(z[:, Q_DIM + c * LANES:Q_DIM + (c + 1) * LANES]).astype(BF16)
    v_ref[...] = z[:, Q_DIM + KV_DIM:].astype(BF16)


def _qkv(h, w_qkv, cos_t, sin_t, tm=512):
    return pl.pallas_call(
        _qkv_kernel,
        out_shape=[jax.ShapeDtypeStruct((N_TOK, Q_DIM), BF16),
                   jax.ShapeDtypeStruct((N_TOK, KV_DIM), BF16),
                   jax.ShapeDtypeStruct((N_TOK, KV_DIM), BF16)],
        grid=(N_TOK // tm,),
        in_specs=[_row_spec(tm, D_MODEL), _const_spec(D_MODEL, Q_DIM + 2 * KV_DIM),
                  _row_spec(tm, LANES), _row_spec(tm, LANES)],
        out_specs=[_row_spec(tm, Q_DIM), _row_spec(tm, KV_DIM), _row_spec(tm, KV_DIM)],
        compiler_params=_cparams("parallel"),
        name="qkv_rope",
    )(h, w_qkv, cos_t, sin_t)


N_BLOCKS = SEQ // BLOCK


ATTN_Q_TILE = 512
ATTN_SUB = ATTN_Q_TILE // BLOCK
ATTN_BAND = ATTN_Q_TILE + BLOCK
KV_PAIRS = N_KV_HEADS // 2

HEAD_ORDER = [h for c in range(KV_PAIRS) for g in range(Q_PER_KV)
              for h in (2 * c * Q_PER_KV + g, (2 * c + 1) * Q_PER_KV + g)]


def _band_bias():
    qi = jnp.arange(Q_PER_KV * BLOCK)[:, None] % BLOCK
    si = jnp.arange(2 * BLOCK)[None, :]
    rel = qi + BLOCK - si
    valid = (rel >= 0) & (rel < WINDOW)
    first = valid & (si >= BLOCK)
    neg = jnp.float32(-jnp.inf)
    return jnp.stack([jnp.where(valid, 0.0, neg), jnp.where(first, 0.0, neg)]).astype(F32)


def _attn_kernel(sink_ref, q_ref, kp_ref, kc_ref, vp_ref, vc_ref, bias_ref, h_ref, w_ref, lg_ref, lb_ref,
                 out_ref, ka_ref, kb_ref, va_ref, vb_ref, o_ref):
    i = pl.program_id(1)
    lane = lax.broadcasted_iota(jnp.int32, (1, KV_DIM), 1)
    low = (lane % LANES) < HEAD_DIM
    zero = jnp.zeros((), BF16)
    for src, dst_a, dst_b in ((kp_ref, ka_ref, kb_ref), (vp_ref, va_ref, vb_ref)):
        t = src[...]
        dst_a[0:BLOCK, :] = jnp.where(low, t, zero)
        dst_b[0:BLOCK, :] = jnp.where(low, zero, t)
    for src, dst_a, dst_b in ((kc_ref, ka_ref, kb_ref), (vc_ref, va_ref, vb_ref)):
        t = src[...]
        dst_a[BLOCK:ATTN_BAND, :] = jnp.where(low, t, zero)
        dst_b[BLOCK:ATTN_BAND, :] = jnp.where(low, zero, t)

    lane1 = lax.broadcasted_iota(jnp.int32, (1, LANES), 1)
    row = lax.broadcasted_iota(jnp.int32, (Q_PER_KV * BLOCK, 1), 0)
    contract_last = (((1,), (1,)), ((), ()))
    for blk in range(ATTN_SUB):
        r0 = blk * BLOCK
        if blk == 0:
            bias = bias_ref[jnp.where(i == 0, 1, 0)]
        else:
            bias = bias_ref[0]
        for c in range(KV_PAIRS):
            cs = slice(c * LANES, (c + 1) * LANES)
            chunks = [c * Q_PER_KV + g for g in range(Q_PER_KV)]
            q4 = jnp.concatenate([q_ref[r0:r0 + BLOCK, m * LANES:(m + 1) * LANES] for m in chunks], axis=0)
            outs = []
            for half, (k_ref, v_ref) in enumerate(((ka_ref, va_ref), (kb_ref, vb_ref))):
                kband = k_ref[r0:r0 + 2 * BLOCK, cs]
                s = lax.dot_general(q4, kband, contract_last, preferred_element_type=F32) + bias
                sink = jnp.zeros((Q_PER_KV * BLOCK, 1), F32)
                for g, m in enumerate(chunks):
                    sink = jnp.where(row // BLOCK == g, sink_ref[HEAD_ORDER[2 * m + half]], sink)
                mx = jnp.maximum(jnp.max(s, axis=-1, keepdims=True), sink)
                p = jnp.exp(s - mx)
                denom = jnp.sum(p, axis=-1, keepdims=True) + jnp.exp(sink - mx)
                pv = jnp.dot(p.astype(BF16), v_ref[r0:r0 + 2 * BLOCK, cs], preferred_element_type=F32)
                outs.append((pv, 1.0 / denom))
            (pv_a, r_a), (pv_b, r_b) = outs
            o = (pv_a + pv_b) * jnp.where(lane1 < HEAD_DIM, r_a, r_b)
            for g, m in enumerate(chunks):
                o_ref[r0:r0 + BLOCK, m * LANES:(m + 1) * LANES] = o[g * BLOCK:(g + 1) * BLOCK].astype(BF16)

    mix = jnp.dot(o_ref[...], w_ref[...], preferred_element_type=F32)
    out_ref[...] = _layer_norm(DEEPNORM_ALPHA * h_ref[...] + mix, lg_ref[...], lb_ref[...])


def _attention_ln(q, k, v, sinks, h, w_out, lg, lb):
    tiles = SEQ // ATTN_Q_TILE
    cur = lambda b, i: (b * tiles + i, 0)
    prev = lambda b, i: (jnp.maximum((b * tiles + i) * ATTN_SUB - 1, 0), 0)
    const2 = lambda b, i: (0, 0)
    return pl.pallas_call(
        _attn_kernel,
        out_shape=jax.ShapeDtypeStruct((N_TOK, D_MODEL), F32),
        grid=(BATCH, tiles),
        in_specs=[pl.BlockSpec(memory_space=pltpu.SMEM),
                  pl.BlockSpec((ATTN_Q_TILE, Q_DIM), cur),
                  pl.BlockSpec((BLOCK, KV_DIM), prev), pl.BlockSpec((ATTN_Q_TILE, KV_DIM), cur),
                  pl.BlockSpec((BLOCK, KV_DIM), prev), pl.BlockSpec((ATTN_Q_TILE, KV_DIM), cur),
                  pl.BlockSpec((2, Q_PER_KV * BLOCK, 2 * BLOCK), lambda b, i: (0, 0, 0)),
                  pl.BlockSpec((ATTN_Q_TILE, D_MODEL), cur),
                  pl.BlockSpec((Q_DIM, D_MODEL), const2),
                  pl.BlockSpec((1, D_MODEL), const2), pl.BlockSpec((1, D_MODEL), const2)],
        out_specs=pl.BlockSpec((ATTN_Q_TILE, D_MODEL), cur),
        scratch_shapes=[pltpu.VMEM((ATTN_BAND, KV_DIM), BF16)] * 4
                       + [pltpu.VMEM((ATTN_Q_TILE, Q_DIM), BF16)],
        compiler_params=_cparams("parallel", "parallel"),
        name="swa_attention_ln",
    )(sinks, q, k, k, v, v, _band_bias(), h, w_out, lg, lb)


ROUTE_TILE = 512
EXPERT_TILE = 512
SLAB = D_MODEL // LANES
EXPERT_REGION = N_TOK
REGION_TILES = EXPERT_REGION // EXPERT_TILE
N_SORTED = N_EXPERTS * EXPERT_REGION
N_EXPERT_TILES = 2 * N_TOK // EXPERT_TILE + N_EXPERTS
META_E1, META_E2, META_R1, META_R2, META_W1, META_W2 = range(6)
META_D1, META_D2 = 8, 9


def _to_slabs(ref, val, rows):
    for c in range(SLAB):
        ref[pl.ds(c, rows, stride=SLAB), :] = val[:, c * LANES:(c + 1) * LANES]


def _from_slabs(ref, rows, first_row=0):
    return jnp.concatenate([ref[pl.ds(first_row * SLAB + c, rows, stride=SLAB), :] for c in range(SLAB)],
                           axis=1)


def _row_slab(ref, row):
    return ref.at[pl.ds(pl.multiple_of(row * SLAB, SLAB), SLAB)]


def _split_bf16(x):
    def top_bits(v):
        bits = lax.bitcast_convert_type(v, jnp.uint32) & jnp.uint32(0xFFFF0000)
        return lax.bitcast_convert_type(bits, F32)
    hi = top_bits(x)
    mid = top_bits(x - hi)
    lo = (x - hi) - mid
    return hi.astype(BF16), mid.astype(BF16), lo.astype(BF16)


def _router_kernel(h_ref, wr_ref, br_ref, tri_ref, meta_ref, counts_ref, xs_hbm,
                   carry_ref, slab_a, slab_b, zero_ref, dest_v, dest_s, count_v, count_s, sem, pad_sem, idx_sem):
    i = pl.program_id(0)

    @pl.when(i == 0)
    def _():
        carry_ref[...] = jnp.zeros_like(carry_ref)
        zero_ref[...] = jnp.zeros_like(zero_ref)

    lane = lax.broadcasted_iota(jnp.int32, (1, LANES), 1)
    lane_f = lane.astype(F32)
    h = h_ref[...]
    h0, h1, h2 = _split_bf16(h)
    w0, w1, w2 = wr_ref[0], wr_ref[1], wr_ref[2]
    dot = functools.partial(jnp.dot, preferred_element_type=F32)
    logits = ((dot(h2, w0) + dot(h0, w2) + dot(h1, w1)) + (dot(h1, w0) + dot(h0, w1)) + dot(h0, w0)
              + br_ref[...])
    logits = jnp.where(lane < N_EXPERTS, logits, -jnp.inf)
    m1 = jnp.max(logits, axis=-1, keepdims=True)
    i1 = jnp.min(jnp.where(logits == m1, lane_f, float(LANES)), axis=-1, keepdims=True)
    rest = jnp.where(lane_f == i1, -jnp.inf, logits)
    m2 = jnp.max(rest, axis=-1, keepdims=True)
    i2 = jnp.min(jnp.where(rest == m2, lane_f, float(LANES)), axis=-1, keepdims=True)
    e2 = jnp.exp(m2 - m1)
    tot = 1.0 + e2
    chosen = jnp.where((lane_f == i1) | (lane_f == i2), 1.0, 0.0)
    before = jnp.dot(tri_ref[...], chosen.astype(BF16), preferred_element_type=F32) + carry_ref[...]
    r1 = jnp.sum(jnp.where(lane_f == i1, before, 0.0), axis=-1, keepdims=True)
    r2 = jnp.sum(jnp.where(lane_f == i2, before, 0.0), axis=-1, keepdims=True)
    new_carry = carry_ref[...] + jnp.sum(chosen, axis=0, keepdims=True)
    carry_ref[...] = new_carry
    counts_ref[...] = new_carry.astype(jnp.int32)
    record = jnp.zeros((ROUTE_TILE, LANES), F32)
    for slot, col in ((META_E1, i1), (META_E2, i2), (META_R1, r1), (META_R2, r2),
                      (META_W1, 1.0 / tot), (META_W2, e2 / tot),
                      (META_D1, i1 * EXPERT_REGION + r1), (META_D2, i2 * EXPERT_REGION + r2)):
        record = jnp.where(lane == slot, col, record)
    meta_ref[...] = record

    dest_v[...] = record.T[META_D1:META_D1 + 8, :].astype(jnp.int32)
    to_smem = pltpu.make_async_copy(dest_v, dest_s, idx_sem)
    to_smem.start()
    to_smem.wait()
    slabs = (slab_a, slab_b)
    last = pl.num_programs(0) - 1

    def wait_scatter(slot):
        for _ in range(2):
            pltpu.make_async_copy(slabs[slot], xs_hbm.at[pl.ds(0, ROUTE_TILE * SLAB)], sem.at[slot]).wait()

    for slot in range(2):
        @pl.when(i % 2 == slot)
        def _():
            _to_slabs(slabs[slot], h, ROUTE_TILE)

            @pl.when(i > 0)
            def _():
                wait_scatter(1 - slot)

            def scatter(t, carry):
                for k in range(2):
                    pltpu.make_async_copy(_row_slab(slabs[slot], t), _row_slab(xs_hbm, dest_s[k, t]),
                                          sem.at[slot]).start(priority=k)
                return carry
            lax.fori_loop(0, ROUTE_TILE, scatter, 0, unroll=8)

            @pl.when(i == last)
            def _():
                wait_scatter(slot)

    @pl.when(i == last)
    def _():
        count_v[...] = jnp.broadcast_to(new_carry.astype(jnp.int32), count_v.shape)
        cp = pltpu.make_async_copy(count_v, count_s, idx_sem)
        cp.start()
        cp.wait()
        for e in range(N_EXPERTS):
            n = count_s[0, e]
            end = (n + EXPERT_TILE - 1) // EXPERT_TILE * EXPERT_TILE
            pad = lambda r: pltpu.make_async_copy(zero_ref, _row_slab(xs_hbm, e * EXPERT_REGION + r), pad_sem)
            lax.fori_loop(n, end, lambda r, c: (pad(r).start(), c)[1], 0)
            lax.fori_loop(n, end, lambda r, c: (pad(r).wait(), c)[1], 0)


def _router(h, w_router, b_router):
    tri = (jnp.arange(ROUTE_TILE)[:, None] > jnp.arange(ROUTE_TILE)[None, :]).astype(BF16)
    w_split = jnp.stack(_split_bf16(w_router))
    return pl.pallas_call(
        _router_kernel,
        out_shape=[jax.ShapeDtypeStruct((N_TOK, LANES), F32),
                   jax.ShapeDtypeStruct((1, LANES), jnp.int32),
                   jax.ShapeDtypeStruct((N_SORTED * SLAB, LANES), F32)],
        grid=(N_TOK // ROUTE_TILE,),
        in_specs=[_row_spec(ROUTE_TILE, D_MODEL), _const_spec(3, D_MODEL, LANES),
                  _const_spec(1, LANES), _const_spec(ROUTE_TILE, ROUTE_TILE)],
        out_specs=[_row_spec(ROUTE_TILE, LANES), _const_spec(1, LANES), pl.BlockSpec(memory_space=pl.ANY)],
        scratch_shapes=[pltpu.VMEM((1, LANES), F32),
                        pltpu.VMEM((ROUTE_TILE * SLAB, LANES), F32), pltpu.VMEM((ROUTE_TILE * SLAB, LANES), F32),
                        pltpu.VMEM((SLAB, LANES), F32),
                        pltpu.VMEM((8, ROUTE_TILE), jnp.int32), pltpu.SMEM((8, ROUTE_TILE), jnp.int32),
                        pltpu.VMEM((8, LANES), jnp.int32), pltpu.SMEM((8, LANES), jnp.int32),
                        pltpu.SemaphoreType.DMA((2,)), pltpu.SemaphoreType.DMA, pltpu.SemaphoreType.DMA],
        compiler_params=_cparams("arbitrary"),
        name="moe_router",
    )(h, w_split, b_router, tri)


def _experts_kernel(tblk_ref, texp_ref, xs_ref, wg_ref, wu_ref, wd_ref, ys_ref):
    x = _from_slabs(xs_ref, EXPERT_TILE).astype(BF16)
    gate = jnp.dot(x, wg_ref[0], preferred_element_type=F32)
    up = jnp.dot(x, wu_ref[0], preferred_element_type=F32)
    act = (jax.nn.silu(gate) * up).astype(BF16)
    _to_slabs(ys_ref, jnp.dot(act, wd_ref[0], preferred_element_type=F32), EXPERT_TILE)


def _experts(tile_block, tile_expert, xs, wg, wu, wd):
    wspec = lambda a, b: pl.BlockSpec((1, a, b), lambda i, tblk, texp: (texp[i], 0, 0))
    rows = pl.BlockSpec((EXPERT_TILE * SLAB, LANES), lambda i, tblk, texp: (tblk[i], 0))
    return pl.pallas_call(
        _experts_kernel,
        out_shape=jax.ShapeDtypeStruct((N_SORTED * SLAB, LANES), F32),
        grid_spec=pltpu.PrefetchScalarGridSpec(
            num_scalar_prefetch=2,
            grid=(N_EXPERT_TILES,),
            in_specs=[rows, wspec(D_MODEL, EXPERT_DIM), wspec(D_MODEL, EXPERT_DIM),
                      wspec(EXPERT_DIM, D_MODEL)],
            out_specs=rows),
        compiler_params=_cparams("arbitrary"),
        name="moe_experts",
    )(tile_block, tile_expert, xs, wg, wu, wd)


def _combine_ln_kernel(dest_ref, meta_ref, h_ref, ys_hbm, lg_ref, lb_ref, o_ref, ya_ref, yb_ref, sem):
    i = pl.program_id(0)
    bufs = (ya_ref, yb_ref)

    def start_gather(tile, slot):
        def body(t, carry):
            for k in range(2):
                row = dest_ref[k * N_TOK + tile * ROUTE_TILE + t]
                pltpu.make_async_copy(_row_slab(ys_hbm, row), _row_slab(bufs[slot], k * ROUTE_TILE + t),
                                      sem.at[slot]).start(priority=k)
            return carry
        lax.fori_loop(0, ROUTE_TILE, body, 0, unroll=8)

    def finish(slot):
        pltpu.make_async_copy(ys_hbm.at[pl.ds(0, 2 * ROUTE_TILE * SLAB)], bufs[slot], sem.at[slot]).wait()
        lane = lax.broadcasted_iota(jnp.int32, (1, LANES), 1)
        meta = meta_ref[...]
        w1 = jnp.sum(jnp.where(lane == META_W1, meta, 0.0), axis=-1, keepdims=True)
        w2 = jnp.sum(jnp.where(lane == META_W2, meta, 0.0), axis=-1, keepdims=True)
        ff = (w1 * _from_slabs(bufs[slot], ROUTE_TILE)
              + w2 * _from_slabs(bufs[slot], ROUTE_TILE, first_row=ROUTE_TILE))
        o_ref[...] = _layer_norm(DEEPNORM_ALPHA * h_ref[...] + ff, lg_ref[...], lb_ref[...])

    @pl.when(i == 0)
    def _():
        start_gather(0, 0)

    for slot in range(2):
        @pl.when((i + 1 < pl.num_programs(0)) & ((i + 1) % 2 == slot))
        def _():
            start_gather(i + 1, slot)

        @pl.when(i % 2 == slot)
        def _():
            finish(slot)


def _combine_ln(dest, meta, h, ys, lg, lb):
    row = lambda width: pl.BlockSpec((ROUTE_TILE, width), lambda i, dest: (i, 0))
    const = pl.BlockSpec((1, D_MODEL), lambda i, dest: (0, 0))
    return pl.pallas_call(
        _combine_ln_kernel,
        out_shape=jax.ShapeDtypeStruct((N_TOK, D_MODEL), F32),
        grid_spec=pltpu.PrefetchScalarGridSpec(
            num_scalar_prefetch=1,
            grid=(N_TOK // ROUTE_TILE,),
            in_specs=[row(LANES), row(D_MODEL), pl.BlockSpec(memory_space=pl.ANY), const, const],
            out_specs=row(D_MODEL),
            scratch_shapes=[pltpu.VMEM((2 * ROUTE_TILE * SLAB, LANES), F32)] * 2
                           + [pltpu.SemaphoreType.DMA((2,))]),
        compiler_params=_cparams("arbitrary"),
        name="moe_combine_ln",
    )(dest, meta, h, ys, lg, lb)


def _moe_ln(h, w_router, b_router, wg, wu, wd, lg, lb):
    meta, counts, xs = _router(h, w_router, b_router)
    expert = jnp.arange(N_EXPERTS)
    tiles = (counts[0, :N_EXPERTS] + EXPERT_TILE - 1) // EXPERT_TILE
    ends = jnp.sum(jnp.where(expert[None, :] <= expert[:, None], tiles[None, :], 0), axis=1)
    step = jnp.minimum(jnp.arange(N_EXPERT_TILES), ends[-1] - 1)
    tile_expert = jnp.sum(step[:, None] >= ends[None, :], axis=1)
    first = jnp.sum(jnp.where(tile_expert[:, None] == expert[None, :], (ends - tiles)[None, :], 0), axis=1)
    tile_block = tile_expert * REGION_TILES + (step - first)
    ys = _experts(tile_block.astype(jnp.int32), tile_expert.astype(jnp.int32), xs, wg, wu, wd)
    dest = jnp.concatenate([meta[:, META_D1], meta[:, META_D2]]).astype(jnp.int32)
    return _combine_ln(dest, meta, h, ys, lg, lb)


def kernel(x, positions, ln_g, ln_b, ssm_lambda_re, ssm_lambda_im, ssm_log_step, ssm_b_re, ssm_b_im, ssm_c_re, ssm_c_im, ssm_d, ssm_w_glu, kv_w, attn_w_q, attn_sinks, attn_w_out, ffn_w_gate, ffn_w_up, ffn_w_down, moe_w_router, moe_b_router, moe_w_gate, moe_w_up, moe_w_down):
    ln = lambda layer, j: (ln_g[layer, j].reshape(1, D_MODEL).astype(F32),
                           ln_b[layer, j].reshape(1, D_MODEL).astype(F32))
    cos_t, sin_t = _rope_tables(positions)
    x2 = x.reshape(N_TOK, D_MODEL)

    params = _ssm_params(ssm_lambda_re[0], ssm_lambda_im[0], ssm_log_step[0], ssm_b_re[0], ssm_b_im[0],
                         ssm_c_re[0], ssm_c_im[0])
    y = _ssm_mixer(x2, params)
    h = _glu_ln(y, x2, ssm_d[0].astype(F32).reshape(1, D_MODEL), ssm_w_glu[0].astype(BF16), *ln(0, 0))
    h = _ffn_ln(h, ffn_w_gate[0].astype(BF16), ffn_w_up[0].astype(BF16), ffn_w_down[0].astype(BF16),
                *ln(0, 1))

    order = jnp.array(HEAD_ORDER)
    w_q = attn_w_q[0].reshape(D_MODEL, N_HEADS, HEAD_DIM)[:, order].reshape(D_MODEL, Q_DIM)
    w_out = attn_w_out[0].reshape(N_HEADS, HEAD_DIM, D_MODEL)[order].reshape(Q_DIM, D_MODEL)
    w_qkv = jnp.concatenate([w_q, kv_w], axis=1).astype(BF16)
    q, k, v = _qkv(h, w_qkv, cos_t, sin_t)
    h = _attention_ln(q, k, v, attn_sinks[0].astype(F32), h, w_out.astype(BF16), *ln(1, 0))
    w_router = jnp.pad(moe_w_router[0].astype(F32), ((0, 0), (0, LANES - N_EXPERTS)))
    b_router = jnp.pad(moe_b_router[0].astype(F32), (0, LANES - N_EXPERTS)).reshape(1, LANES)
    h = _moe_ln(h, w_router, b_router, moe_w_gate[0].astype(BF16), moe_w_up[0].astype(BF16),
                moe_w_down[0].astype(BF16), *ln(1, 1))
    return h.reshape(BATCH, SEQ, D_MODEL)
```

```python
import functools
import math

import jax
import jax.numpy as jnp
from jax import lax
from jax.experimental import pallas as pl
from jax.experimental.pallas import tpu as pltpu

F32 = jnp.float32
BF16 = jnp.bfloat16

D_MODEL = 1024
BATCH = 16
SEQ = 2048
N_TOK = BATCH * SEQ
DEPTH = 2

SSM_GROUP_CH = 16
SSM_GROUPS = D_MODEL // SSM_GROUP_CH
SSM_STATE = 64
SSM_CHUNK = 16
SSM_NCHUNK = SEQ // SSM_CHUNK
SSM_ROWS = SSM_NCHUNK * BATCH
SSM_WIDTH = SSM_CHUNK * SSM_GROUP_CH
SSM_SCAN_STEPS = int(math.log2(SSM_NCHUNK))

N_HEADS = 16
HEAD_DIM = 64
N_KV_HEADS = 4
Q_PER_KV = N_HEADS // N_KV_HEADS
KV_DIM = N_KV_HEADS * HEAD_DIM
WINDOW = 128
BLOCK = 128
ROT_DIM = HEAD_DIM // 4
ROT_HALF = ROT_DIM // 2
ROPE_THETA = 500000.0

FFN_DIM = 2816
N_EXPERTS = 8
EXPERT_DIM = 1024

DEEPNORM_ALPHA = (2 * DEPTH) ** 0.25
LN_EPS = 1e-5

LANES = 128
VMEM_LIMIT = 56 * 1024 * 1024


def _cparams(*sem):
    return pltpu.CompilerParams(dimension_semantics=sem, vmem_limit_bytes=VMEM_LIMIT)


def _layer_norm(r, g, b):
    mu = jnp.mean(r, axis=-1, keepdims=True)
    xc = r - mu
    var = jnp.mean(xc * xc, axis=-1, keepdims=True)
    return xc * lax.rsqrt(var + LN_EPS) * g + b


def _rope_kernel(invf_ref, pos_ref, cos_ref, sin_ref):
    pos = pos_ref[...]
    for f in range(ROT_HALF):
        ang = pos * invf_ref[f]
        cos_ref[f] = jnp.cos(ang)
        sin_ref[f] = jnp.sin(ang)


def _rope_tables(positions):
    inv_freq = ROPE_THETA ** (-jnp.arange(0, ROT_DIM, 2, dtype=F32) / ROT_DIM)
    pos = positions.astype(F32)
    cos8, sin8 = pl.pallas_call(
        _rope_kernel,
        out_shape=[jax.ShapeDtypeStruct((ROT_HALF, BATCH, SEQ), F32)] * 2,
        in_specs=[pl.BlockSpec(memory_space=pltpu.SMEM),
                  pl.BlockSpec(memory_space=pltpu.VMEM)],
        out_specs=[pl.BlockSpec(memory_space=pltpu.VMEM)] * 2,
        name="rope_tables",
    )(inv_freq, pos)
    cos8 = cos8.transpose(1, 2, 0).reshape(N_TOK, ROT_HALF)
    sin8 = sin8.transpose(1, 2, 0).reshape(N_TOK, ROT_HALF)
    rest = HEAD_DIM - ROT_DIM
    cos_head = jnp.concatenate([cos8, cos8, jnp.ones((N_TOK, rest), F32)], axis=1)
    sin_head = jnp.concatenate([-sin8, sin8, jnp.zeros((N_TOK, rest), F32)], axis=1)
    reps = LANES // HEAD_DIM
    return jnp.tile(cos_head, (1, reps)), jnp.tile(sin_head, (1, reps))


def _ssm_params(lam_re, lam_im, log_step, b_re, b_im, c_re, c_im):
    hp = lax.Precision.HIGHEST
    lr, li = lam_re.astype(F32), lam_im.astype(F32)
    dt = jnp.exp(log_step.astype(F32))[:, None]
    mag = jnp.exp(lr * dt)
    ar = mag * jnp.cos(li * dt)
    ai = mag * jnp.sin(li * dt)
    nr = ar - 1.0
    den = lr * lr + li * li
    kr = (nr * lr + ai * li) / den
    ki = (ai * lr - nr * li) / den
    br, bi = b_re.astype(F32), b_im.astype(F32)
    bbar_r = kr[..., None] * br - ki[..., None] * bi
    bbar_i = kr[..., None] * bi + ki[..., None] * br
    cr, ci = c_re.astype(F32), c_im.astype(F32)

    def powers(taus):
        t = taus.astype(F32)[:, None, None]
        m = jnp.exp(lr[None] * dt[None] * t)
        ang = li[None] * dt[None] * t
        return m * jnp.cos(ang), m * jnp.sin(ang)

    er, ei = powers(jnp.arange(SSM_CHUNK + 1))
    w_r = er[:, :, :, None] * bbar_r[None] - ei[:, :, :, None] * bbar_i[None]
    w_i = er[:, :, :, None] * bbar_i[None] + ei[:, :, :, None] * bbar_r[None]
    kern = (jnp.einsum('gcp,tgpd->gtcd', cr, w_r[:SSM_CHUNK], precision=hp)
            - jnp.einsum('gcp,tgpd->gtcd', ci, w_i[:SSM_CHUNK], precision=hp))
    s_idx = jnp.arange(SSM_CHUNK)[:, None, None]
    t_idx = jnp.arange(SSM_CHUNK)[None, :, None]
    lag = (t_idx - s_idx == jnp.arange(SSM_CHUNK)[None, None, :]).astype(F32)
    toep = jnp.einsum('stu,gucd->gtcsd', lag, kern, precision=hp)
    mt = toep.reshape(SSM_GROUPS, SSM_WIDTH, SSM_WIDTH)
    rev = SSM_CHUNK - 1 - jnp.arange(SSM_CHUNK)
    g_r = w_r[rev].transpose(1, 2, 0, 3).reshape(SSM_GROUPS, SSM_STATE, SSM_WIDTH)
    g_i = w_i[rev].transpose(1, 2, 0, 3).reshape(SSM_GROUPS, SSM_STATE, SSM_WIDTH)
    gt = jnp.concatenate([g_r, g_i], axis=1)
    e1r, e1i = er[1:], ei[1:]
    ce_r = cr[None] * e1r[:, :, None, :] - ci[None] * e1i[:, :, None, :]
    ce_i = cr[None] * e1i[:, :, None, :] + ci[None] * e1r[:, :, None, :]
    c_re = ce_r.transpose(1, 0, 2, 3).reshape(SSM_GROUPS, SSM_WIDTH, SSM_STATE)
    c_im = (-ce_i).transpose(1, 0, 2, 3).reshape(SSM_GROUPS, SSM_WIDTH, SSM_STATE)
    ct = jnp.concatenate([c_re, c_im], axis=-1)
    sr, si = powers(SSM_CHUNK * (2 ** jnp.arange(8)))
    coef = lambda t: jnp.concatenate([t, t], axis=-1).transpose(1, 0, 2)
    return mt.astype(BF16), gt.astype(BF16), ct.astype(BF16), coef(sr), coef(si)


SSM_LANE_GROUPS = LANES // SSM_GROUP_CH
SSM_LANE_CHUNKS = D_MODEL // LANES
SSM_BATCH_TILE = 4
SSM_COLS = SSM_BATCH_TILE * SSM_NCHUNK


def _ssm_kernel(x_ref, mt_ref, gt_ref, ct_ref, cr_ref, ci_ref, y_ref, v_ref, yt_ref):
    for bl in range(SSM_BATCH_TILE):
        cols = slice(bl * SSM_NCHUNK, (bl + 1) * SSM_NCHUNK)
        for s in range(SSM_CHUNK):
            a = x_ref[pl.ds(bl * SEQ + s, SSM_NCHUNK, stride=SSM_CHUNK), :]
            at = a.T.astype(BF16)
            for g in range(SSM_LANE_GROUPS):
                v_ref[g, s * SSM_GROUP_CH:(s + 1) * SSM_GROUP_CH, cols] = (
                    at[g * SSM_GROUP_CH:(g + 1) * SSM_GROUP_CH])

    row_j = lax.broadcasted_iota(jnp.int32, (SSM_NCHUNK, 1), 0)

    def group_body(g, carry):
        ut = v_ref[g]
        xt = jnp.dot(gt_ref[g], ut, preferred_element_type=F32)
        cr_all = cr_ref[g]
        ci_all = ci_ref[g]

        def seg(rows, b):
            return xt[rows, b * SSM_NCHUNK:(b + 1) * SSM_NCHUNK]

        def shifted(x, d):
            if d < 8:
                return jnp.where(row_j >= d, pltpu.roll(x, d, 0), 0.0)
            return jnp.concatenate([jnp.zeros((d, LANES), F32), x[:SSM_NCHUNK - d]], axis=0)

        re_rows, im_rows = slice(0, SSM_STATE), slice(SSM_STATE, 2 * SSM_STATE)
        sr, si = [], []
        for q in range(SSM_BATCH_TILE // 2):
            xr = jnp.concatenate([seg(re_rows, 2 * q), seg(re_rows, 2 * q + 1)], axis=0).T
            xi = jnp.concatenate([seg(im_rows, 2 * q), seg(im_rows, 2 * q + 1)], axis=0).T
            for k in range(SSM_SCAN_STEPS):
                cr = cr_all[k:k + 1, :]
                ci = ci_all[k:k + 1, :]
                rr = shifted(xr, 1 << k)
                ri = shifted(xi, 1 << k)
                xr, xi = xr + cr * rr - ci * ri, xi + cr * ri + ci * rr
            pr = shifted(xr, 1).T
            pi = shifted(xi, 1).T
            sr += [pr[:SSM_STATE], pr[SSM_STATE:]]
            si += [pi[:SSM_STATE], pi[SSM_STATE:]]
        sprev = jnp.concatenate([jnp.concatenate(sr, axis=1), jnp.concatenate(si, axis=1)],
                                axis=0).astype(BF16)
        yt_ref[g] = (jnp.dot(mt_ref[g], ut, preferred_element_type=F32)
                     + jnp.dot(ct_ref[g], sprev, preferred_element_type=F32))
        return carry

    lax.fori_loop(0, SSM_LANE_GROUPS, group_body, 0, unroll=2)

    for bl in range(SSM_BATCH_TILE):
        cols = slice(bl * SSM_NCHUNK, (bl + 1) * SSM_NCHUNK)
        for t in range(SSM_CHUNK):
            tile = jnp.concatenate(
                [yt_ref[g, t * SSM_GROUP_CH:(t + 1) * SSM_GROUP_CH, cols] for g in range(SSM_LANE_GROUPS)],
                axis=0)
            y_ref[pl.ds(bl * SEQ + t, SSM_NCHUNK, stride=SSM_CHUNK), :] = tile.T


def _ssm_mixer(x2, params):
    mt, gt, ct, coef_r, coef_i = params
    rows = SSM_BATCH_TILE * SEQ
    pspec = lambda a, b: pl.BlockSpec((SSM_LANE_GROUPS, a, b), lambda k, q: (k, 0, 0))
    return pl.pallas_call(
        _ssm_kernel,
        out_shape=jax.ShapeDtypeStruct((N_TOK, D_MODEL), F32),
        grid=(SSM_LANE_CHUNKS, BATCH // SSM_BATCH_TILE),
        in_specs=[pl.BlockSpec((rows, LANES), lambda k, q: (q, k)),
                  pspec(SSM_WIDTH, SSM_WIDTH), pspec(2 * SSM_STATE, SSM_WIDTH), pspec(SSM_WIDTH, 2 * SSM_STATE),
                  pspec(8, 2 * SSM_STATE), pspec(8, 2 * SSM_STATE)],
        out_specs=pl.BlockSpec((rows, LANES), lambda k, q: (q, k)),
        scratch_shapes=[pltpu.VMEM((SSM_LANE_GROUPS, SSM_WIDTH, SSM_COLS), BF16),
                        pltpu.VMEM((SSM_LANE_GROUPS, SSM_WIDTH, SSM_COLS), F32)],
        compiler_params=_cparams("parallel", "parallel"),
        name="ssm_mixer",
    )(x2, mt, gt, ct, coef_r, coef_i)


def _glu_ln_kernel(y_ref, x_ref, d_ref, w_ref, lg_ref, lb_ref, o_ref):
    x = x_ref[...]
    act = jax.nn.gelu(y_ref[...] + d_ref[...] * x).astype(BF16)
    z = jnp.dot(act, w_ref[...], preferred_element_type=F32)
    mix = z[:, :D_MODEL] * jax.nn.sigmoid(z[:, D_MODEL:])
    o_ref[...] = _layer_norm(DEEPNORM_ALPHA * x + mix, lg_ref[...], lb_ref[...])


def _row_spec(tm, width):
    return pl.BlockSpec((tm, width), lambda i: (i, 0))


def _const_spec(*shape):
    return pl.BlockSpec(shape, lambda *_: (0,) * len(shape))


def _glu_ln(y, x2, d_skip, w_glu, lg, lb, tm=512):
    return pl.pallas_call(
        _glu_ln_kernel,
        out_shape=jax.ShapeDtypeStruct((N_TOK, D_MODEL), F32),
        grid=(N_TOK // tm,),
        in_specs=[_row_spec(tm, D_MODEL), _row_spec(tm, D_MODEL), _const_spec(1, D_MODEL),
                  _const_spec(D_MODEL, 2 * D_MODEL), _const_spec(1, D_MODEL), _const_spec(1, D_MODEL)],
        out_specs=_row_spec(tm, D_MODEL),
        compiler_params=_cparams("parallel"),
        name="glu_ln",
    )(y, x2, d_skip, w_glu, lg, lb)


FFN_SPLIT = 2
FFN_TILE = FFN_DIM // FFN_SPLIT


def _ffn_ln_kernel(h_ref, wg_ref, wu_ref, wd_ref, lg_ref, lb_ref, o_ref):
    h = h_ref[...]
    hb = h.astype(BF16)
    ff = None
    for f in range(FFN_SPLIT):
        cols = slice(f * FFN_TILE, (f + 1) * FFN_TILE)
        gate = jnp.dot(hb, wg_ref[:, cols], preferred_element_type=F32)
        up = jnp.dot(hb, wu_ref[:, cols], preferred_element_type=F32)
        act = (jax.nn.silu(gate) * up).astype(BF16)
        part = jnp.dot(act, wd_ref[cols, :], preferred_element_type=F32)
        ff = part if ff is None else ff + part
    o_ref[...] = _layer_norm(DEEPNORM_ALPHA * h + ff, lg_ref[...], lb_ref[...])


def _resident_spec(*shape):
    return pl.BlockSpec(shape, lambda *_: (0,) * len(shape), pipeline_mode=pl.Buffered(1))


def _ffn_ln(h, wg, wu, wd, lg, lb, tm=512):
    return pl.pallas_call(
        _ffn_ln_kernel,
        out_shape=jax.ShapeDtypeStruct((N_TOK, D_MODEL), F32),
        grid=(N_TOK // tm,),
        in_specs=[_row_spec(tm, D_MODEL),
                  _resident_spec(D_MODEL, FFN_DIM), _resident_spec(D_MODEL, FFN_DIM),
                  _resident_spec(FFN_DIM, D_MODEL),
                  _const_spec(1, D_MODEL), _const_spec(1, D_MODEL)],
        out_specs=_row_spec(tm, D_MODEL),
        compiler_params=_cparams("parallel"),
        name="ffn_ln",
    )(h, wg, wu, wd, lg, lb)


Q_DIM = N_HEADS * HEAD_DIM


def _qkv_kernel(h_ref, w_ref, cos_ref, sin_ref, q_ref, k_ref, v_ref):
    hb = h_ref[...].astype(BF16)
    z = jnp.dot(hb, w_ref[...], preferred_element_type=F32)
    cos_t = cos_ref[...]
    sin_t = sin_ref[...]
    lane = lax.broadcasted_iota(jnp.int32, (1, LANES), 1)
    first_half = (lane % HEAD_DIM) < ROT_HALF

    def rope(t):
        partner = jnp.where(first_half, pltpu.roll(t, LANES - ROT_HALF, 1), pltpu.roll(t, ROT_HALF, 1))
        return t * cos_t + partner * sin_t

    scale = HEAD_DIM ** -0.5
    for c in range(Q_DIM // LANES):
        sl = slice(c * LANES, (c + 1) * LANES)
        q_ref[:, sl] = (rope(z[:, sl]) * scale).astype(BF16)
    for c in range(KV_DIM // LANES):
        sl = slice(c * LANES, (c + 1) * LANES)
        k_ref[:, sl] = rope(z[:, Q_DIM + c * LANES:Q_DIM + (c + 1) * LANES]).astype(BF16)
    v_ref[...] = z[:, Q_DIM + KV_DIM:].astype(BF16)


def _qkv(h, w_qkv, cos_t, sin_t, tm=512):
    return pl.pallas_call(
        _qkv_kernel,
        out_shape=[jax.ShapeDtypeStruct((N_TOK, Q_DIM), BF16),
                   jax.ShapeDtypeStruct((N_TOK, KV_DIM), BF16),
                   jax.ShapeDtypeStruct((N_TOK, KV_DIM), BF16)],
        grid=(N_TOK // tm,),
        in_specs=[_row_spec(tm, D_MODEL), _const_spec(D_MODEL, Q_DIM + 2 * KV_DIM),
                  _row_spec(tm, LANES), _row_spec(tm, LANES)],
        out_specs=[_row_spec(tm, Q_DIM), _row_spec(tm, KV_DIM), _row_spec(tm, KV_DIM)],
        compiler_params=_cparams("parallel"),
        name="qkv_rope",
    )(h, w_qkv, cos_t, sin_t)


N_BLOCKS = SEQ // BLOCK


ATTN_Q_TILE = 512
ATTN_SUB = ATTN_Q_TILE // BLOCK
ATTN_BAND = ATTN_Q_TILE + BLOCK
KV_PAIRS = N_KV_HEADS // 2

HEAD_ORDER = [h for c in range(KV_PAIRS) for g in range(Q_PER_KV)
              for h in (2 * c * Q_PER_KV + g, (2 * c + 1) * Q_PER_KV + g)]


def _band_bias():
    qi = jnp.arange(Q_PER_KV * BLOCK)[:, None] % BLOCK
    si = jnp.arange(2 * BLOCK)[None, :]
    rel = qi + BLOCK - si
    valid = (rel >= 0) & (rel < WINDOW)
    first = valid & (si >= BLOCK)
    neg = jnp.float32(-jnp.inf)
    return jnp.stack([jnp.where(valid, 0.0, neg), jnp.where(first, 0.0, neg)]).astype(F32)


def _attn_kernel(sink_ref, q_ref, kp_ref, kc_ref, vp_ref, vc_ref, bias_ref, h_ref, w_ref, lg_ref, lb_ref,
                 out_ref, ka_ref, kb_ref, va_ref, vb_ref, o_ref):
    i = pl.program_id(1)
    lane = lax.broadcasted_iota(jnp.int32, (1, KV_DIM), 1)
    low = (lane % LANES) < HEAD_DIM
    zero = jnp.zeros((), BF16)
    for src, dst_a, dst_b in ((kp_ref, ka_ref, kb_ref), (vp_ref, va_ref, vb_ref)):
        t = src[...]
        dst_a[0:BLOCK, :] = jnp.where(low, t, zero)
        dst_b[0:BLOCK, :] = jnp.where(low, zero, t)
    for src, dst_a, dst_b in ((kc_ref, ka_ref, kb_ref), (vc_ref, va_ref, vb_ref)):
        t = src[...]
        dst_a[BLOCK:ATTN_BAND, :] = jnp.where(low, t, zero)
        dst_b[BLOCK:ATTN_BAND, :] = jnp.where(low, zero, t)

    lane1 = lax.broadcasted_iota(jnp.int32, (1, LANES), 1)
    row = lax.broadcasted_iota(jnp.int32, (Q_PER_KV * BLOCK, 1), 0)
    contract_last = (((1,), (1,)), ((), ()))
    for blk in range(ATTN_SUB):
        r0 = blk * BLOCK
        if blk == 0:
            bias = bias_ref[jnp.where(i == 0, 1, 0)]
        else:
            bias = bias_ref[0]
        for c in range(KV_PAIRS):
            cs = slice(c * LANES, (c + 1) * LANES)
            chunks = [c * Q_PER_KV + g for g in range(Q_PER_KV)]
            q4 = jnp.concatenate([q_ref[r0:r0 + BLOCK, m * LANES:(m + 1) * LANES] for m in chunks], axis=0)
            outs = []
            for half, (k_ref, v_ref) in enumerate(((ka_ref, va_ref), (kb_ref, vb_ref))):
                kband = k_ref[r0:r0 + 2 * BLOCK, cs]
                s = lax.dot_general(q4, kband, contract_last, preferred_element_type=F32) + bias
                sink = jnp.zeros((Q_PER_KV * BLOCK, 1), F32)
                for g, m in enumerate(chunks):
                    sink = jnp.where(row // BLOCK == g, sink_ref[HEAD_ORDER[2 * m + half]], sink)
                mx = jnp.maximum(jnp.max(s, axis=-1, keepdims=True), sink)
                p = jnp.exp(s - mx)
                denom = jnp.sum(p, axis=-1, keepdims=True) + jnp.exp(sink - mx)
                pv = jnp.dot(p.astype(BF16), v_ref[r0:r0 + 2 * BLOCK, cs], preferred_element_type=F32)
                outs.append((pv, 1.0 / denom))
            (pv_a, r_a), (pv_b, r_b) = outs
            o = (pv_a + pv_b) * jnp.where(lane1 < HEAD_DIM, r_a, r_b)
            for g, m in enumerate(chunks):
                o_ref[r0:r0 + BLOCK, m * LANES:(m + 1) * LANES] = o[g * BLOCK:(g + 1) * BLOCK].astype(BF16)

    mix = jnp.dot(o_ref[...], w_ref[...], preferred_element_type=F32)
    out_ref[...] = _layer_norm(DEEPNORM_ALPHA * h_ref[...] + mix, lg_ref[...], lb_ref[...])


def _attention_ln(q, k, v, sinks, h, w_out, lg, lb):
    tiles = SEQ // ATTN_Q_TILE
    cur = lambda b, i: (b * tiles + i, 0)
    prev = lambda b, i: (jnp.maximum((b * tiles + i) * ATTN_SUB - 1, 0), 0)
    const2 = lambda b, i: (0, 0)
    return pl.pallas_call(
        _attn_kernel,
        out_shape=jax.ShapeDtypeStruct((N_TOK, D_MODEL), F32),
        grid=(BATCH, tiles),
        in_specs=[pl.BlockSpec(memory_space=pltpu.SMEM),
                  pl.BlockSpec((ATTN_Q_TILE, Q_DIM), cur),
                  pl.BlockSpec((BLOCK, KV_DIM), prev), pl.BlockSpec((ATTN_Q_TILE, KV_DIM), cur),
                  pl.BlockSpec((BLOCK, KV_DIM), prev), pl.BlockSpec((ATTN_Q_TILE, KV_DIM), cur),
                  pl.BlockSpec((2, Q_PER_KV * BLOCK, 2 * BLOCK), lambda b, i: (0, 0, 0)),
                  pl.BlockSpec((ATTN_Q_TILE, D_MODEL), cur),
                  pl.BlockSpec((Q_DIM, D_MODEL), const2),
                  pl.BlockSpec((1, D_MODEL), const2), pl.BlockSpec((1, D_MODEL), const2)],
        out_specs=pl.BlockSpec((ATTN_Q_TILE, D_MODEL), cur),
        scratch_shapes=[pltpu.VMEM((ATTN_BAND, KV_DIM), BF16)] * 4
                       + [pltpu.VMEM((ATTN_Q_TILE, Q_DIM), BF16)],
        compiler_params=_cparams("parallel", "parallel"),
        name="swa_attention_ln",
    )(sinks, q, k, k, v, v, _band_bias(), h, w_out, lg, lb)


ROUTE_TILE = 512
EXPERT_TILE = 512
SLAB = D_MODEL // LANES
EXPERT_REGION = N_TOK
REGION_TILES = EXPERT_REGION // EXPERT_TILE
N_SORTED = N_EXPERTS * EXPERT_REGION
N_EXPERT_TILES = 2 * N_TOK // EXPERT_TILE + N_EXPERTS
META_E1, META_E2, META_R1, META_R2, META_W1, META_W2 = range(6)
META_D1, META_D2 = 8, 9


def _to_slabs(ref, val, rows):
    for c in range(SLAB):
        ref[pl.ds(c, rows, stride=SLAB), :] = val[:, c * LANES:(c + 1) * LANES]


def _from_slabs(ref, rows, first_row=0):
    return jnp.concatenate([ref[pl.ds(first_row * SLAB + c, rows, stride=SLAB), :] for c in range(SLAB)],
                           axis=1)


def _row_slab(ref, row):
    return ref.at[pl.ds(pl.multiple_of(row * SLAB, SLAB), SLAB)]


def _split_bf16(x):
    def top_bits(v):
        bits = lax.bitcast_convert_type(v, jnp.uint32) & jnp.uint32(0xFFFF0000)
        return lax.bitcast_convert_type(bits, F32)
    hi = top_bits(x)
    mid = top_bits(x - hi)
    lo = (x - hi) - mid
    return hi.astype(BF16), mid.astype(BF16), lo.astype(BF16)


def _router_kernel(h_ref, wr_ref, br_ref, tri_ref, meta_ref, counts_ref, xs_hbm,
                   carry_ref, slab_a, slab_b, zero_ref, dest_v, dest_s, count_v, count_s, sem, pad_sem, idx_sem):
    i = pl.program_id(0)

    @pl.when(i == 0)
    def _():
        carry_ref[...] = jnp.zeros_like(carry_ref)
        zero_ref[...] = jnp.zeros_like(zero_ref)

    lane = lax.broadcasted_iota(jnp.int32, (1, LANES), 1)
    lane_f = lane.astype(F32)
    h = h_ref[...]
    h0, h1, h2 = _split_bf16(h)
    w0, w1, w2 = wr_ref[0], wr_ref[1], wr_ref[2]
    dot = functools.partial(jnp.dot, preferred_element_type=F32)
    logits = ((dot(h2, w0) + dot(h0, w2) + dot(h1, w1)) + (dot(h1, w0) + dot(h0, w1)) + dot(h0, w0)
              + br_ref[...])
    logits = jnp.where(lane < N_EXPERTS, logits, -jnp.inf)
    m1 = jnp.max(logits, axis=-1, keepdims=True)
    i1 = jnp.min(jnp.where(logits == m1, lane_f, float(LANES)), axis=-1, keepdims=True)
    rest = jnp.where(lane_f == i1, -jnp.inf, logits)
    m2 = jnp.max(rest, axis=-1, keepdims=True)
    i2 = jnp.min(jnp.where(rest == m2, lane_f, float(LANES)), axis=-1, keepdims=True)
    e2 = jnp.exp(m2 - m1)
    tot = 1.0 + e2
    chosen = jnp.where((lane_f == i1) | (lane_f == i2), 1.0, 0.0)
    before = jnp.dot(tri_ref[...], chosen.astype(BF16), preferred_element_type=F32) + carry_ref[...]
    r1 = jnp.sum(jnp.where(lane_f == i1, before, 0.0), axis=-1, keepdims=True)
    r2 = jnp.sum(jnp.where(lane_f == i2, before, 0.0), axis=-1, keepdims=True)
    new_carry = carry_ref[...] + jnp.sum(chosen, axis=0, keepdims=True)
    carry_ref[...] = new_carry
    counts_ref[...] = new_carry.astype(jnp.int32)
    record = jnp.zeros((ROUTE_TILE, LANES), F32)
    for slot, col in ((META_E1, i1), (META_E2, i2), (META_R1, r1), (META_R2, r2),
                      (META_W1, 1.0 / tot), (META_W2, e2 / tot),
                      (META_D1, i1 * EXPERT_REGION + r1), (META_D2, i2 * EXPERT_REGION + r2)):
        record = jnp.where(lane == slot, col, record)
    meta_ref[...] = record

    dest_v[...] = record.T[META_D1:META_D1 + 8, :].astype(jnp.int32)
    to_smem = pltpu.make_async_copy(dest_v, dest_s, idx_sem)
    to_smem.start()
    to_smem.wait()
    slabs = (slab_a, slab_b)
    last = pl.num_programs(0) - 1

    def wait_scatter(slot):
        for _ in range(2):
            pltpu.make_async_copy(slabs[slot], xs_hbm.at[pl.ds(0, ROUTE_TILE * SLAB)], sem.at[slot]).wait()

    for slot in range(2):
        @pl.when(i % 2 == slot)
        def _():
            _to_slabs(slabs[slot], h, ROUTE_TILE)

            @pl.when(i > 0)
            def _():
                wait_scatter(1 - slot)

            def scatter(t, carry):
                for k in range(2):
                    pltpu.make_async_copy(_row_slab(slabs[slot], t), _row_slab(xs_hbm, dest_s[k, t]),
                                          sem.at[slot]).start(priority=k)
                return carry
            lax.fori_loop(0, ROUTE_TILE, scatter, 0, unroll=8)

            @pl.when(i == last)
            def _():
                wait_scatter(slot)

    @pl.when(i == last)
    def _():
        count_v[...] = jnp.broadcast_to(new_carry.astype(jnp.int32), count_v.shape)
        cp = pltpu.make_async_copy(count_v, count_s, idx_sem)
        cp.start()
        cp.wait()
        for e in range(N_EXPERTS):
            n = count_s[0, e]
            end = (n + EXPERT_TILE - 1) // EXPERT_TILE * EXPERT_TILE
            pad = lambda r: pltpu.make_async_copy(zero_ref, _row_slab(xs_hbm, e * EXPERT_REGION + r), pad_sem)
            lax.fori_loop(n, end, lambda r, c: (pad(r).start(), c)[1], 0)
            lax.fori_loop(n, end, lambda r, c: (pad(r).wait(), c)[1], 0)


def _router(h, w_router, b_router):
    tri = (jnp.arange(ROUTE_TILE)[:, None] > jnp.arange(ROUTE_TILE)[None, :]).astype(BF16)
    w_split = jnp.stack(_split_bf16(w_router))
    return pl.pallas_call(
        _router_kernel,
        out_shape=[jax.ShapeDtypeStruct((N_TOK, LANES), F32),
                   jax.ShapeDtypeStruct((1, LANES), jnp.int32),
                   jax.ShapeDtypeStruct((N_SORTED * SLAB, LANES), F32)],
        grid=(N_TOK // ROUTE_TILE,),
        in_specs=[_row_spec(ROUTE_TILE, D_MODEL), _const_spec(3, D_MODEL, LANES),
                  _const_spec(1, LANES), _const_spec(ROUTE_TILE, ROUTE_TILE)],
        out_specs=[_row_spec(ROUTE_TILE, LANES), _const_spec(1, LANES), pl.BlockSpec(memory_space=pl.ANY)],
        scratch_shapes=[pltpu.VMEM((1, LANES), F32),
                        pltpu.VMEM((ROUTE_TILE * SLAB, LANES), F32), pltpu.VMEM((ROUTE_TILE * SLAB, LANES), F32),
                        pltpu.VMEM((SLAB, LANES), F32),
                        pltpu.VMEM((8, ROUTE_TILE), jnp.int32), pltpu.SMEM((8, ROUTE_TILE), jnp.int32),
                        pltpu.VMEM((8, LANES), jnp.int32), pltpu.SMEM((8, LANES), jnp.int32),
                        pltpu.SemaphoreType.DMA((2,)), pltpu.SemaphoreType.DMA, pltpu.SemaphoreType.DMA],
        compiler_params=_cparams("arbitrary"),
        name="moe_router",
    )(h, w_split, b_router, tri)


def _experts_kernel(tblk_ref, texp_ref, xs_ref, wg_ref, wu_ref, wd_ref, ys_ref):
    x = _from_slabs(xs_ref, EXPERT_TILE).astype(BF16)
    gate = jnp.dot(x, wg_ref[0], preferred_element_type=F32)
    up = jnp.dot(x, wu_ref[0], preferred_element_type=F32)
    act = (jax.nn.silu(gate) * up).astype(BF16)
    _to_slabs(ys_ref, jnp.dot(act, wd_ref[0], preferred_element_type=F32), EXPERT_TILE)


def _experts(tile_block, tile_expert, xs, wg, wu, wd):
    wspec = lambda a, b: pl.BlockSpec((1, a, b), lambda i, tblk, texp: (texp[i], 0, 0))
    rows = pl.BlockSpec((EXPERT_TILE * SLAB, LANES), lambda i, tblk, texp: (tblk[i], 0))
    return pl.pallas_call(
        _experts_kernel,
        out_shape=jax.ShapeDtypeStruct((N_SORTED * SLAB, LANES), F32),
        grid_spec=pltpu.PrefetchScalarGridSpec(
            num_scalar_prefetch=2,
            grid=(N_EXPERT_TILES,),
            in_specs=[rows, wspec(D_MODEL, EXPERT_DIM), wspec(D_MODEL, EXPERT_DIM),
                      wspec(EXPERT_DIM, D_MODEL)],
            out_specs=rows),
        compiler_params=_cparams("arbitrary"),
        name="moe_experts",
    )(tile_block, tile_expert, xs, wg, wu, wd)


def _combine_ln_kernel(dest_ref, meta_ref, h_ref, ys_hbm, lg_ref, lb_ref, o_ref, ya_ref, yb_ref, sem):
    i = pl.program_id(0)
    bufs = (ya_ref, yb_ref)

    def start_gather(tile, slot):
        def body(t, carry):
            for k in range(2):
                row = dest_ref[k * N_TOK + tile * ROUTE_TILE + t]
                pltpu.make_async_copy(_row_slab(ys_hbm, row), _row_slab(bufs[slot], k * ROUTE_TILE + t),
                                      sem.at[slot]).start(priority=k)
            return carry
        lax.fori_loop(0, ROUTE_TILE, body, 0, unroll=8)

    def finish(slot):
        pltpu.make_async_copy(ys_hbm.at[pl.ds(0, 2 * ROUTE_TILE * SLAB)], bufs[slot], sem.at[slot]).wait()
        lane = lax.broadcasted_iota(jnp.int32, (1, LANES), 1)
        meta = meta_ref[...]
        w1 = jnp.sum(jnp.where(lane == META_W1, meta, 0.0), axis=-1, keepdims=True)
        w2 = jnp.sum(jnp.where(lane == META_W2, meta, 0.0), axis=-1, keepdims=True)
        ff = (w1 * _from_slabs(bufs[slot], ROUTE_TILE)
              + w2 * _from_slabs(bufs[slot], ROUTE_TILE, first_row=ROUTE_TILE))
        o_ref[...] = _layer_norm(DEEPNORM_ALPHA * h_ref[...] + ff, lg_ref[...], lb_ref[...])

    @pl.when(i == 0)
    def _():
        start_gather(0, 0)

    for slot in range(2):
        @pl.when((i + 1 < pl.num_programs(0)) & ((i + 1) % 2 == slot))
        def _():
            start_gather(i + 1, slot)

        @pl.when(i % 2 == slot)
        def _():
            finish(slot)


def _combine_ln(dest, meta, h, ys, lg, lb):
    row = lambda width: pl.BlockSpec((ROUTE_TILE, width), lambda i, dest: (i, 0))
    const = pl.BlockSpec((1, D_MODEL), lambda i, dest: (0, 0))
    return pl.pallas_call(
        _combine_ln_kernel,
        out_shape=jax.ShapeDtypeStruct((N_TOK, D_MODEL), F32),
        grid_spec=pltpu.PrefetchScalarGridSpec(
            num_scalar_prefetch=1,
            grid=(N_TOK // ROUTE_TILE,),
            in_specs=[row(LANES), row(D_MODEL), pl.BlockSpec(memory_space=pl.ANY), const, const],
            out_specs=row(D_MODEL),
            scratch_shapes=[pltpu.VMEM((2 * ROUTE_TILE * SLAB, LANES), F32)] * 2
                           + [pltpu.SemaphoreType.DMA((2,))]),
        compiler_params=_cparams("arbitrary"),
        name="moe_combine_ln",
    )(dest, meta, h, ys, lg, lb)


def _moe_ln(h, w_router, b_router, wg, wu, wd, lg, lb):
    meta, counts, xs = _router(h, w_router, b_router)
    expert = jnp.arange(N_EXPERTS)
    tiles = (counts[0, :N_EXPERTS] + EXPERT_TILE - 1) // EXPERT_TILE
    ends = jnp.sum(jnp.where(expert[None, :] <= expert[:, None], tiles[None, :], 0), axis=1)
    step = jnp.minimum(jnp.arange(N_EXPERT_TILES), ends[-1] - 1)
    tile_expert = jnp.sum(step[:, None] >= ends[None, :], axis=1)
    first = jnp.sum(jnp.where(tile_expert[:, None] == expert[None, :], (ends - tiles)[None, :], 0), axis=1)
    tile_block = tile_expert * REGION_TILES + (step - first)
    ys = _experts(tile_block.astype(jnp.int32), tile_expert.astype(jnp.int32), xs, wg, wu, wd)
    dest = jnp.concatenate([meta[:, META_D1], meta[:, META_D2]]).astype(jnp.int32)
    return _combine_ln(dest, meta, h, ys, lg, lb)


def kernel(x, positions, ln_g, ln_b, ssm_lambda_re, ssm_lambda_im, ssm_log_step, ssm_b_re, ssm_b_im, ssm_c_re, ssm_c_im, ssm_d, ssm_w_glu, kv_w, attn_w_q, attn_sinks, attn_w_out, ffn_w_gate, ffn_w_up, ffn_w_down, moe_w_router, moe_b_router, moe_w_gate, moe_w_up, moe_w_down):
    ln = lambda layer, j: (ln_g[layer, j].reshape(1, D_MODEL).astype(F32),
                           ln_b[layer, j].reshape(1, D_MODEL).astype(F32))
    cos_t, sin_t = _rope_tables(positions)
    x2 = x.reshape(N_TOK, D_MODEL)

    params = _ssm_params(ssm_lambda_re[0], ssm_lambda_im[0], ssm_log_step[0], ssm_b_re[0], ssm_b_im[0],
                         ssm_c_re[0], ssm_c_im[0])
    y = _ssm_mixer(x2, params)
    h = _glu_ln(y, x2, ssm_d[0].astype(F32).reshape(1, D_MODEL), ssm_w_glu[0].astype(BF16), *ln(0, 0))
    h = _ffn_ln(h, ffn_w_gate[0].astype(BF16), ffn_w_up[0].astype(BF16), ffn_w_down[0].astype(BF16),
                *ln(0, 1))

    order = jnp.array(HEAD_ORDER)
    w_q = attn_w_q[0].reshape(D_MODEL, N_HEADS, HEAD_DIM)[:, order].reshape(D_MODEL, Q_DIM)
    w_out = attn_w_out[0].reshape(N_HEADS, HEAD_DIM, D_MODEL)[order].reshape(Q_DIM, D_MODEL)
    w_qkv = jnp.concatenate([w_q, kv_w], axis=1).astype(BF16)
    q, k, v = _qkv(h, w_qkv, cos_t, sin_t)
    h = _attention_ln(q, k, v, attn_sinks[0].astype(F32), h, w_out.astype(BF16), *ln(1, 0))
    w_router = jnp.pad(moe_w_router[0].astype(F32), ((0, 0), (0, LANES - N_EXPERTS)))
    b_router = jnp.pad(moe_b_router[0].astype(F32), (0, LANES - N_EXPERTS)).reshape(1, LANES)
    h = _moe_ln(h, w_router, b_router, moe_w_gate[0].astype(BF16), moe_w_up[0].astype(BF16),
                moe_w_down[0].astype(BF16), *ln(1, 1))
    return h.reshape(BATCH, SEQ, D_MODEL)
```

```python
import functools
import math

import jax
import jax.numpy as jnp
from jax import lax
from jax.experimental import pallas as pl
from jax.experimental.pallas import tpu as pltpu

F32 = jnp.float32
BF16 = jnp.bfloat16

D_MODEL = 1024
BATCH = 16
SEQ = 2048
N_TOK = BATCH * SEQ
DEPTH = 2

SSM_GROUP_CH = 16
SSM_GROUPS = D_MODEL // SSM_GROUP_CH
SSM_STATE = 64
SSM_CHUNK = 16
SSM_NCHUNK = SEQ // SSM_CHUNK
SSM_ROWS = SSM_NCHUNK * BATCH
SSM_WIDTH = SSM_CHUNK * SSM_GROUP_CH
SSM_SCAN_STEPS = int(math.log2(SSM_NCHUNK))

N_HEADS = 16
HEAD_DIM = 64
N_KV_HEADS = 4
Q_PER_KV = N_HEADS // N_KV_HEADS
KV_DIM = N_KV_HEADS * HEAD_DIM
WINDOW = 128
BLOCK = 128
ROT_DIM = HEAD_DIM // 4
ROT_HALF = ROT_DIM // 2
ROPE_THETA = 500000.0

FFN_DIM = 2816
N_EXPERTS = 8
EXPERT_DIM = 1024

DEEPNORM_ALPHA = (2 * DEPTH) ** 0.25
LN_EPS = 1e-5

LANES = 128
VMEM_LIMIT = 56 * 1024 * 1024


def _cparams(*sem):
    return pltpu.CompilerParams(dimension_semantics=sem, vmem_limit_bytes=VMEM_LIMIT)


def _layer_norm(r, g, b):
    mu = jnp.mean(r, axis=-1, keepdims=True)
    xc = r - mu
    var = jnp.mean(xc * xc, axis=-1, keepdims=True)
    return xc * lax.rsqrt(var + LN_EPS) * g + b


def _rope_kernel(invf_ref, pos_ref, cos_ref, sin_ref):
    pos = pos_ref[...]
    for f in range(ROT_HALF):
        ang = pos * invf_ref[f]
        cos_ref[f] = jnp.cos(ang)
        sin_ref[f] = jnp.sin(ang)


def _rope_tables(positions):
    inv_freq = ROPE_THETA ** (-jnp.arange(0, ROT_DIM, 2, dtype=F32) / ROT_DIM)
    pos = positions.astype(F32)
    cos8, sin8 = pl.pallas_call(
        _rope_kernel,
        out_shape=[jax.ShapeDtypeStruct((ROT_HALF, BATCH, SEQ), F32)] * 2,
        in_specs=[pl.BlockSpec(memory_space=pltpu.SMEM),
                  pl.BlockSpec(memory_space=pltpu.VMEM)],
        out_specs=[pl.BlockSpec(memory_space=pltpu.VMEM)] * 2,
        name="rope_tables",
    )(inv_freq, pos)
    cos8 = cos8.transpose(1, 2, 0).reshape(N_TOK, ROT_HALF)
    sin8 = sin8.transpose(1, 2, 0).reshape(N_TOK, ROT_HALF)
    rest = HEAD_DIM - ROT_DIM
    cos_head = jnp.concatenate([cos8, cos8, jnp.ones((N_TOK, rest), F32)], axis=1)
    sin_head = jnp.concatenate([-sin8, sin8, jnp.zeros((N_TOK, rest), F32)], axis=1)
    reps = LANES // HEAD_DIM
    return jnp.tile(cos_head, (1, reps)), jnp.tile(sin_head, (1, reps))


def _ssm_params(lam_re, lam_im, log_step, b_re, b_im, c_re, c_im):
    hp = lax.Precision.HIGHEST
    lr, li = lam_re.astype(F32), lam_im.astype(F32)
    dt = jnp.exp(log_step.astype(F32))[:, None]
    mag = jnp.exp(lr * dt)
    ar = mag * jnp.cos(li * dt)
    ai = mag * jnp.sin(li * dt)
    nr = ar - 1.0
    den = lr * lr + li * li
    kr = (nr * lr + ai * li) / den
    ki = (ai * lr - nr * li) / den
    br, bi = b_re.astype(F32), b_im.astype(F32)
    bbar_r = kr[..., None] * br - ki[..., None] * bi
    bbar_i = kr[..., None] * bi + ki[..., None] * br
    cr, ci = c_re.astype(F32), c_im.astype(F32)

    def powers(taus):
        t = taus.astype(F32)[:, None, None]
        m = jnp.exp(lr[None] * dt[None] * t)
        ang = li[None] * dt[None] * t
        return m * jnp.cos(ang), m * jnp.sin(ang)

    er, ei = powers(jnp.arange(SSM_CHUNK + 1))
    w_r = er[:, :, :, None] * bbar_r[None] - ei[:, :, :, None] * bbar_i[None]
    w_i = er[:, :, :, None] * bbar_i[None] + ei[:, :, :, None] * bbar_r[None]
    kern = (jnp.einsum('gcp,tgpd->gtcd', cr, w_r[:SSM_CHUNK], precision=hp)
            - jnp.einsum('gcp,tgpd->gtcd', ci, w_i[:SSM_CHUNK], precision=hp))
    s_idx = jnp.arange(SSM_CHUNK)[:, None, None]
    t_idx = jnp.arange(SSM_CHUNK)[None, :, None]
    lag = (t_idx - s_idx == jnp.arange(SSM_CHUNK)[None, None, :]).astype(F32)
    toep = jnp.einsum('stu,gucd->gtcsd', lag, kern, precision=hp)
    mt = toep.reshape(SSM_GROUPS, SSM_WIDTH, SSM_WIDTH)
    rev = SSM_CHUNK - 1 - jnp.arange(SSM_CHUNK)
    g_r = w_r[rev].transpose(1, 2, 0, 3).reshape(SSM_GROUPS, SSM_STATE, SSM_WIDTH)
    g_i = w_i[rev].transpose(1, 2, 0, 3).reshape(SSM_GROUPS, SSM_STATE, SSM_WIDTH)
    gt = jnp.concatenate([g_r, g_i], axis=1)
    e1r, e1i = er[1:], ei[1:]
    ce_r = cr[None] * e1r[:, :, None, :] - ci[None] * e1i[:, :, None, :]
    ce_i = cr[None] * e1i[:, :, None, :] + ci[None] * e1r[:, :, None, :]
    c_re = ce_r.transpose(1, 0, 2, 3).reshape(SSM_GROUPS, SSM_WIDTH, SSM_STATE)
    c_im = (-ce_i).transpose(1, 0, 2, 3).reshape(SSM_GROUPS, SSM_WIDTH, SSM_STATE)
    ct = jnp.concatenate([c_re, c_im], axis=-1)
    sr, si = powers(SSM_CHUNK * (2 ** jnp.arange(8)))
    coef = lambda t: jnp.concatenate([t, t], axis=-1).transpose(1, 0, 2)
    return mt.astype(BF16), gt.astype(BF16), ct.astype(BF16), coef(sr), coef(si)


SSM_LANE_GROUPS = LANES // SSM_GROUP_CH
SSM_LANE_CHUNKS = D_MODEL // LANES
SSM_BATCH_TILE = 4
SSM_COLS = SSM_BATCH_TILE * SSM_NCHUNK


def _ssm_kernel(x_ref, mt_ref, gt_ref, ct_ref, cr_ref, ci_ref, y_ref, v_ref, yt_ref):
    for bl in range(SSM_BATCH_TILE):
        cols = slice(bl * SSM_NCHUNK, (bl + 1) * SSM_NCHUNK)
        for s in range(SSM_CHUNK):
            a = x_ref[pl.ds(bl * SEQ + s, SSM_NCHUNK, stride=SSM_CHUNK), :]
            at = a.T.astype(BF16)
            for g in range(SSM_LANE_GROUPS):
                v_ref[g, s * SSM_GROUP_CH:(s + 1) * SSM_GROUP_CH, cols] = (
                    at[g * SSM_GROUP_CH:(g + 1) * SSM_GROUP_CH])

    row_j = lax.broadcasted_iota(jnp.int32, (SSM_NCHUNK, 1), 0)

    def group_body(g, carry):
        ut = v_ref[g]
        xt = jnp.dot(gt_ref[g], ut, preferred_element_type=F32)
        cr_all = cr_ref[g]
        ci_all = ci_ref[g]

        def seg(rows, b):
            return xt[rows, b * SSM_NCHUNK:(b + 1) * SSM_NCHUNK]

        def shifted(x, d):
            if d < 8:
                return jnp.where(row_j >= d, pltpu.roll(x, d, 0), 0.0)
            return jnp.concatenate([jnp.zeros((d, LANES), F32), x[:SSM_NCHUNK - d]], axis=0)

        re_rows, im_rows = slice(0, SSM_STATE), slice(SSM_STATE, 2 * SSM_STATE)
        sr, si = [], []
        for q in range(SSM_BATCH_TILE // 2):
            xr = jnp.concatenate([seg(re_rows, 2 * q), seg(re_rows, 2 * q + 1)], axis=0).T
            xi = jnp.concatenate([seg(im_rows, 2 * q), seg(im_rows, 2 * q + 1)], axis=0).T
            for k in range(SSM_SCAN_STEPS):
                cr = cr_all[k:k + 1, :]
                ci = ci_all[k:k + 1, :]
                rr = shifted(xr, 1 << k)
                ri = shifted(xi, 1 << k)
                xr, xi = xr + cr * rr - ci * ri, xi + cr * ri + ci * rr
            pr = shifted(xr, 1).T
            pi = shifted(xi, 1).T
            sr += [pr[:SSM_STATE], pr[SSM_STATE:]]
            si += [pi[:SSM_STATE], pi[SSM_STATE:]]
        sprev = jnp.concatenate([jnp.concatenate(sr, axis=1), jnp.concatenate(si, axis=1)],
                                axis=0).astype(BF16)
        yt_ref[g] = (jnp.dot(mt_ref[g], ut, preferred_element_type=F32)
                     + jnp.dot(ct_ref[g], sprev, preferred_element_type=F32))
        return carry

    lax.fori_loop(0, SSM_LANE_GROUPS, group_body, 0, unroll=2)

    for bl in range(SSM_BATCH_TILE):
        cols = slice(bl * SSM_NCHUNK, (bl + 1) * SSM_NCHUNK)
        for t in range(SSM_CHUNK):
            tile = jnp.concatenate(
                [yt_ref[g, t * SSM_GROUP_CH:(t + 1) * SSM_GROUP_CH, cols] for g in range(SSM_LANE_GROUPS)],
                axis=0)
            y_ref[pl.ds(bl * SEQ + t, SSM_NCHUNK, stride=SSM_CHUNK), :] = tile.T


def _ssm_mixer(x2, params):
    mt, gt, ct, coef_r, coef_i = params
    rows = SSM_BATCH_TILE * SEQ
    pspec = lambda a, b: pl.BlockSpec((SSM_LANE_GROUPS, a, b), lambda k, q: (k, 0, 0))
    return pl.pallas_call(
        _ssm_kernel,
        out_shape=jax.ShapeDtypeStruct((N_TOK, D_MODEL), F32),
        grid=(SSM_LANE_CHUNKS, BATCH // SSM_BATCH_TILE),
        in_specs=[pl.BlockSpec((rows, LANES), lambda k, q: (q, k)),
                  pspec(SSM_WIDTH, SSM_WIDTH), pspec(2 * SSM_STATE, SSM_WIDTH), pspec(SSM_WIDTH, 2 * SSM_STATE),
                  pspec(8, 2 * SSM_STATE), pspec(8, 2 * SSM_STATE)],
        out_specs=pl.BlockSpec((rows, LANES), lambda k, q: (q, k)),
        scratch_shapes=[pltpu.VMEM((SSM_LANE_GROUPS, SSM_WIDTH, SSM_COLS), BF16),
                        pltpu.VMEM((SSM_LANE_GROUPS, SSM_WIDTH, SSM_COLS), F32)],
        compiler_params=_cparams("parallel", "parallel"),
        name="ssm_mixer",
    )(x2, mt, gt, ct, coef_r, coef_i)


def _glu_ln_kernel(y_ref, x_ref, d_ref, w_ref, lg_ref, lb_ref, o_ref):
    x = x_ref[...]
    act = jax.nn.gelu(y_ref[...] + d_ref[...] * x).astype(BF16)
    z = jnp.dot(act, w_ref[...], preferred_element_type=F32)
    mix = z[:, :D_MODEL] * jax.nn.sigmoid(z[:, D_MODEL:])
    o_ref[...] = _layer_norm(DEEPNORM_ALPHA * x + mix, lg_ref[...], lb_ref[...])


def _row_spec(tm, width):
    return pl.BlockSpec((tm, width), lambda i: (i, 0))


def _const_spec(*shape):
    return pl.BlockSpec(shape, lambda *_: (0,) * len(shape))


def _glu_ln(y, x2, d_skip, w_glu, lg, lb, tm=512):
    return pl.pallas_call(
        _glu_ln_kernel,
        out_shape=jax.ShapeDtypeStruct((N_TOK, D_MODEL), F32),
        grid=(N_TOK // tm,),
        in_specs=[_row_spec(tm, D_MODEL), _row_spec(tm, D_MODEL), _const_spec(1, D_MODEL),
                  _const_spec(D_MODEL, 2 * D_MODEL), _const_spec(1, D_MODEL), _const_spec(1, D_MODEL)],
        out_specs=_row_spec(tm, D_MODEL),
        compiler_params=_cparams("parallel"),
        name="glu_ln",
    )(y, x2, d_skip, w_glu, lg, lb)


FFN_SPLIT = 2
FFN_TILE = FFN_DIM // FFN_SPLIT


def _ffn_ln_kernel(h_ref, wg_ref, wu_ref, wd_ref, lg_ref, lb_ref, o_ref):
    h = h_ref[...]
    hb = h.astype(BF16)
    ff = None
    for f in range(FFN_SPLIT):
        cols = slice(f * FFN_TILE, (f + 1) * FFN_TILE)
        gate = jnp.dot(hb, wg_ref[:, cols], preferred_element_type=F32)
        up = jnp.dot(hb, wu_ref[:, cols], preferred_element_type=F32)
        act = (jax.nn.silu(gate) * up).astype(BF16)
        part = jnp.dot(act, wd_ref[cols, :], preferred_element_type=F32)
        ff = part if ff is None else ff + part
    o_ref[...] = _layer_norm(DEEPNORM_ALPHA * h + ff, lg_ref[...], lb_ref[...])


def _resident_spec(*shape):
    return pl.BlockSpec(shape, lambda *_: (0,) * len(shape), pipeline_mode=pl.Buffered(1))


def _ffn_ln(h, wg, wu, wd, lg, lb, tm=512):
    return pl.pallas_call(
        _ffn_ln_kernel,
        out_shape=jax.ShapeDtypeStruct((N_TOK, D_MODEL), F32),
        grid=(N_TOK // tm,),
        in_specs=[_row_spec(tm, D_MODEL),
                  _resident_spec(D_MODEL, FFN_DIM), _resident_spec(D_MODEL, FFN_DIM),
                  _resident_spec(FFN_DIM, D_MODEL),
                  _const_spec(1, D_MODEL), _const_spec(1, D_MODEL)],
        out_specs=_row_spec(tm, D_MODEL),
        compiler_params=_cparams("parallel"),
        name="ffn_ln",
    )(h, wg, wu, wd, lg, lb)


Q_DIM = N_HEADS * HEAD_DIM


def _qkv_kernel(h_ref, w_ref, cos_ref, sin_ref, q_ref, k_ref, v_ref):
    hb = h_ref[...].astype(BF16)
    z = jnp.dot(hb, w_ref[...], preferred_element_type=F32)
    cos_t = cos_ref[...]
    sin_t = sin_ref[...]
    lane = lax.broadcasted_iota(jnp.int32, (1, LANES), 1)
    first_half = (lane % HEAD_DIM) < ROT_HALF

    def rope(t):
        partner = jnp.where(first_half, pltpu.roll(t, LANES - ROT_HALF, 1), pltpu.roll(t, ROT_HALF, 1))
        return t * cos_t + partner * sin_t

    scale = HEAD_DIM ** -0.5
    for c in range(Q_DIM // LANES):
        sl = slice(c * LANES, (c + 1) * LANES)
        q_ref[:, sl] = (rope(z[:, sl]) * scale).astype(BF16)
    for c in range(KV_DIM // LANES):
        sl = slice(c * LANES, (c + 1) * LANES)
        k_ref[:, sl] = rope(z[:, Q_DIM + c * LANES:Q_DIM + (c + 1) * LANES]).astype(BF16)
    v_ref[...] = z[:, Q_DIM + KV_DIM:].astype(BF16)


def _qkv(h, w_qkv, cos_t, sin_t, tm=512):
    return pl.pallas_call(
        _qkv_kernel,
        out_shape=[jax.ShapeDtypeStruct((N_TOK, Q_DIM), BF16),
                   jax.ShapeDtypeStruct((N_TOK, KV_DIM), BF16),
                   jax.ShapeDtypeStruct((N_TOK, KV_DIM), BF16)],
        grid=(N_TOK // tm,),
        in_specs=[_row_spec(tm, D_MODEL), _const_spec(D_MODEL, Q_DIM + 2 * KV_DIM),
                  _row_spec(tm, LANES), _row_spec(tm, LANES)],
        out_specs=[_row_spec(tm, Q_DIM), _row_spec(tm, KV_DIM), _row_spec(tm, KV_DIM)],
        compiler_params=_cparams("parallel"),
        name="qkv_rope",
    )(h, w_qkv, cos_t, sin_t)


N_BLOCKS = SEQ // BLOCK


ATTN_Q_TILE = 512
ATTN_SUB = ATTN_Q_TILE // BLOCK
ATTN_BAND = ATTN_Q_TILE + BLOCK
KV_PAIRS = N_KV_HEADS // 2

HEAD_ORDER = [h for c in range(KV_PAIRS) for g in range(Q_PER_KV)
              for h in (2 * c * Q_PER_KV + g, (2 * c + 1) * Q_PER_KV + g)]


def _band_bias():
    qi = jnp.arange(Q_PER_KV * BLOCK)[:, None] % BLOCK
    si = jnp.arange(2 * BLOCK)[None, :]
    rel = qi + BLOCK - si
    valid = (rel >= 0) & (rel < WINDOW)
    first = valid & (si >= BLOCK)
    neg = jnp.float32(-jnp.inf)
    return jnp.stack([jnp.where(valid, 0.0, neg), jnp.where(first, 0.0, neg)]).astype(F32)


def _attn_kernel(sink_ref, q_ref, kp_ref, kc_ref, vp_ref, vc_ref, bias_ref, h_ref, w_ref, lg_ref, lb_ref,
                 out_ref, ka_ref, kb_ref, va_ref, vb_ref, o_ref):
    i = pl.program_id(1)
    lane = lax.broadcasted_iota(jnp.int32, (1, KV_DIM), 1)
    low = (lane % LANES) < HEAD_DIM
    zero = jnp.zeros((), BF16)
    for src, dst_a, dst_b in ((kp_ref, ka_ref, kb_ref), (vp_ref, va_ref, vb_ref)):
        t = src[...]
        dst_a[0:BLOCK, :] = jnp.where(low, t, zero)
        dst_b[0:BLOCK, :] = jnp.where(low, zero, t)
    for src, dst_a, dst_b in ((kc_ref, ka_ref, kb_ref), (vc_ref, va_ref, vb_ref)):
        t = src[...]
        dst_a[BLOCK:ATTN_BAND, :] = jnp.where(low, t, zero)
        dst_b[BLOCK:ATTN_BAND, :] = jnp.where(low, zero, t)

    lane1 = lax.broadcasted_iota(jnp.int32, (1, LANES), 1)
    row = lax.broadcasted_iota(jnp.int32, (Q_PER_KV * BLOCK, 1), 0)
    contract_last = (((1,), (1,)), ((), ()))
    for blk in range(ATTN_SUB):
        r0 = blk * BLOCK
        if blk == 0:
            bias = bias_ref[jnp.where(i == 0, 1, 0)]
        else:
            bias = bias_ref[0]
        for c in range(KV_PAIRS):
            cs = slice(c * LANES, (c + 1) * LANES)
            chunks = [c * Q_PER_KV + g for g in range(Q_PER_KV)]
            q4 = jnp.concatenate([q_ref[r0:r0 + BLOCK, m * LANES:(m + 1) * LANES] for m in chunks], axis=0)
            outs = []
            for half, (k_ref, v_ref) in enumerate(((ka_ref, va_ref), (kb_ref, vb_ref))):
                kband = k_ref[r0:r0 + 2 * BLOCK, cs]
                s = lax.dot_general(q4, kband, contract_last, preferred_element_type=F32) + bias
                sink = jnp.zeros((Q_PER_KV * BLOCK, 1), F32)
                for g, m in enumerate(chunks):
                    sink = jnp.where(row // BLOCK == g, sink_ref[HEAD_ORDER[2 * m + half]], sink)
                mx = jnp.maximum(jnp.max(s, axis=-1, keepdims=True), sink)
                p = jnp.exp(s - mx)
                denom = jnp.sum(p, axis=-1, keepdims=True) + jnp.exp(sink - mx)
                pv = jnp.dot(p.astype(BF16), v_ref[r0:r0 + 2 * BLOCK, cs], preferred_element_type=F32)
                outs.append((pv, 1.0 / denom))
            (pv_a, r_a), (pv_b, r_b) = outs
            o = (pv_a + pv_b) * jnp.where(lane1 < HEAD_DIM, r_a, r_b)
            for g, m in enumerate(chunks):
                o_ref[r0:r0 + BLOCK, m * LANES:(m + 1) * LANES] = o[g * BLOCK:(g + 1) * BLOCK].astype(BF16)

    mix = jnp.dot(o_ref[...], w_ref[...], preferred_element_type=F32)
    out_ref[...] = _layer_norm(DEEPNORM_ALPHA * h_ref[...] + mix, lg_ref[...], lb_ref[...])


def _attention_ln(q, k, v, sinks, h, w_out, lg, lb):
    tiles = SEQ // ATTN_Q_TILE
    cur = lambda b, i: (b * tiles + i, 0)
    prev = lambda b, i: (jnp.maximum((b * tiles + i) * ATTN_SUB - 1, 0), 0)
    const2 = lambda b, i: (0, 0)
    return pl.pallas_call(
        _attn_kernel,
        out_shape=jax.ShapeDtypeStruct((N_TOK, D_MODEL), F32),
        grid=(BATCH, tiles),
        in_specs=[pl.BlockSpec(memory_space=pltpu.SMEM),
                  pl.BlockSpec((ATTN_Q_TILE, Q_DIM), cur),
                  pl.BlockSpec((BLOCK, KV_DIM), prev), pl.BlockSpec((ATTN_Q_TILE, KV_DIM), cur),
                  pl.BlockSpec((BLOCK, KV_DIM), prev), pl.BlockSpec((ATTN_Q_TILE, KV_DIM), cur),
                  pl.BlockSpec((2, Q_PER_KV * BLOCK, 2 * BLOCK), lambda b, i: (0, 0, 0)),
                  pl.BlockSpec((ATTN_Q_TILE, D_MODEL), cur),
                  pl.BlockSpec((Q_DIM, D_MODEL), const2),
                  pl.BlockSpec((1, D_MODEL), const2), pl.BlockSpec((1, D_MODEL), const2)],
        out_specs=pl.BlockSpec((ATTN_Q_TILE, D_MODEL), cur),
        scratch_shapes=[pltpu.VMEM((ATTN_BAND, KV_DIM), BF16)] * 4
                       + [pltpu.VMEM((ATTN_Q_TILE, Q_DIM), BF16)],
        compiler_params=_cparams("parallel", "parallel"),
        name="swa_attention_ln",
    )(sinks, q, k, k, v, v, _band_bias(), h, w_out, lg, lb)


ROUTE_TILE = 512
EXPERT_TILE = 512
SLAB = D_MODEL // LANES
EXPERT_REGION = N_TOK
REGION_TILES = EXPERT_REGION // EXPERT_TILE
N_SORTED = N_EXPERTS * EXPERT_REGION
N_EXPERT_TILES = 2 * N_TOK // EXPERT_TILE + N_EXPERTS
META_E1, META_E2, META_R1, META_R2, META_W1, META_W2 = range(6)
META_D1, META_D2 = 8, 9
COMBINE_CHUNK = 64


def _to_slabs(ref, val, rows):
    for c in range(SLAB):
        ref[pl.ds(c, rows, stride=SLAB), :] = val[:, c * LANES:(c + 1) * LANES]


def _from_slabs(ref, rows, first_row=0):
    return jnp.concatenate([ref[pl.ds(first_row * SLAB + c, rows, stride=SLAB), :] for c in range(SLAB)],
                           axis=1)


def _row_slab(ref, row):
    return ref.at[pl.ds(pl.multiple_of(row * SLAB, SLAB), SLAB)]


def _split_bf16(x):
    def top_bits(v):
        bits = lax.bitcast_convert_type(v, jnp.uint32) & jnp.uint32(0xFFFF0000)
        return lax.bitcast_convert_type(bits, F32)
    hi = top_bits(x)
    mid = top_bits(x - hi)
    lo = (x - hi) - mid
    return hi.astype(BF16), mid.astype(BF16), lo.astype(BF16)


def _router_kernel(h_ref, wr_ref, br_ref, tri_ref, meta_ref, counts_ref, xs_hbm,
                   carry_ref, slab_a, slab_b, zero_ref, dest_v, dest_s, count_v, count_s, sem, pad_sem, idx_sem):
    i = pl.program_id(0)

    @pl.when(i == 0)
    def _():
        carry_ref[...] = jnp.zeros_like(carry_ref)
        zero_ref[...] = jnp.zeros_like(zero_ref)

    lane = lax.broadcasted_iota(jnp.int32, (1, LANES), 1)
    lane_f = lane.astype(F32)
    h = h_ref[...]
    h0, h1, h2 = _split_bf16(h)
    w0, w1, w2 = wr_ref[0], wr_ref[1], wr_ref[2]
    dot = functools.partial(jnp.dot, preferred_element_type=F32)
    logits = ((dot(h2, w0) + dot(h0, w2) + dot(h1, w1)) + (dot(h1, w0) + dot(h0, w1)) + dot(h0, w0)
              + br_ref[...])
    logits = jnp.where(lane < N_EXPERTS, logits, -jnp.inf)
    m1 = jnp.max(logits, axis=-1, keepdims=True)
    i1 = jnp.min(jnp.where(logits == m1, lane_f, float(LANES)), axis=-1, keepdims=True)
    rest = jnp.where(lane_f == i1, -jnp.inf, logits)
    m2 = jnp.max(rest, axis=-1, keepdims=True)
    i2 = jnp.min(jnp.where(rest == m2, lane_f, float(LANES)), axis=-1, keepdims=True)
    e2 = jnp.exp(m2 - m1)
    tot = 1.0 + e2
    chosen = jnp.where((lane_f == i1) | (lane_f == i2), 1.0, 0.0)
    before = jnp.dot(tri_ref[...], chosen.astype(BF16), preferred_element_type=F32) + carry_ref[...]
    r1 = jnp.sum(jnp.where(lane_f == i1, before, 0.0), axis=-1, keepdims=True)
    r2 = jnp.sum(jnp.where(lane_f == i2, before, 0.0), axis=-1, keepdims=True)
    new_carry = carry_ref[...] + jnp.sum(chosen, axis=0, keepdims=True)
    carry_ref[...] = new_carry
    counts_ref[...] = new_carry.astype(jnp.int32)
    record = jnp.zeros((ROUTE_TILE, LANES), F32)
    for slot, col in ((META_E1, i1), (META_E2, i2), (META_R1, r1), (META_R2, r2),
                      (META_W1, 1.0 / tot), (META_W2, e2 / tot),
                      (META_D1, i1 * EXPERT_REGION + r1), (META_D2, i2 * EXPERT_REGION + r2)):
        record = jnp.where(lane == slot, col, record)
    meta_ref[...] = record

    dest_v[...] = record.T[META_D1:META_D1 + 8, :].astype(jnp.int32)
    to_smem = pltpu.make_async_copy(dest_v, dest_s, idx_sem)
    to_smem.start()
    to_smem.wait()
    slabs = (slab_a, slab_b)
    last = pl.num_programs(0) - 1

    def wait_scatter(slot):
        for _ in range(2):
            pltpu.make_async_copy(slabs[slot], xs_hbm.at[pl.ds(0, ROUTE_TILE * SLAB)], sem.at[slot]).wait()

    for slot in range(2):
        @pl.when(i % 2 == slot)
        def _():
            _to_slabs(slabs[slot], h, ROUTE_TILE)

            @pl.when(i > 0)
            def _():
                wait_scatter(1 - slot)

            def scatter(t, carry):
                for k in range(2):
                    pltpu.make_async_copy(_row_slab(slabs[slot], t), _row_slab(xs_hbm, dest_s[k, t]),
                                          sem.at[slot]).start(priority=k)
                return carry
            lax.fori_loop(0, ROUTE_TILE, scatter, 0, unroll=8)

            @pl.when(i == last)
            def _():
                wait_scatter(slot)

    @pl.when(i == last)
    def _():
        count_v[...] = jnp.broadcast_to(new_carry.astype(jnp.int32), count_v.shape)
        cp = pltpu.make_async_copy(count_v, count_s, idx_sem)
        cp.start()
        cp.wait()
        for e in range(N_EXPERTS):
            n = count_s[0, e]
            end = (n + EXPERT_TILE - 1) // EXPERT_TILE * EXPERT_TILE
            pad = lambda r: pltpu.make_async_copy(zero_ref, _row_slab(xs_hbm, e * EXPERT_REGION + r), pad_sem)
            lax.fori_loop(n, end, lambda r, c: (pad(r).start(), c)[1], 0)
            lax.fori_loop(n, end, lambda r, c: (pad(r).wait(), c)[1], 0)


def _router(h, w_router, b_router):
    tri = (jnp.arange(ROUTE_TILE)[:, None] > jnp.arange(ROUTE_TILE)[None, :]).astype(BF16)
    w_split = jnp.stack(_split_bf16(w_router))
    return pl.pallas_call(
        _router_kernel,
        out_shape=[jax.ShapeDtypeStruct((N_TOK, LANES), F32),
                   jax.ShapeDtypeStruct((1, LANES), jnp.int32),
                   jax.ShapeDtypeStruct((N_SORTED * SLAB, LANES), F32)],
        grid=(N_TOK // ROUTE_TILE,),
        in_specs=[_row_spec(ROUTE_TILE, D_MODEL), _const_spec(3, D_MODEL, LANES),
                  _const_spec(1, LANES), _const_spec(ROUTE_TILE, ROUTE_TILE)],
        out_specs=[_row_spec(ROUTE_TILE, LANES), _const_spec(1, LANES), pl.BlockSpec(memory_space=pl.ANY)],
        scratch_shapes=[pltpu.VMEM((1, LANES), F32),
                        pltpu.VMEM((ROUTE_TILE * SLAB, LANES), F32), pltpu.VMEM((ROUTE_TILE * SLAB, LANES), F32),
                        pltpu.VMEM((SLAB, LANES), F32),
                        pltpu.VMEM((8, ROUTE_TILE), jnp.int32), pltpu.SMEM((8, ROUTE_TILE), jnp.int32),
                        pltpu.VMEM((8, LANES), jnp.int32), pltpu.SMEM((8, LANES), jnp.int32),
                        pltpu.SemaphoreType.DMA((2,)), pltpu.SemaphoreType.DMA, pltpu.SemaphoreType.DMA],
        compiler_params=_cparams("arbitrary"),
        name="moe_router",
    )(h, w_split, b_router, tri)


def _experts_kernel(tblk_ref, texp_ref, xs_ref, wg_ref, wu_ref, wd_ref, ys_ref):
    x = _from_slabs(xs_ref, EXPERT_TILE).astype(BF16)
    gate = jnp.dot(x, wg_ref[0], preferred_element_type=F32)
    up = jnp.dot(x, wu_ref[0], preferred_element_type=F32)
    act = (jax.nn.silu(gate) * up).astype(BF16)
    _to_slabs(ys_ref, jnp.dot(act, wd_ref[0], preferred_element_type=F32), EXPERT_TILE)


def _experts(tile_block, tile_expert, xs, wg, wu, wd):
    wspec = lambda a, b: pl.BlockSpec((1, a, b), lambda i, tblk, texp: (texp[i], 0, 0))
    rows = pl.BlockSpec((EXPERT_TILE * SLAB, LANES), lambda i, tblk, texp: (tblk[i], 0))
    return pl.pallas_call(
        _experts_kernel,
        out_shape=jax.ShapeDtypeStruct((N_SORTED * SLAB, LANES), F32),
        grid_spec=pltpu.PrefetchScalarGridSpec(
            num_scalar_prefetch=2,
            grid=(N_EXPERT_TILES,),
            in_specs=[rows, wspec(D_MODEL, EXPERT_DIM), wspec(D_MODEL, EXPERT_DIM),
                      wspec(EXPERT_DIM, D_MODEL)],
            out_specs=rows),
        compiler_params=_cparams("arbitrary"),
        name="moe_experts",
    )(tile_block, tile_expert, xs, wg, wu, wd)


def _combine_ln_kernel(dest_ref, meta_ref, h_ref, ys_hbm, lg_ref, lb_ref, o_ref, ya_ref, yb_ref, sem):
    i = pl.program_id(0)
    bufs = (ya_ref, yb_ref)

    def start_gather(tile, slot):
        def body(t, carry):
            for k in range(2):
                row = dest_ref[k * N_TOK + tile * ROUTE_TILE + t]
                pltpu.make_async_copy(_row_slab(ys_hbm, row), _row_slab(bufs[slot], k * ROUTE_TILE + t),
                                      sem.at[slot]).start(priority=k)
            return carry
        lax.fori_loop(0, ROUTE_TILE, body, 0, unroll=8)

    def wait_gather(slot):
        pltpu.make_async_copy(ys_hbm.at[pl.ds(0, 2 * ROUTE_TILE * SLAB)], bufs[slot], sem.at[slot]).wait()

    last = pl.num_programs(0) - 1
    lane = lax.broadcasted_iota(jnp.int32, (1, LANES), 1)

    def finish_and_prefetch(slot):
        nxt = jnp.minimum(i + 1, last) * ROUTE_TILE
        wait_gather(slot)
        for r0 in range(0, ROUTE_TILE, COMBINE_CHUNK):
            rows = slice(r0, r0 + COMBINE_CHUNK)
            meta = meta_ref[rows, :]
            w1 = jnp.sum(jnp.where(lane == META_W1, meta, 0.0), axis=-1, keepdims=True)
            w2 = jnp.sum(jnp.where(lane == META_W2, meta, 0.0), axis=-1, keepdims=True)
            ff = (w1 * _from_slabs(bufs[slot], COMBINE_CHUNK, first_row=r0)
                  + w2 * _from_slabs(bufs[slot], COMBINE_CHUNK, first_row=ROUTE_TILE + r0))
            o_ref[rows, :] = _layer_norm(DEEPNORM_ALPHA * h_ref[rows, :] + ff, lg_ref[...], lb_ref[...])
            for t in range(r0, r0 + COMBINE_CHUNK):
                for k in range(2):
                    row = dest_ref[k * N_TOK + nxt + t]
                    pltpu.make_async_copy(_row_slab(ys_hbm, row),
                                          _row_slab(bufs[1 - slot], k * ROUTE_TILE + t),
                                          sem.at[1 - slot]).start(priority=k)

        @pl.when(i == last)
        def _():
            wait_gather(1 - slot)

    @pl.when(i == 0)
    def _():
        start_gather(0, 0)

    for slot in range(2):
        @pl.when(i % 2 == slot)
        def _():
            finish_and_prefetch(slot)


def _combine_ln(dest, meta, h, ys, lg, lb):
    row = lambda width: pl.BlockSpec((ROUTE_TILE, width), lambda i, dest: (i, 0))
    const = pl.BlockSpec((1, D_MODEL), lambda i, dest: (0, 0))
    return pl.pallas_call(
        _combine_ln_kernel,
        out_shape=jax.ShapeDtypeStruct((N_TOK, D_MODEL), F32),
        grid_spec=pltpu.PrefetchScalarGridSpec(
            num_scalar_prefetch=1,
            grid=(N_TOK // ROUTE_TILE,),
            in_specs=[row(LANES), row(D_MODEL), pl.BlockSpec(memory_space=pl.ANY), const, const],
            out_specs=row(D_MODEL),
            scratch_shapes=[pltpu.VMEM((2 * ROUTE_TILE * SLAB, LANES), F32)] * 2
                           + [pltpu.SemaphoreType.DMA((2,))]),
        compiler_params=_cparams("arbitrary"),
        name="moe_combine_ln",
    )(dest, meta, h, ys, lg, lb)


def _moe_ln(h, w_router, b_router, wg, wu, wd, lg, lb):
    meta, counts, xs = _router(h, w_router, b_router)
    expert = jnp.arange(N_EXPERTS)
    tiles = (counts[0, :N_EXPERTS] + EXPERT_TILE - 1) // EXPERT_TILE
    ends = jnp.sum(jnp.where(expert[None, :] <= expert[:, None], tiles[None, :], 0), axis=1)
    step = jnp.minimum(jnp.arange(N_EXPERT_TILES), ends[-1] - 1)
    tile_expert = jnp.sum(step[:, None] >= ends[None, :], axis=1)
    first = jnp.sum(jnp.where(tile_expert[:, None] == expert[None, :], (ends - tiles)[None, :], 0), axis=1)
    tile_block = tile_expert * REGION_TILES + (step - first)
    ys = _experts(tile_block.astype(jnp.int32), tile_expert.astype(jnp.int32), xs, wg, wu, wd)
    dest = jnp.concatenate([meta[:, META_D1], meta[:, META_D2]]).astype(jnp.int32)
    return _combine_ln(dest, meta, h, ys, lg, lb)


def kernel(x, positions, ln_g, ln_b, ssm_lambda_re, ssm_lambda_im, ssm_log_step, ssm_b_re, ssm_b_im, ssm_c_re, ssm_c_im, ssm_d, ssm_w_glu, kv_w, attn_w_q, attn_sinks, attn_w_out, ffn_w_gate, ffn_w_up, ffn_w_down, moe_w_router, moe_b_router, moe_w_gate, moe_w_up, moe_w_down):
    ln = lambda layer, j: (ln_g[layer, j].reshape(1, D_MODEL).astype(F32),
                           ln_b[layer, j].reshape(1, D_MODEL).astype(F32))
    cos_t, sin_t = _rope_tables(positions)
    x2 = x.reshape(N_TOK, D_MODEL)

    params = _ssm_params(ssm_lambda_re[0], ssm_lambda_im[0], ssm_log_step[0], ssm_b_re[0], ssm_b_im[0],
                         ssm_c_re[0], ssm_c_im[0])
    y = _ssm_mixer(x2, params)
    h = _glu_ln(y, x2, ssm_d[0].astype(F32).reshape(1, D_MODEL), ssm_w_glu[0].astype(BF16), *ln(0, 0))
    h = _ffn_ln(h, ffn_w_gate[0].astype(BF16), ffn_w_up[0].astype(BF16), ffn_w_down[0].astype(BF16),
                *ln(0, 1))

    order = jnp.array(HEAD_ORDER)
    w_q = attn_w_q[0].reshape(D_MODEL, N_HEADS, HEAD_DIM)[:, order].reshape(D_MODEL, Q_DIM)
    w_out = attn_w_out[0].reshape(N_HEADS, HEAD_DIM, D_MODEL)[order].reshape(Q_DIM, D_MODEL)
    w_qkv = jnp.concatenate([w_q, kv_w], axis=1).astype(BF16)
    q, k, v = _qkv(h, w_qkv, cos_t, sin_t)
    h = _attention_ln(q, k, v, attn_sinks[0].astype(F32), h, w_out.astype(BF16), *ln(1, 0))
    w_router = jnp.pad(moe_w_router[0].astype(F32), ((0, 0), (0, LANES - N_EXPERTS)))
    b_router = jnp.pad(moe_b_router[0].astype(F32), (0, LANES - N_EXPERTS)).reshape(1, LANES)
    h = _moe_ln(h, w_router, b_router, moe_w_gate[0].astype(BF16), moe_w_up[0].astype(BF16),
                moe_w_down[0].astype(BF16), *ln(1, 1))
    return h.reshape(BATCH, SEQ, D_MODEL)
```

```python
import functools
import math

import jax
import jax.numpy as jnp
from jax import lax
from jax.experimental import pallas as pl
from jax.experimental.pallas import tpu as pltpu

F32 = jnp.float32
BF16 = jnp.bfloat16

D_MODEL = 1024
BATCH = 16
SEQ = 2048
N_TOK = BATCH * SEQ
DEPTH = 2

SSM_GROUP_CH = 16
SSM_GROUPS = D_MODEL // SSM_GROUP_CH
SSM_STATE = 64
SSM_CHUNK = 16
SSM_NCHUNK = SEQ // SSM_CHUNK
SSM_ROWS = SSM_NCHUNK * BATCH
SSM_WIDTH = SSM_CHUNK * SSM_GROUP_CH
SSM_SCAN_STEPS = int(math.log2(SSM_NCHUNK))

N_HEADS = 16
HEAD_DIM = 64
N_KV_HEADS = 4
Q_PER_KV = N_HEADS // N_KV_HEADS
KV_DIM = N_KV_HEADS * HEAD_DIM
WINDOW = 128
BLOCK = 128
ROT_DIM = HEAD_DIM // 4
ROT_HALF = ROT_DIM // 2
ROPE_THETA = 500000.0

FFN_DIM = 2816
N_EXPERTS = 8
EXPERT_DIM = 1024

DEEPNORM_ALPHA = (2 * DEPTH) ** 0.25
LN_EPS = 1e-5

LANES = 128
VMEM_LIMIT = 56 * 1024 * 1024


def _cparams(*sem):
    return pltpu.CompilerParams(dimension_semantics=sem, vmem_limit_bytes=VMEM_LIMIT)


def _layer_norm(r, g, b):
    mu = jnp.mean(r, axis=-1, keepdims=True)
    xc = r - mu
    var = jnp.mean(xc * xc, axis=-1, keepdims=True)
    return xc * lax.rsqrt(var + LN_EPS) * g + b


def _rope_kernel(invf_ref, pos_ref, cos_ref, sin_ref):
    pos = pos_ref[...]
    for f in range(ROT_HALF):
        ang = pos * invf_ref[f]
        cos_ref[f] = jnp.cos(ang)
        sin_ref[f] = jnp.sin(ang)


def _rope_tables(positions):
    inv_freq = ROPE_THETA ** (-jnp.arange(0, ROT_DIM, 2, dtype=F32) / ROT_DIM)
    pos = positions.astype(F32)
    cos8, sin8 = pl.pallas_call(
        _rope_kernel,
        out_shape=[jax.ShapeDtypeStruct((ROT_HALF, BATCH, SEQ), F32)] * 2,
        in_specs=[pl.BlockSpec(memory_space=pltpu.SMEM),
                  pl.BlockSpec(memory_space=pltpu.VMEM)],
        out_specs=[pl.BlockSpec(memory_space=pltpu.VMEM)] * 2,
        name="rope_tables",
    )(inv_freq, pos)
    cos8 = cos8.transpose(1, 2, 0).reshape(N_TOK, ROT_HALF)
    sin8 = sin8.transpose(1, 2, 0).reshape(N_TOK, ROT_HALF)
    rest = HEAD_DIM - ROT_DIM
    cos_head = jnp.concatenate([cos8, cos8, jnp.ones((N_TOK, rest), F32)], axis=1)
    sin_head = jnp.concatenate([-sin8, sin8, jnp.zeros((N_TOK, rest), F32)], axis=1)
    reps = LANES // HEAD_DIM
    return jnp.tile(cos_head, (1, reps)), jnp.tile(sin_head, (1, reps))


def _ssm_params(lam_re, lam_im, log_step, b_re, b_im, c_re, c_im):
    hp = lax.Precision.HIGHEST
    lr, li = lam_re.astype(F32), lam_im.astype(F32)
    dt = jnp.exp(log_step.astype(F32))[:, None]
    mag = jnp.exp(lr * dt)
    ar = mag * jnp.cos(li * dt)
    ai = mag * jnp.sin(li * dt)
    nr = ar - 1.0
    den = lr * lr + li * li
    kr = (nr * lr + ai * li) / den
    ki = (ai * lr - nr * li) / den
    br, bi = b_re.astype(F32), b_im.astype(F32)
    bbar_r = kr[..., None] * br - ki[..., None] * bi
    bbar_i = kr[..., None] * bi + ki[..., None] * br
    cr, ci = c_re.astype(F32), c_im.astype(F32)

    def powers(taus):
        t = taus.astype(F32)[:, None, None]
        m = jnp.exp(lr[None] * dt[None] * t)
        ang = li[None] * dt[None] * t
        return m * jnp.cos(ang), m * jnp.sin(ang)

    er, ei = powers(jnp.arange(SSM_CHUNK + 1))
    w_r = er[:, :, :, None] * bbar_r[None] - ei[:, :, :, None] * bbar_i[None]
    w_i = er[:, :, :, None] * bbar_i[None] + ei[:, :, :, None] * bbar_r[None]
    kern = (jnp.einsum('gcp,tgpd->gtcd', cr, w_r[:SSM_CHUNK], precision=hp)
            - jnp.einsum('gcp,tgpd->gtcd', ci, w_i[:SSM_CHUNK], precision=hp))
    s_idx = jnp.arange(SSM_CHUNK)[:, None, None]
    t_idx = jnp.arange(SSM_CHUNK)[None, :, None]
    lag = (t_idx - s_idx == jnp.arange(SSM_CHUNK)[None, None, :]).astype(F32)
    toep = jnp.einsum('stu,gucd->gtcsd', lag, kern, precision=hp)
    mt = toep.reshape(SSM_GROUPS, SSM_WIDTH, SSM_WIDTH)
    rev = SSM_CHUNK - 1 - jnp.arange(SSM_CHUNK)
    g_r = w_r[rev].transpose(1, 2, 0, 3).reshape(SSM_GROUPS, SSM_STATE, SSM_WIDTH)
    g_i = w_i[rev].transpose(1, 2, 0, 3).reshape(SSM_GROUPS, SSM_STATE, SSM_WIDTH)
    gt = jnp.concatenate([g_r, g_i], axis=1)
    e1r, e1i = er[1:], ei[1:]
    ce_r = cr[None] * e1r[:, :, None, :] - ci[None] * e1i[:, :, None, :]
    ce_i = cr[None] * e1i[:, :, None, :] + ci[None] * e1r[:, :, None, :]
    c_re = ce_r.transpose(1, 0, 2, 3).reshape(SSM_GROUPS, SSM_WIDTH, SSM_STATE)
    c_im = (-ce_i).transpose(1, 0, 2, 3).reshape(SSM_GROUPS, SSM_WIDTH, SSM_STATE)
    ct = jnp.concatenate([c_re, c_im], axis=-1)
    sr, si = powers(SSM_CHUNK * (2 ** jnp.arange(8)))
    coef = lambda t: jnp.concatenate([t, t], axis=-1).transpose(1, 0, 2)
    return mt.astype(BF16), gt.astype(BF16), ct.astype(BF16), coef(sr), coef(si)


SSM_LANE_GROUPS = LANES // SSM_GROUP_CH
SSM_LANE_CHUNKS = D_MODEL // LANES
SSM_BATCH_TILE = 4
SSM_COLS = SSM_BATCH_TILE * SSM_NCHUNK


def _ssm_kernel(x_ref, mt_ref, gt_ref, ct_ref, cr_ref, ci_ref, y_ref, v_ref, yt_ref):
    for bl in range(SSM_BATCH_TILE):
        cols = slice(bl * SSM_NCHUNK, (bl + 1) * SSM_NCHUNK)
        for s in range(SSM_CHUNK):
            a = x_ref[pl.ds(bl * SEQ + s, SSM_NCHUNK, stride=SSM_CHUNK), :]
            at = a.T.astype(BF16)
            for g in range(SSM_LANE_GROUPS):
                v_ref[g, s * SSM_GROUP_CH:(s + 1) * SSM_GROUP_CH, cols] = (
                    at[g * SSM_GROUP_CH:(g + 1) * SSM_GROUP_CH])

    row_j = lax.broadcasted_iota(jnp.int32, (SSM_NCHUNK, 1), 0)

    def group_body(g, carry):
        ut = v_ref[g]
        xt = jnp.dot(gt_ref[g], ut, preferred_element_type=F32)
        cr_all = cr_ref[g]
        ci_all = ci_ref[g]

        def seg(rows, b):
            return xt[rows, b * SSM_NCHUNK:(b + 1) * SSM_NCHUNK]

        def shifted(x, d):
            if d < 8:
                return jnp.where(row_j >= d, pltpu.roll(x, d, 0), 0.0)
            return jnp.concatenate([jnp.zeros((d, LANES), F32), x[:SSM_NCHUNK - d]], axis=0)

        re_rows, im_rows = slice(0, SSM_STATE), slice(SSM_STATE, 2 * SSM_STATE)
        sr, si = [], []
        for q in range(SSM_BATCH_TILE // 2):
            xr = jnp.concatenate([seg(re_rows, 2 * q), seg(re_rows, 2 * q + 1)], axis=0).T
            xi = jnp.concatenate([seg(im_rows, 2 * q), seg(im_rows, 2 * q + 1)], axis=0).T
            for k in range(SSM_SCAN_STEPS):
                cr = cr_all[k:k + 1, :]
                ci = ci_all[k:k + 1, :]
                rr = shifted(xr, 1 << k)
                ri = shifted(xi, 1 << k)
                xr, xi = xr + cr * rr - ci * ri, xi + cr * ri + ci * rr
            pr = shifted(xr, 1).T
            pi = shifted(xi, 1).T
            sr += [pr[:SSM_STATE], pr[SSM_STATE:]]
            si += [pi[:SSM_STATE], pi[SSM_STATE:]]
        sprev = jnp.concatenate([jnp.concatenate(sr, axis=1), jnp.concatenate(si, axis=1)],
                                axis=0).astype(BF16)
        yt_ref[g] = (jnp.dot(mt_ref[g], ut, preferred_element_type=F32)
                     + jnp.dot(ct_ref[g], sprev, preferred_element_type=F32))
        return carry

    lax.fori_loop(0, SSM_LANE_GROUPS, group_body, 0, unroll=2)

    for bl in range(SSM_BATCH_TILE):
        cols = slice(bl * SSM_NCHUNK, (bl + 1) * SSM_NCHUNK)
        for t in range(SSM_CHUNK):
            tile = jnp.concatenate(
                [yt_ref[g, t * SSM_GROUP_CH:(t + 1) * SSM_GROUP_CH, cols] for g in range(SSM_LANE_GROUPS)],
                axis=0)
            y_ref[pl.ds(bl * SEQ + t, SSM_NCHUNK, stride=SSM_CHUNK), :] = tile.T


def _ssm_mixer(x2, params):
    mt, gt, ct, coef_r, coef_i = params
    rows = SSM_BATCH_TILE * SEQ
    pspec = lambda a, b: pl.BlockSpec((SSM_LANE_GROUPS, a, b), lambda k, q: (k, 0, 0))
    return pl.pallas_call(
        _ssm_kernel,
        out_shape=jax.ShapeDtypeStruct((N_TOK, D_MODEL), F32),
        grid=(SSM_LANE_CHUNKS, BATCH // SSM_BATCH_TILE),
        in_specs=[pl.BlockSpec((rows, LANES), lambda k, q: (q, k)),
                  pspec(SSM_WIDTH, SSM_WIDTH), pspec(2 * SSM_STATE, SSM_WIDTH), pspec(SSM_WIDTH, 2 * SSM_STATE),
                  pspec(8, 2 * SSM_STATE), pspec(8, 2 * SSM_STATE)],
        out_specs=pl.BlockSpec((rows, LANES), lambda k, q: (q, k)),
        scratch_shapes=[pltpu.VMEM((SSM_LANE_GROUPS, SSM_WIDTH, SSM_COLS), BF16),
                        pltpu.VMEM((SSM_LANE_GROUPS, SSM_WIDTH, SSM_COLS), F32)],
        compiler_params=_cparams("parallel", "parallel"),
        name="ssm_mixer",
    )(x2, mt, gt, ct, coef_r, coef_i)


def _glu_ln_kernel(y_ref, x_ref, d_ref, w_ref, lg_ref, lb_ref, o_ref):
    x = x_ref[...]
    act = jax.nn.gelu(y_ref[...] + d_ref[...] * x).astype(BF16)
    z = jnp.dot(act, w_ref[...], preferred_element_type=F32)
    mix = z[:, :D_MODEL] * jax.nn.sigmoid(z[:, D_MODEL:])
    o_ref[...] = _layer_norm(DEEPNORM_ALPHA * x + mix, lg_ref[...], lb_ref[...])


def _row_spec(tm, width):
    return pl.BlockSpec((tm, width), lambda i: (i, 0))


def _const_spec(*shape):
    return pl.BlockSpec(shape, lambda *_: (0,) * len(shape))


def _glu_ln(y, x2, d_skip, w_glu, lg, lb, tm=512):
    return pl.pallas_call(
        _glu_ln_kernel,
        out_shape=jax.ShapeDtypeStruct((N_TOK, D_MODEL), F32),
        grid=(N_TOK // tm,),
        in_specs=[_row_spec(tm, D_MODEL), _row_spec(tm, D_MODEL), _const_spec(1, D_MODEL),
                  _const_spec(D_MODEL, 2 * D_MODEL), _const_spec(1, D_MODEL), _const_spec(1, D_MODEL)],
        out_specs=_row_spec(tm, D_MODEL),
        compiler_params=_cparams("parallel"),
        name="glu_ln",
    )(y, x2, d_skip, w_glu, lg, lb)


FFN_SPLIT = 2
FFN_TILE = FFN_DIM // FFN_SPLIT


def _ffn_ln_kernel(h_ref, wg_ref, wu_ref, wd_ref, lg_ref, lb_ref, o_ref):
    h = h_ref[...]
    hb = h.astype(BF16)
    ff = None
    for f in range(FFN_SPLIT):
        cols = slice(f * FFN_TILE, (f + 1) * FFN_TILE)
        gate = jnp.dot(hb, wg_ref[:, cols], preferred_element_type=F32)
        up = jnp.dot(hb, wu_ref[:, cols], preferred_element_type=F32)
        act = (jax.nn.silu(gate) * up).astype(BF16)
        part = jnp.dot(act, wd_ref[cols, :], preferred_element_type=F32)
        ff = part if ff is None else ff + part
    o_ref[...] = _layer_norm(DEEPNORM_ALPHA * h + ff, lg_ref[...], lb_ref[...])


def _resident_spec(*shape):
    return pl.BlockSpec(shape, lambda *_: (0,) * len(shape), pipeline_mode=pl.Buffered(1))


def _ffn_ln(h, wg, wu, wd, lg, lb, tm=512):
    return pl.pallas_call(
        _ffn_ln_kernel,
        out_shape=jax.ShapeDtypeStruct((N_TOK, D_MODEL), F32),
        grid=(N_TOK // tm,),
        in_specs=[_row_spec(tm, D_MODEL),
                  _resident_spec(D_MODEL, FFN_DIM), _resident_spec(D_MODEL, FFN_DIM),
                  _resident_spec(FFN_DIM, D_MODEL),
                  _const_spec(1, D_MODEL), _const_spec(1, D_MODEL)],
        out_specs=_row_spec(tm, D_MODEL),
        compiler_params=_cparams("parallel"),
        name="ffn_ln",
    )(h, wg, wu, wd, lg, lb)


Q_DIM = N_HEADS * HEAD_DIM


def _qkv_kernel(h_ref, w_ref, cos_ref, sin_ref, q_ref, k_ref, v_ref):
    hb = h_ref[...].astype(BF16)
    z = jnp.dot(hb, w_ref[...], preferred_element_type=F32)
    cos_t = cos_ref[...]
    sin_t = sin_ref[...]
    lane = lax.broadcasted_iota(jnp.int32, (1, LANES), 1)
    first_half = (lane % HEAD_DIM) < ROT_HALF

    def rope(t):
        partner = jnp.where(first_half, pltpu.roll(t, LANES - ROT_HALF, 1), pltpu.roll(t, ROT_HALF, 1))
        return t * cos_t + partner * sin_t

    scale = HEAD_DIM ** -0.5
    for c in range(Q_DIM // LANES):
        sl = slice(c * LANES, (c + 1) * LANES)
        q_ref[:, sl] = (rope(z[:, sl]) * scale).astype(BF16)
    for c in range(KV_DIM // LANES):
        sl = slice(c * LANES, (c + 1) * LANES)
        k_ref[:, sl] = rope(z[:, Q_DIM + c * LANES:Q_DIM + (c + 1) * LANES]).astype(BF16)
    v_ref[...] = z[:, Q_DIM + KV_DIM:].astype(BF16)


def _qkv(h, w_qkv, cos_t, sin_t, tm=512):
    return pl.pallas_call(
        _qkv_kernel,
        out_shape=[jax.ShapeDtypeStruct((N_TOK, Q_DIM), BF16),
                   jax.ShapeDtypeStruct((N_TOK, KV_DIM), BF16),
                   jax.ShapeDtypeStruct((N_TOK, KV_DIM), BF16)],
        grid=(N_TOK // tm,),
        in_specs=[_row_spec(tm, D_MODEL), _const_spec(D_MODEL, Q_DIM + 2 * KV_DIM),
                  _row_spec(tm, LANES), _row_spec(tm, LANES)],
        out_specs=[_row_spec(tm, Q_DIM), _row_spec(tm, KV_DIM), _row_spec(tm, KV_DIM)],
        compiler_params=_cparams("parallel"),
        name="qkv_rope",
    )(h, w_qkv, cos_t, sin_t)


N_BLOCKS = SEQ // BLOCK


ATTN_Q_TILE = 512
ATTN_SUB = ATTN_Q_TILE // BLOCK
ATTN_BAND = ATTN_Q_TILE + BLOCK
KV_PAIRS = N_KV_HEADS // 2

HEAD_ORDER = [h for c in range(KV_PAIRS) for g in range(Q_PER_KV)
              for h in (2 * c * Q_PER_KV + g, (2 * c + 1) * Q_PER_KV + g)]


def _band_bias():
    qi = jnp.arange(Q_PER_KV * BLOCK)[:, None] % BLOCK
    si = jnp.arange(2 * BLOCK)[None, :]
    rel = qi + BLOCK - si
    valid = (rel >= 0) & (rel < WINDOW)
    first = valid & (si >= BLOCK)
    neg = jnp.float32(-jnp.inf)
    return jnp.stack([jnp.where(valid, 0.0, neg), jnp.where(first, 0.0, neg)]).astype(F32)


def _attn_kernel(sink_ref, q_ref, kp_ref, kc_ref, vp_ref, vc_ref, bias_ref, h_ref, w_ref, lg_ref, lb_ref,
                 out_ref, ka_ref, kb_ref, va_ref, vb_ref, o_ref):
    i = pl.program_id(1)
    lane = lax.broadcasted_iota(jnp.int32, (1, KV_DIM), 1)
    low = (lane % LANES) < HEAD_DIM
    zero = jnp.zeros((), BF16)
    for src, dst_a, dst_b in ((kp_ref, ka_ref, kb_ref), (vp_ref, va_ref, vb_ref)):
        t = src[...]
        dst_a[0:BLOCK, :] = jnp.where(low, t, zero)
        dst_b[0:BLOCK, :] = jnp.where(low, zero, t)
    for src, dst_a, dst_b in ((kc_ref, ka_ref, kb_ref), (vc_ref, va_ref, vb_ref)):
        t = src[...]
        dst_a[BLOCK:ATTN_BAND, :] = jnp.where(low, t, zero)
        dst_b[BLOCK:ATTN_BAND, :] = jnp.where(low, zero, t)

    lane1 = lax.broadcasted_iota(jnp.int32, (1, LANES), 1)
    row = lax.broadcasted_iota(jnp.int32, (Q_PER_KV * BLOCK, 1), 0)
    contract_last = (((1,), (1,)), ((), ()))
    for blk in range(ATTN_SUB):
        r0 = blk * BLOCK
        if blk == 0:
            bias = bias_ref[jnp.where(i == 0, 1, 0)]
        else:
            bias = bias_ref[0]
        for c in range(KV_PAIRS):
            cs = slice(c * LANES, (c + 1) * LANES)
            chunks = [c * Q_PER_KV + g for g in range(Q_PER_KV)]
            q4 = jnp.concatenate([q_ref[r0:r0 + BLOCK, m * LANES:(m + 1) * LANES] for m in chunks], axis=0)
            outs = []
            for half, (k_ref, v_ref) in enumerate(((ka_ref, va_ref), (kb_ref, vb_ref))):
                kband = k_ref[r0:r0 + 2 * BLOCK, cs]
                s = lax.dot_general(q4, kband, contract_last, preferred_element_type=F32) + bias
                sink = jnp.zeros((Q_PER_KV * BLOCK, 1), F32)
                for g, m in enumerate(chunks):
                    sink = jnp.where(row // BLOCK == g, sink_ref[HEAD_ORDER[2 * m + half]], sink)
                mx = jnp.maximum(jnp.max(s, axis=-1, keepdims=True), sink)
                p = jnp.exp(s - mx)
                denom = jnp.sum(p, axis=-1, keepdims=True) + jnp.exp(sink - mx)
                pv = jnp.dot(p.astype(BF16), v_ref[r0:r0 + 2 * BLOCK, cs], preferred_element_type=F32)
                outs.append((pv, 1.0 / denom))
            (pv_a, r_a), (pv_b, r_b) = outs
            o = (pv_a + pv_b) * jnp.where(lane1 < HEAD_DIM, r_a, r_b)
            for g, m in enumerate(chunks):
                o_ref[r0:r0 + BLOCK, m * LANES:(m + 1) * LANES] = o[g * BLOCK:(g + 1) * BLOCK].astype(BF16)

    mix = jnp.dot(o_ref[...], w_ref[...], preferred_element_type=F32)
    out_ref[...] = _layer_norm(DEEPNORM_ALPHA * h_ref[...] + mix, lg_ref[...], lb_ref[...])


def _attention_ln(q, k, v, sinks, h, w_out, lg, lb):
    tiles = SEQ // ATTN_Q_TILE
    cur = lambda b, i: (b * tiles + i, 0)
    prev = lambda b, i: (jnp.maximum((b * tiles + i) * ATTN_SUB - 1, 0), 0)
    const2 = lambda b, i: (0, 0)
    return pl.pallas_call(
        _attn_kernel,
        out_shape=jax.ShapeDtypeStruct((N_TOK, D_MODEL), F32),
        grid=(BATCH, tiles),
        in_specs=[pl.BlockSpec(memory_space=pltpu.SMEM),
                  pl.BlockSpec((ATTN_Q_TILE, Q_DIM), cur),
                  pl.BlockSpec((BLOCK, KV_DIM), prev), pl.BlockSpec((ATTN_Q_TILE, KV_DIM), cur),
                  pl.BlockSpec((BLOCK, KV_DIM), prev), pl.BlockSpec((ATTN_Q_TILE, KV_DIM), cur),
                  pl.BlockSpec((2, Q_PER_KV * BLOCK, 2 * BLOCK), lambda b, i: (0, 0, 0)),
                  pl.BlockSpec((ATTN_Q_TILE, D_MODEL), cur),
                  pl.BlockSpec((Q_DIM, D_MODEL), const2),
                  pl.BlockSpec((1, D_MODEL), const2), pl.BlockSpec((1, D_MODEL), const2)],
        out_specs=pl.BlockSpec((ATTN_Q_TILE, D_MODEL), cur),
        scratch_shapes=[pltpu.VMEM((ATTN_BAND, KV_DIM), BF16)] * 4
                       + [pltpu.VMEM((ATTN_Q_TILE, Q_DIM), BF16)],
        compiler_params=_cparams("parallel", "parallel"),
        name="swa_attention_ln",
    )(sinks, q, k, k, v, v, _band_bias(), h, w_out, lg, lb)


ROUTE_TILE = 512
EXPERT_TILE = 512
SLAB = D_MODEL // LANES
EXPERT_REGION = N_TOK
REGION_TILES = EXPERT_REGION // EXPERT_TILE
N_SORTED = N_EXPERTS * EXPERT_REGION
N_EXPERT_TILES = 2 * N_TOK // EXPERT_TILE + N_EXPERTS
META_E1, META_E2, META_R1, META_R2, META_W1, META_W2 = range(6)
META_D1, META_D2 = 8, 9
COMBINE_CHUNK = 64
N_ROUTE_TILES = N_TOK // ROUTE_TILE
ROUTER_BURST = 64


def _to_slabs(ref, val, rows):
    for c in range(SLAB):
        ref[pl.ds(c, rows, stride=SLAB), :] = val[:, c * LANES:(c + 1) * LANES]


def _from_slabs(ref, rows, first_row=0):
    return jnp.concatenate([ref[pl.ds(first_row * SLAB + c, rows, stride=SLAB), :] for c in range(SLAB)],
                           axis=1)


def _row_slab(ref, row):
    return ref.at[pl.ds(pl.multiple_of(row * SLAB, SLAB), SLAB)]


def _split_bf16(x):
    def top_bits(v):
        bits = lax.bitcast_convert_type(v, jnp.uint32) & jnp.uint32(0xFFFF0000)
        return lax.bitcast_convert_type(bits, F32)
    hi = top_bits(x)
    mid = top_bits(x - hi)
    lo = (x - hi) - mid
    return hi.astype(BF16), mid.astype(BF16), lo.astype(BF16)


def _router_kernel(h_ref, wr_ref, br_ref, tri_ref, meta_ref, counts_ref, xs_hbm,
                   carry_ref, slab_a, slab_b, zero_ref, dest_v, dest_a, dest_b, count_v, count_s,
                   sem, pad_sem, idx_sem):
    i = pl.program_id(0)
    slabs = (slab_a, slab_b)
    dests = (dest_a, dest_b)

    @pl.when(i == 0)
    def _():
        carry_ref[...] = jnp.zeros_like(carry_ref)
        zero_ref[...] = jnp.zeros_like(zero_ref)

    def start_row(slot, t):
        for k in range(2):
            pltpu.make_async_copy(_row_slab(slabs[slot], t), _row_slab(xs_hbm, dests[slot][k, t]),
                                  sem.at[slot]).start(priority=k)

    def wait_scatter(slot):
        for _ in range(2):
            pltpu.make_async_copy(slabs[slot], xs_hbm.at[pl.ds(0, ROUTE_TILE * SLAB)], sem.at[slot]).wait()

    def route(slot, scatter_previous):
        pending = list(range(0, ROUTE_TILE, ROUTER_BURST)) if scatter_previous else []

        def burst():
            if pending:
                r0 = pending.pop(0)
                for t in range(r0, r0 + ROUTER_BURST):
                    start_row(1 - slot, t)

        lane = lax.broadcasted_iota(jnp.int32, (1, LANES), 1)
        lane_f = lane.astype(F32)
        h = h_ref[...]
        h0, h1, h2 = _split_bf16(h)
        w0, w1, w2 = wr_ref[0], wr_ref[1], wr_ref[2]
        small = None
        for a, b in ((h2, w0), (h0, w2), (h1, w1)):
            part = jnp.dot(a, b, preferred_element_type=F32)
            small = part if small is None else small + part
            burst()
        mid = None
        for a, b in ((h1, w0), (h0, w1)):
            part = jnp.dot(a, b, preferred_element_type=F32)
            mid = part if mid is None else mid + part
            burst()
        logits = (small + mid) + jnp.dot(h0, w0, preferred_element_type=F32) + br_ref[...]
        burst()
        logits = jnp.where(lane < N_EXPERTS, logits, -jnp.inf)
        m1 = jnp.max(logits, axis=-1, keepdims=True)
        i1 = jnp.min(jnp.where(logits == m1, lane_f, float(LANES)), axis=-1, keepdims=True)
        rest = jnp.where(lane_f == i1, -jnp.inf, logits)
        m2 = jnp.max(rest, axis=-1, keepdims=True)
        i2 = jnp.min(jnp.where(rest == m2, lane_f, float(LANES)), axis=-1, keepdims=True)
        e2 = jnp.exp(m2 - m1)
        tot = 1.0 + e2
        burst()
        chosen = jnp.where((lane_f == i1) | (lane_f == i2), 1.0, 0.0)
        before = jnp.dot(tri_ref[...], chosen.astype(BF16), preferred_element_type=F32) + carry_ref[...]
        r1 = jnp.sum(jnp.where(lane_f == i1, before, 0.0), axis=-1, keepdims=True)
        r2 = jnp.sum(jnp.where(lane_f == i2, before, 0.0), axis=-1, keepdims=True)
        new_carry = carry_ref[...] + jnp.sum(chosen, axis=0, keepdims=True)
        carry_ref[...] = new_carry
        counts_ref[...] = new_carry.astype(jnp.int32)
        burst()
        record = jnp.zeros((ROUTE_TILE, LANES), F32)
        for lane_id, col in ((META_E1, i1), (META_E2, i2), (META_R1, r1), (META_R2, r2),
                             (META_W1, 1.0 / tot), (META_W2, e2 / tot),
                             (META_D1, i1 * EXPERT_REGION + r1), (META_D2, i2 * EXPERT_REGION + r2)):
            record = jnp.where(lane == lane_id, col, record)
        meta_ref[...] = record
        _to_slabs(slabs[slot], h, ROUTE_TILE)
        while pending:
            burst()
        dest_v[...] = record.T[META_D1:META_D1 + 8, :].astype(jnp.int32)
        to_smem = pltpu.make_async_copy(dest_v, dests[slot], idx_sem)
        to_smem.start()
        to_smem.wait()

    @pl.when(i == 0)
    def _():
        route(0, False)

    for slot in range(2):
        @pl.when((i > 0) & (i % 2 == slot))
        def _():
            @pl.when(i >= 2)
            def _():
                wait_scatter(slot)
            route(slot, True)

    @pl.when(i == N_ROUTE_TILES - 1)
    def _():
        slot = (N_ROUTE_TILES - 1) % 2
        lax.fori_loop(0, ROUTE_TILE, lambda t, c: (start_row(slot, t), c)[1], 0, unroll=8)
        wait_scatter(1 - slot)
        wait_scatter(slot)
        count_v[...] = jnp.broadcast_to(carry_ref[...].astype(jnp.int32), count_v.shape)
        cp = pltpu.make_async_copy(count_v, count_s, idx_sem)
        cp.start()
        cp.wait()
        for e in range(N_EXPERTS):
            n = count_s[0, e]
            end = (n + EXPERT_TILE - 1) // EXPERT_TILE * EXPERT_TILE
            pad = lambda r: pltpu.make_async_copy(zero_ref, _row_slab(xs_hbm, e * EXPERT_REGION + r), pad_sem)
            lax.fori_loop(n, end, lambda r, c: (pad(r).start(), c)[1], 0)
            lax.fori_loop(n, end, lambda r, c: (pad(r).wait(), c)[1], 0)


def _router(h, w_router, b_router):
    tri = (jnp.arange(ROUTE_TILE)[:, None] > jnp.arange(ROUTE_TILE)[None, :]).astype(BF16)
    w_split = jnp.stack(_split_bf16(w_router))
    return pl.pallas_call(
        _router_kernel,
        out_shape=[jax.ShapeDtypeStruct((N_TOK, LANES), F32),
                   jax.ShapeDtypeStruct((1, LANES), jnp.int32),
                   jax.ShapeDtypeStruct((N_SORTED * SLAB, LANES), F32)],
        grid=(N_TOK // ROUTE_TILE,),
        in_specs=[_row_spec(ROUTE_TILE, D_MODEL), _const_spec(3, D_MODEL, LANES),
                  _const_spec(1, LANES), _const_spec(ROUTE_TILE, ROUTE_TILE)],
        out_specs=[_row_spec(ROUTE_TILE, LANES), _const_spec(1, LANES), pl.BlockSpec(memory_space=pl.ANY)],
        scratch_shapes=[pltpu.VMEM((1, LANES), F32),
                        pltpu.VMEM((ROUTE_TILE * SLAB, LANES), F32), pltpu.VMEM((ROUTE_TILE * SLAB, LANES), F32),
                        pltpu.VMEM((SLAB, LANES), F32),
                        pltpu.VMEM((8, ROUTE_TILE), jnp.int32),
                        pltpu.SMEM((8, ROUTE_TILE), jnp.int32), pltpu.SMEM((8, ROUTE_TILE), jnp.int32),
                        pltpu.VMEM((8, LANES), jnp.int32), pltpu.SMEM((8, LANES), jnp.int32),
                        pltpu.SemaphoreType.DMA((2,)), pltpu.SemaphoreType.DMA, pltpu.SemaphoreType.DMA],
        compiler_params=_cparams("arbitrary"),
        name="moe_router",
    )(h, w_split, b_router, tri)


def _experts_kernel(tblk_ref, texp_ref, xs_ref, wg_ref, wu_ref, wd_ref, ys_ref):
    x = _from_slabs(xs_ref, EXPERT_TILE).astype(BF16)
    gate = jnp.dot(x, wg_ref[0], preferred_element_type=F32)
    up = jnp.dot(x, wu_ref[0], preferred_element_type=F32)
    act = (jax.nn.silu(gate) * up).astype(BF16)
    _to_slabs(ys_ref, jnp.dot(act, wd_ref[0], preferred_element_type=F32), EXPERT_TILE)


def _experts(tile_block, tile_expert, xs, wg, wu, wd):
    wspec = lambda a, b: pl.BlockSpec((1, a, b), lambda i, tblk, texp: (texp[i], 0, 0))
    rows = pl.BlockSpec((EXPERT_TILE * SLAB, LANES), lambda i, tblk, texp: (tblk[i], 0))
    return pl.pallas_call(
        _experts_kernel,
        out_shape=jax.ShapeDtypeStruct((N_SORTED * SLAB, LANES), F32),
        grid_spec=pltpu.PrefetchScalarGridSpec(
            num_scalar_prefetch=2,
            grid=(N_EXPERT_TILES,),
            in_specs=[rows, wspec(D_MODEL, EXPERT_DIM), wspec(D_MODEL, EXPERT_DIM),
                      wspec(EXPERT_DIM, D_MODEL)],
            out_specs=rows),
        compiler_params=_cparams("arbitrary"),
        name="moe_experts",
    )(tile_block, tile_expert, xs, wg, wu, wd)


def _combine_ln_kernel(dest_ref, meta_ref, h_ref, ys_hbm, lg_ref, lb_ref, o_ref, ya_ref, yb_ref, sem):
    i = pl.program_id(0)
    bufs = (ya_ref, yb_ref)

    def start_gather(tile, slot):
        def body(t, carry):
            for k in range(2):
                row = dest_ref[k * N_TOK + tile * ROUTE_TILE + t]
                pltpu.make_async_copy(_row_slab(ys_hbm, row), _row_slab(bufs[slot], k * ROUTE_TILE + t),
                                      sem.at[slot]).start(priority=k)
            return carry
        lax.fori_loop(0, ROUTE_TILE, body, 0, unroll=8)

    def wait_gather(slot):
        pltpu.make_async_copy(ys_hbm.at[pl.ds(0, 2 * ROUTE_TILE * SLAB)], bufs[slot], sem.at[slot]).wait()

    last = pl.num_programs(0) - 1
    lane = lax.broadcasted_iota(jnp.int32, (1, LANES), 1)

    def finish_and_prefetch(slot):
        nxt = jnp.minimum(i + 1, last) * ROUTE_TILE
        wait_gather(slot)
        for r0 in range(0, ROUTE_TILE, COMBINE_CHUNK):
            rows = slice(r0, r0 + COMBINE_CHUNK)
            meta = meta_ref[rows, :]
            w1 = jnp.sum(jnp.where(lane == META_W1, meta, 0.0), axis=-1, keepdims=True)
            w2 = jnp.sum(jnp.where(lane == META_W2, meta, 0.0), axis=-1, keepdims=True)
            ff = (w1 * _from_slabs(bufs[slot], COMBINE_CHUNK, first_row=r0)
                  + w2 * _from_slabs(bufs[slot], COMBINE_CHUNK, first_row=ROUTE_TILE + r0))
            o_ref[rows, :] = _layer_norm(DEEPNORM_ALPHA * h_ref[rows, :] + ff, lg_ref[...], lb_ref[...])
            for t in range(r0, r0 + COMBINE_CHUNK):
                for k in range(2):
                    row = dest_ref[k * N_TOK + nxt + t]
                    pltpu.make_async_copy(_row_slab(ys_hbm, row),
                                          _row_slab(bufs[1 - slot], k * ROUTE_TILE + t),
                                          sem.at[1 - slot]).start(priority=k)

        @pl.when(i == last)
        def _():
            wait_gather(1 - slot)

    @pl.when(i == 0)
    def _():
        start_gather(0, 0)

    for slot in range(2):
        @pl.when(i % 2 == slot)
        def _():
            finish_and_prefetch(slot)


def _combine_ln(dest, meta, h, ys, lg, lb):
    row = lambda width: pl.BlockSpec((ROUTE_TILE, width), lambda i, dest: (i, 0))
    const = pl.BlockSpec((1, D_MODEL), lambda i, dest: (0, 0))
    return pl.pallas_call(
        _combine_ln_kernel,
        out_shape=jax.ShapeDtypeStruct((N_TOK, D_MODEL), F32),
        grid_spec=pltpu.PrefetchScalarGridSpec(
            num_scalar_prefetch=1,
            grid=(N_TOK // ROUTE_TILE,),
            in_specs=[row(LANES), row(D_MODEL), pl.BlockSpec(memory_space=pl.ANY), const, const],
            out_specs=row(D_MODEL),
            scratch_shapes=[pltpu.VMEM((2 * ROUTE_TILE * SLAB, LANES), F32)] * 2
                           + [pltpu.SemaphoreType.DMA((2,))]),
        compiler_params=_cparams("arbitrary"),
        name="moe_combine_ln",
    )(dest, meta, h, ys, lg, lb)


def _moe_ln(h, w_router, b_router, wg, wu, wd, lg, lb):
    meta, counts, xs = _router(h, w_router, b_router)
    expert = jnp.arange(N_EXPERTS)
    tiles = (counts[0, :N_EXPERTS] + EXPERT_TILE - 1) // EXPERT_TILE
    ends = jnp.sum(jnp.where(expert[None, :] <= expert[:, None], tiles[None, :], 0), axis=1)
    step = jnp.minimum(jnp.arange(N_EXPERT_TILES), ends[-1] - 1)
    tile_expert = jnp.sum(step[:, None] >= ends[None, :], axis=1)
    first = jnp.sum(jnp.where(tile_expert[:, None] == expert[None, :], (ends - tiles)[None, :], 0), axis=1)
    tile_block = tile_expert * REGION_TILES + (step - first)
    ys = _experts(tile_block.astype(jnp.int32), tile_expert.astype(jnp.int32), xs, wg, wu, wd)
    dest = jnp.concatenate([meta[:, META_D1], meta[:, META_D2]]).astype(jnp.int32)
    return _combine_ln(dest, meta, h, ys, lg, lb)


def kernel(x, positions, ln_g, ln_b, ssm_lambda_re, ssm_lambda_im, ssm_log_step, ssm_b_re, ssm_b_im, ssm_c_re, ssm_c_im, ssm_d, ssm_w_glu, kv_w, attn_w_q, attn_sinks, attn_w_out, ffn_w_gate, ffn_w_up, ffn_w_down, moe_w_router, moe_b_router, moe_w_gate, moe_w_up, moe_w_down):
    ln = lambda layer, j: (ln_g[layer, j].reshape(1, D_MODEL).astype(F32),
                           ln_b[layer, j].reshape(1, D_MODEL).astype(F32))
    cos_t, sin_t = _rope_tables(positions)
    x2 = x.reshape(N_TOK, D_MODEL)

    params = _ssm_params(ssm_lambda_re[0], ssm_lambda_im[0], ssm_log_step[0], ssm_b_re[0], ssm_b_im[0],
                         ssm_c_re[0], ssm_c_im[0])
    y = _ssm_mixer(x2, params)
    h = _glu_ln(y, x2, ssm_d[0].astype(F32).reshape(1, D_MODEL), ssm_w_glu[0].astype(BF16), *ln(0, 0))
    h = _ffn_ln(h, ffn_w_gate[0].astype(BF16), ffn_w_up[0].astype(BF16), ffn_w_down[0].astype(BF16),
                *ln(0, 1))

    order = jnp.array(HEAD_ORDER)
    w_q = attn_w_q[0].reshape(D_MODEL, N_HEADS, HEAD_DIM)[:, order].reshape(D_MODEL, Q_DIM)
    w_out = attn_w_out[0].reshape(N_HEADS, HEAD_DIM, D_MODEL)[order].reshape(Q_DIM, D_MODEL)
    w_qkv = jnp.concatenate([w_q, kv_w], axis=1).astype(BF16)
    q, k, v = _qkv(h, w_qkv, cos_t, sin_t)
    h = _attention_ln(q, k, v, attn_sinks[0].astype(F32), h, w_out.astype(BF16), *ln(1, 0))
    w_router = jnp.pad(moe_w_router[0].astype(F32), ((0, 0), (0, LANES - N_EXPERTS)))
    b_router = jnp.pad(moe_b_router[0].astype(F32), (0, LANES - N_EXPERTS)).reshape(1, LANES)
    h = _moe_ln(h, w_router, b_router, moe_w_gate[0].astype(BF16), moe_w_up[0].astype(BF16),
                moe_w_down[0].astype(BF16), *ln(1, 1))
    return h.reshape(BATCH, SEQ, D_MODEL)
```

```python
import functools
import math

import jax
import jax.numpy as jnp
from jax import lax
from jax.experimental import pallas as pl
from jax.experimental.pallas import tpu as pltpu

F32 = jnp.float32
BF16 = jnp.bfloat16

D_MODEL = 1024
BATCH = 16
SEQ = 2048
N_TOK = BATCH * SEQ
DEPTH = 2

SSM_GROUP_CH = 16
SSM_GROUPS = D_MODEL // SSM_GROUP_CH
SSM_STATE = 64
SSM_CHUNK = 16
SSM_NCHUNK = SEQ // SSM_CHUNK
SSM_ROWS = SSM_NCHUNK * BATCH
SSM_WIDTH = SSM_CHUNK * SSM_GROUP_CH
SSM_SCAN_STEPS = int(math.log2(SSM_NCHUNK))

N_HEADS = 16
HEAD_DIM = 64
N_KV_HEADS = 4
Q_PER_KV = N_HEADS // N_KV_HEADS
KV_DIM = N_KV_HEADS * HEAD_DIM
WINDOW = 128
BLOCK = 128
ROT_DIM = HEAD_DIM // 4
ROT_HALF = ROT_DIM // 2
ROPE_THETA = 500000.0

FFN_DIM = 2816
N_EXPERTS = 8
EXPERT_DIM = 1024

DEEPNORM_ALPHA = (2 * DEPTH) ** 0.25
LN_EPS = 1e-5

LANES = 128
VMEM_LIMIT = 56 * 1024 * 1024


def _cparams(*sem):
    return pltpu.CompilerParams(dimension_semantics=sem, vmem_limit_bytes=VMEM_LIMIT)


def _layer_norm(r, g, b):
    mu = jnp.mean(r, axis=-1, keepdims=True)
    xc = r - mu
    var = jnp.mean(xc * xc, axis=-1, keepdims=True)
    return xc * lax.rsqrt(var + LN_EPS) * g + b


def _rope_kernel(invf_ref, pos_ref, cos_ref, sin_ref):
    pos = pos_ref[...]
    for f in range(ROT_HALF):
        ang = pos * invf_ref[f]
        cos_ref[f] = jnp.cos(ang)
        sin_ref[f] = jnp.sin(ang)


def _rope_tables(positions):
    inv_freq = ROPE_THETA ** (-jnp.arange(0, ROT_DIM, 2, dtype=F32) / ROT_DIM)
    pos = positions.astype(F32)
    cos8, sin8 = pl.pallas_call(
        _rope_kernel,
        out_shape=[jax.ShapeDtypeStruct((ROT_HALF, BATCH, SEQ), F32)] * 2,
        in_specs=[pl.BlockSpec(memory_space=pltpu.SMEM),
                  pl.BlockSpec(memory_space=pltpu.VMEM)],
        out_specs=[pl.BlockSpec(memory_space=pltpu.VMEM)] * 2,
        name="rope_tables",
    )(inv_freq, pos)
    cos8 = cos8.transpose(1, 2, 0).reshape(N_TOK, ROT_HALF)
    sin8 = sin8.transpose(1, 2, 0).reshape(N_TOK, ROT_HALF)
    rest = HEAD_DIM - ROT_DIM
    return jnp.concatenate([cos8, cos8, jnp.ones((N_TOK, rest), F32),
                            -sin8, sin8, jnp.zeros((N_TOK, rest), F32)], axis=1)


def _ssm_params(lam_re, lam_im, log_step, b_re, b_im, c_re, c_im):
    hp = lax.Precision.HIGHEST
    lr, li = lam_re.astype(F32), lam_im.astype(F32)
    dt = jnp.exp(log_step.astype(F32))[:, None]
    mag = jnp.exp(lr * dt)
    ar = mag * jnp.cos(li * dt)
    ai = mag * jnp.sin(li * dt)
    nr = ar - 1.0
    den = lr * lr + li * li
    kr = (nr * lr + ai * li) / den
    ki = (ai * lr - nr * li) / den
    br, bi = b_re.astype(F32), b_im.astype(F32)
    bbar_r = kr[..., None] * br - ki[..., None] * bi
    bbar_i = kr[..., None] * bi + ki[..., None] * br
    cr, ci = c_re.astype(F32), c_im.astype(F32)

    def powers(taus):
        t = taus.astype(F32)[:, None, None]
        m = jnp.exp(lr[None] * dt[None] * t)
        ang = li[None] * dt[None] * t
        return m * jnp.cos(ang), m * jnp.sin(ang)

    er, ei = powers(jnp.arange(SSM_CHUNK + 1))
    w_r = er[:, :, :, None] * bbar_r[None] - ei[:, :, :, None] * bbar_i[None]
    w_i = er[:, :, :, None] * bbar_i[None] + ei[:, :, :, None] * bbar_r[None]
    kern = (jnp.einsum('gcp,tgpd->gtcd', cr, w_r[:SSM_CHUNK], precision=hp)
            - jnp.einsum('gcp,tgpd->gtcd', ci, w_i[:SSM_CHUNK], precision=hp))
    s_idx = jnp.arange(SSM_CHUNK)[:, None, None]
    t_idx = jnp.arange(SSM_CHUNK)[None, :, None]
    lag = (t_idx - s_idx == jnp.arange(SSM_CHUNK)[None, None, :]).astype(F32)
    toep = jnp.einsum('stu,gucd->gtcsd', lag, kern, precision=hp)
    mt = toep.reshape(SSM_GROUPS, SSM_WIDTH, SSM_WIDTH)
    rev = SSM_CHUNK - 1 - jnp.arange(SSM_CHUNK)
    g_r = w_r[rev].transpose(1, 2, 0, 3).reshape(SSM_GROUPS, SSM_STATE, SSM_WIDTH)
    g_i = w_i[rev].transpose(1, 2, 0, 3).reshape(SSM_GROUPS, SSM_STATE, SSM_WIDTH)
    gt = jnp.concatenate([g_r, g_i], axis=1)
    e1r, e1i = er[1:], ei[1:]
    ce_r = cr[None] * e1r[:, :, None, :] - ci[None] * e1i[:, :, None, :]
    ce_i = cr[None] * e1i[:, :, None, :] + ci[None] * e1r[:, :, None, :]
    c_re = ce_r.transpose(1, 0, 2, 3).reshape(SSM_GROUPS, SSM_WIDTH, SSM_STATE)
    c_im = (-ce_i).transpose(1, 0, 2, 3).reshape(SSM_GROUPS, SSM_WIDTH, SSM_STATE)
    ct = jnp.concatenate([c_re, c_im], axis=-1)
    sr, si = powers(SSM_CHUNK * (2 ** jnp.arange(8)))
    coef = lambda t: jnp.concatenate([t, t], axis=-1).transpose(1, 0, 2)
    return mt.astype(BF16), gt.astype(BF16), ct.astype(BF16), coef(sr), coef(si)


SSM_LANE_GROUPS = LANES // SSM_GROUP_CH
SSM_LANE_CHUNKS = D_MODEL // LANES
SSM_BATCH_TILE = 4
SSM_COLS = SSM_BATCH_TILE * SSM_NCHUNK


def _ssm_kernel(x_ref, mt_ref, gt_ref, ct_ref, cr_ref, ci_ref, y_ref, v_ref, yt_ref):
    for bl in range(SSM_BATCH_TILE):
        cols = slice(bl * SSM_NCHUNK, (bl + 1) * SSM_NCHUNK)
        for s in range(SSM_CHUNK):
            a = x_ref[pl.ds(bl * SEQ + s, SSM_NCHUNK, stride=SSM_CHUNK), :]
            at = a.T.astype(BF16)
            for g in range(SSM_LANE_GROUPS):
                v_ref[g, s * SSM_GROUP_CH:(s + 1) * SSM_GROUP_CH, cols] = (
                    at[g * SSM_GROUP_CH:(g + 1) * SSM_GROUP_CH])

    row_j = lax.broadcasted_iota(jnp.int32, (SSM_NCHUNK, 1), 0)

    def group_body(g, carry):
        ut = v_ref[g]
        xt = jnp.dot(gt_ref[g], ut, preferred_element_type=F32)
        cr_all = cr_ref[g]
        ci_all = ci_ref[g]

        def seg(rows, b):
            return xt[rows, b * SSM_NCHUNK:(b + 1) * SSM_NCHUNK]

        def shifted(x, d):
            if d < 8:
                return jnp.where(row_j >= d, pltpu.roll(x, d, 0), 0.0)
            return jnp.concatenate([jnp.zeros((d, LANES), F32), x[:SSM_NCHUNK - d]], axis=0)

        re_rows, im_rows = slice(0, SSM_STATE), slice(SSM_STATE, 2 * SSM_STATE)
        sr, si = [], []
        for q in range(SSM_BATCH_TILE // 2):
            xr = jnp.concatenate([seg(re_rows, 2 * q), seg(re_rows, 2 * q + 1)], axis=0).T
            xi = jnp.concatenate([seg(im_rows, 2 * q), seg(im_rows, 2 * q + 1)], axis=0).T
            for k in range(SSM_SCAN_STEPS):
                cr = cr_all[k:k + 1, :]
                ci = ci_all[k:k + 1, :]
                rr = shifted(xr, 1 << k)
                ri = shifted(xi, 1 << k)
                xr, xi = xr + cr * rr - ci * ri, xi + cr * ri + ci * rr
            pr = shifted(xr, 1).T
            pi = shifted(xi, 1).T
            sr += [pr[:SSM_STATE], pr[SSM_STATE:]]
            si += [pi[:SSM_STATE], pi[SSM_STATE:]]
        sprev = jnp.concatenate([jnp.concatenate(sr, axis=1), jnp.concatenate(si, axis=1)],
                                axis=0).astype(BF16)
        yt_ref[g] = (jnp.dot(mt_ref[g], ut, preferred_element_type=F32)
                     + jnp.dot(ct_ref[g], sprev, preferred_element_type=F32))
        return carry

    lax.fori_loop(0, SSM_LANE_GROUPS, group_body, 0, unroll=2)

    for bl in range(SSM_BATCH_TILE):
        cols = slice(bl * SSM_NCHUNK, (bl + 1) * SSM_NCHUNK)
        for t in range(SSM_CHUNK):
            tile = jnp.concatenate(
                [yt_ref[g, t * SSM_GROUP_CH:(t + 1) * SSM_GROUP_CH, cols] for g in range(SSM_LANE_GROUPS)],
                axis=0)
            y_ref[pl.ds(bl * SEQ + t, SSM_NCHUNK, stride=SSM_CHUNK), :] = tile.T


def _ssm_mixer(x2, params):
    mt, gt, ct, coef_r, coef_i = params
    rows = SSM_BATCH_TILE * SEQ
    pspec = lambda a, b: pl.BlockSpec((SSM_LANE_GROUPS, a, b), lambda k, q: (k, 0, 0))
    return pl.pallas_call(
        _ssm_kernel,
        out_shape=jax.ShapeDtypeStruct((N_TOK, D_MODEL), F32),
        grid=(SSM_LANE_CHUNKS, BATCH // SSM_BATCH_TILE),
        in_specs=[pl.BlockSpec((rows, LANES), lambda k, q: (q, k)),
                  pspec(SSM_WIDTH, SSM_WIDTH), pspec(2 * SSM_STATE, SSM_WIDTH), pspec(SSM_WIDTH, 2 * SSM_STATE),
                  pspec(8, 2 * SSM_STATE), pspec(8, 2 * SSM_STATE)],
        out_specs=pl.BlockSpec((rows, LANES), lambda k, q: (q, k)),
        scratch_shapes=[pltpu.VMEM((SSM_LANE_GROUPS, SSM_WIDTH, SSM_COLS), BF16),
                        pltpu.VMEM((SSM_LANE_GROUPS, SSM_WIDTH, SSM_COLS), F32)],
        compiler_params=_cparams("parallel", "parallel"),
        name="ssm_mixer",
    )(x2, mt, gt, ct, coef_r, coef_i)


GLU_COLS = 256


def _glu_ln_kernel(y_ref, x_ref, d_ref, w_ref, lg_ref, lb_ref, o_ref):
    x = x_ref[...]
    act = jax.nn.gelu(y_ref[...] + d_ref[...] * x).astype(BF16)
    parts = []
    for c in range(0, D_MODEL, GLU_COLS):
        val = jnp.dot(act, w_ref[:, c:c + GLU_COLS], preferred_element_type=F32)
        gate = jnp.dot(act, w_ref[:, D_MODEL + c:D_MODEL + c + GLU_COLS], preferred_element_type=F32)
        parts.append(val * jax.nn.sigmoid(gate))
    mix = jnp.concatenate(parts, axis=1)
    o_ref[...] = _layer_norm(DEEPNORM_ALPHA * x + mix, lg_ref[...], lb_ref[...])


def _row_spec(tm, width):
    return pl.BlockSpec((tm, width), lambda i: (i, 0))


def _const_spec(*shape):
    return pl.BlockSpec(shape, lambda *_: (0,) * len(shape))


def _glu_ln(y, x2, d_skip, w_glu, lg, lb, tm=512):
    return pl.pallas_call(
        _glu_ln_kernel,
        out_shape=jax.ShapeDtypeStruct((N_TOK, D_MODEL), F32),
        grid=(N_TOK // tm,),
        in_specs=[_row_spec(tm, D_MODEL), _row_spec(tm, D_MODEL), _const_spec(1, D_MODEL),
                  _const_spec(D_MODEL, 2 * D_MODEL), _const_spec(1, D_MODEL), _const_spec(1, D_MODEL)],
        out_specs=_row_spec(tm, D_MODEL),
        compiler_params=_cparams("parallel"),
        name="glu_ln",
    )(y, x2, d_skip, w_glu, lg, lb)


FFN_SPLIT = 2
FFN_TILE = FFN_DIM // FFN_SPLIT


def _ffn_ln_kernel(h_ref, wg_ref, wu_ref, wd_ref, lg_ref, lb_ref, o_ref):
    h = h_ref[...]
    hb = h.astype(BF16)
    ff = None
    for f in range(FFN_SPLIT):
        cols = slice(f * FFN_TILE, (f + 1) * FFN_TILE)
        gate = jnp.dot(hb, wg_ref[:, cols], preferred_element_type=F32)
        up = jnp.dot(hb, wu_ref[:, cols], preferred_element_type=F32)
        act = (jax.nn.silu(gate) * up).astype(BF16)
        part = jnp.dot(act, wd_ref[cols, :], preferred_element_type=F32)
        ff = part if ff is None else ff + part
    o_ref[...] = _layer_norm(DEEPNORM_ALPHA * h + ff, lg_ref[...], lb_ref[...])


def _resident_spec(*shape):
    return pl.BlockSpec(shape, lambda *_: (0,) * len(shape), pipeline_mode=pl.Buffered(1))


def _ffn_ln(h, wg, wu, wd, lg, lb, tm=512):
    return pl.pallas_call(
        _ffn_ln_kernel,
        out_shape=jax.ShapeDtypeStruct((N_TOK, D_MODEL), F32),
        grid=(N_TOK // tm,),
        in_specs=[_row_spec(tm, D_MODEL),
                  _resident_spec(D_MODEL, FFN_DIM), _resident_spec(D_MODEL, FFN_DIM),
                  _resident_spec(FFN_DIM, D_MODEL),
                  _const_spec(1, D_MODEL), _const_spec(1, D_MODEL)],
        out_specs=_row_spec(tm, D_MODEL),
        compiler_params=_cparams("parallel"),
        name="ffn_ln",
    )(h, wg, wu, wd, lg, lb)


Q_DIM = N_HEADS * HEAD_DIM


def _qkv_kernel(h_ref, w_ref, rope_ref, q_ref, k_ref, v_ref):
    hb = h_ref[...].astype(BF16)
    z = jnp.dot(hb, w_ref[...], preferred_element_type=F32)
    lane = lax.broadcasted_iota(jnp.int32, (1, LANES), 1)
    table = rope_ref[...]
    swapped = pltpu.roll(table, HEAD_DIM, 1)
    cos_t = jnp.where(lane < HEAD_DIM, table, swapped)
    sin_t = jnp.where(lane < HEAD_DIM, swapped, table)
    first_half = (lane % HEAD_DIM) < ROT_HALF

    def rope(t):
        partner = jnp.where(first_half, pltpu.roll(t, LANES - ROT_HALF, 1), pltpu.roll(t, ROT_HALF, 1))
        return t * cos_t + partner * sin_t

    scale = HEAD_DIM ** -0.5
    for c in range(Q_DIM // LANES):
        sl = slice(c * LANES, (c + 1) * LANES)
        q_ref[:, sl] = (rope(z[:, sl]) * scale).astype(BF16)
    for c in range(KV_DIM // LANES):
        sl = slice(c * LANES, (c + 1) * LANES)
        k_ref[:, sl] = rope(z[:, Q_DIM + c * LANES:Q_DIM + (c + 1) * LANES]).astype(BF16)
    v_ref[...] = z[:, Q_DIM + KV_DIM:].astype(BF16)


def _qkv(h, w_qkv, rope_table, tm=512):
    return pl.pallas_call(
        _qkv_kernel,
        out_shape=[jax.ShapeDtypeStruct((N_TOK, Q_DIM), BF16),
                   jax.ShapeDtypeStruct((N_TOK, KV_DIM), BF16),
                   jax.ShapeDtypeStruct((N_TOK, KV_DIM), BF16)],
        grid=(N_TOK // tm,),
        in_specs=[_row_spec(tm, D_MODEL), _const_spec(D_MODEL, Q_DIM + 2 * KV_DIM),
                  _row_spec(tm, LANES)],
        out_specs=[_row_spec(tm, Q_DIM), _row_spec(tm, KV_DIM), _row_spec(tm, KV_DIM)],
        compiler_params=_cparams("parallel"),
        name="qkv_rope",
    )(h, w_qkv, rope_table)


N_BLOCKS = SEQ // BLOCK


ATTN_Q_TILE = 512
ATTN_SUB = ATTN_Q_TILE // BLOCK
ATTN_BAND = ATTN_Q_TILE + BLOCK
KV_PAIRS = N_KV_HEADS // 2

HEAD_ORDER = [h for c in range(KV_PAIRS) for g in range(Q_PER_KV)
              for h in (2 * c * Q_PER_KV + g, (2 * c + 1) * Q_PER_KV + g)]


def _band_bias():
    qi = jnp.arange(Q_PER_KV * BLOCK)[:, None] % BLOCK
    si = jnp.arange(2 * BLOCK)[None, :]
    rel = qi + BLOCK - si
    valid = (rel >= 0) & (rel < WINDOW)
    first = valid & (si >= BLOCK)
    neg = jnp.float32(-jnp.inf)
    return jnp.stack([jnp.where(valid, 0.0, neg), jnp.where(first, 0.0, neg)]).astype(F32)


def _attn_kernel(sink_ref, q_ref, kp_ref, kc_ref, vp_ref, vc_ref, bias_ref, h_ref, w_ref, lg_ref, lb_ref,
                 out_ref, ka_ref, kb_ref, va_ref, vb_ref, o_ref):
    i = pl.program_id(1)
    lane = lax.broadcasted_iota(jnp.int32, (1, KV_DIM), 1)
    low = (lane % LANES) < HEAD_DIM
    zero = jnp.zeros((), BF16)
    for src, dst_a, dst_b in ((kp_ref, ka_ref, kb_ref), (vp_ref, va_ref, vb_ref)):
        t = src[...]
        dst_a[0:BLOCK, :] = jnp.where(low, t, zero)
        dst_b[0:BLOCK, :] = jnp.where(low, zero, t)
    for src, dst_a, dst_b in ((kc_ref, ka_ref, kb_ref), (vc_ref, va_ref, vb_ref)):
        t = src[...]
        dst_a[BLOCK:ATTN_BAND, :] = jnp.where(low, t, zero)
        dst_b[BLOCK:ATTN_BAND, :] = jnp.where(low, zero, t)

    lane1 = lax.broadcasted_iota(jnp.int32, (1, LANES), 1)
    row = lax.broadcasted_iota(jnp.int32, (Q_PER_KV * BLOCK, 1), 0)
    contract_last = (((1,), (1,)), ((), ()))
    for blk in range(ATTN_SUB):
        r0 = blk * BLOCK
        if blk == 0:
            bias = bias_ref[jnp.where(i == 0, 1, 0)]
        else:
            bias = bias_ref[0]
        for c in range(KV_PAIRS):
            cs = slice(c * LANES, (c + 1) * LANES)
            chunks = [c * Q_PER_KV + g for g in range(Q_PER_KV)]
            q4 = jnp.concatenate([q_ref[r0:r0 + BLOCK, m * LANES:(m + 1) * LANES] for m in chunks], axis=0)
            outs = []
            for half, (k_ref, v_ref) in enumerate(((ka_ref, va_ref), (kb_ref, vb_ref))):
                kband = k_ref[r0:r0 + 2 * BLOCK, cs]
                s = lax.dot_general(q4, kband, contract_last, preferred_element_type=F32) + bias
                sink = jnp.zeros((Q_PER_KV * BLOCK, 1), F32)
                for g, m in enumerate(chunks):
                    sink = jnp.where(row // BLOCK == g, sink_ref[HEAD_ORDER[2 * m + half]], sink)
                mx = jnp.maximum(jnp.max(s, axis=-1, keepdims=True), sink)
                p = jnp.exp(s - mx)
                denom = jnp.sum(p, axis=-1, keepdims=True) + jnp.exp(sink - mx)
                pv = jnp.dot(p.astype(BF16), v_ref[r0:r0 + 2 * BLOCK, cs], preferred_element_type=F32)
                outs.append((pv, 1.0 / denom))
            (pv_a, r_a), (pv_b, r_b) = outs
            o = (pv_a + pv_b) * jnp.where(lane1 < HEAD_DIM, r_a, r_b)
            for g, m in enumerate(chunks):
                o_ref[r0:r0 + BLOCK, m * LANES:(m + 1) * LANES] = o[g * BLOCK:(g + 1) * BLOCK].astype(BF16)

    mix = jnp.dot(o_ref[...], w_ref[...], preferred_element_type=F32)
    out_ref[...] = _layer_norm(DEEPNORM_ALPHA * h_ref[...] + mix, lg_ref[...], lb_ref[...])


def _attention_ln(q, k, v, sinks, h, w_out, lg, lb):
    tiles = SEQ // ATTN_Q_TILE
    cur = lambda b, i: (b * tiles + i, 0)
    prev = lambda b, i: (jnp.maximum((b * tiles + i) * ATTN_SUB - 1, 0), 0)
    const2 = lambda b, i: (0, 0)
    return pl.pallas_call(
        _attn_kernel,
        out_shape=jax.ShapeDtypeStruct((N_TOK, D_MODEL), F32),
        grid=(BATCH, tiles),
        in_specs=[pl.BlockSpec(memory_space=pltpu.SMEM),
                  pl.BlockSpec((ATTN_Q_TILE, Q_DIM), cur),
                  pl.BlockSpec((BLOCK, KV_DIM), prev), pl.BlockSpec((ATTN_Q_TILE, KV_DIM), cur),
                  pl.BlockSpec((BLOCK, KV_DIM), prev), pl.BlockSpec((ATTN_Q_TILE, KV_DIM), cur),
                  pl.BlockSpec((2, Q_PER_KV * BLOCK, 2 * BLOCK), lambda b, i: (0, 0, 0)),
                  pl.BlockSpec((ATTN_Q_TILE, D_MODEL), cur),
                  pl.BlockSpec((Q_DIM, D_MODEL), const2),
                  pl.BlockSpec((1, D_MODEL), const2), pl.BlockSpec((1, D_MODEL), const2)],
        out_specs=pl.BlockSpec((ATTN_Q_TILE, D_MODEL), cur),
        scratch_shapes=[pltpu.VMEM((ATTN_BAND, KV_DIM), BF16)] * 4
                       + [pltpu.VMEM((ATTN_Q_TILE, Q_DIM), BF16)],
        compiler_params=_cparams("parallel", "parallel"),
        name="swa_attention_ln",
    )(sinks, q, k, k, v, v, _band_bias(), h, w_out, lg, lb)


ROUTE_TILE = 512
EXPERT_TILE = 512
SLAB = D_MODEL // LANES
EXPERT_REGION = N_TOK
REGION_TILES = EXPERT_REGION // EXPERT_TILE
N_SORTED = N_EXPERTS * EXPERT_REGION
N_EXPERT_TILES = 2 * N_TOK // EXPERT_TILE + N_EXPERTS
META_E1, META_E2, META_R1, META_R2, META_W1, META_W2 = range(6)
META_D1, META_D2 = 8, 9
COMBINE_CHUNK = 64
N_ROUTE_TILES = N_TOK // ROUTE_TILE
ROUTER_BURST = 64


def _to_slabs(ref, val, rows):
    for c in range(SLAB):
        ref[pl.ds(c, rows, stride=SLAB), :] = val[:, c * LANES:(c + 1) * LANES]


def _from_slabs(ref, rows, first_row=0):
    return jnp.concatenate([ref[pl.ds(first_row * SLAB + c, rows, stride=SLAB), :] for c in range(SLAB)],
                           axis=1)


def _row_slab(ref, row):
    return ref.at[pl.ds(pl.multiple_of(row * SLAB, SLAB), SLAB)]


def _split_bf16(x):
    def top_bits(v):
        bits = lax.bitcast_convert_type(v, jnp.uint32) & jnp.uint32(0xFFFF0000)
        return lax.bitcast_convert_type(bits, F32)
    hi = top_bits(x)
    mid = top_bits(x - hi)
    lo = (x - hi) - mid
    return hi.astype(BF16), mid.astype(BF16), lo.astype(BF16)


def _router_kernel(h_ref, wr_ref, br_ref, tri_ref, meta_ref, counts_ref, dest_ref, xs_hbm,
                   carry_ref, slab_a, slab_b, zero_ref, dest_v, dest_a, dest_b, count_v, count_s,
                   sem, pad_sem, idx_sem):
    i = pl.program_id(0)
    slabs = (slab_a, slab_b)
    dests = (dest_a, dest_b)

    @pl.when(i == 0)
    def _():
        carry_ref[...] = jnp.zeros_like(carry_ref)
        zero_ref[...] = jnp.zeros_like(zero_ref)

    def start_row(slot, t):
        for k in range(2):
            pltpu.make_async_copy(_row_slab(slabs[slot], t), _row_slab(xs_hbm, dests[slot][k, t]),
                                  sem.at[slot]).start(priority=k)

    def wait_scatter(slot):
        for _ in range(2):
            pltpu.make_async_copy(slabs[slot], xs_hbm.at[pl.ds(0, ROUTE_TILE * SLAB)], sem.at[slot]).wait()

    def route(slot, scatter_previous):
        pending = list(range(0, ROUTE_TILE, ROUTER_BURST)) if scatter_previous else []

        def burst():
            if pending:
                r0 = pending.pop(0)
                for t in range(r0, r0 + ROUTER_BURST):
                    start_row(1 - slot, t)

        lane = lax.broadcasted_iota(jnp.int32, (1, LANES), 1)
        lane_f = lane.astype(F32)
        h = h_ref[...]
        h0, h1, h2 = _split_bf16(h)
        w0, w1, w2 = wr_ref[0], wr_ref[1], wr_ref[2]
        small = None
        for a, b in ((h2, w0), (h0, w2), (h1, w1)):
            part = jnp.dot(a, b, preferred_element_type=F32)
            small = part if small is None else small + part
            burst()
        mid = None
        for a, b in ((h1, w0), (h0, w1)):
            part = jnp.dot(a, b, preferred_element_type=F32)
            mid = part if mid is None else mid + part
            burst()
        logits = (small + mid) + jnp.dot(h0, w0, preferred_element_type=F32) + br_ref[...]
        burst()
        logits = jnp.where(lane < N_EXPERTS, logits, -jnp.inf)
        m1 = jnp.max(logits, axis=-1, keepdims=True)
        i1 = jnp.min(jnp.where(logits == m1, lane_f, float(LANES)), axis=-1, keepdims=True)
        rest = jnp.where(lane_f == i1, -jnp.inf, logits)
        m2 = jnp.max(rest, axis=-1, keepdims=True)
        i2 = jnp.min(jnp.where(rest == m2, lane_f, float(LANES)), axis=-1, keepdims=True)
        e2 = jnp.exp(m2 - m1)
        tot = 1.0 + e2
        burst()
        chosen = jnp.where((lane_f == i1) | (lane_f == i2), 1.0, 0.0)
        before = jnp.dot(tri_ref[...], chosen.astype(BF16), preferred_element_type=F32) + carry_ref[...]
        r1 = jnp.sum(jnp.where(lane_f == i1, before, 0.0), axis=-1, keepdims=True)
        r2 = jnp.sum(jnp.where(lane_f == i2, before, 0.0), axis=-1, keepdims=True)
        new_carry = carry_ref[...] + jnp.sum(chosen, axis=0, keepdims=True)
        carry_ref[...] = new_carry
        counts_ref[...] = new_carry.astype(jnp.int32)
        burst()
        record = jnp.zeros((ROUTE_TILE, LANES), F32)
        for lane_id, col in ((META_E1, i1), (META_E2, i2), (META_R1, r1), (META_R2, r2),
                             (META_W1, 1.0 / tot), (META_W2, e2 / tot),
                             (META_D1, i1 * EXPERT_REGION + r1), (META_D2, i2 * EXPERT_REGION + r2)):
            record = jnp.where(lane == lane_id, col, record)
        meta_ref[...] = record
        _to_slabs(slabs[slot], h, ROUTE_TILE)
        while pending:
            burst()
        dest_rows = record.T[META_D1:META_D1 + 8, :].astype(jnp.int32)
        dest_ref[...] = dest_rows
        dest_v[...] = dest_rows
        to_smem = pltpu.make_async_copy(dest_v, dests[slot], idx_sem)
        to_smem.start()
        to_smem.wait()

    @pl.when(i == 0)
    def _():
        route(0, False)

    for slot in range(2):
        @pl.when((i > 0) & (i % 2 == slot))
        def _():
            @pl.when(i >= 2)
            def _():
                wait_scatter(slot)
            route(slot, True)

    @pl.when(i == N_ROUTE_TILES - 1)
    def _():
        slot = (N_ROUTE_TILES - 1) % 2
        lax.fori_loop(0, ROUTE_TILE, lambda t, c: (start_row(slot, t), c)[1], 0, unroll=8)
        wait_scatter(1 - slot)
        wait_scatter(slot)
        count_v[...] = jnp.broadcast_to(carry_ref[...].astype(jnp.int32), count_v.shape)
        cp = pltpu.make_async_copy(count_v, count_s, idx_sem)
        cp.start()
        cp.wait()
        for e in range(N_EXPERTS):
            n = count_s[0, e]
            end = (n + EXPERT_TILE - 1) // EXPERT_TILE * EXPERT_TILE
            pad = lambda r: pltpu.make_async_copy(zero_ref, _row_slab(xs_hbm, e * EXPERT_REGION + r), pad_sem)
            lax.fori_loop(n, end, lambda r, c: (pad(r).start(), c)[1], 0)
            lax.fori_loop(n, end, lambda r, c: (pad(r).wait(), c)[1], 0)


def _router(h, w_router, b_router):
    tri = (jnp.arange(ROUTE_TILE)[:, None] > jnp.arange(ROUTE_TILE)[None, :]).astype(BF16)
    w_split = jnp.stack(_split_bf16(w_router))
    return pl.pallas_call(
        _router_kernel,
        out_shape=[jax.ShapeDtypeStruct((N_TOK, LANES), F32),
                   jax.ShapeDtypeStruct((1, LANES), jnp.int32),
                   jax.ShapeDtypeStruct((8, N_TOK), jnp.int32),
                   jax.ShapeDtypeStruct((N_SORTED * SLAB, LANES), F32)],
        grid=(N_TOK // ROUTE_TILE,),
        in_specs=[_row_spec(ROUTE_TILE, D_MODEL), _const_spec(3, D_MODEL, LANES),
                  _const_spec(1, LANES), _const_spec(ROUTE_TILE, ROUTE_TILE)],
        out_specs=[_row_spec(ROUTE_TILE, LANES), _const_spec(1, LANES),
                   pl.BlockSpec((8, ROUTE_TILE), lambda i: (0, i)), pl.BlockSpec(memory_space=pl.ANY)],
        scratch_shapes=[pltpu.VMEM((1, LANES), F32),
                        pltpu.VMEM((ROUTE_TILE * SLAB, LANES), F32), pltpu.VMEM((ROUTE_TILE * SLAB, LANES), F32),
                        pltpu.VMEM((SLAB, LANES), F32),
                        pltpu.VMEM((8, ROUTE_TILE), jnp.int32),
                        pltpu.SMEM((8, ROUTE_TILE), jnp.int32), pltpu.SMEM((8, ROUTE_TILE), jnp.int32),
                        pltpu.VMEM((8, LANES), jnp.int32), pltpu.SMEM((8, LANES), jnp.int32),
                        pltpu.SemaphoreType.DMA((2,)), pltpu.SemaphoreType.DMA, pltpu.SemaphoreType.DMA],
        compiler_params=_cparams("arbitrary"),
        name="moe_router",
    )(h, w_split, b_router, tri)


def _experts_kernel(tblk_ref, texp_ref, xs_ref, wg_ref, wu_ref, wd_ref, ys_ref, wb_ref):
    i = pl.program_id(0)

    @pl.when((i == 0) | (texp_ref[i] != texp_ref[jnp.maximum(i - 1, 0)]))
    def _():
        for j, w_ref in enumerate((wg_ref, wu_ref, wd_ref)):
            wb_ref[j] = w_ref[0].astype(BF16)

    x = _from_slabs(xs_ref, EXPERT_TILE).astype(BF16)
    gate = jnp.dot(x, wb_ref[0], preferred_element_type=F32)
    up = jnp.dot(x, wb_ref[1], preferred_element_type=F32)
    act = (jax.nn.silu(gate) * up).astype(BF16)
    _to_slabs(ys_ref, jnp.dot(act, wb_ref[2], preferred_element_type=F32), EXPERT_TILE)


def _experts(tile_block, tile_expert, xs, wg, wu, wd):
    wspec = lambda a, b: pl.BlockSpec((1, a, b), lambda i, tblk, texp: (texp[i], 0, 0))
    rows = pl.BlockSpec((EXPERT_TILE * SLAB, LANES), lambda i, tblk, texp: (tblk[i], 0))
    return pl.pallas_call(
        _experts_kernel,
        out_shape=jax.ShapeDtypeStruct((N_SORTED * SLAB, LANES), F32),
        grid_spec=pltpu.PrefetchScalarGridSpec(
            num_scalar_prefetch=2,
            grid=(N_EXPERT_TILES,),
            in_specs=[rows, wspec(D_MODEL, EXPERT_DIM), wspec(D_MODEL, EXPERT_DIM),
                      wspec(EXPERT_DIM, D_MODEL)],
            out_specs=rows,
            scratch_shapes=[pltpu.VMEM((3, D_MODEL, EXPERT_DIM), BF16)]),
        compiler_params=_cparams("arbitrary"),
        name="moe_experts",
    )(tile_block, tile_expert, xs, wg, wu, wd)


def _combine_ln_kernel(dest_ref, meta_ref, h_ref, ys_hbm, lg_ref, lb_ref, o_ref, ya_ref, yb_ref, sem):
    i = pl.program_id(0)
    bufs = (ya_ref, yb_ref)

    def start_gather(tile, slot):
        def body(t, carry):
            for k in range(2):
                row = dest_ref[k * N_TOK + tile * ROUTE_TILE + t]
                pltpu.make_async_copy(_row_slab(ys_hbm, row), _row_slab(bufs[slot], k * ROUTE_TILE + t),
                                      sem.at[slot]).start(priority=k)
            return carry
        lax.fori_loop(0, ROUTE_TILE, body, 0, unroll=8)

    def wait_gather(slot):
        pltpu.make_async_copy(ys_hbm.at[pl.ds(0, 2 * ROUTE_TILE * SLAB)], bufs[slot], sem.at[slot]).wait()

    last = pl.num_programs(0) - 1
    lane = lax.broadcasted_iota(jnp.int32, (1, LANES), 1)

    def finish_and_prefetch(slot):
        nxt = jnp.minimum(i + 1, last) * ROUTE_TILE
        wait_gather(slot)
        for r0 in range(0, ROUTE_TILE, COMBINE_CHUNK):
            rows = slice(r0, r0 + COMBINE_CHUNK)
            meta = meta_ref[rows, :]
            w1 = jnp.sum(jnp.where(lane == META_W1, meta, 0.0), axis=-1, keepdims=True)
            w2 = jnp.sum(jnp.where(lane == META_W2, meta, 0.0), axis=-1, keepdims=True)
            ff = (w1 * _from_slabs(bufs[slot], COMBINE_CHUNK, first_row=r0)
                  + w2 * _from_slabs(bufs[slot], COMBINE_CHUNK, first_row=ROUTE_TILE + r0))
            o_ref[rows, :] = _layer_norm(DEEPNORM_ALPHA * h_ref[rows, :] + ff, lg_ref[...], lb_ref[...])
            for t in range(r0, r0 + COMBINE_CHUNK):
                for k in range(2):
                    row = dest_ref[k * N_TOK + nxt + t]
                    pltpu.make_async_copy(_row_slab(ys_hbm, row),
                                          _row_slab(bufs[1 - slot], k * ROUTE_TILE + t),
                                          sem.at[1 - slot]).start(priority=k)

        @pl.when(i == last)
        def _():
            wait_gather(1 - slot)

    @pl.when(i == 0)
    def _():
        start_gather(0, 0)

    for slot in range(2):
        @pl.when(i % 2 == slot)
        def _():
            finish_and_prefetch(slot)


def _combine_ln(dest, meta, h, ys, lg, lb):
    row = lambda width: pl.BlockSpec((ROUTE_TILE, width), lambda i, dest: (i, 0))
    const = pl.BlockSpec((1, D_MODEL), lambda i, dest: (0, 0))
    return pl.pallas_call(
        _combine_ln_kernel,
        out_shape=jax.ShapeDtypeStruct((N_TOK, D_MODEL), F32),
        grid_spec=pltpu.PrefetchScalarGridSpec(
            num_scalar_prefetch=1,
            grid=(N_TOK // ROUTE_TILE,),
            in_specs=[row(LANES), row(D_MODEL), pl.BlockSpec(memory_space=pl.ANY), const, const],
            out_specs=row(D_MODEL),
            scratch_shapes=[pltpu.VMEM((2 * ROUTE_TILE * SLAB, LANES), F32)] * 2
                           + [pltpu.SemaphoreType.DMA((2,))]),
        compiler_params=_cparams("arbitrary"),
        name="moe_combine_ln",
    )(dest, meta, h, ys, lg, lb)


def _moe_ln(h, w_router, b_router, wg, wu, wd, lg, lb):
    meta, counts, dest_rows, xs = _router(h, w_router, b_router)
    expert = jnp.arange(N_EXPERTS)
    tiles = (counts[0, :N_EXPERTS] + EXPERT_TILE - 1) // EXPERT_TILE
    ends = jnp.sum(jnp.where(expert[None, :] <= expert[:, None], tiles[None, :], 0), axis=1)
    step = jnp.minimum(jnp.arange(N_EXPERT_TILES), ends[-1] - 1)
    tile_expert = jnp.sum(step[:, None] >= ends[None, :], axis=1)
    first = jnp.sum(jnp.where(tile_expert[:, None] == expert[None, :], (ends - tiles)[None, :], 0), axis=1)
    tile_block = tile_expert * REGION_TILES + (step - first)
    ys = _experts(tile_block.astype(jnp.int32), tile_expert.astype(jnp.int32), xs, wg, wu, wd)
    dest = dest_rows[:2].reshape(2 * N_TOK)
    return _combine_ln(dest, meta, h, ys, lg, lb)


def kernel(x, positions, ln_g, ln_b, ssm_lambda_re, ssm_lambda_im, ssm_log_step, ssm_b_re, ssm_b_im, ssm_c_re, ssm_c_im, ssm_d, ssm_w_glu, kv_w, attn_w_q, attn_sinks, attn_w_out, ffn_w_gate, ffn_w_up, ffn_w_down, moe_w_router, moe_b_router, moe_w_gate, moe_w_up, moe_w_down):
    ln = lambda layer, j: (ln_g[layer, j].reshape(1, D_MODEL).astype(F32),
                           ln_b[layer, j].reshape(1, D_MODEL).astype(F32))
    rope_table = _rope_tables(positions)
    x2 = x.reshape(N_TOK, D_MODEL)

    params = _ssm_params(ssm_lambda_re[0], ssm_lambda_im[0], ssm_log_step[0], ssm_b_re[0], ssm_b_im[0],
                         ssm_c_re[0], ssm_c_im[0])
    y = _ssm_mixer(x2, params)
    h = _glu_ln(y, x2, ssm_d[0].astype(F32).reshape(1, D_MODEL), ssm_w_glu[0].astype(BF16), *ln(0, 0))
    h = _ffn_ln(h, ffn_w_gate[0].astype(BF16), ffn_w_up[0].astype(BF16), ffn_w_down[0].astype(BF16),
                *ln(0, 1))

    order = jnp.array(HEAD_ORDER)
    w_q = attn_w_q[0].reshape(D_MODEL, N_HEADS, HEAD_DIM)[:, order].reshape(D_MODEL, Q_DIM)
    w_out = attn_w_out[0].reshape(N_HEADS, HEAD_DIM, D_MODEL)[order].reshape(Q_DIM, D_MODEL)
    w_qkv = jnp.concatenate([w_q, kv_w], axis=1).astype(BF16)
    q, k, v = _qkv(h, w_qkv, rope_table)
    h = _attention_ln(q, k, v, attn_sinks[0].astype(F32), h, w_out.astype(BF16), *ln(1, 0))
    w_router = jnp.pad(moe_w_router[0].astype(F32), ((0, 0), (0, LANES - N_EXPERTS)))
    b_router = jnp.pad(moe_b_router[0].astype(F32), (0, LANES - N_EXPERTS)).reshape(1, LANES)
    h = _moe_ln(h, w_router, b_router, moe_w_gate[0].astype(F32), moe_w_up[0].astype(F32),
                moe_w_down[0].astype(F32), *ln(1, 1))
    return h.reshape(BATCH, SEQ, D_MODEL)
```

```python
import functools
import math

import jax
import jax.numpy as jnp
import numpy as np
from jax import lax
from jax.experimental import pallas as pl
from jax.experimental.pallas import tpu as pltpu

F32 = jnp.float32
BF16 = jnp.bfloat16

D_MODEL = 1024
BATCH = 16
SEQ = 2048
N_TOK = BATCH * SEQ
DEPTH = 2

SSM_GROUP_CH = 16
SSM_GROUPS = D_MODEL // SSM_GROUP_CH
SSM_STATE = 64
SSM_CHUNK = 16
SSM_NCHUNK = SEQ // SSM_CHUNK
SSM_ROWS = SSM_NCHUNK * BATCH
SSM_WIDTH = SSM_CHUNK * SSM_GROUP_CH
SSM_SCAN_STEPS = int(math.log2(SSM_NCHUNK))

N_HEADS = 16
HEAD_DIM = 64
N_KV_HEADS = 4
Q_PER_KV = N_HEADS // N_KV_HEADS
KV_DIM = N_KV_HEADS * HEAD_DIM
WINDOW = 128
BLOCK = 128
ROT_DIM = HEAD_DIM // 4
ROT_HALF = ROT_DIM // 2
ROPE_THETA = 500000.0

FFN_DIM = 2816
N_EXPERTS = 8
EXPERT_DIM = 1024

DEEPNORM_ALPHA = (2 * DEPTH) ** 0.25
LN_EPS = 1e-5

LANES = 128
VMEM_LIMIT = 56 * 1024 * 1024


def _cparams(*sem):
    return pltpu.CompilerParams(dimension_semantics=sem, vmem_limit_bytes=VMEM_LIMIT)


def _layer_norm(r, g, b):
    mu = jnp.mean(r, axis=-1, keepdims=True)
    xc = r - mu
    var = jnp.mean(xc * xc, axis=-1, keepdims=True)
    return xc * lax.rsqrt(var + LN_EPS) * g + b


def _rope_kernel(invf_ref, pos_ref, cs_ref):
    pos = pos_ref[...]
    for f in range(ROT_HALF):
        ang = pos * invf_ref[f]
        cs_ref[f] = jnp.cos(ang)
        cs_ref[ROT_HALF + f] = jnp.sin(ang)


def _rope_tables(positions):
    inv_freq = ROPE_THETA ** (-jnp.arange(0, ROT_DIM, 2, dtype=F32) / ROT_DIM)
    pos = positions.astype(F32)
    cs = pl.pallas_call(
        _rope_kernel,
        out_shape=jax.ShapeDtypeStruct((ROT_DIM, BATCH, SEQ), F32),
        in_specs=[pl.BlockSpec(memory_space=pltpu.SMEM),
                  pl.BlockSpec(memory_space=pltpu.VMEM)],
        out_specs=pl.BlockSpec(memory_space=pltpu.VMEM),
        name="rope_tables",
    )(inv_freq, pos)
    return cs.reshape(ROT_DIM, N_TOK).T


def _rope_placement():
    f = np.arange(ROT_HALF)
    place = np.zeros((ROT_DIM, LANES), np.float32)
    place[f, f] = place[f, ROT_HALF + f] = 1.0
    place[ROT_HALF + f, HEAD_DIM + f] = -1.0
    place[ROT_HALF + f, HEAD_DIM + ROT_HALF + f] = 1.0
    lane = np.arange(LANES)
    bias = ((lane >= ROT_DIM) & (lane < HEAD_DIM)).astype(np.float32).reshape(1, LANES)
    return jnp.asarray(place, BF16), jnp.asarray(bias)


def _ssm_params(lam_re, lam_im, log_step, b_re, b_im, c_re, c_im):
    hp = lax.Precision.HIGHEST
    lr, li = lam_re.astype(F32), lam_im.astype(F32)
    dt = jnp.exp(log_step.astype(F32))[:, None]
    mag = jnp.exp(lr * dt)
    ar = mag * jnp.cos(li * dt)
    ai = mag * jnp.sin(li * dt)
    nr = ar - 1.0
    den = lr * lr + li * li
    kr = (nr * lr + ai * li) / den
    ki = (ai * lr - nr * li) / den
    br, bi = b_re.astype(F32), b_im.astype(F32)
    bbar_r = kr[..., None] * br - ki[..., None] * bi
    bbar_i = kr[..., None] * bi + ki[..., None] * br
    cr, ci = c_re.astype(F32), c_im.astype(F32)

    def powers(taus):
        t = taus.astype(F32)[:, None, None]
        m = jnp.exp(lr[None] * dt[None] * t)
        ang = li[None] * dt[None] * t
        return m * jnp.cos(ang), m * jnp.sin(ang)

    er, ei = powers(jnp.arange(SSM_CHUNK + 1))
    w_r = er[:, :, :, None] * bbar_r[None] - ei[:, :, :, None] * bbar_i[None]
    w_i = er[:, :, :, None] * bbar_i[None] + ei[:, :, :, None] * bbar_r[None]
    kern = (jnp.einsum('gcp,tgpd->gtcd', cr, w_r[:SSM_CHUNK], precision=hp)
            - jnp.einsum('gcp,tgpd->gtcd', ci, w_i[:SSM_CHUNK], precision=hp))
    s_idx = jnp.arange(SSM_CHUNK)[:, None, None]
    t_idx = jnp.arange(SSM_CHUNK)[None, :, None]
    lag = (t_idx - s_idx == jnp.arange(SSM_CHUNK)[None, None, :]).astype(F32)
    toep = jnp.einsum('stu,gucd->gtcsd', lag, kern, precision=hp)
    mt = toep.reshape(SSM_GROUPS, SSM_WIDTH, SSM_WIDTH)
    rev = SSM_CHUNK - 1 - jnp.arange(SSM_CHUNK)
    g_r = w_r[rev].transpose(1, 2, 0, 3).reshape(SSM_GROUPS, SSM_STATE, SSM_WIDTH)
    g_i = w_i[rev].transpose(1, 2, 0, 3).reshape(SSM_GROUPS, SSM_STATE, SSM_WIDTH)
    gt = jnp.concatenate([g_r, g_i], axis=1)
    e1r, e1i = er[1:], ei[1:]
    ce_r = cr[None] * e1r[:, :, None, :] - ci[None] * e1i[:, :, None, :]
    ce_i = cr[None] * e1i[:, :, None, :] + ci[None] * e1r[:, :, None, :]
    c_re = ce_r.transpose(1, 0, 2, 3).reshape(SSM_GROUPS, SSM_WIDTH, SSM_STATE)
    c_im = (-ce_i).transpose(1, 0, 2, 3).reshape(SSM_GROUPS, SSM_WIDTH, SSM_STATE)
    ct = jnp.concatenate([c_re, c_im], axis=-1)
    sr, si = powers(SSM_CHUNK * (2 ** jnp.arange(8)))
    coef = lambda t: jnp.concatenate([t, t], axis=-1).transpose(1, 0, 2)
    return mt.astype(BF16), gt.astype(BF16), ct.astype(BF16), coef(sr), coef(si)


SSM_LANE_GROUPS = LANES // SSM_GROUP_CH
SSM_LANE_CHUNKS = D_MODEL // LANES
SSM_BATCH_TILE = 4
SSM_COLS = SSM_BATCH_TILE * SSM_NCHUNK


def _ssm_kernel(x_ref, mt_ref, gt_ref, ct_ref, cr_ref, ci_ref, y_ref, v_ref, yt_ref):
    for bl in range(SSM_BATCH_TILE):
        cols = slice(bl * SSM_NCHUNK, (bl + 1) * SSM_NCHUNK)
        for s in range(SSM_CHUNK):
            a = x_ref[pl.ds(bl * SEQ + s, SSM_NCHUNK, stride=SSM_CHUNK), :]
            at = a.T.astype(BF16)
            for g in range(SSM_LANE_GROUPS):
                v_ref[g, s * SSM_GROUP_CH:(s + 1) * SSM_GROUP_CH, cols] = (
                    at[g * SSM_GROUP_CH:(g + 1) * SSM_GROUP_CH])

    row_j = lax.broadcasted_iota(jnp.int32, (SSM_NCHUNK, 1), 0)

    def group_body(g, carry):
        ut = v_ref[g]
        xt = jnp.dot(gt_ref[g], ut, preferred_element_type=F32)
        cr_all = cr_ref[g]
        ci_all = ci_ref[g]

        def seg(rows, b):
            return xt[rows, b * SSM_NCHUNK:(b + 1) * SSM_NCHUNK]

        def shifted(x, d):
            if d < 8:
                return jnp.where(row_j >= d, pltpu.roll(x, d, 0), 0.0)
            return jnp.concatenate([jnp.zeros((d, LANES), F32), x[:SSM_NCHUNK - d]], axis=0)

        re_rows, im_rows = slice(0, SSM_STATE), slice(SSM_STATE, 2 * SSM_STATE)
        sr, si = [], []
        for q in range(SSM_BATCH_TILE // 2):
            xr = jnp.concatenate([seg(re_rows, 2 * q), seg(re_rows, 2 * q + 1)], axis=0).T
            xi = jnp.concatenate([seg(im_rows, 2 * q), seg(im_rows, 2 * q + 1)], axis=0).T
            for k in range(SSM_SCAN_STEPS):
                cr = cr_all[k:k + 1, :]
                ci = ci_all[k:k + 1, :]
                rr = shifted(xr, 1 << k)
                ri = shifted(xi, 1 << k)
                xr, xi = xr + cr * rr - ci * ri, xi + cr * ri + ci * rr
            pr = shifted(xr, 1).T
            pi = shifted(xi, 1).T
            sr += [pr[:SSM_STATE], pr[SSM_STATE:]]
            si += [pi[:SSM_STATE], pi[SSM_STATE:]]
        sprev = jnp.concatenate([jnp.concatenate(sr, axis=1), jnp.concatenate(si, axis=1)],
                                axis=0).astype(BF16)
        yt_ref[g] = (jnp.dot(mt_ref[g], ut, preferred_element_type=F32)
                     + jnp.dot(ct_ref[g], sprev, preferred_element_type=F32))
        return carry

    lax.fori_loop(0, SSM_LANE_GROUPS, group_body, 0, unroll=2)

    for bl in range(SSM_BATCH_TILE):
        cols = slice(bl * SSM_NCHUNK, (bl + 1) * SSM_NCHUNK)
        for t in range(SSM_CHUNK):
            tile = jnp.concatenate(
                [yt_ref[g, t * SSM_GROUP_CH:(t + 1) * SSM_GROUP_CH, cols] for g in range(SSM_LANE_GROUPS)],
                axis=0)
            y_ref[pl.ds(bl * SEQ + t, SSM_NCHUNK, stride=SSM_CHUNK), :] = tile.T


def _ssm_mixer(x2, params):
    mt, gt, ct, coef_r, coef_i = params
    rows = SSM_BATCH_TILE * SEQ
    pspec = lambda a, b: pl.BlockSpec((SSM_LANE_GROUPS, a, b), lambda k, q: (k, 0, 0))
    return pl.pallas_call(
        _ssm_kernel,
        out_shape=jax.ShapeDtypeStruct((N_TOK, D_MODEL), F32),
        grid=(SSM_LANE_CHUNKS, BATCH // SSM_BATCH_TILE),
        in_specs=[pl.BlockSpec((rows, LANES), lambda k, q: (q, k)),
                  pspec(SSM_WIDTH, SSM_WIDTH), pspec(2 * SSM_STATE, SSM_WIDTH), pspec(SSM_WIDTH, 2 * SSM_STATE),
                  pspec(8, 2 * SSM_STATE), pspec(8, 2 * SSM_STATE)],
        out_specs=pl.BlockSpec((rows, LANES), lambda k, q: (q, k)),
        scratch_shapes=[pltpu.VMEM((SSM_LANE_GROUPS, SSM_WIDTH, SSM_COLS), BF16),
                        pltpu.VMEM((SSM_LANE_GROUPS, SSM_WIDTH, SSM_COLS), F32)],
        compiler_params=_cparams("parallel", "parallel"),
        name="ssm_mixer",
    )(x2, mt, gt, ct, coef_r, coef_i)


GLU_COLS = 256


def _glu_ln_kernel(y_ref, x_ref, d_ref, w_ref, lg_ref, lb_ref, o_ref):
    x = x_ref[...]
    act = jax.nn.gelu(y_ref[...] + d_ref[...] * x).astype(BF16)
    parts = []
    for c in range(0, D_MODEL, GLU_COLS):
        val = jnp.dot(act, w_ref[:, c:c + GLU_COLS], preferred_element_type=F32)
        gate = jnp.dot(act, w_ref[:, D_MODEL + c:D_MODEL + c + GLU_COLS], preferred_element_type=F32)
        parts.append(val * jax.nn.sigmoid(gate))
    mix = jnp.concatenate(parts, axis=1)
    o_ref[...] = _layer_norm(DEEPNORM_ALPHA * x + mix, lg_ref[...], lb_ref[...])


def _row_spec(tm, width):
    return pl.BlockSpec((tm, width), lambda i: (i, 0))


def _const_spec(*shape):
    return pl.BlockSpec(shape, lambda *_: (0,) * len(shape))


def _glu_ln(y, x2, d_skip, w_glu, lg, lb, tm=512):
    return pl.pallas_call(
        _glu_ln_kernel,
        out_shape=jax.ShapeDtypeStruct((N_TOK, D_MODEL), F32),
        grid=(N_TOK // tm,),
        in_specs=[_row_spec(tm, D_MODEL), _row_spec(tm, D_MODEL), _const_spec(1, D_MODEL),
                  _const_spec(D_MODEL, 2 * D_MODEL), _const_spec(1, D_MODEL), _const_spec(1, D_MODEL)],
        out_specs=_row_spec(tm, D_MODEL),
        compiler_params=_cparams("parallel"),
        name="glu_ln",
    )(y, x2, d_skip, w_glu, lg, lb)


FFN_SPLIT = 2
FFN_TILE = FFN_DIM // FFN_SPLIT


def _ffn_ln_kernel(h_ref, wg_ref, wu_ref, wd_ref, lg_ref, lb_ref, o_ref):
    h = h_ref[...]
    hb = h.astype(BF16)
    ff = None
    for f in range(FFN_SPLIT):
        cols = slice(f * FFN_TILE, (f + 1) * FFN_TILE)
        gate = jnp.dot(hb, wg_ref[:, cols], preferred_element_type=F32)
        up = jnp.dot(hb, wu_ref[:, cols], preferred_element_type=F32)
        act = (jax.nn.silu(gate) * up).astype(BF16)
        part = jnp.dot(act, wd_ref[cols, :], preferred_element_type=F32)
        ff = part if ff is None else ff + part
    o_ref[...] = _layer_norm(DEEPNORM_ALPHA * h + ff, lg_ref[...], lb_ref[...])


def _resident_spec(*shape):
    return pl.BlockSpec(shape, lambda *_: (0,) * len(shape), pipeline_mode=pl.Buffered(1))


def _ffn_ln(h, wg, wu, wd, lg, lb, tm=512):
    return pl.pallas_call(
        _ffn_ln_kernel,
        out_shape=jax.ShapeDtypeStruct((N_TOK, D_MODEL), F32),
        grid=(N_TOK // tm,),
        in_specs=[_row_spec(tm, D_MODEL),
                  _resident_spec(D_MODEL, FFN_DIM), _resident_spec(D_MODEL, FFN_DIM),
                  _resident_spec(FFN_DIM, D_MODEL),
                  _const_spec(1, D_MODEL), _const_spec(1, D_MODEL)],
        out_specs=_row_spec(tm, D_MODEL),
        compiler_params=_cparams("parallel"),
        name="ffn_ln",
    )(h, wg, wu, wd, lg, lb)


Q_DIM = N_HEADS * HEAD_DIM


def _qkv_kernel(h_ref, w_ref, cs_ref, place_ref, bias_ref, q_ref, k_ref, v_ref):
    hb = h_ref[...].astype(BF16)
    z = jnp.dot(hb, w_ref[...], preferred_element_type=F32)
    lane = lax.broadcasted_iota(jnp.int32, (1, LANES), 1)
    table = bias_ref[...]
    for term in _split_bf16(cs_ref[...]):
        table = table + jnp.dot(term, place_ref[...], preferred_element_type=F32)
    swapped = pltpu.roll(table, HEAD_DIM, 1)
    cos_t = jnp.where(lane < HEAD_DIM, table, swapped)
    sin_t = jnp.where(lane < HEAD_DIM, swapped, table)
    first_half = (lane % HEAD_DIM) < ROT_HALF

    def rope(t):
        partner = jnp.where(first_half, pltpu.roll(t, LANES - ROT_HALF, 1), pltpu.roll(t, ROT_HALF, 1))
        return t * cos_t + partner * sin_t

    scale = HEAD_DIM ** -0.5
    for c in range(Q_DIM // LANES):
        sl = slice(c * LANES, (c + 1) * LANES)
        q_ref[:, sl] = (rope(z[:, sl]) * scale).astype(BF16)
    for c in range(KV_DIM // LANES):
        sl = slice(c * LANES, (c + 1) * LANES)
        k_ref[:, sl] = rope(z[:, Q_DIM + c * LANES:Q_DIM + (c + 1) * LANES]).astype(BF16)
    v_ref[...] = z[:, Q_DIM + KV_DIM:].astype(BF16)


def _qkv(h, w_qkv, rope_table, tm=512):
    return pl.pallas_call(
        _qkv_kernel,
        out_shape=[jax.ShapeDtypeStruct((N_TOK, Q_DIM), BF16),
                   jax.ShapeDtypeStruct((N_TOK, KV_DIM), BF16),
                   jax.ShapeDtypeStruct((N_TOK, KV_DIM), BF16)],
        grid=(N_TOK // tm,),
        in_specs=[_row_spec(tm, D_MODEL), _const_spec(D_MODEL, Q_DIM + 2 * KV_DIM),
                  _row_spec(tm, ROT_DIM), _const_spec(ROT_DIM, LANES), _const_spec(1, LANES)],
        out_specs=[_row_spec(tm, Q_DIM), _row_spec(tm, KV_DIM), _row_spec(tm, KV_DIM)],
        compiler_params=_cparams("parallel"),
        name="qkv_rope",
    )(h, w_qkv, rope_table, *_rope_placement())


N_BLOCKS = SEQ // BLOCK


ATTN_Q_TILE = 512
ATTN_SUB = ATTN_Q_TILE // BLOCK
ATTN_BAND = ATTN_Q_TILE + BLOCK
KV_PAIRS = N_KV_HEADS // 2

HEAD_ORDER = [h for c in range(KV_PAIRS) for g in range(Q_PER_KV)
              for h in (2 * c * Q_PER_KV + g, (2 * c + 1) * Q_PER_KV + g)]


def _band_bias():
    qi = jnp.arange(Q_PER_KV * BLOCK)[:, None] % BLOCK
    si = jnp.arange(2 * BLOCK)[None, :]
    rel = qi + BLOCK - si
    valid = (rel >= 0) & (rel < WINDOW)
    first = valid & (si >= BLOCK)
    neg = jnp.float32(-jnp.inf)
    return jnp.stack([jnp.where(valid, 0.0, neg), jnp.where(first, 0.0, neg)]).astype(F32)


def _attn_kernel(sink_ref, q_ref, kp_ref, kc_ref, vp_ref, vc_ref, bias_ref, h_ref, w_ref, lg_ref, lb_ref,
                 out_ref, ka_ref, kb_ref, va_ref, vb_ref, o_ref):
    i = pl.program_id(1)
    lane = lax.broadcasted_iota(jnp.int32, (1, KV_DIM), 1)
    low = (lane % LANES) < HEAD_DIM
    zero = jnp.zeros((), BF16)
    for src, dst_a, dst_b in ((kp_ref, ka_ref, kb_ref), (vp_ref, va_ref, vb_ref)):
        t = src[...]
        dst_a[0:BLOCK, :] = jnp.where(low, t, zero)
        dst_b[0:BLOCK, :] = jnp.where(low, zero, t)
    for src, dst_a, dst_b in ((kc_ref, ka_ref, kb_ref), (vc_ref, va_ref, vb_ref)):
        t = src[...]
        dst_a[BLOCK:ATTN_BAND, :] = jnp.where(low, t, zero)
        dst_b[BLOCK:ATTN_BAND, :] = jnp.where(low, zero, t)

    lane1 = lax.broadcasted_iota(jnp.int32, (1, LANES), 1)
    row = lax.broadcasted_iota(jnp.int32, (Q_PER_KV * BLOCK, 1), 0)
    contract_last = (((1,), (1,)), ((), ()))
    for blk in range(ATTN_SUB):
        r0 = blk * BLOCK
        if blk == 0:
            bias = bias_ref[jnp.where(i == 0, 1, 0)]
        else:
            bias = bias_ref[0]
        for c in range(KV_PAIRS):
            cs = slice(c * LANES, (c + 1) * LANES)
            chunks = [c * Q_PER_KV + g for g in range(Q_PER_KV)]
            q4 = jnp.concatenate([q_ref[r0:r0 + BLOCK, m * LANES:(m + 1) * LANES] for m in chunks], axis=0)
            outs = []
            for half, (k_ref, v_ref) in enumerate(((ka_ref, va_ref), (kb_ref, vb_ref))):
                kband = k_ref[r0:r0 + 2 * BLOCK, cs]
                s = lax.dot_general(q4, kband, contract_last, preferred_element_type=F32) + bias
                sink = jnp.zeros((Q_PER_KV * BLOCK, 1), F32)
                for g, m in enumerate(chunks):
                    sink = jnp.where(row // BLOCK == g, sink_ref[HEAD_ORDER[2 * m + half]], sink)
                mx = jnp.maximum(jnp.max(s, axis=-1, keepdims=True), sink)
                p = jnp.exp(s - mx)
                denom = jnp.sum(p, axis=-1, keepdims=True) + jnp.exp(sink - mx)
                pv = jnp.dot(p.astype(BF16), v_ref[r0:r0 + 2 * BLOCK, cs], preferred_element_type=F32)
                outs.append((pv, 1.0 / denom))
            (pv_a, r_a), (pv_b, r_b) = outs
            o = (pv_a + pv_b) * jnp.where(lane1 < HEAD_DIM, r_a, r_b)
            for g, m in enumerate(chunks):
                o_ref[r0:r0 + BLOCK, m * LANES:(m + 1) * LANES] = o[g * BLOCK:(g + 1) * BLOCK].astype(BF16)

    mix = jnp.dot(o_ref[...], w_ref[...], preferred_element_type=F32)
    out_ref[...] = _layer_norm(DEEPNORM_ALPHA * h_ref[...] + mix, lg_ref[...], lb_ref[...])


def _attention_ln(q, k, v, sinks, h, w_out, lg, lb):
    tiles = SEQ // ATTN_Q_TILE
    cur = lambda b, i: (b * tiles + i, 0)
    prev = lambda b, i: (jnp.maximum((b * tiles + i) * ATTN_SUB - 1, 0), 0)
    const2 = lambda b, i: (0, 0)
    return pl.pallas_call(
        _attn_kernel,
        out_shape=jax.ShapeDtypeStruct((N_TOK, D_MODEL), F32),
        grid=(BATCH, tiles),
        in_specs=[pl.BlockSpec(memory_space=pltpu.SMEM),
                  pl.BlockSpec((ATTN_Q_TILE, Q_DIM), cur),
                  pl.BlockSpec((BLOCK, KV_DIM), prev), pl.BlockSpec((ATTN_Q_TILE, KV_DIM), cur),
                  pl.BlockSpec((BLOCK, KV_DIM), prev), pl.BlockSpec((ATTN_Q_TILE, KV_DIM), cur),
                  pl.BlockSpec((2, Q_PER_KV * BLOCK, 2 * BLOCK), lambda b, i: (0, 0, 0)),
                  pl.BlockSpec((ATTN_Q_TILE, D_MODEL), cur),
                  pl.BlockSpec((Q_DIM, D_MODEL), const2),
                  pl.BlockSpec((1, D_MODEL), const2), pl.BlockSpec((1, D_MODEL), const2)],
        out_specs=pl.BlockSpec((ATTN_Q_TILE, D_MODEL), cur),
        scratch_shapes=[pltpu.VMEM((ATTN_BAND, KV_DIM), BF16)] * 4
                       + [pltpu.VMEM((ATTN_Q_TILE, Q_DIM), BF16)],
        compiler_params=_cparams("parallel", "parallel"),
        name="swa_attention_ln",
    )(sinks, q, k, k, v, v, _band_bias(), h, w_out, lg, lb)


ROUTE_TILE = 512
EXPERT_TILE = 512
SLAB = D_MODEL // LANES
EXPERT_REGION = N_TOK
REGION_TILES = EXPERT_REGION // EXPERT_TILE
N_SORTED = N_EXPERTS * EXPERT_REGION
N_EXPERT_TILES = 2 * N_TOK // EXPERT_TILE + N_EXPERTS
META_E1, META_E2, META_R1, META_R2, META_W1, META_W2 = range(6)
META_D1, META_D2 = 8, 9
COMBINE_CHUNK = 64
N_ROUTE_TILES = N_TOK // ROUTE_TILE
ROUTER_BURST = 64


def _to_slabs(ref, val, rows):
    for c in range(SLAB):
        ref[pl.ds(c, rows, stride=SLAB), :] = val[:, c * LANES:(c + 1) * LANES]


def _from_slabs(ref, rows, first_row=0):
    return jnp.concatenate([ref[pl.ds(first_row * SLAB + c, rows, stride=SLAB), :] for c in range(SLAB)],
                           axis=1)


def _row_slab(ref, row):
    return ref.at[pl.ds(pl.multiple_of(row * SLAB, SLAB), SLAB)]


def _split_bf16(x):
    def top_bits(v):
        bits = lax.bitcast_convert_type(v, jnp.uint32) & jnp.uint32(0xFFFF0000)
        return lax.bitcast_convert_type(bits, F32)
    hi = top_bits(x)
    mid = top_bits(x - hi)
    lo = (x - hi) - mid
    return hi.astype(BF16), mid.astype(BF16), lo.astype(BF16)


def _router_kernel(h_ref, wr_ref, br_ref, tri_ref, meta_ref, counts_ref, dest_ref, xs_hbm,
                   carry_ref, slab_a, slab_b, zero_ref, dest_v, dest_a, dest_b, count_v, count_s,
                   sem, pad_sem, idx_sem):
    i = pl.program_id(0)
    slabs = (slab_a, slab_b)
    dests = (dest_a, dest_b)

    @pl.when(i == 0)
    def _():
        carry_ref[...] = jnp.zeros_like(carry_ref)
        zero_ref[...] = jnp.zeros_like(zero_ref)

    def start_row(slot, t):
        for k in range(2):
            pltpu.make_async_copy(_row_slab(slabs[slot], t), _row_slab(xs_hbm, dests[slot][k, t]),
                                  sem.at[slot]).start(priority=k)

    def wait_scatter(slot):
        for _ in range(2):
            pltpu.make_async_copy(slabs[slot], xs_hbm.at[pl.ds(0, ROUTE_TILE * SLAB)], sem.at[slot]).wait()

    def route(slot, scatter_previous):
        pending = list(range(0, ROUTE_TILE, ROUTER_BURST)) if scatter_previous else []

        def burst():
            if pending:
                r0 = pending.pop(0)
                for t in range(r0, r0 + ROUTER_BURST):
                    start_row(1 - slot, t)

        lane = lax.broadcasted_iota(jnp.int32, (1, LANES), 1)
        lane_f = lane.astype(F32)
        h = h_ref[...]
        h0, h1, h2 = _split_bf16(h)
        w0, w1, w2 = wr_ref[0], wr_ref[1], wr_ref[2]
        small = None
        for a, b in ((h2, w0), (h0, w2), (h1, w1)):
            part = jnp.dot(a, b, preferred_element_type=F32)
            small = part if small is None else small + part
            burst()
        mid = None
        for a, b in ((h1, w0), (h0, w1)):
            part = jnp.dot(a, b, preferred_element_type=F32)
            mid = part if mid is None else mid + part
            burst()
        logits = (small + mid) + jnp.dot(h0, w0, preferred_element_type=F32) + br_ref[...]
        burst()
        logits = jnp.where(lane < N_EXPERTS, logits, -jnp.inf)
        m1 = jnp.max(logits, axis=-1, keepdims=True)
        i1 = jnp.min(jnp.where(logits == m1, lane_f, float(LANES)), axis=-1, keepdims=True)
        rest = jnp.where(lane_f == i1, -jnp.inf, logits)
        m2 = jnp.max(rest, axis=-1, keepdims=True)
        i2 = jnp.min(jnp.where(rest == m2, lane_f, float(LANES)), axis=-1, keepdims=True)
        e2 = jnp.exp(m2 - m1)
        tot = 1.0 + e2
        burst()
        chosen = jnp.where((lane_f == i1) | (lane_f == i2), 1.0, 0.0)
        before = jnp.dot(tri_ref[...], chosen.astype(BF16), preferred_element_type=F32) + carry_ref[...]
        r1 = jnp.sum(jnp.where(lane_f == i1, before, 0.0), axis=-1, keepdims=True)
        r2 = jnp.sum(jnp.where(lane_f == i2, before, 0.0), axis=-1, keepdims=True)
        new_carry = carry_ref[...] + jnp.sum(chosen, axis=0, keepdims=True)
        carry_ref[...] = new_carry
        counts_ref[...] = new_carry.astype(jnp.int32)
        burst()
        record = jnp.zeros((ROUTE_TILE, LANES), F32)
        for lane_id, col in ((META_E1, i1), (META_E2, i2), (META_R1, r1), (META_R2, r2),
                             (META_W1, 1.0 / tot), (META_W2, e2 / tot),
                             (META_D1, i1 * EXPERT_REGION + r1), (META_D2, i2 * EXPERT_REGION + r2)):
            record = jnp.where(lane == lane_id, col, record)
        meta_ref[...] = record
        _to_slabs(slabs[slot], h, ROUTE_TILE)
        while pending:
            burst()
        dest_rows = record.T[META_D1:META_D1 + 8, :].astype(jnp.int32)
        dest_ref[...] = dest_rows
        dest_v[...] = dest_rows
        to_smem = pltpu.make_async_copy(dest_v, dests[slot], idx_sem)
        to_smem.start()
        to_smem.wait()

    @pl.when(i == 0)
    def _():
        route(0, False)

    for slot in range(2):
        @pl.when((i > 0) & (i % 2 == slot))
        def _():
            @pl.when(i >= 2)
            def _():
                wait_scatter(slot)
            route(slot, True)

    @pl.when(i == N_ROUTE_TILES - 1)
    def _():
        slot = (N_ROUTE_TILES - 1) % 2
        lax.fori_loop(0, ROUTE_TILE, lambda t, c: (start_row(slot, t), c)[1], 0, unroll=8)
        wait_scatter(1 - slot)
        wait_scatter(slot)
        count_v[...] = jnp.broadcast_to(carry_ref[...].astype(jnp.int32), count_v.shape)
        cp = pltpu.make_async_copy(count_v, count_s, idx_sem)
        cp.start()
        cp.wait()
        for e in range(N_EXPERTS):
            n = count_s[0, e]
            end = (n + EXPERT_TILE - 1) // EXPERT_TILE * EXPERT_TILE
            pad = lambda r: pltpu.make_async_copy(zero_ref, _row_slab(xs_hbm, e * EXPERT_REGION + r), pad_sem)
            lax.fori_loop(n, end, lambda r, c: (pad(r).start(), c)[1], 0)
            lax.fori_loop(n, end, lambda r, c: (pad(r).wait(), c)[1], 0)


def _router(h, w_router, b_router):
    tri = (jnp.arange(ROUTE_TILE)[:, None] > jnp.arange(ROUTE_TILE)[None, :]).astype(BF16)
    w_split = jnp.stack(_split_bf16(w_router))
    return pl.pallas_call(
        _router_kernel,
        out_shape=[jax.ShapeDtypeStruct((N_TOK, LANES), F32),
                   jax.ShapeDtypeStruct((1, LANES), jnp.int32),
                   jax.ShapeDtypeStruct((8, N_TOK), jnp.int32),
                   jax.ShapeDtypeStruct((N_SORTED * SLAB, LANES), F32)],
        grid=(N_TOK // ROUTE_TILE,),
        in_specs=[_row_spec(ROUTE_TILE, D_MODEL), _const_spec(3, D_MODEL, LANES),
                  _const_spec(1, LANES), _const_spec(ROUTE_TILE, ROUTE_TILE)],
        out_specs=[_row_spec(ROUTE_TILE, LANES), _const_spec(1, LANES),
                   pl.BlockSpec((8, ROUTE_TILE), lambda i: (0, i)), pl.BlockSpec(memory_space=pl.ANY)],
        scratch_shapes=[pltpu.VMEM((1, LANES), F32),
                        pltpu.VMEM((ROUTE_TILE * SLAB, LANES), F32), pltpu.VMEM((ROUTE_TILE * SLAB, LANES), F32),
                        pltpu.VMEM((SLAB, LANES), F32),
                        pltpu.VMEM((8, ROUTE_TILE), jnp.int32),
                        pltpu.SMEM((8, ROUTE_TILE), jnp.int32), pltpu.SMEM((8, ROUTE_TILE), jnp.int32),
                        pltpu.VMEM((8, LANES), jnp.int32), pltpu.SMEM((8, LANES), jnp.int32),
                        pltpu.SemaphoreType.DMA((2,)), pltpu.SemaphoreType.DMA, pltpu.SemaphoreType.DMA],
        compiler_params=_cparams("arbitrary"),
        name="moe_router",
    )(h, w_split, b_router, tri)


def _experts_kernel(tblk_ref, texp_ref, xs_ref, wg_ref, wu_ref, wd_ref, ys_ref, wb_ref):
    i = pl.program_id(0)

    @pl.when((i == 0) | (texp_ref[i] != texp_ref[jnp.maximum(i - 1, 0)]))
    def _():
        for j, w_ref in enumerate((wg_ref, wu_ref, wd_ref)):
            wb_ref[j] = w_ref[0].astype(BF16)

    x = _from_slabs(xs_ref, EXPERT_TILE).astype(BF16)
    gate = jnp.dot(x, wb_ref[0], preferred_element_type=F32)
    up = jnp.dot(x, wb_ref[1], preferred_element_type=F32)
    act = (jax.nn.silu(gate) * up).astype(BF16)
    _to_slabs(ys_ref, jnp.dot(act, wb_ref[2], preferred_element_type=F32), EXPERT_TILE)


def _experts(tile_block, tile_expert, xs, wg, wu, wd):
    wspec = lambda a, b: pl.BlockSpec((1, a, b), lambda i, tblk, texp: (texp[i], 0, 0))
    rows = pl.BlockSpec((EXPERT_TILE * SLAB, LANES), lambda i, tblk, texp: (tblk[i], 0))
    return pl.pallas_call(
        _experts_kernel,
        out_shape=jax.ShapeDtypeStruct((N_SORTED * SLAB, LANES), F32),
        grid_spec=pltpu.PrefetchScalarGridSpec(
            num_scalar_prefetch=2,
            grid=(N_EXPERT_TILES,),
            in_specs=[rows, wspec(D_MODEL, EXPERT_DIM), wspec(D_MODEL, EXPERT_DIM),
                      wspec(EXPERT_DIM, D_MODEL)],
            out_specs=rows,
            scratch_shapes=[pltpu.VMEM((3, D_MODEL, EXPERT_DIM), BF16)]),
        compiler_params=_cparams("arbitrary"),
        name="moe_experts",
    )(tile_block, tile_expert, xs, wg, wu, wd)


def _combine_ln_kernel(dest_ref, meta_ref, h_ref, ys_hbm, lg_ref, lb_ref, o_ref, ya_ref, yb_ref, sem):
    i = pl.program_id(0)
    bufs = (ya_ref, yb_ref)

    def start_gather(tile, slot):
        def body(t, carry):
            for k in range(2):
                row = dest_ref[k * N_TOK + tile * ROUTE_TILE + t]
                pltpu.make_async_copy(_row_slab(ys_hbm, row), _row_slab(bufs[slot], k * ROUTE_TILE + t),
                                      sem.at[slot]).start(priority=k)
            return carry
        lax.fori_loop(0, ROUTE_TILE, body, 0, unroll=8)

    def wait_gather(slot):
        pltpu.make_async_copy(ys_hbm.at[pl.ds(0, 2 * ROUTE_TILE * SLAB)], bufs[slot], sem.at[slot]).wait()

    last = pl.num_programs(0) - 1
    lane = lax.broadcasted_iota(jnp.int32, (1, LANES), 1)

    def finish_and_prefetch(slot):
        nxt = jnp.minimum(i + 1, last) * ROUTE_TILE
        wait_gather(slot)
        for r0 in range(0, ROUTE_TILE, COMBINE_CHUNK):
            rows = slice(r0, r0 + COMBINE_CHUNK)
            meta = meta_ref[rows, :]
            w1 = jnp.sum(jnp.where(lane == META_W1, meta, 0.0), axis=-1, keepdims=True)
            w2 = jnp.sum(jnp.where(lane == META_W2, meta, 0.0), axis=-1, keepdims=True)
            ff = (w1 * _from_slabs(bufs[slot], COMBINE_CHUNK, first_row=r0)
                  + w2 * _from_slabs(bufs[slot], COMBINE_CHUNK, first_row=ROUTE_TILE + r0))
            o_ref[rows, :] = _layer_norm(DEEPNORM_ALPHA * h_ref[rows, :] + ff, lg_ref[...], lb_ref[...])
            for t in range(r0, r0 + COMBINE_CHUNK):
                for k in range(2):
                    row = dest_ref[k * N_TOK + nxt + t]
                    pltpu.make_async_copy(_row_slab(ys_hbm, row),
                                          _row_slab(bufs[1 - slot], k * ROUTE_TILE + t),
                                          sem.at[1 - slot]).start(priority=k)

        @pl.when(i == last)
        def _():
            wait_gather(1 - slot)

    @pl.when(i == 0)
    def _():
        start_gather(0, 0)

    for slot in range(2):
        @pl.when(i % 2 == slot)
        def _():
            finish_and_prefetch(slot)


def _combine_ln(dest, meta, h, ys, lg, lb):
    row = lambda width: pl.BlockSpec((ROUTE_TILE, width), lambda i, dest: (i, 0))
    const = pl.BlockSpec((1, D_MODEL), lambda i, dest: (0, 0))
    return pl.pallas_call(
        _combine_ln_kernel,
        out_shape=jax.ShapeDtypeStruct((N_TOK, D_MODEL), F32),
        grid_spec=pltpu.PrefetchScalarGridSpec(
            num_scalar_prefetch=1,
            grid=(N_TOK // ROUTE_TILE,),
            in_specs=[row(LANES), row(D_MODEL), pl.BlockSpec(memory_space=pl.ANY), const, const],
            out_specs=row(D_MODEL),
            scratch_shapes=[pltpu.VMEM((2 * ROUTE_TILE * SLAB, LANES), F32)] * 2
                           + [pltpu.SemaphoreType.DMA((2,))]),
        compiler_params=_cparams("arbitrary"),
        name="moe_combine_ln",
    )(dest, meta, h, ys, lg, lb)


def _moe_ln(h, w_router, b_router, wg, wu, wd, lg, lb):
    meta, counts, dest_rows, xs = _router(h, w_router, b_router)
    expert = jnp.arange(N_EXPERTS)
    tiles = (counts[0, :N_EXPERTS] + EXPERT_TILE - 1) // EXPERT_TILE
    ends = jnp.sum(jnp.where(expert[None, :] <= expert[:, None], tiles[None, :], 0), axis=1)
    step = jnp.minimum(jnp.arange(N_EXPERT_TILES), ends[-1] - 1)
    tile_expert = jnp.sum(step[:, None] >= ends[None, :], axis=1)
    first = jnp.sum(jnp.where(tile_expert[:, None] == expert[None, :], (ends - tiles)[None, :], 0), axis=1)
    tile_block = tile_expert * REGION_TILES + (step - first)
    ys = _experts(tile_block.astype(jnp.int32), tile_expert.astype(jnp.int32), xs, wg, wu, wd)
    dest = dest_rows[:2].reshape(2 * N_TOK)
    return _combine_ln(dest, meta, h, ys, lg, lb)


def kernel(x, positions, ln_g, ln_b, ssm_lambda_re, ssm_lambda_im, ssm_log_step, ssm_b_re, ssm_b_im, ssm_c_re, ssm_c_im, ssm_d, ssm_w_glu, kv_w, attn_w_q, attn_sinks, attn_w_out, ffn_w_gate, ffn_w_up, ffn_w_down, moe_w_router, moe_b_router, moe_w_gate, moe_w_up, moe_w_down):
    ln = lambda layer, j: (ln_g[layer, j].reshape(1, D_MODEL).astype(F32),
                           ln_b[layer, j].reshape(1, D_MODEL).astype(F32))
    rope_table = _rope_tables(positions)
    x2 = x.reshape(N_TOK, D_MODEL)

    params = _ssm_params(ssm_lambda_re[0], ssm_lambda_im[0], ssm_log_step[0], ssm_b_re[0], ssm_b_im[0],
                         ssm_c_re[0], ssm_c_im[0])
    y = _ssm_mixer(x2, params)
    h = _glu_ln(y, x2, ssm_d[0].astype(F32).reshape(1, D_MODEL), ssm_w_glu[0].astype(BF16), *ln(0, 0))
    h = _ffn_ln(h, ffn_w_gate[0].astype(BF16), ffn_w_up[0].astype(BF16), ffn_w_down[0].astype(BF16),
                *ln(0, 1))

    order = jnp.array(HEAD_ORDER)
    w_q = attn_w_q[0].reshape(D_MODEL, N_HEADS, HEAD_DIM)[:, order].reshape(D_MODEL, Q_DIM)
    w_out = attn_w_out[0].reshape(N_HEADS, HEAD_DIM, D_MODEL)[order].reshape(Q_DIM, D_MODEL)
    w_qkv = jnp.concatenate([w_q, kv_w], axis=1).astype(BF16)
    q, k, v = _qkv(h, w_qkv, rope_table)
    h = _attention_ln(q, k, v, attn_sinks[0].astype(F32), h, w_out.astype(BF16), *ln(1, 0))
    w_router = jnp.pad(moe_w_router[0].astype(F32), ((0, 0), (0, LANES - N_EXPERTS)))
    b_router = jnp.pad(moe_b_router[0].astype(F32), (0, LANES - N_EXPERTS)).reshape(1, LANES)
    h = _moe_ln(h, w_router, b_router, moe_w_gate[0].astype(F32), moe_w_up[0].astype(F32),
                moe_w_down[0].astype(F32), *ln(1, 1))
    return h.reshape(BATCH, SEQ, D_MODEL)
```

```python
import functools
import math

import jax
import jax.numpy as jnp
import numpy as np
from jax import lax
from jax.experimental import pallas as pl
from jax.experimental.pallas import tpu as pltpu

F32 = jnp.float32
BF16 = jnp.bfloat16

D_MODEL = 1024
BATCH = 16
SEQ = 2048
N_TOK = BATCH * SEQ
DEPTH = 2

SSM_GROUP_CH = 16
SSM_GROUPS = D_MODEL // SSM_GROUP_CH
SSM_STATE = 64
SSM_CHUNK = 16
SSM_NCHUNK = SEQ // SSM_CHUNK
SSM_ROWS = SSM_NCHUNK * BATCH
SSM_WIDTH = SSM_CHUNK * SSM_GROUP_CH
SSM_SCAN_STEPS = int(math.log2(SSM_NCHUNK))

N_HEADS = 16
HEAD_DIM = 64
N_KV_HEADS = 4
Q_PER_KV = N_HEADS // N_KV_HEADS
KV_DIM = N_KV_HEADS * HEAD_DIM
WINDOW = 128
BLOCK = 128
ROT_DIM = HEAD_DIM // 4
ROT_HALF = ROT_DIM // 2
ROPE_THETA = 500000.0

FFN_DIM = 2816
N_EXPERTS = 8
EXPERT_DIM = 1024

DEEPNORM_ALPHA = (2 * DEPTH) ** 0.25
LN_EPS = 1e-5

LANES = 128
VMEM_LIMIT = 56 * 1024 * 1024


def _cparams(*sem):
    return pltpu.CompilerParams(dimension_semantics=sem, vmem_limit_bytes=VMEM_LIMIT)


def _layer_norm(r, g, b):
    mu = jnp.mean(r, axis=-1, keepdims=True)
    xc = r - mu
    var = jnp.mean(xc * xc, axis=-1, keepdims=True)
    return xc * lax.rsqrt(var + LN_EPS) * g + b


def _rope_kernel(invf_ref, pos_ref, cs_ref):
    pos = pos_ref[...]
    for f in range(ROT_HALF):
        ang = pos * invf_ref[f]
        cs_ref[f] = jnp.cos(ang)
        cs_ref[ROT_HALF + f] = jnp.sin(ang)


def _rope_tables(positions):
    inv_freq = ROPE_THETA ** (-jnp.arange(0, ROT_DIM, 2, dtype=F32) / ROT_DIM)
    pos = positions.astype(F32)
    cs = pl.pallas_call(
        _rope_kernel,
        out_shape=jax.ShapeDtypeStruct((ROT_DIM, BATCH, SEQ), F32),
        in_specs=[pl.BlockSpec(memory_space=pltpu.SMEM),
                  pl.BlockSpec(memory_space=pltpu.VMEM)],
        out_specs=pl.BlockSpec(memory_space=pltpu.VMEM),
        name="rope_tables",
    )(inv_freq, pos)
    return cs.reshape(ROT_DIM, N_TOK).T


def _rope_placement():
    f = np.arange(ROT_HALF)
    place = np.zeros((ROT_DIM, LANES), np.float32)
    place[f, f] = place[f, ROT_HALF + f] = 1.0
    place[ROT_HALF + f, HEAD_DIM + f] = -1.0
    place[ROT_HALF + f, HEAD_DIM + ROT_HALF + f] = 1.0
    lane = np.arange(LANES)
    bias = ((lane >= ROT_DIM) & (lane < HEAD_DIM)).astype(np.float32).reshape(1, LANES)
    return jnp.asarray(place, BF16), jnp.asarray(bias)


def _ssm_params(lam_re, lam_im, log_step, b_re, b_im, c_re, c_im):
    hp = lax.Precision.HIGHEST
    lr, li = lam_re.astype(F32), lam_im.astype(F32)
    dt = jnp.exp(log_step.astype(F32))[:, None]
    mag = jnp.exp(lr * dt)
    ar = mag * jnp.cos(li * dt)
    ai = mag * jnp.sin(li * dt)
    nr = ar - 1.0
    den = lr * lr + li * li
    kr = (nr * lr + ai * li) / den
    ki = (ai * lr - nr * li) / den
    br, bi = b_re.astype(F32), b_im.astype(F32)
    bbar_r = kr[..., None] * br - ki[..., None] * bi
    bbar_i = kr[..., None] * bi + ki[..., None] * br
    cr, ci = c_re.astype(F32), c_im.astype(F32)

    def powers(taus):
        t = taus.astype(F32)[:, None, None]
        m = jnp.exp(lr[None] * dt[None] * t)
        ang = li[None] * dt[None] * t
        return m * jnp.cos(ang), m * jnp.sin(ang)

    er, ei = powers(jnp.arange(SSM_CHUNK + 1))
    w_r = er[:, :, :, None] * bbar_r[None] - ei[:, :, :, None] * bbar_i[None]
    w_i = er[:, :, :, None] * bbar_i[None] + ei[:, :, :, None] * bbar_r[None]
    kern = (jnp.einsum('gcp,tgpd->gtcd', cr, w_r[:SSM_CHUNK], precision=hp)
            - jnp.einsum('gcp,tgpd->gtcd', ci, w_i[:SSM_CHUNK], precision=hp))
    s_idx = jnp.arange(SSM_CHUNK)[:, None, None]
    t_idx = jnp.arange(SSM_CHUNK)[None, :, None]
    lag = (t_idx - s_idx == jnp.arange(SSM_CHUNK)[None, None, :]).astype(F32)
    toep = jnp.einsum('stu,gucd->gtcsd', lag, kern, precision=hp)
    mt = toep.reshape(SSM_GROUPS, SSM_WIDTH, SSM_WIDTH)
    rev = SSM_CHUNK - 1 - jnp.arange(SSM_CHUNK)
    g_r = w_r[rev].transpose(1, 2, 0, 3).reshape(SSM_GROUPS, SSM_STATE, SSM_WIDTH)
    g_i = w_i[rev].transpose(1, 2, 0, 3).reshape(SSM_GROUPS, SSM_STATE, SSM_WIDTH)
    gt = jnp.concatenate([g_r, g_i], axis=1)
    e1r, e1i = er[1:], ei[1:]
    ce_r = cr[None] * e1r[:, :, None, :] - ci[None] * e1i[:, :, None, :]
    ce_i = cr[None] * e1i[:, :, None, :] + ci[None] * e1r[:, :, None, :]
    c_re = ce_r.transpose(1, 0, 2, 3).reshape(SSM_GROUPS, SSM_WIDTH, SSM_STATE)
    c_im = (-ce_i).transpose(1, 0, 2, 3).reshape(SSM_GROUPS, SSM_WIDTH, SSM_STATE)
    ct = jnp.concatenate([c_re, c_im], axis=-1)
    sr, si = powers(SSM_CHUNK * (2 ** jnp.arange(8)))
    coef = lambda t: jnp.concatenate([t, t], axis=-1).transpose(1, 0, 2)
    return mt.astype(BF16), gt.astype(BF16), ct.astype(BF16), coef(sr), coef(si)


SSM_LANE_GROUPS = LANES // SSM_GROUP_CH
SSM_LANE_CHUNKS = D_MODEL // LANES
SSM_BATCH_TILE = 4
SSM_COLS = SSM_BATCH_TILE * SSM_NCHUNK


def _ssm_kernel(x_ref, mt_ref, gt_ref, ct_ref, cr_ref, ci_ref, y_ref, v_ref, yt_ref):
    for bl in range(SSM_BATCH_TILE):
        cols = slice(bl * SSM_NCHUNK, (bl + 1) * SSM_NCHUNK)
        for s in range(SSM_CHUNK):
            a = x_ref[pl.ds(bl * SEQ + s, SSM_NCHUNK, stride=SSM_CHUNK), :]
            at = a.T.astype(BF16)
            for g in range(SSM_LANE_GROUPS):
                v_ref[g, s * SSM_GROUP_CH:(s + 1) * SSM_GROUP_CH, cols] = (
                    at[g * SSM_GROUP_CH:(g + 1) * SSM_GROUP_CH])

    row_j = lax.broadcasted_iota(jnp.int32, (SSM_NCHUNK, 1), 0)

    def group_body(g, carry):
        ut = v_ref[g]
        xt = jnp.dot(gt_ref[g], ut, preferred_element_type=F32)
        cr_all = cr_ref[g]
        ci_all = ci_ref[g]

        def seg(rows, b):
            return xt[rows, b * SSM_NCHUNK:(b + 1) * SSM_NCHUNK]

        def shifted(x, d):
            if d < 8:
                return jnp.where(row_j >= d, pltpu.roll(x, d, 0), 0.0)
            return jnp.concatenate([jnp.zeros((d, LANES), F32), x[:SSM_NCHUNK - d]], axis=0)

        re_rows, im_rows = slice(0, SSM_STATE), slice(SSM_STATE, 2 * SSM_STATE)
        sr, si = [], []
        for q in range(SSM_BATCH_TILE // 2):
            xr = jnp.concatenate([seg(re_rows, 2 * q), seg(re_rows, 2 * q + 1)], axis=0).T
            xi = jnp.concatenate([seg(im_rows, 2 * q), seg(im_rows, 2 * q + 1)], axis=0).T
            for k in range(SSM_SCAN_STEPS):
                cr = cr_all[k:k + 1, :]
                ci = ci_all[k:k + 1, :]
                rr = shifted(xr, 1 << k)
                ri = shifted(xi, 1 << k)
                xr, xi = xr + cr * rr - ci * ri, xi + cr * ri + ci * rr
            pr = shifted(xr, 1).T
            pi = shifted(xi, 1).T
            sr += [pr[:SSM_STATE], pr[SSM_STATE:]]
            si += [pi[:SSM_STATE], pi[SSM_STATE:]]
        sprev = jnp.concatenate([jnp.concatenate(sr, axis=1), jnp.concatenate(si, axis=1)],
                                axis=0).astype(BF16)
        yt_ref[g] = (jnp.dot(mt_ref[g], ut, preferred_element_type=F32)
                     + jnp.dot(ct_ref[g], sprev, preferred_element_type=F32))
        return carry

    lax.fori_loop(0, SSM_LANE_GROUPS, group_body, 0, unroll=2)

    for bl in range(SSM_BATCH_TILE):
        cols = slice(bl * SSM_NCHUNK, (bl + 1) * SSM_NCHUNK)
        for t in range(SSM_CHUNK):
            tile = jnp.concatenate(
                [yt_ref[g, t * SSM_GROUP_CH:(t + 1) * SSM_GROUP_CH, cols] for g in range(SSM_LANE_GROUPS)],
                axis=0)
            y_ref[pl.ds(bl * SEQ + t, SSM_NCHUNK, stride=SSM_CHUNK), :] = tile.T


def _ssm_mixer(x2, params):
    mt, gt, ct, coef_r, coef_i = params
    rows = SSM_BATCH_TILE * SEQ
    pspec = lambda a, b: pl.BlockSpec((SSM_LANE_GROUPS, a, b), lambda k, q: (k, 0, 0))
    return pl.pallas_call(
        _ssm_kernel,
        out_shape=jax.ShapeDtypeStruct((N_TOK, D_MODEL), F32),
        grid=(SSM_LANE_CHUNKS, BATCH // SSM_BATCH_TILE),
        in_specs=[pl.BlockSpec((rows, LANES), lambda k, q: (q, k)),
                  pspec(SSM_WIDTH, SSM_WIDTH), pspec(2 * SSM_STATE, SSM_WIDTH), pspec(SSM_WIDTH, 2 * SSM_STATE),
                  pspec(8, 2 * SSM_STATE), pspec(8, 2 * SSM_STATE)],
        out_specs=pl.BlockSpec((rows, LANES), lambda k, q: (q, k)),
        scratch_shapes=[pltpu.VMEM((SSM_LANE_GROUPS, SSM_WIDTH, SSM_COLS), BF16),
                        pltpu.VMEM((SSM_LANE_GROUPS, SSM_WIDTH, SSM_COLS), F32)],
        compiler_params=_cparams("parallel", "parallel"),
        name="ssm_mixer",
    )(x2, mt, gt, ct, coef_r, coef_i)


GLU_COLS = 256


def _glu_ln_kernel(y_ref, x_ref, d_ref, w_ref, lg_ref, lb_ref, o_ref):
    x = x_ref[...]
    act = jax.nn.gelu(y_ref[...] + d_ref[...] * x).astype(BF16)
    parts = []
    for c in range(0, D_MODEL, GLU_COLS):
        val = jnp.dot(act, w_ref[:, c:c + GLU_COLS], preferred_element_type=F32)
        gate = jnp.dot(act, w_ref[:, D_MODEL + c:D_MODEL + c + GLU_COLS], preferred_element_type=F32)
        parts.append(val * jax.nn.sigmoid(gate))
    mix = jnp.concatenate(parts, axis=1)
    o_ref[...] = _layer_norm(DEEPNORM_ALPHA * x + mix, lg_ref[...], lb_ref[...])


def _row_spec(tm, width):
    return pl.BlockSpec((tm, width), lambda i: (i, 0))


def _const_spec(*shape):
    return pl.BlockSpec(shape, lambda *_: (0,) * len(shape))


def _glu_ln(y, x2, d_skip, w_glu, lg, lb, tm=512):
    return pl.pallas_call(
        _glu_ln_kernel,
        out_shape=jax.ShapeDtypeStruct((N_TOK, D_MODEL), F32),
        grid=(N_TOK // tm,),
        in_specs=[_row_spec(tm, D_MODEL), _row_spec(tm, D_MODEL), _const_spec(1, D_MODEL),
                  _const_spec(D_MODEL, 2 * D_MODEL), _const_spec(1, D_MODEL), _const_spec(1, D_MODEL)],
        out_specs=_row_spec(tm, D_MODEL),
        compiler_params=_cparams("parallel"),
        name="glu_ln",
    )(y, x2, d_skip, w_glu, lg, lb)


FFN_SPLIT = 2
FFN_TILE = FFN_DIM // FFN_SPLIT


def _ffn_ln_kernel(h_ref, wg_ref, wu_ref, wd_ref, lg_ref, lb_ref, o_ref):
    h = h_ref[...]
    hb = h.astype(BF16)
    ff = None
    for f in range(FFN_SPLIT):
        cols = slice(f * FFN_TILE, (f + 1) * FFN_TILE)
        gate = jnp.dot(hb, wg_ref[:, cols], preferred_element_type=F32)
        up = jnp.dot(hb, wu_ref[:, cols], preferred_element_type=F32)
        act = (jax.nn.silu(gate) * up).astype(BF16)
        part = jnp.dot(act, wd_ref[cols, :], preferred_element_type=F32)
        ff = part if ff is None else ff + part
    o_ref[...] = _layer_norm(DEEPNORM_ALPHA * h + ff, lg_ref[...], lb_ref[...])


def _resident_spec(*shape):
    return pl.BlockSpec(shape, lambda *_: (0,) * len(shape), pipeline_mode=pl.Buffered(1))


def _ffn_ln(h, wg, wu, wd, lg, lb, tm=512):
    return pl.pallas_call(
        _ffn_ln_kernel,
        out_shape=jax.ShapeDtypeStruct((N_TOK, D_MODEL), F32),
        grid=(N_TOK // tm,),
        in_specs=[_row_spec(tm, D_MODEL),
                  _resident_spec(D_MODEL, FFN_DIM), _resident_spec(D_MODEL, FFN_DIM),
                  _resident_spec(FFN_DIM, D_MODEL),
                  _const_spec(1, D_MODEL), _const_spec(1, D_MODEL)],
        out_specs=_row_spec(tm, D_MODEL),
        compiler_params=_cparams("parallel"),
        name="ffn_ln",
    )(h, wg, wu, wd, lg, lb)


Q_DIM = N_HEADS * HEAD_DIM


def _qkv_kernel(h_ref, w_ref, cs_ref, place_ref, bias_ref, q_ref, k_ref, v_ref):
    hb = h_ref[...].astype(BF16)
    z = jnp.dot(hb, w_ref[...], preferred_element_type=F32)
    lane = lax.broadcasted_iota(jnp.int32, (1, LANES), 1)
    table = bias_ref[...]
    for term in _split_bf16(cs_ref[...]):
        table = table + jnp.dot(term, place_ref[...], preferred_element_type=F32)
    swapped = pltpu.roll(table, HEAD_DIM, 1)
    cos_t = jnp.where(lane < HEAD_DIM, table, swapped)
    sin_t = jnp.where(lane < HEAD_DIM, swapped, table)
    first_half = (lane % HEAD_DIM) < ROT_HALF

    def rope(t):
        partner = jnp.where(first_half, pltpu.roll(t, LANES - ROT_HALF, 1), pltpu.roll(t, ROT_HALF, 1))
        return t * cos_t + partner * sin_t

    scale = HEAD_DIM ** -0.5
    for c in range(Q_DIM // LANES):
        sl = slice(c * LANES, (c + 1) * LANES)
        q_ref[:, sl] = (rope(z[:, sl]) * scale).astype(BF16)
    for c in range(KV_DIM // LANES):
        sl = slice(c * LANES, (c + 1) * LANES)
        k_ref[:, sl] = rope(z[:, Q_DIM + c * LANES:Q_DIM + (c + 1) * LANES]).astype(BF16)
    v_ref[...] = z[:, Q_DIM + KV_DIM:].astype(BF16)


def _qkv(h, w_qkv, rope_table, tm=512):
    return pl.pallas_call(
        _qkv_kernel,
        out_shape=[jax.ShapeDtypeStruct((N_TOK, Q_DIM), BF16),
                   jax.ShapeDtypeStruct((N_TOK, KV_DIM), BF16),
                   jax.ShapeDtypeStruct((N_TOK, KV_DIM), BF16)],
        grid=(N_TOK // tm,),
        in_specs=[_row_spec(tm, D_MODEL), _const_spec(D_MODEL, Q_DIM + 2 * KV_DIM),
                  _row_spec(tm, ROT_DIM), _const_spec(ROT_DIM, LANES), _const_spec(1, LANES)],
        out_specs=[_row_spec(tm, Q_DIM), _row_spec(tm, KV_DIM), _row_spec(tm, KV_DIM)],
        compiler_params=_cparams("parallel"),
        name="qkv_rope",
    )(h, w_qkv, rope_table, *_rope_placement())


N_BLOCKS = SEQ // BLOCK


ATTN_Q_TILE = 512
ATTN_SUB = ATTN_Q_TILE // BLOCK
ATTN_BAND = ATTN_Q_TILE + BLOCK
KV_PAIRS = N_KV_HEADS // 2

HEAD_ORDER = [h for c in range(KV_PAIRS) for g in range(Q_PER_KV)
              for h in (2 * c * Q_PER_KV + g, (2 * c + 1) * Q_PER_KV + g)]


def _band_bias():
    qi = jnp.arange(Q_PER_KV * BLOCK)[:, None] % BLOCK
    si = jnp.arange(2 * BLOCK)[None, :]
    rel = qi + BLOCK - si
    valid = (rel >= 0) & (rel < WINDOW)
    first = valid & (si >= BLOCK)
    neg = jnp.float32(-jnp.inf)
    return jnp.stack([jnp.where(valid, 0.0, neg), jnp.where(first, 0.0, neg)]).astype(F32)


def _attn_kernel(sink_ref, q_ref, kp_ref, kc_ref, vp_ref, vc_ref, bias_ref, h_ref, w_ref, lg_ref, lb_ref,
                 out_ref, ka_ref, kb_ref, va_ref, vb_ref, o_ref):
    i = pl.program_id(1)
    lane = lax.broadcasted_iota(jnp.int32, (1, KV_DIM), 1)
    low = (lane % LANES) < HEAD_DIM
    zero = jnp.zeros((), BF16)
    for src, dst_a, dst_b in ((kp_ref, ka_ref, kb_ref), (vp_ref, va_ref, vb_ref)):
        t = src[...]
        dst_a[0:BLOCK, :] = jnp.where(low, t, zero)
        dst_b[0:BLOCK, :] = jnp.where(low, zero, t)
    for src, dst_a, dst_b in ((kc_ref, ka_ref, kb_ref), (vc_ref, va_ref, vb_ref)):
        t = src[...]
        dst_a[BLOCK:ATTN_BAND, :] = jnp.where(low, t, zero)
        dst_b[BLOCK:ATTN_BAND, :] = jnp.where(low, zero, t)

    lane1 = lax.broadcasted_iota(jnp.int32, (1, LANES), 1)
    row = lax.broadcasted_iota(jnp.int32, (Q_PER_KV * BLOCK, 1), 0)
    contract_last = (((1,), (1,)), ((), ()))
    for blk in range(ATTN_SUB):
        r0 = blk * BLOCK
        if blk == 0:
            bias = bias_ref[jnp.where(i == 0, 1, 0)]
        else:
            bias = bias_ref[0]
        for c in range(KV_PAIRS):
            cs = slice(c * LANES, (c + 1) * LANES)
            chunks = [c * Q_PER_KV + g for g in range(Q_PER_KV)]
            q4 = jnp.concatenate([q_ref[r0:r0 + BLOCK, m * LANES:(m + 1) * LANES] for m in chunks], axis=0)
            outs = []
            for half, (k_ref, v_ref) in enumerate(((ka_ref, va_ref), (kb_ref, vb_ref))):
                kband = k_ref[r0:r0 + 2 * BLOCK, cs]
                s = lax.dot_general(q4, kband, contract_last, preferred_element_type=F32) + bias
                sink = jnp.zeros((Q_PER_KV * BLOCK, 1), F32)
                for g, m in enumerate(chunks):
                    sink = jnp.where(row // BLOCK == g, sink_ref[HEAD_ORDER[2 * m + half]], sink)
                mx = jnp.maximum(jnp.max(s, axis=-1, keepdims=True), sink)
                p = jnp.exp(s - mx)
                denom = jnp.sum(p, axis=-1, keepdims=True) + jnp.exp(sink - mx)
                pv = jnp.dot(p.astype(BF16), v_ref[r0:r0 + 2 * BLOCK, cs], preferred_element_type=F32)
                outs.append((pv, 1.0 / denom))
            (pv_a, r_a), (pv_b, r_b) = outs
            o = (pv_a + pv_b) * jnp.where(lane1 < HEAD_DIM, r_a, r_b)
            for g, m in enumerate(chunks):
                o_ref[r0:r0 + BLOCK, m * LANES:(m + 1) * LANES] = o[g * BLOCK:(g + 1) * BLOCK].astype(BF16)

    mix = jnp.dot(o_ref[...], w_ref[...], preferred_element_type=F32)
    out_ref[...] = _layer_norm(DEEPNORM_ALPHA * h_ref[...] + mix, lg_ref[...], lb_ref[...])


def _attention_ln(q, k, v, sinks, h, w_out, lg, lb):
    tiles = SEQ // ATTN_Q_TILE
    cur = lambda b, i: (b * tiles + i, 0)
    prev = lambda b, i: (jnp.maximum((b * tiles + i) * ATTN_SUB - 1, 0), 0)
    const2 = lambda b, i: (0, 0)
    return pl.pallas_call(
        _attn_kernel,
        out_shape=jax.ShapeDtypeStruct((N_TOK, D_MODEL), F32),
        grid=(BATCH, tiles),
        in_specs=[pl.BlockSpec(memory_space=pltpu.SMEM),
                  pl.BlockSpec((ATTN_Q_TILE, Q_DIM), cur),
                  pl.BlockSpec((BLOCK, KV_DIM), prev), pl.BlockSpec((ATTN_Q_TILE, KV_DIM), cur),
                  pl.BlockSpec((BLOCK, KV_DIM), prev), pl.BlockSpec((ATTN_Q_TILE, KV_DIM), cur),
                  pl.BlockSpec((2, Q_PER_KV * BLOCK, 2 * BLOCK), lambda b, i: (0, 0, 0)),
                  pl.BlockSpec((ATTN_Q_TILE, D_MODEL), cur),
                  pl.BlockSpec((Q_DIM, D_MODEL), const2),
                  pl.BlockSpec((1, D_MODEL), const2), pl.BlockSpec((1, D_MODEL), const2)],
        out_specs=pl.BlockSpec((ATTN_Q_TILE, D_MODEL), cur),
        scratch_shapes=[pltpu.VMEM((ATTN_BAND, KV_DIM), BF16)] * 4
                       + [pltpu.VMEM((ATTN_Q_TILE, Q_DIM), BF16)],
        compiler_params=_cparams("parallel", "parallel"),
        name="swa_attention_ln",
    )(sinks, q, k, k, v, v, _band_bias(), h, w_out, lg, lb)


ROUTE_TILE = 512
EXPERT_TILE = 512
SLAB = D_MODEL // LANES
EXPERT_REGION = N_TOK
REGION_TILES = EXPERT_REGION // EXPERT_TILE
N_SORTED = N_EXPERTS * EXPERT_REGION
N_EXPERT_TILES = 2 * N_TOK // EXPERT_TILE + N_EXPERTS
META_E1, META_E2, META_R1, META_R2, META_W1, META_W2 = range(6)
META_D1, META_D2 = 8, 9
COMBINE_CHUNK = 64
N_ROUTE_TILES = N_TOK // ROUTE_TILE
ROUTER_BURST = 64


def _to_slabs(ref, val, rows):
    for c in range(SLAB):
        ref[pl.ds(c, rows, stride=SLAB), :] = val[:, c * LANES:(c + 1) * LANES]


def _from_slabs(ref, rows, first_row=0):
    return jnp.concatenate([ref[pl.ds(first_row * SLAB + c, rows, stride=SLAB), :] for c in range(SLAB)],
                           axis=1)


def _row_slab(ref, row):
    return ref.at[pl.ds(pl.multiple_of(row * SLAB, SLAB), SLAB)]


def _split_bf16(x):
    def top_bits(v):
        bits = lax.bitcast_convert_type(v, jnp.uint32) & jnp.uint32(0xFFFF0000)
        return lax.bitcast_convert_type(bits, F32)
    hi = top_bits(x)
    mid = top_bits(x - hi)
    lo = (x - hi) - mid
    return hi.astype(BF16), mid.astype(BF16), lo.astype(BF16)


def _router_kernel(h_ref, wr_ref, br_ref, tri_ref, meta_ref, counts_ref, dest_ref, xs_hbm,
                   carry_ref, slab_a, slab_b, zero_ref, dest_v, dest_a, dest_b, count_v, count_s,
                   sem, pad_sem, idx_sem):
    i = pl.program_id(0)
    slabs = (slab_a, slab_b)
    dests = (dest_a, dest_b)

    @pl.when(i == 0)
    def _():
        carry_ref[...] = jnp.zeros_like(carry_ref)
        zero_ref[...] = jnp.zeros_like(zero_ref)

    def start_row(slot, t):
        for k in range(2):
            pltpu.make_async_copy(_row_slab(slabs[slot], t), _row_slab(xs_hbm, dests[slot][k, t]),
                                  sem.at[slot]).start(priority=k)

    def wait_scatter(slot):
        for _ in range(2):
            pltpu.make_async_copy(slabs[slot], xs_hbm.at[pl.ds(0, ROUTE_TILE * SLAB)], sem.at[slot]).wait()

    def route(slot, scatter_previous):
        pending = list(range(0, ROUTE_TILE, ROUTER_BURST)) if scatter_previous else []

        def burst():
            if pending:
                r0 = pending.pop(0)
                for t in range(r0, r0 + ROUTER_BURST):
                    start_row(1 - slot, t)

        lane = lax.broadcasted_iota(jnp.int32, (1, LANES), 1)
        lane_f = lane.astype(F32)
        h = h_ref[...]
        h0, h1, h2 = _split_bf16(h)
        w0, w1, w2 = wr_ref[0], wr_ref[1], wr_ref[2]
        small = None
        for a, b in ((h2, w0), (h0, w2), (h1, w1)):
            part = jnp.dot(a, b, preferred_element_type=F32)
            small = part if small is None else small + part
            burst()
        mid = None
        for a, b in ((h1, w0), (h0, w1)):
            part = jnp.dot(a, b, preferred_element_type=F32)
            mid = part if mid is None else mid + part
            burst()
        logits = (small + mid) + jnp.dot(h0, w0, preferred_element_type=F32) + br_ref[...]
        burst()
        logits = jnp.where(lane < N_EXPERTS, logits, -jnp.inf)
        m1 = jnp.max(logits, axis=-1, keepdims=True)
        i1 = jnp.min(jnp.where(logits == m1, lane_f, float(LANES)), axis=-1, keepdims=True)
        rest = jnp.where(lane_f == i1, -jnp.inf, logits)
        m2 = jnp.max(rest, axis=-1, keepdims=True)
        i2 = jnp.min(jnp.where(rest == m2, lane_f, float(LANES)), axis=-1, keepdims=True)
        e2 = jnp.exp(m2 - m1)
        tot = 1.0 + e2
        burst()
        chosen = jnp.where((lane_f == i1) | (lane_f == i2), 1.0, 0.0)
        before = jnp.dot(tri_ref[...], chosen.astype(BF16), preferred_element_type=F32) + carry_ref[...]
        r1 = jnp.sum(jnp.where(lane_f == i1, before, 0.0), axis=-1, keepdims=True)
        r2 = jnp.sum(jnp.where(lane_f == i2, before, 0.0), axis=-1, keepdims=True)
        new_carry = carry_ref[...] + jnp.sum(chosen, axis=0, keepdims=True)
        carry_ref[...] = new_carry
        counts_ref[...] = new_carry.astype(jnp.int32)
        burst()
        record = jnp.zeros((ROUTE_TILE, LANES), F32)
        for lane_id, col in ((META_E1, i1), (META_E2, i2), (META_R1, r1), (META_R2, r2),
                             (META_W1, 1.0 / tot), (META_W2, e2 / tot),
                             (META_D1, i1 * EXPERT_REGION + r1), (META_D2, i2 * EXPERT_REGION + r2)):
            record = jnp.where(lane == lane_id, col, record)
        meta_ref[...] = record
        _to_slabs(slabs[slot], h, ROUTE_TILE)
        while pending:
            burst()
        dest_rows = record.T[META_D1:META_D1 + 8, :].astype(jnp.int32)
        dest_ref[...] = dest_rows
        dest_v[...] = dest_rows
        to_smem = pltpu.make_async_copy(dest_v, dests[slot], idx_sem)
        to_smem.start()
        to_smem.wait()

    @pl.when(i == 0)
    def _():
        route(0, False)

    for slot in range(2):
        @pl.when((i > 0) & (i % 2 == slot))
        def _():
            @pl.when(i >= 2)
            def _():
                wait_scatter(slot)
            route(slot, True)

    @pl.when(i == N_ROUTE_TILES - 1)
    def _():
        slot = (N_ROUTE_TILES - 1) % 2
        lax.fori_loop(0, ROUTE_TILE, lambda t, c: (start_row(slot, t), c)[1], 0, unroll=8)
        wait_scatter(1 - slot)
        wait_scatter(slot)
        count_v[...] = jnp.broadcast_to(carry_ref[...].astype(jnp.int32), count_v.shape)
        cp = pltpu.make_async_copy(count_v, count_s, idx_sem)
        cp.start()
        cp.wait()
        for e in range(N_EXPERTS):
            n = count_s[0, e]
            end = (n + EXPERT_TILE - 1) // EXPERT_TILE * EXPERT_TILE
            pad = lambda r: pltpu.make_async_copy(zero_ref, _row_slab(xs_hbm, e * EXPERT_REGION + r), pad_sem)
            lax.fori_loop(n, end, lambda r, c: (pad(r).start(), c)[1], 0)
            lax.fori_loop(n, end, lambda r, c: (pad(r).wait(), c)[1], 0)


def _router(h, w_router, b_router):
    tri = (jnp.arange(ROUTE_TILE)[:, None] > jnp.arange(ROUTE_TILE)[None, :]).astype(BF16)
    w_split = jnp.stack(_split_bf16(w_router))
    return pl.pallas_call(
        _router_kernel,
        out_shape=[jax.ShapeDtypeStruct((N_TOK, LANES), F32),
                   jax.ShapeDtypeStruct((1, LANES), jnp.int32),
                   jax.ShapeDtypeStruct((8, N_TOK), jnp.int32),
                   jax.ShapeDtypeStruct((N_SORTED * SLAB, LANES), F32)],
        grid=(N_TOK // ROUTE_TILE,),
        in_specs=[_row_spec(ROUTE_TILE, D_MODEL), _const_spec(3, D_MODEL, LANES),
                  _const_spec(1, LANES), _const_spec(ROUTE_TILE, ROUTE_TILE)],
        out_specs=[_row_spec(ROUTE_TILE, LANES), _const_spec(1, LANES),
                   pl.BlockSpec((8, ROUTE_TILE), lambda i: (0, i)), pl.BlockSpec(memory_space=pl.ANY)],
        scratch_shapes=[pltpu.VMEM((1, LANES), F32),
                        pltpu.VMEM((ROUTE_TILE * SLAB, LANES), F32), pltpu.VMEM((ROUTE_TILE * SLAB, LANES), F32),
                        pltpu.VMEM((SLAB, LANES), F32),
                        pltpu.VMEM((8, ROUTE_TILE), jnp.int32),
                        pltpu.SMEM((8, ROUTE_TILE), jnp.int32), pltpu.SMEM((8, ROUTE_TILE), jnp.int32),
                        pltpu.VMEM((8, LANES), jnp.int32), pltpu.SMEM((8, LANES), jnp.int32),
                        pltpu.SemaphoreType.DMA((2,)), pltpu.SemaphoreType.DMA, pltpu.SemaphoreType.DMA],
        compiler_params=_cparams("arbitrary"),
        name="moe_router",
    )(h, w_split, b_router, tri)


def _experts_kernel(tblk_ref, texp_ref, xs_ref, wg_ref, wu_ref, wd_ref, ys_ref, wb_ref):
    i = pl.program_id(0)

    @pl.when((i == 0) | (texp_ref[i] != texp_ref[jnp.maximum(i - 1, 0)]))
    def _():
        for j, w_ref in enumerate((wg_ref, wu_ref, wd_ref)):
            wb_ref[j] = w_ref[0].astype(BF16)

    @pl.when((i == 0) | (tblk_ref[i] != tblk_ref[jnp.maximum(i - 1, 0)]))
    def _():
        x = _from_slabs(xs_ref, EXPERT_TILE).astype(BF16)
        gate = jnp.dot(x, wb_ref[0], preferred_element_type=F32)
        up = jnp.dot(x, wb_ref[1], preferred_element_type=F32)
        act = (jax.nn.silu(gate) * up).astype(BF16)
        _to_slabs(ys_ref, jnp.dot(act, wb_ref[2], preferred_element_type=F32), EXPERT_TILE)


def _experts(tile_block, tile_expert, xs, wg, wu, wd):
    wspec = lambda a, b: pl.BlockSpec((1, a, b), lambda i, tblk, texp: (texp[i], 0, 0))
    rows = pl.BlockSpec((EXPERT_TILE * SLAB, LANES), lambda i, tblk, texp: (tblk[i], 0))
    return pl.pallas_call(
        _experts_kernel,
        out_shape=jax.ShapeDtypeStruct((N_SORTED * SLAB, LANES), F32),
        grid_spec=pltpu.PrefetchScalarGridSpec(
            num_scalar_prefetch=2,
            grid=(N_EXPERT_TILES,),
            in_specs=[rows, wspec(D_MODEL, EXPERT_DIM), wspec(D_MODEL, EXPERT_DIM),
                      wspec(EXPERT_DIM, D_MODEL)],
            out_specs=rows,
            scratch_shapes=[pltpu.VMEM((3, D_MODEL, EXPERT_DIM), BF16)]),
        compiler_params=_cparams("arbitrary"),
        name="moe_experts",
    )(tile_block, tile_expert, xs, wg, wu, wd)


def _combine_ln_kernel(dest_ref, meta_ref, h_ref, ys_hbm, lg_ref, lb_ref, o_ref, ya_ref, yb_ref, sem):
    i = pl.program_id(0)
    bufs = (ya_ref, yb_ref)

    def start_gather(tile, slot):
        def body(t, carry):
            for k in range(2):
                row = dest_ref[k * N_TOK + tile * ROUTE_TILE + t]
                pltpu.make_async_copy(_row_slab(ys_hbm, row), _row_slab(bufs[slot], k * ROUTE_TILE + t),
                                      sem.at[slot]).start(priority=k)
            return carry
        lax.fori_loop(0, ROUTE_TILE, body, 0, unroll=8)

    def wait_gather(slot):
        pltpu.make_async_copy(ys_hbm.at[pl.ds(0, 2 * ROUTE_TILE * SLAB)], bufs[slot], sem.at[slot]).wait()

    last = pl.num_programs(0) - 1
    lane = lax.broadcasted_iota(jnp.int32, (1, LANES), 1)

    def finish_and_prefetch(slot):
        nxt = jnp.minimum(i + 1, last) * ROUTE_TILE
        wait_gather(slot)
        for r0 in range(0, ROUTE_TILE, COMBINE_CHUNK):
            rows = slice(r0, r0 + COMBINE_CHUNK)
            meta = meta_ref[rows, :]
            w1 = jnp.sum(jnp.where(lane == META_W1, meta, 0.0), axis=-1, keepdims=True)
            w2 = jnp.sum(jnp.where(lane == META_W2, meta, 0.0), axis=-1, keepdims=True)
            ff = (w1 * _from_slabs(bufs[slot], COMBINE_CHUNK, first_row=r0)
                  + w2 * _from_slabs(bufs[slot], COMBINE_CHUNK, first_row=ROUTE_TILE + r0))
            o_ref[rows, :] = _layer_norm(DEEPNORM_ALPHA * h_ref[rows, :] + ff, lg_ref[...], lb_ref[...])
            for t in range(r0, r0 + COMBINE_CHUNK):
                for k in range(2):
                    row = dest_ref[k * N_TOK + nxt + t]
                    pltpu.make_async_copy(_row_slab(ys_hbm, row),
                                          _row_slab(bufs[1 - slot], k * ROUTE_TILE + t),
                                          sem.at[1 - slot]).start(priority=k)

        @pl.when(i == last)
        def _():
            wait_gather(1 - slot)

    @pl.when(i == 0)
    def _():
        start_gather(0, 0)

    for slot in range(2):
        @pl.when(i % 2 == slot)
        def _():
            finish_and_prefetch(slot)


def _combine_ln(dest, meta, h, ys, lg, lb):
    row = lambda width: pl.BlockSpec((ROUTE_TILE, width), lambda i, dest: (i, 0))
    const = pl.BlockSpec((1, D_MODEL), lambda i, dest: (0, 0))
    return pl.pallas_call(
        _combine_ln_kernel,
        out_shape=jax.ShapeDtypeStruct((N_TOK, D_MODEL), F32),
        grid_spec=pltpu.PrefetchScalarGridSpec(
            num_scalar_prefetch=1,
            grid=(N_TOK // ROUTE_TILE,),
            in_specs=[row(LANES), row(D_MODEL), pl.BlockSpec(memory_space=pl.ANY), const, const],
            out_specs=row(D_MODEL),
            scratch_shapes=[pltpu.VMEM((2 * ROUTE_TILE * SLAB, LANES), F32)] * 2
                           + [pltpu.SemaphoreType.DMA((2,))]),
        compiler_params=_cparams("arbitrary"),
        name="moe_combine_ln",
    )(dest, meta, h, ys, lg, lb)


def _moe_ln(h, w_router, b_router, wg, wu, wd, lg, lb):
    meta, counts, dest_rows, xs = _router(h, w_router, b_router)
    expert = jnp.arange(N_EXPERTS)
    tiles = (counts[0, :N_EXPERTS] + EXPERT_TILE - 1) // EXPERT_TILE
    ends = jnp.sum(jnp.where(expert[None, :] <= expert[:, None], tiles[None, :], 0), axis=1)
    step = jnp.minimum(jnp.arange(N_EXPERT_TILES), ends[-1] - 1)
    tile_expert = jnp.sum(step[:, None] >= ends[None, :], axis=1)
    first = jnp.sum(jnp.where(tile_expert[:, None] == expert[None, :], (ends - tiles)[None, :], 0), axis=1)
    tile_block = tile_expert * REGION_TILES + (step - first)
    ys = _experts(tile_block.astype(jnp.int32), tile_expert.astype(jnp.int32), xs, wg, wu, wd)
    dest = dest_rows[:2].reshape(2 * N_TOK)
    return _combine_ln(dest, meta, h, ys, lg, lb)


def kernel(x, positions, ln_g, ln_b, ssm_lambda_re, ssm_lambda_im, ssm_log_step, ssm_b_re, ssm_b_im, ssm_c_re, ssm_c_im, ssm_d, ssm_w_glu, kv_w, attn_w_q, attn_sinks, attn_w_out, ffn_w_gate, ffn_w_up, ffn_w_down, moe_w_router, moe_b_router, moe_w_gate, moe_w_up, moe_w_down):
    ln = lambda layer, j: (ln_g[layer, j].reshape(1, D_MODEL).astype(F32),
                           ln_b[layer, j].reshape(1, D_MODEL).astype(F32))
    rope_table = _rope_tables(positions)
    x2 = x.reshape(N_TOK, D_MODEL)

    params = _ssm_params(ssm_lambda_re[0], ssm_lambda_im[0], ssm_log_step[0], ssm_b_re[0], ssm_b_im[0],
                         ssm_c_re[0], ssm_c_im[0])
    y = _ssm_mixer(x2, params)
    h = _glu_ln(y, x2, ssm_d[0].astype(F32).reshape(1, D_MODEL), ssm_w_glu[0].astype(BF16), *ln(0, 0))
    h = _ffn_ln(h, ffn_w_gate[0].astype(BF16), ffn_w_up[0].astype(BF16), ffn_w_down[0].astype(BF16),
                *ln(0, 1))

    order = jnp.array(HEAD_ORDER)
    w_q = attn_w_q[0].reshape(D_MODEL, N_HEADS, HEAD_DIM)[:, order].reshape(D_MODEL, Q_DIM)
    w_out = attn_w_out[0].reshape(N_HEADS, HEAD_DIM, D_MODEL)[order].reshape(Q_DIM, D_MODEL)
    w_qkv = jnp.concatenate([w_q, kv_w], axis=1).astype(BF16)
    q, k, v = _qkv(h, w_qkv, rope_table)
    h = _attention_ln(q, k, v, attn_sinks[0].astype(F32), h, w_out.astype(BF16), *ln(1, 0))
    w_router = jnp.pad(moe_w_router[0].astype(F32), ((0, 0), (0, LANES - N_EXPERTS)))
    b_router = jnp.pad(moe_b_router[0].astype(F32), (0, LANES - N_EXPERTS)).reshape(1, LANES)
    h = _moe_ln(h, w_router, b_router, moe_w_gate[0].astype(F32), moe_w_up[0].astype(F32),
                moe_w_down[0].astype(F32), *ln(1, 1))
    return h.reshape(BATCH, SEQ, D_MODEL)
```

```python
import math

import jax
import jax.numpy as jnp
import numpy as np
from jax import lax
from jax.experimental import pallas as pl
from jax.experimental.pallas import tpu as pltpu

F32 = jnp.float32
BF16 = jnp.bfloat16

D_MODEL = 1024
BATCH = 16
SEQ = 2048
N_TOK = BATCH * SEQ
DEPTH = 2

SSM_GROUP_CH = 16
SSM_GROUPS = D_MODEL // SSM_GROUP_CH
SSM_STATE = 64
SSM_CHUNK = 16
SSM_NCHUNK = SEQ // SSM_CHUNK
SSM_WIDTH = SSM_CHUNK * SSM_GROUP_CH
SSM_SCAN_STEPS = int(math.log2(SSM_NCHUNK))

N_HEADS = 16
HEAD_DIM = 64
N_KV_HEADS = 4
Q_PER_KV = N_HEADS // N_KV_HEADS
KV_DIM = N_KV_HEADS * HEAD_DIM
WINDOW = 128
BLOCK = 128
ROT_DIM = HEAD_DIM // 4
ROT_HALF = ROT_DIM // 2
ROPE_THETA = 500000.0

FFN_DIM = 2816
N_EXPERTS = 8
EXPERT_DIM = 1024

DEEPNORM_ALPHA = (2 * DEPTH) ** 0.25
LOG2_E = math.log2(math.e)
LN_EPS = 1e-5

LANES = 128
VMEM_LIMIT = 56 * 1024 * 1024


def _cparams(*sem):
    return pltpu.CompilerParams(dimension_semantics=sem, vmem_limit_bytes=VMEM_LIMIT)


def _layer_norm(r, g, b):
    mu = jnp.mean(r, axis=-1, keepdims=True)
    xc = r - mu
    var = jnp.mean(xc * xc, axis=-1, keepdims=True)
    return xc * lax.rsqrt(var + LN_EPS) * g + b


def _rope_kernel(invf_ref, pos_ref, cs_ref):
    pos = pos_ref[...]
    for f in range(ROT_HALF):
        ang = pos * invf_ref[f]
        cs_ref[f] = jnp.cos(ang)
        cs_ref[ROT_HALF + f] = jnp.sin(ang)


def _rope_tables(positions):
    inv_freq = ROPE_THETA ** (-jnp.arange(0, ROT_DIM, 2, dtype=F32) / ROT_DIM)
    pos = positions.astype(F32)
    cs = pl.pallas_call(
        _rope_kernel,
        out_shape=jax.ShapeDtypeStruct((ROT_DIM, BATCH, SEQ), F32),
        in_specs=[pl.BlockSpec(memory_space=pltpu.SMEM),
                  pl.BlockSpec(memory_space=pltpu.VMEM)],
        out_specs=pl.BlockSpec(memory_space=pltpu.VMEM),
        name="rope_tables",
    )(inv_freq, pos)
    return cs.reshape(ROT_DIM, N_TOK).T


def _rope_placement():
    f = np.arange(ROT_HALF)
    place = np.zeros((ROT_DIM, LANES), np.float32)
    place[f, f] = place[f, ROT_HALF + f] = 1.0
    place[ROT_HALF + f, HEAD_DIM + f] = -1.0
    place[ROT_HALF + f, HEAD_DIM + ROT_HALF + f] = 1.0
    lane = np.arange(LANES)
    bias = ((lane >= ROT_DIM) & (lane < HEAD_DIM)).astype(np.float32).reshape(1, LANES)
    return jnp.asarray(place, BF16), jnp.asarray(bias)


def _ssm_params(lam_re, lam_im, log_step, b_re, b_im, c_re, c_im):
    hp = lax.Precision.HIGHEST
    lr, li = lam_re.astype(F32), lam_im.astype(F32)
    dt = jnp.exp(log_step.astype(F32))[:, None]
    mag = jnp.exp(lr * dt)
    ar = mag * jnp.cos(li * dt)
    ai = mag * jnp.sin(li * dt)
    nr = ar - 1.0
    den = lr * lr + li * li
    kr = (nr * lr + ai * li) / den
    ki = (ai * lr - nr * li) / den
    br, bi = b_re.astype(F32), b_im.astype(F32)
    bbar_r = kr[..., None] * br - ki[..., None] * bi
    bbar_i = kr[..., None] * bi + ki[..., None] * br
    cr, ci = c_re.astype(F32), c_im.astype(F32)

    def powers(taus):
        t = taus.astype(F32)[:, None, None]
        m = jnp.exp(lr[None] * dt[None] * t)
        ang = li[None] * dt[None] * t
        return m * jnp.cos(ang), m * jnp.sin(ang)

    er, ei = powers(jnp.arange(SSM_CHUNK + 1))
    w_r = er[:, :, :, None] * bbar_r[None] - ei[:, :, :, None] * bbar_i[None]
    w_i = er[:, :, :, None] * bbar_i[None] + ei[:, :, :, None] * bbar_r[None]
    kern = (jnp.einsum('gcp,tgpd->gtcd', cr, w_r[:SSM_CHUNK], precision=hp)
            - jnp.einsum('gcp,tgpd->gtcd', ci, w_i[:SSM_CHUNK], precision=hp))
    s_idx = jnp.arange(SSM_CHUNK)[:, None, None]
    t_idx = jnp.arange(SSM_CHUNK)[None, :, None]
    lag = (t_idx - s_idx == jnp.arange(SSM_CHUNK)[None, None, :]).astype(F32)
    toep = jnp.einsum('stu,gucd->gtcsd', lag, kern, precision=hp)
    mt = toep.reshape(SSM_GROUPS, SSM_WIDTH, SSM_WIDTH)
    rev = SSM_CHUNK - 1 - jnp.arange(SSM_CHUNK)
    g_r = w_r[rev].transpose(1, 2, 0, 3).reshape(SSM_GROUPS, SSM_STATE, SSM_WIDTH)
    g_i = w_i[rev].transpose(1, 2, 0, 3).reshape(SSM_GROUPS, SSM_STATE, SSM_WIDTH)
    gt = jnp.concatenate([g_r, g_i], axis=1)
    e1r, e1i = er[1:], ei[1:]
    ce_r = cr[None] * e1r[:, :, None, :] - ci[None] * e1i[:, :, None, :]
    ce_i = cr[None] * e1i[:, :, None, :] + ci[None] * e1r[:, :, None, :]
    c_re = ce_r.transpose(1, 0, 2, 3).reshape(SSM_GROUPS, SSM_WIDTH, SSM_STATE)
    c_im = (-ce_i).transpose(1, 0, 2, 3).reshape(SSM_GROUPS, SSM_WIDTH, SSM_STATE)
    ct = jnp.concatenate([c_re, c_im], axis=-1)
    sr, si = powers(SSM_CHUNK * (2 ** jnp.arange(8)))
    coef = lambda t: jnp.concatenate([t, t], axis=-1).transpose(1, 0, 2)
    return mt.astype(BF16), gt.astype(BF16), ct.astype(BF16), coef(sr), coef(si)


SSM_LANE_GROUPS = LANES // SSM_GROUP_CH
SSM_LANE_CHUNKS = D_MODEL // LANES
SSM_BATCH_TILE = 4
SSM_COLS = SSM_BATCH_TILE * SSM_NCHUNK


def _ssm_kernel(x_ref, mt_ref, gt_ref, ct_ref, cr_ref, ci_ref, y_ref, v_ref, yt_ref):
    for bl in range(SSM_BATCH_TILE):
        cols = slice(bl * SSM_NCHUNK, (bl + 1) * SSM_NCHUNK)
        for s in range(SSM_CHUNK):
            a = x_ref[pl.ds(bl * SEQ + s, SSM_NCHUNK, stride=SSM_CHUNK), :]
            at = a.T.astype(BF16)
            for g in range(SSM_LANE_GROUPS):
                v_ref[g, s * SSM_GROUP_CH:(s + 1) * SSM_GROUP_CH, cols] = (
                    at[g * SSM_GROUP_CH:(g + 1) * SSM_GROUP_CH])

    row_j = lax.broadcasted_iota(jnp.int32, (SSM_NCHUNK, 1), 0)

    def group_body(g, carry):
        ut = v_ref[g]
        xt = jnp.dot(gt_ref[g], ut, preferred_element_type=F32)
        cr_all = cr_ref[g]
        ci_all = ci_ref[g]

        def seg(rows, b):
            return xt[rows, b * SSM_NCHUNK:(b + 1) * SSM_NCHUNK]

        def shifted(x, d):
            if d < 8:
                return jnp.where(row_j >= d, pltpu.roll(x, d, 0), 0.0)
            return jnp.concatenate([jnp.zeros((d, LANES), F32), x[:SSM_NCHUNK - d]], axis=0)

        re_rows, im_rows = slice(0, SSM_STATE), slice(SSM_STATE, 2 * SSM_STATE)
        sr, si = [], []
        for q in range(SSM_BATCH_TILE // 2):
            xr = jnp.concatenate([seg(re_rows, 2 * q), seg(re_rows, 2 * q + 1)], axis=0).T
            xi = jnp.concatenate([seg(im_rows, 2 * q), seg(im_rows, 2 * q + 1)], axis=0).T
            for k in range(SSM_SCAN_STEPS):
                cr = cr_all[k:k + 1, :]
                ci = ci_all[k:k + 1, :]
                rr = shifted(xr, 1 << k)
                ri = shifted(xi, 1 << k)
                xr, xi = xr + cr * rr - ci * ri, xi + cr * ri + ci * rr
            pr = shifted(xr, 1).T
            pi = shifted(xi, 1).T
            sr += [pr[:SSM_STATE], pr[SSM_STATE:]]
            si += [pi[:SSM_STATE], pi[SSM_STATE:]]
        sprev = jnp.concatenate([jnp.concatenate(sr, axis=1), jnp.concatenate(si, axis=1)],
                                axis=0).astype(BF16)
        yt_ref[g] = (jnp.dot(mt_ref[g], ut, preferred_element_type=F32)
                     + jnp.dot(ct_ref[g], sprev, preferred_element_type=F32))
        return carry

    lax.fori_loop(0, SSM_LANE_GROUPS, group_body, 0, unroll=2)

    for bl in range(SSM_BATCH_TILE):
        cols = slice(bl * SSM_NCHUNK, (bl + 1) * SSM_NCHUNK)
        for t in range(SSM_CHUNK):
            tile = jnp.concatenate(
                [yt_ref[g, t * SSM_GROUP_CH:(t + 1) * SSM_GROUP_CH, cols] for g in range(SSM_LANE_GROUPS)],
                axis=0)
            y_ref[pl.ds(bl * SEQ + t, SSM_NCHUNK, stride=SSM_CHUNK), :] = tile.T


def _ssm_mixer(x2, params):
    mt, gt, ct, coef_r, coef_i = params
    rows = SSM_BATCH_TILE * SEQ
    pspec = lambda a, b: pl.BlockSpec((SSM_LANE_GROUPS, a, b), lambda k, q: (k, 0, 0))
    return pl.pallas_call(
        _ssm_kernel,
        out_shape=jax.ShapeDtypeStruct((N_TOK, D_MODEL), F32),
        grid=(SSM_LANE_CHUNKS, BATCH // SSM_BATCH_TILE),
        in_specs=[pl.BlockSpec((rows, LANES), lambda k, q: (q, k)),
                  pspec(SSM_WIDTH, SSM_WIDTH), pspec(2 * SSM_STATE, SSM_WIDTH), pspec(SSM_WIDTH, 2 * SSM_STATE),
                  pspec(8, 2 * SSM_STATE), pspec(8, 2 * SSM_STATE)],
        out_specs=pl.BlockSpec((rows, LANES), lambda k, q: (q, k)),
        scratch_shapes=[pltpu.VMEM((SSM_LANE_GROUPS, SSM_WIDTH, SSM_COLS), BF16),
                        pltpu.VMEM((SSM_LANE_GROUPS, SSM_WIDTH, SSM_COLS), F32)],
        compiler_params=_cparams("parallel", "parallel"),
        name="ssm_mixer",
    )(x2, mt, gt, ct, coef_r, coef_i)


GLU_COLS = 256


def _glu_ln_kernel(y_ref, x_ref, d_ref, w_ref, lg_ref, lb_ref, o_ref):
    x = x_ref[...]
    act = jax.nn.gelu(y_ref[...] + d_ref[...] * x).astype(BF16)
    parts = []
    for c in range(0, D_MODEL, GLU_COLS):
        val = jnp.dot(act, w_ref[:, c:c + GLU_COLS], preferred_element_type=F32)
        gate = jnp.dot(act, w_ref[:, D_MODEL + c:D_MODEL + c + GLU_COLS], preferred_element_type=F32)
        parts.append(val * jax.nn.sigmoid(gate))
    mix = jnp.concatenate(parts, axis=1)
    o_ref[...] = _layer_norm(DEEPNORM_ALPHA * x + mix, lg_ref[...], lb_ref[...])


def _row_spec(tm, width):
    return pl.BlockSpec((tm, width), lambda i: (i, 0))


def _const_spec(*shape):
    return pl.BlockSpec(shape, lambda *_: (0,) * len(shape))


def _glu_ln(y, x2, d_skip, w_glu, lg, lb, tm=512):
    return pl.pallas_call(
        _glu_ln_kernel,
        out_shape=jax.ShapeDtypeStruct((N_TOK, D_MODEL), F32),
        grid=(N_TOK // tm,),
        in_specs=[_row_spec(tm, D_MODEL), _row_spec(tm, D_MODEL), _const_spec(1, D_MODEL),
                  _const_spec(D_MODEL, 2 * D_MODEL), _const_spec(1, D_MODEL), _const_spec(1, D_MODEL)],
        out_specs=_row_spec(tm, D_MODEL),
        compiler_params=_cparams("parallel"),
        name="glu_ln",
    )(y, x2, d_skip, w_glu, lg, lb)


FFN_SPLIT = 2
FFN_TILE = FFN_DIM // FFN_SPLIT


def _ffn_ln_kernel(h_ref, wg_ref, wu_ref, wd_ref, lg_ref, lb_ref, o_ref):
    h = h_ref[...]
    hb = h.astype(BF16)
    ff = None
    for f in range(FFN_SPLIT):
        cols = slice(f * FFN_TILE, (f + 1) * FFN_TILE)
        gate = jnp.dot(hb, wg_ref[:, cols], preferred_element_type=F32)
        up = jnp.dot(hb, wu_ref[:, cols], preferred_element_type=F32)
        act = (jax.nn.silu(gate) * up).astype(BF16)
        part = jnp.dot(act, wd_ref[cols, :], preferred_element_type=F32)
        ff = part if ff is None else ff + part
    o_ref[...] = _layer_norm(DEEPNORM_ALPHA * h + ff, lg_ref[...], lb_ref[...])


def _resident_spec(*shape):
    return pl.BlockSpec(shape, lambda *_: (0,) * len(shape), pipeline_mode=pl.Buffered(1))


def _ffn_ln(h, wg, wu, wd, lg, lb, tm=512):
    return pl.pallas_call(
        _ffn_ln_kernel,
        out_shape=jax.ShapeDtypeStruct((N_TOK, D_MODEL), F32),
        grid=(N_TOK // tm,),
        in_specs=[_row_spec(tm, D_MODEL),
                  _resident_spec(D_MODEL, FFN_DIM), _resident_spec(D_MODEL, FFN_DIM),
                  _resident_spec(FFN_DIM, D_MODEL),
                  _const_spec(1, D_MODEL), _const_spec(1, D_MODEL)],
        out_specs=_row_spec(tm, D_MODEL),
        compiler_params=_cparams("parallel"),
        name="ffn_ln",
    )(h, wg, wu, wd, lg, lb)


Q_DIM = N_HEADS * HEAD_DIM


def _qkv_kernel(h_ref, w_ref, cs_ref, place_ref, bias_ref, q_ref, k_ref, v_ref):
    hb = h_ref[...].astype(BF16)
    z = jnp.dot(hb, w_ref[...], preferred_element_type=F32)
    lane = lax.broadcasted_iota(jnp.int32, (1, LANES), 1)
    table = bias_ref[...]
    for term in _split_bf16(cs_ref[...]):
        table = table + jnp.dot(term, place_ref[...], preferred_element_type=F32)
    swapped = pltpu.roll(table, HEAD_DIM, 1)
    cos_t = jnp.where(lane < HEAD_DIM, table, swapped)
    sin_t = jnp.where(lane < HEAD_DIM, swapped, table)
    first_half = (lane % HEAD_DIM) < ROT_HALF

    def rope(t):
        partner = jnp.where(first_half, pltpu.roll(t, LANES - ROT_HALF, 1), pltpu.roll(t, ROT_HALF, 1))
        return t * cos_t + partner * sin_t

    scale = HEAD_DIM ** -0.5 * LOG2_E
    for c in range(Q_DIM // LANES):
        sl = slice(c * LANES, (c + 1) * LANES)
        q_ref[:, sl] = (rope(z[:, sl]) * scale).astype(BF16)
    for c in range(KV_DIM // LANES):
        sl = slice(c * LANES, (c + 1) * LANES)
        k_ref[:, sl] = rope(z[:, Q_DIM + c * LANES:Q_DIM + (c + 1) * LANES]).astype(BF16)
    v_ref[...] = z[:, Q_DIM + KV_DIM:].astype(BF16)


def _qkv(h, w_qkv, rope_table, tm=512):
    return pl.pallas_call(
        _qkv_kernel,
        out_shape=[jax.ShapeDtypeStruct((N_TOK, Q_DIM), BF16),
                   jax.ShapeDtypeStruct((N_TOK, KV_DIM), BF16),
                   jax.ShapeDtypeStruct((N_TOK, KV_DIM), BF16)],
        grid=(N_TOK // tm,),
        in_specs=[_row_spec(tm, D_MODEL), _const_spec(D_MODEL, Q_DIM + 2 * KV_DIM),
                  _row_spec(tm, ROT_DIM), _const_spec(ROT_DIM, LANES), _const_spec(1, LANES)],
        out_specs=[_row_spec(tm, Q_DIM), _row_spec(tm, KV_DIM), _row_spec(tm, KV_DIM)],
        compiler_params=_cparams("parallel"),
        name="qkv_rope",
    )(h, w_qkv, rope_table, *_rope_placement())


ATTN_Q_TILE = 512
ATTN_SUB = ATTN_Q_TILE // BLOCK
ATTN_BAND = ATTN_Q_TILE + BLOCK
KV_PAIRS = N_KV_HEADS // 2

HEAD_ORDER = [h for c in range(KV_PAIRS) for g in range(Q_PER_KV)
              for h in (2 * c * Q_PER_KV + g, (2 * c + 1) * Q_PER_KV + g)]


def _band_bias():
    qi = jnp.arange(Q_PER_KV * BLOCK)[:, None] % BLOCK
    si = jnp.arange(2 * BLOCK)[None, :]
    rel = qi + BLOCK - si
    valid = (rel >= 0) & (rel < WINDOW)
    first = valid & (si >= BLOCK)
    neg = jnp.float32(-jnp.inf)
    return jnp.stack([jnp.where(valid, 0.0, neg), jnp.where(first, 0.0, neg)]).astype(F32)


def _attn_kernel(sink_ref, q_ref, kp_ref, kc_ref, vp_ref, vc_ref, bias_ref, h_ref, w_ref, lg_ref, lb_ref,
                 out_ref, ka_ref, kb_ref, va_ref, vb_ref, o_ref):
    i = pl.program_id(1)
    lane = lax.broadcasted_iota(jnp.int32, (1, KV_DIM), 1)
    low = (lane % LANES) < HEAD_DIM
    zero = jnp.zeros((), BF16)
    for src, dst_a, dst_b in ((kp_ref, ka_ref, kb_ref), (vp_ref, va_ref, vb_ref)):
        t = src[...]
        dst_a[0:BLOCK, :] = jnp.where(low, t, zero)
        dst_b[0:BLOCK, :] = jnp.where(low, zero, t)
    for src, dst_a, dst_b in ((kc_ref, ka_ref, kb_ref), (vc_ref, va_ref, vb_ref)):
        t = src[...]
        dst_a[BLOCK:ATTN_BAND, :] = jnp.where(low, t, zero)
        dst_b[BLOCK:ATTN_BAND, :] = jnp.where(low, zero, t)

    lane1 = lax.broadcasted_iota(jnp.int32, (1, LANES), 1)
    row = lax.broadcasted_iota(jnp.int32, (Q_PER_KV * BLOCK, 1), 0)
    contract_last = (((1,), (1,)), ((), ()))
    for blk in range(ATTN_SUB):
        r0 = blk * BLOCK
        if blk == 0:
            bias = bias_ref[jnp.where(i == 0, 1, 0)]
        else:
            bias = bias_ref[0]
        for c in range(KV_PAIRS):
            cs = slice(c * LANES, (c + 1) * LANES)
            chunks = [c * Q_PER_KV + g for g in range(Q_PER_KV)]
            q4 = jnp.concatenate([q_ref[r0:r0 + BLOCK, m * LANES:(m + 1) * LANES] for m in chunks], axis=0)
            outs = []
            for half, (k_ref, v_ref) in enumerate(((ka_ref, va_ref), (kb_ref, vb_ref))):
                kband = k_ref[r0:r0 + 2 * BLOCK, cs]
                s = lax.dot_general(q4, kband, contract_last, preferred_element_type=F32) + bias
                sink = jnp.zeros((Q_PER_KV * BLOCK, 1), F32)
                for g, m in enumerate(chunks):
                    sink = jnp.where(row // BLOCK == g, sink_ref[HEAD_ORDER[2 * m + half]] * LOG2_E, sink)
                mx = jnp.maximum(jnp.max(s, axis=-1, keepdims=True), sink)
                p = jnp.exp2(s - mx)
                denom = jnp.sum(p, axis=-1, keepdims=True) + jnp.exp2(sink - mx)
                pv = jnp.dot(p.astype(BF16), v_ref[r0:r0 + 2 * BLOCK, cs], preferred_element_type=F32)
                outs.append((pv, 1.0 / denom))
            (pv_a, r_a), (pv_b, r_b) = outs
            o = (pv_a + pv_b) * jnp.where(lane1 < HEAD_DIM, r_a, r_b)
            for g, m in enumerate(chunks):
                o_ref[r0:r0 + BLOCK, m * LANES:(m + 1) * LANES] = o[g * BLOCK:(g + 1) * BLOCK].astype(BF16)

    mix = jnp.dot(o_ref[...], w_ref[...], preferred_element_type=F32)
    out_ref[...] = _layer_norm(DEEPNORM_ALPHA * h_ref[...] + mix, lg_ref[...], lb_ref[...])


def _attention_ln(q, k, v, sinks, h, w_out, lg, lb):
    tiles = SEQ // ATTN_Q_TILE
    cur = lambda b, i: (b * tiles + i, 0)
    prev = lambda b, i: (jnp.maximum((b * tiles + i) * ATTN_SUB - 1, 0), 0)
    const2 = lambda b, i: (0, 0)
    return pl.pallas_call(
        _attn_kernel,
        out_shape=jax.ShapeDtypeStruct((N_TOK, D_MODEL), F32),
        grid=(BATCH, tiles),
        in_specs=[pl.BlockSpec(memory_space=pltpu.SMEM),
                  pl.BlockSpec((ATTN_Q_TILE, Q_DIM), cur),
                  pl.BlockSpec((BLOCK, KV_DIM), prev), pl.BlockSpec((ATTN_Q_TILE, KV_DIM), cur),
                  pl.BlockSpec((BLOCK, KV_DIM), prev), pl.BlockSpec((ATTN_Q_TILE, KV_DIM), cur),
                  pl.BlockSpec((2, Q_PER_KV * BLOCK, 2 * BLOCK), lambda b, i: (0, 0, 0)),
                  pl.BlockSpec((ATTN_Q_TILE, D_MODEL), cur),
                  pl.BlockSpec((Q_DIM, D_MODEL), const2),
                  pl.BlockSpec((1, D_MODEL), const2), pl.BlockSpec((1, D_MODEL), const2)],
        out_specs=pl.BlockSpec((ATTN_Q_TILE, D_MODEL), cur),
        scratch_shapes=[pltpu.VMEM((ATTN_BAND, KV_DIM), BF16)] * 4
                       + [pltpu.VMEM((ATTN_Q_TILE, Q_DIM), BF16)],
        compiler_params=_cparams("parallel", "parallel"),
        name="swa_attention_ln",
    )(sinks, q, k, k, v, v, _band_bias(), h, w_out, lg, lb)


ROUTE_TILE = 512
EXPERT_TILE = 512
SLAB = D_MODEL // LANES
EXPERT_REGION = N_TOK
REGION_TILES = EXPERT_REGION // EXPERT_TILE
N_SORTED = N_EXPERTS * EXPERT_REGION
N_EXPERT_TILES = 2 * N_TOK // EXPERT_TILE + N_EXPERTS
META_E1, META_E2, META_R1, META_R2, META_W1, META_W2 = range(6)
META_D1, META_D2 = 8, 9
COMBINE_CHUNK = 64
N_ROUTE_TILES = N_TOK // ROUTE_TILE
ROUTER_BURST = 64


def _to_slabs(ref, val, rows):
    for c in range(SLAB):
        ref[pl.ds(c, rows, stride=SLAB), :] = val[:, c * LANES:(c + 1) * LANES]


def _from_slabs(ref, rows, first_row=0):
    return jnp.concatenate([ref[pl.ds(first_row * SLAB + c, rows, stride=SLAB), :] for c in range(SLAB)],
                           axis=1)


def _row_slab(ref, row):
    return ref.at[pl.ds(pl.multiple_of(row * SLAB, SLAB), SLAB)]


def _split_bf16(x):
    def top_bits(v):
        bits = lax.bitcast_convert_type(v, jnp.uint32) & jnp.uint32(0xFFFF0000)
        return lax.bitcast_convert_type(bits, F32)
    hi = top_bits(x)
    mid = top_bits(x - hi)
    lo = (x - hi) - mid
    return hi.astype(BF16), mid.astype(BF16), lo.astype(BF16)


def _router_kernel(h_ref, wr_ref, br_ref, tri_ref, meta_ref, counts_ref, dest_ref, xs_hbm,
                   carry_ref, slab_a, slab_b, zero_ref, dest_v, dest_a, dest_b, count_v, count_s,
                   sem, pad_sem, idx_sem):
    i = pl.program_id(0)
    slabs = (slab_a, slab_b)
    dests = (dest_a, dest_b)

    @pl.when(i == 0)
    def _():
        carry_ref[...] = jnp.zeros_like(carry_ref)
        zero_ref[...] = jnp.zeros_like(zero_ref)

    def start_row(slot, t):
        for k in range(2):
            pltpu.make_async_copy(_row_slab(slabs[slot], t), _row_slab(xs_hbm, dests[slot][k, t]),
                                  sem.at[slot]).start(priority=k)

    def wait_scatter(slot):
        for _ in range(2):
            pltpu.make_async_copy(slabs[slot], xs_hbm.at[pl.ds(0, ROUTE_TILE * SLAB)], sem.at[slot]).wait()

    def route(slot, scatter_previous):
        pending = list(range(0, ROUTE_TILE, ROUTER_BURST)) if scatter_previous else []

        def burst():
            if pending:
                r0 = pending.pop(0)
                for t in range(r0, r0 + ROUTER_BURST):
                    start_row(1 - slot, t)

        lane = lax.broadcasted_iota(jnp.int32, (1, LANES), 1)
        lane_f = lane.astype(F32)
        h = h_ref[...]
        h0, h1, h2 = _split_bf16(h)
        w0, w1, w2 = wr_ref[0], wr_ref[1], wr_ref[2]
        small = None
        for a, b in ((h2, w0), (h0, w2), (h1, w1)):
            part = jnp.dot(a, b, preferred_element_type=F32)
            small = part if small is None else small + part
            burst()
        mid = None
        for a, b in ((h1, w0), (h0, w1)):
            part = jnp.dot(a, b, preferred_element_type=F32)
            mid = part if mid is None else mid + part
            burst()
        logits = (small + mid) + jnp.dot(h0, w0, preferred_element_type=F32) + br_ref[...]
        burst()
        logits = jnp.where(lane < N_EXPERTS, logits, -jnp.inf)
        m1 = jnp.max(logits, axis=-1, keepdims=True)
        i1 = jnp.min(jnp.where(logits == m1, lane_f, float(LANES)), axis=-1, keepdims=True)
        rest = jnp.where(lane_f == i1, -jnp.inf, logits)
        m2 = jnp.max(rest, axis=-1, keepdims=True)
        i2 = jnp.min(jnp.where(rest == m2, lane_f, float(LANES)), axis=-1, keepdims=True)
        e2 = jnp.exp(m2 - m1)
        tot = 1.0 + e2
        burst()
        chosen = jnp.where((lane_f == i1) | (lane_f == i2), 1.0, 0.0)
        before = jnp.dot(tri_ref[...], chosen.astype(BF16), preferred_element_type=F32) + carry_ref[...]
        r1 = jnp.sum(jnp.where(lane_f == i1, before, 0.0), axis=-1, keepdims=True)
        r2 = jnp.sum(jnp.where(lane_f == i2, before, 0.0), axis=-1, keepdims=True)
        new_carry = carry_ref[...] + jnp.sum(chosen, axis=0, keepdims=True)
        carry_ref[...] = new_carry
        counts_ref[...] = new_carry.astype(jnp.int32)
        burst()
        record = jnp.zeros((ROUTE_TILE, LANES), F32)
        for lane_id, col in ((META_E1, i1), (META_E2, i2), (META_R1, r1), (META_R2, r2),
                             (META_W1, 1.0 / tot), (META_W2, e2 / tot),
                             (META_D1, i1 * EXPERT_REGION + r1), (META_D2, i2 * EXPERT_REGION + r2)):
            record = jnp.where(lane == lane_id, col, record)
        meta_ref[...] = record
        _to_slabs(slabs[slot], h, ROUTE_TILE)
        while pending:
            burst()
        dest_rows = record.T[META_D1:META_D1 + 8, :].astype(jnp.int32)
        dest_ref[...] = dest_rows
        dest_v[...] = dest_rows
        to_smem = pltpu.make_async_copy(dest_v, dests[slot], idx_sem)
        to_smem.start()
        to_smem.wait()

    @pl.when(i == 0)
    def _():
        route(0, False)

    for slot in range(2):
        @pl.when((i > 0) & (i % 2 == slot))
        def _():
            @pl.when(i >= 2)
            def _():
                wait_scatter(slot)
            route(slot, True)

    @pl.when(i == N_ROUTE_TILES - 1)
    def _():
        slot = (N_ROUTE_TILES - 1) % 2
        lax.fori_loop(0, ROUTE_TILE, lambda t, c: (start_row(slot, t), c)[1], 0, unroll=8)
        wait_scatter(1 - slot)
        wait_scatter(slot)
        count_v[...] = jnp.broadcast_to(carry_ref[...].astype(jnp.int32), count_v.shape)
        cp = pltpu.make_async_copy(count_v, count_s, idx_sem)
        cp.start()
        cp.wait()
        for e in range(N_EXPERTS):
            n = count_s[0, e]
            end = (n + EXPERT_TILE - 1) // EXPERT_TILE * EXPERT_TILE
            pad = lambda r: pltpu.make_async_copy(zero_ref, _row_slab(xs_hbm, e * EXPERT_REGION + r), pad_sem)
            lax.fori_loop(n, end, lambda r, c: (pad(r).start(), c)[1], 0)
            lax.fori_loop(n, end, lambda r, c: (pad(r).wait(), c)[1], 0)


def _router(h, w_router, b_router):
    tri = (jnp.arange(ROUTE_TILE)[:, None] > jnp.arange(ROUTE_TILE)[None, :]).astype(BF16)
    w_split = jnp.stack(_split_bf16(w_router))
    return pl.pallas_call(
        _router_kernel,
        out_shape=[jax.ShapeDtypeStruct((N_TOK, LANES), F32),
                   jax.ShapeDtypeStruct((1, LANES), jnp.int32),
                   jax.ShapeDtypeStruct((8, N_TOK), jnp.int32),
                   jax.ShapeDtypeStruct((N_SORTED * SLAB, LANES), F32)],
        grid=(N_TOK // ROUTE_TILE,),
        in_specs=[_row_spec(ROUTE_TILE, D_MODEL), _const_spec(3, D_MODEL, LANES),
                  _const_spec(1, LANES), _const_spec(ROUTE_TILE, ROUTE_TILE)],
        out_specs=[_row_spec(ROUTE_TILE, LANES), _const_spec(1, LANES),
                   pl.BlockSpec((8, ROUTE_TILE), lambda i: (0, i)), pl.BlockSpec(memory_space=pl.ANY)],
        scratch_shapes=[pltpu.VMEM((1, LANES), F32),
                        pltpu.VMEM((ROUTE_TILE * SLAB, LANES), F32), pltpu.VMEM((ROUTE_TILE * SLAB, LANES), F32),
                        pltpu.VMEM((SLAB, LANES), F32),
                        pltpu.VMEM((8, ROUTE_TILE), jnp.int32),
                        pltpu.SMEM((8, ROUTE_TILE), jnp.int32), pltpu.SMEM((8, ROUTE_TILE), jnp.int32),
                        pltpu.VMEM((8, LANES), jnp.int32), pltpu.SMEM((8, LANES), jnp.int32),
                        pltpu.SemaphoreType.DMA((2,)), pltpu.SemaphoreType.DMA, pltpu.SemaphoreType.DMA],
        compiler_params=_cparams("arbitrary"),
        name="moe_router",
    )(h, w_split, b_router, tri)


def _experts_kernel(tblk_ref, texp_ref, xs_ref, wg_ref, wu_ref, wd_ref, ys_ref, wb_ref):
    i = pl.program_id(0)

    @pl.when((i == 0) | (texp_ref[i] != texp_ref[jnp.maximum(i - 1, 0)]))
    def _():
        for j, w_ref in enumerate((wg_ref, wu_ref, wd_ref)):
            wb_ref[j] = w_ref[0].astype(BF16)

    @pl.when((i == 0) | (tblk_ref[i] != tblk_ref[jnp.maximum(i - 1, 0)]))
    def _():
        x = _from_slabs(xs_ref, EXPERT_TILE).astype(BF16)
        gate = jnp.dot(x, wb_ref[0], preferred_element_type=F32)
        up = jnp.dot(x, wb_ref[1], preferred_element_type=F32)
        act = (jax.nn.silu(gate) * up).astype(BF16)
        _to_slabs(ys_ref, jnp.dot(act, wb_ref[2], preferred_element_type=F32), EXPERT_TILE)


def _experts(tile_block, tile_expert, xs, wg, wu, wd):
    wspec = lambda a, b: pl.BlockSpec((1, a, b), lambda i, tblk, texp: (texp[i], 0, 0))
    rows = pl.BlockSpec((EXPERT_TILE * SLAB, LANES), lambda i, tblk, texp: (tblk[i], 0))
    return pl.pallas_call(
        _experts_kernel,
        out_shape=jax.ShapeDtypeStruct((N_SORTED * SLAB, LANES), F32),
        grid_spec=pltpu.PrefetchScalarGridSpec(
            num_scalar_prefetch=2,
            grid=(N_EXPERT_TILES,),
            in_specs=[rows, wspec(D_MODEL, EXPERT_DIM), wspec(D_MODEL, EXPERT_DIM),
                      wspec(EXPERT_DIM, D_MODEL)],
            out_specs=rows,
            scratch_shapes=[pltpu.VMEM((3, D_MODEL, EXPERT_DIM), BF16)]),
        compiler_params=_cparams("arbitrary"),
        name="moe_experts",
    )(tile_block, tile_expert, xs, wg, wu, wd)


def _combine_ln_kernel(dest_ref, meta_ref, h_ref, ys_hbm, lg_ref, lb_ref, o_ref, ya_ref, yb_ref, sem):
    i = pl.program_id(0)
    bufs = (ya_ref, yb_ref)

    def start_gather(tile, slot):
        def body(t, carry):
            for k in range(2):
                row = dest_ref[k * N_TOK + tile * ROUTE_TILE + t]
                pltpu.make_async_copy(_row_slab(ys_hbm, row), _row_slab(bufs[slot], k * ROUTE_TILE + t),
                                      sem.at[slot]).start(priority=k)
            return carry
        lax.fori_loop(0, ROUTE_TILE, body, 0, unroll=8)

    def wait_gather(slot):
        pltpu.make_async_copy(ys_hbm.at[pl.ds(0, 2 * ROUTE_TILE * SLAB)], bufs[slot], sem.at[slot]).wait()

    last = pl.num_programs(0) - 1
    lane = lax.broadcasted_iota(jnp.int32, (1, LANES), 1)

    def finish_and_prefetch(slot):
        nxt = jnp.minimum(i + 1, last) * ROUTE_TILE
        wait_gather(slot)
        for r0 in range(0, ROUTE_TILE, COMBINE_CHUNK):
            rows = slice(r0, r0 + COMBINE_CHUNK)
            meta = meta_ref[rows, :]
            w1 = jnp.sum(jnp.where(lane == META_W1, meta, 0.0), axis=-1, keepdims=True)
            w2 = jnp.sum(jnp.where(lane == META_W2, meta, 0.0), axis=-1, keepdims=True)
            ff = (w1 * _from_slabs(bufs[slot], COMBINE_CHUNK, first_row=r0)
                  + w2 * _from_slabs(bufs[slot], COMBINE_CHUNK, first_row=ROUTE_TILE + r0))
            o_ref[rows, :] = _layer_norm(DEEPNORM_ALPHA * h_ref[rows, :] + ff, lg_ref[...], lb_ref[...])
            for t in range(r0, r0 + COMBINE_CHUNK):
                for k in range(2):
                    row = dest_ref[k * N_TOK + nxt + t]
                    pltpu.make_async_copy(_row_slab(ys_hbm, row),
                                          _row_slab(bufs[1 - slot], k * ROUTE_TILE + t),
                                          sem.at[1 - slot]).start(priority=k)

        @pl.when(i == last)
        def _():
            wait_gather(1 - slot)

    @pl.when(i == 0)
    def _():
        start_gather(0, 0)

    for slot in range(2):
        @pl.when(i % 2 == slot)
        def _():
            finish_and_prefetch(slot)


def _combine_ln(dest, meta, h, ys, lg, lb):
    row = lambda width: pl.BlockSpec((ROUTE_TILE, width), lambda i, dest: (i, 0))
    const = pl.BlockSpec((1, D_MODEL), lambda i, dest: (0, 0))
    return pl.pallas_call(
        _combine_ln_kernel,
        out_shape=jax.ShapeDtypeStruct((N_TOK, D_MODEL), F32),
        grid_spec=pltpu.PrefetchScalarGridSpec(
            num_scalar_prefetch=1,
            grid=(N_TOK // ROUTE_TILE,),
            in_specs=[row(LANES), row(D_MODEL), pl.BlockSpec(memory_space=pl.ANY), const, const],
            out_specs=row(D_MODEL),
            scratch_shapes=[pltpu.VMEM((2 * ROUTE_TILE * SLAB, LANES), F32)] * 2
                           + [pltpu.SemaphoreType.DMA((2,))]),
        compiler_params=_cparams("arbitrary"),
        name="moe_combine_ln",
    )(dest, meta, h, ys, lg, lb)


def _moe_ln(h, w_router, b_router, wg, wu, wd, lg, lb):
    meta, counts, dest_rows, xs = _router(h, w_router, b_router)
    expert = jnp.arange(N_EXPERTS)
    tiles = (counts[0, :N_EXPERTS] + EXPERT_TILE - 1) // EXPERT_TILE
    ends = jnp.sum(jnp.where(expert[None, :] <= expert[:, None], tiles[None, :], 0), axis=1)
    step = jnp.minimum(jnp.arange(N_EXPERT_TILES), ends[-1] - 1)
    tile_expert = jnp.sum(step[:, None] >= ends[None, :], axis=1)
    first = jnp.sum(jnp.where(tile_expert[:, None] == expert[None, :], (ends - tiles)[None, :], 0), axis=1)
    tile_block = tile_expert * REGION_TILES + (step - first)
    ys = _experts(tile_block.astype(jnp.int32), tile_expert.astype(jnp.int32), xs, wg, wu, wd)
    dest = dest_rows[:2].reshape(2 * N_TOK)
    return _combine_ln(dest, meta, h, ys, lg, lb)


def kernel(x, positions, ln_g, ln_b, ssm_lambda_re, ssm_lambda_im, ssm_log_step, ssm_b_re, ssm_b_im, ssm_c_re, ssm_c_im, ssm_d, ssm_w_glu, kv_w, attn_w_q, attn_sinks, attn_w_out, ffn_w_gate, ffn_w_up, ffn_w_down, moe_w_router, moe_b_router, moe_w_gate, moe_w_up, moe_w_down):
    ln = lambda layer, j: (ln_g[layer, j].reshape(1, D_MODEL).astype(F32),
                           ln_b[layer, j].reshape(1, D_MODEL).astype(F32))
    rope_table = _rope_tables(positions)
    x2 = x.reshape(N_TOK, D_MODEL)

    params = _ssm_params(ssm_lambda_re[0], ssm_lambda_im[0], ssm_log_step[0], ssm_b_re[0], ssm_b_im[0],
                         ssm_c_re[0], ssm_c_im[0])
    y = _ssm_mixer(x2, params)
    h = _glu_ln(y, x2, ssm_d[0].astype(F32).reshape(1, D_MODEL), ssm_w_glu[0].astype(BF16), *ln(0, 0))
    h = _ffn_ln(h, ffn_w_gate[0].astype(BF16), ffn_w_up[0].astype(BF16), ffn_w_down[0].astype(BF16),
                *ln(0, 1))

    order = jnp.array(HEAD_ORDER)
    w_q = attn_w_q[0].reshape(D_MODEL, N_HEADS, HEAD_DIM)[:, order].reshape(D_MODEL, Q_DIM)
    w_out = attn_w_out[0].reshape(N_HEADS, HEAD_DIM, D_MODEL)[order].reshape(Q_DIM, D_MODEL)
    w_qkv = jnp.concatenate([w_q, kv_w], axis=1).astype(BF16)
    q, k, v = _qkv(h, w_qkv, rope_table)
    h = _attention_ln(q, k, v, attn_sinks[0].astype(F32), h, w_out.astype(BF16), *ln(1, 0))
    w_router = jnp.pad(moe_w_router[0].astype(F32), ((0, 0), (0, LANES - N_EXPERTS)))
    b_router = jnp.pad(moe_b_router[0].astype(F32), (0, LANES - N_EXPERTS)).reshape(1, LANES)
    h = _moe_ln(h, w_router, b_router, moe_w_gate[0].astype(F32), moe_w_up[0].astype(F32),
                moe_w_down[0].astype(F32), *ln(1, 1))
    return h.reshape(BATCH, SEQ, D_MODEL)
```

```python
import math

import jax
import jax.numpy as jnp
import numpy as np
from jax import lax
from jax.experimental import pallas as pl
from jax.experimental.pallas import tpu as pltpu

F32 = jnp.float32
BF16 = jnp.bfloat16

D_MODEL = 1024
BATCH = 16
SEQ = 2048
N_TOK = BATCH * SEQ
DEPTH = 2

SSM_GROUP_CH = 16
SSM_GROUPS = D_MODEL // SSM_GROUP_CH
SSM_STATE = 64
SSM_CHUNK = 16
SSM_NCHUNK = SEQ // SSM_CHUNK
SSM_WIDTH = SSM_CHUNK * SSM_GROUP_CH
SSM_SCAN_STEPS = int(math.log2(SSM_NCHUNK))

N_HEADS = 16
HEAD_DIM = 64
N_KV_HEADS = 4
Q_PER_KV = N_HEADS // N_KV_HEADS
KV_DIM = N_KV_HEADS * HEAD_DIM
WINDOW = 128
BLOCK = 128
ROT_DIM = HEAD_DIM // 4
ROT_HALF = ROT_DIM // 2
ROPE_THETA = 500000.0

FFN_DIM = 2816
N_EXPERTS = 8
EXPERT_DIM = 1024

DEEPNORM_ALPHA = (2 * DEPTH) ** 0.25
LOG2_E = math.log2(math.e)
LN_EPS = 1e-5

LANES = 128
VMEM_LIMIT = 56 * 1024 * 1024


def _cparams(*sem):
    return pltpu.CompilerParams(dimension_semantics=sem, vmem_limit_bytes=VMEM_LIMIT)


def _layer_norm(r, g, b):
    mu = jnp.mean(r, axis=-1, keepdims=True)
    xc = r - mu
    var = jnp.mean(xc * xc, axis=-1, keepdims=True)
    return xc * lax.rsqrt(var + LN_EPS) * g + b


def _rope_kernel(invf_ref, pos_ref, cs_ref):
    pos = pos_ref[...]
    for f in range(ROT_HALF):
        ang = pos * invf_ref[f]
        cs_ref[f] = jnp.cos(ang)
        cs_ref[ROT_HALF + f] = jnp.sin(ang)


def _rope_tables(positions):
    inv_freq = ROPE_THETA ** (-jnp.arange(0, ROT_DIM, 2, dtype=F32) / ROT_DIM)
    pos = positions.astype(F32)
    cs = pl.pallas_call(
        _rope_kernel,
        out_shape=jax.ShapeDtypeStruct((ROT_DIM, BATCH, SEQ), F32),
        in_specs=[pl.BlockSpec(memory_space=pltpu.SMEM),
                  pl.BlockSpec(memory_space=pltpu.VMEM)],
        out_specs=pl.BlockSpec(memory_space=pltpu.VMEM),
        name="rope_tables",
    )(inv_freq, pos)
    return cs.reshape(ROT_DIM, N_TOK).T


def _rope_placement():
    f = np.arange(ROT_HALF)
    place = np.zeros((ROT_DIM, LANES), np.float32)
    place[f, f] = place[f, ROT_HALF + f] = 1.0
    place[ROT_HALF + f, HEAD_DIM + f] = -1.0
    place[ROT_HALF + f, HEAD_DIM + ROT_HALF + f] = 1.0
    lane = np.arange(LANES)
    bias = ((lane >= ROT_DIM) & (lane < HEAD_DIM)).astype(np.float32).reshape(1, LANES)
    return jnp.asarray(place, BF16), jnp.asarray(bias)


def _ssm_params(lam_re, lam_im, log_step, b_re, b_im, c_re, c_im):
    hp = lax.Precision.HIGHEST
    lr, li = lam_re.astype(F32), lam_im.astype(F32)
    dt = jnp.exp(log_step.astype(F32))[:, None]
    mag = jnp.exp(lr * dt)
    ar = mag * jnp.cos(li * dt)
    ai = mag * jnp.sin(li * dt)
    nr = ar - 1.0
    den = lr * lr + li * li
    kr = (nr * lr + ai * li) / den
    ki = (ai * lr - nr * li) / den
    br, bi = b_re.astype(F32), b_im.astype(F32)
    bbar_r = kr[..., None] * br - ki[..., None] * bi
    bbar_i = kr[..., None] * bi + ki[..., None] * br
    cr, ci = c_re.astype(F32), c_im.astype(F32)

    def powers(taus):
        t = taus.astype(F32)[:, None, None]
        m = jnp.exp(lr[None] * dt[None] * t)
        ang = li[None] * dt[None] * t
        return m * jnp.cos(ang), m * jnp.sin(ang)

    er, ei = powers(jnp.arange(SSM_CHUNK + 1))
    w_r = er[:, :, :, None] * bbar_r[None] - ei[:, :, :, None] * bbar_i[None]
    w_i = er[:, :, :, None] * bbar_i[None] + ei[:, :, :, None] * bbar_r[None]
    kern = (jnp.einsum('gcp,tgpd->gtcd', cr, w_r[:SSM_CHUNK], precision=hp)
            - jnp.einsum('gcp,tgpd->gtcd', ci, w_i[:SSM_CHUNK], precision=hp))
    s_idx = jnp.arange(SSM_CHUNK)[:, None, None]
    t_idx = jnp.arange(SSM_CHUNK)[None, :, None]
    lag = (t_idx - s_idx == jnp.arange(SSM_CHUNK)[None, None, :]).astype(F32)
    toep = jnp.einsum('stu,gucd->gtcsd', lag, kern, precision=hp)
    mt = toep.reshape(SSM_GROUPS, SSM_WIDTH, SSM_WIDTH)
    rev = SSM_CHUNK - 1 - jnp.arange(SSM_CHUNK)
    g_r = w_r[rev].transpose(1, 2, 0, 3).reshape(SSM_GROUPS, SSM_STATE, SSM_WIDTH)
    g_i = w_i[rev].transpose(1, 2, 0, 3).reshape(SSM_GROUPS, SSM_STATE, SSM_WIDTH)
    gt = jnp.concatenate([g_r, g_i], axis=1)
    e1r, e1i = er[1:], ei[1:]
    ce_r = cr[None] * e1r[:, :, None, :] - ci[None] * e1i[:, :, None, :]
    ce_i = cr[None] * e1i[:, :, None, :] + ci[None] * e1r[:, :, None, :]
    c_re = ce_r.transpose(1, 0, 2, 3).reshape(SSM_GROUPS, SSM_WIDTH, SSM_STATE)
    c_im = (-ce_i).transpose(1, 0, 2, 3).reshape(SSM_GROUPS, SSM_WIDTH, SSM_STATE)
    ct = jnp.concatenate([c_re, c_im], axis=-1)
    sr, si = powers(SSM_CHUNK * (2 ** jnp.arange(8)))
    coef = lambda t: jnp.concatenate([t, t], axis=-1).transpose(1, 0, 2)
    return mt.astype(BF16), gt.astype(BF16), ct.astype(BF16), coef(sr), coef(si)


SSM_LANE_GROUPS = LANES // SSM_GROUP_CH
SSM_LANE_CHUNKS = D_MODEL // LANES
SSM_BATCH_TILE = 4
SSM_COLS = SSM_BATCH_TILE * SSM_NCHUNK


def _ssm_kernel(x_ref, mt_ref, gt_ref, ct_ref, cr_ref, ci_ref, y_ref, v_ref, yt_ref):
    for bl in range(SSM_BATCH_TILE):
        cols = slice(bl * SSM_NCHUNK, (bl + 1) * SSM_NCHUNK)
        for s in range(SSM_CHUNK):
            a = x_ref[pl.ds(bl * SEQ + s, SSM_NCHUNK, stride=SSM_CHUNK), :]
            at = a.T.astype(BF16)
            for g in range(SSM_LANE_GROUPS):
                v_ref[g, s * SSM_GROUP_CH:(s + 1) * SSM_GROUP_CH, cols] = (
                    at[g * SSM_GROUP_CH:(g + 1) * SSM_GROUP_CH])

    row_j = lax.broadcasted_iota(jnp.int32, (SSM_NCHUNK, 1), 0)

    def group_body(g, carry):
        ut = v_ref[g]
        xt = jnp.dot(gt_ref[g], ut, preferred_element_type=F32)
        cr_all = cr_ref[g]
        ci_all = ci_ref[g]

        def seg(rows, b):
            return xt[rows, b * SSM_NCHUNK:(b + 1) * SSM_NCHUNK]

        def shifted(x, d):
            if d < 8:
                return jnp.where(row_j >= d, pltpu.roll(x, d, 0), 0.0)
            return jnp.concatenate([jnp.zeros((d, LANES), F32), x[:SSM_NCHUNK - d]], axis=0)

        re_rows, im_rows = slice(0, SSM_STATE), slice(SSM_STATE, 2 * SSM_STATE)
        sr, si = [], []
        for q in range(SSM_BATCH_TILE // 2):
            xr = jnp.concatenate([seg(re_rows, 2 * q), seg(re_rows, 2 * q + 1)], axis=0).T
            xi = jnp.concatenate([seg(im_rows, 2 * q), seg(im_rows, 2 * q + 1)], axis=0).T
            for k in range(SSM_SCAN_STEPS):
                cr = cr_all[k:k + 1, :]
                ci = ci_all[k:k + 1, :]
                rr = shifted(xr, 1 << k)
                ri = shifted(xi, 1 << k)
                xr, xi = xr + cr * rr - ci * ri, xi + cr * ri + ci * rr
            pr = shifted(xr, 1).T
            pi = shifted(xi, 1).T
            sr += [pr[:SSM_STATE], pr[SSM_STATE:]]
            si += [pi[:SSM_STATE], pi[SSM_STATE:]]
        sprev = jnp.concatenate([jnp.concatenate(sr, axis=1), jnp.concatenate(si, axis=1)],
                                axis=0).astype(BF16)
        yt_ref[g] = (jnp.dot(mt_ref[g], ut, preferred_element_type=F32)
                     + jnp.dot(ct_ref[g], sprev, preferred_element_type=F32))
        return carry

    lax.fori_loop(0, SSM_LANE_GROUPS, group_body, 0, unroll=2)

    for bl in range(SSM_BATCH_TILE):
        cols = slice(bl * SSM_NCHUNK, (bl + 1) * SSM_NCHUNK)
        for t in range(SSM_CHUNK):
            tile = jnp.concatenate(
                [yt_ref[g, t * SSM_GROUP_CH:(t + 1) * SSM_GROUP_CH, cols] for g in range(SSM_LANE_GROUPS)],
                axis=0)
            y_ref[pl.ds(bl * SEQ + t, SSM_NCHUNK, stride=SSM_CHUNK), :] = tile.T


def _ssm_mixer(x2, params):
    mt, gt, ct, coef_r, coef_i = params
    rows = SSM_BATCH_TILE * SEQ
    pspec = lambda a, b: pl.BlockSpec((SSM_LANE_GROUPS, a, b), lambda k, q: (k, 0, 0))
    return pl.pallas_call(
        _ssm_kernel,
        out_shape=jax.ShapeDtypeStruct((N_TOK, D_MODEL), F32),
        grid=(SSM_LANE_CHUNKS, BATCH // SSM_BATCH_TILE),
        in_specs=[pl.BlockSpec((rows, LANES), lambda k, q: (q, k)),
                  pspec(SSM_WIDTH, SSM_WIDTH), pspec(2 * SSM_STATE, SSM_WIDTH), pspec(SSM_WIDTH, 2 * SSM_STATE),
                  pspec(8, 2 * SSM_STATE), pspec(8, 2 * SSM_STATE)],
        out_specs=pl.BlockSpec((rows, LANES), lambda k, q: (q, k)),
        scratch_shapes=[pltpu.VMEM((SSM_LANE_GROUPS, SSM_WIDTH, SSM_COLS), BF16),
                        pltpu.VMEM((SSM_LANE_GROUPS, SSM_WIDTH, SSM_COLS), F32)],
        compiler_params=_cparams("parallel", "parallel"),
        name="ssm_mixer",
    )(x2, mt, gt, ct, coef_r, coef_i)


GLU_COLS = 256


def _glu_ln_kernel(y_ref, x_ref, d_ref, w_ref, lg_ref, lb_ref, o_ref):
    x = x_ref[...]
    act = jax.nn.gelu(y_ref[...] + d_ref[...] * x).astype(BF16)
    parts = []
    for c in range(0, D_MODEL, GLU_COLS):
        val = jnp.dot(act, w_ref[:, c:c + GLU_COLS], preferred_element_type=F32)
        gate = jnp.dot(act, w_ref[:, D_MODEL + c:D_MODEL + c + GLU_COLS], preferred_element_type=F32)
        parts.append(val * jax.nn.sigmoid(gate))
    mix = jnp.concatenate(parts, axis=1)
    o_ref[...] = _layer_norm(DEEPNORM_ALPHA * x + mix, lg_ref[...], lb_ref[...])


def _row_spec(tm, width):
    return pl.BlockSpec((tm, width), lambda i: (i, 0))


def _const_spec(*shape):
    return pl.BlockSpec(shape, lambda *_: (0,) * len(shape))


def _glu_ln(y, x2, d_skip, w_glu, lg, lb, tm=512):
    return pl.pallas_call(
        _glu_ln_kernel,
        out_shape=jax.ShapeDtypeStruct((N_TOK, D_MODEL), F32),
        grid=(N_TOK // tm,),
        in_specs=[_row_spec(tm, D_MODEL), _row_spec(tm, D_MODEL), _const_spec(1, D_MODEL),
                  _const_spec(D_MODEL, 2 * D_MODEL), _const_spec(1, D_MODEL), _const_spec(1, D_MODEL)],
        out_specs=_row_spec(tm, D_MODEL),
        compiler_params=_cparams("parallel"),
        name="glu_ln",
    )(y, x2, d_skip, w_glu, lg, lb)


FFN_SPLIT = 2
FFN_TILE = FFN_DIM // FFN_SPLIT


def _ffn_ln_kernel(h_ref, wg_ref, wu_ref, wd_ref, lg_ref, lb_ref, o_ref):
    h = h_ref[...]
    hb = h.astype(BF16)
    ff = None
    for f in range(FFN_SPLIT):
        cols = slice(f * FFN_TILE, (f + 1) * FFN_TILE)
        gate = jnp.dot(hb, wg_ref[:, cols], preferred_element_type=F32)
        up = jnp.dot(hb, wu_ref[:, cols], preferred_element_type=F32)
        act = (jax.nn.silu(gate) * up).astype(BF16)
        part = jnp.dot(act, wd_ref[cols, :], preferred_element_type=F32)
        ff = part if ff is None else ff + part
    o_ref[...] = _layer_norm(DEEPNORM_ALPHA * h + ff, lg_ref[...], lb_ref[...])


def _resident_spec(*shape):
    return pl.BlockSpec(shape, lambda *_: (0,) * len(shape), pipeline_mode=pl.Buffered(1))


def _ffn_ln(h, wg, wu, wd, lg, lb, tm=512):
    return pl.pallas_call(
        _ffn_ln_kernel,
        out_shape=jax.ShapeDtypeStruct((N_TOK, D_MODEL), F32),
        grid=(N_TOK // tm,),
        in_specs=[_row_spec(tm, D_MODEL),
                  _resident_spec(D_MODEL, FFN_DIM), _resident_spec(D_MODEL, FFN_DIM),
                  _resident_spec(FFN_DIM, D_MODEL),
                  _const_spec(1, D_MODEL), _const_spec(1, D_MODEL)],
        out_specs=_row_spec(tm, D_MODEL),
        compiler_params=_cparams("parallel"),
        name="ffn_ln",
    )(h, wg, wu, wd, lg, lb)


Q_DIM = N_HEADS * HEAD_DIM


def _qkv_kernel(h_ref, w_ref, cs_ref, place_ref, bias_ref, q_ref, k_ref, v_ref):
    hb = h_ref[...].astype(BF16)
    z = jnp.dot(hb, w_ref[...], preferred_element_type=F32)
    lane = lax.broadcasted_iota(jnp.int32, (1, LANES), 1)
    table = bias_ref[...]
    for term in _split_bf16(cs_ref[...]):
        table = table + jnp.dot(term, place_ref[...], preferred_element_type=F32)
    swapped = pltpu.roll(table, HEAD_DIM, 1)
    cos_t = jnp.where(lane < HEAD_DIM, table, swapped)
    sin_t = jnp.where(lane < HEAD_DIM, swapped, table)
    first_half = (lane % HEAD_DIM) < ROT_HALF

    def rope(t):
        partner = jnp.where(first_half, pltpu.roll(t, LANES - ROT_HALF, 1), pltpu.roll(t, ROT_HALF, 1))
        return t * cos_t + partner * sin_t

    scale = HEAD_DIM ** -0.5 * LOG2_E
    for c in range(Q_DIM // LANES):
        sl = slice(c * LANES, (c + 1) * LANES)
        q_ref[:, sl] = (rope(z[:, sl]) * scale).astype(BF16)
    for c in range(KV_DIM // LANES):
        sl = slice(c * LANES, (c + 1) * LANES)
        k_ref[:, sl] = rope(z[:, Q_DIM + c * LANES:Q_DIM + (c + 1) * LANES]).astype(BF16)
    v_ref[...] = z[:, Q_DIM + KV_DIM:].astype(BF16)


def _qkv(h, w_qkv, rope_table, tm=512):
    return pl.pallas_call(
        _qkv_kernel,
        out_shape=[jax.ShapeDtypeStruct((N_TOK, Q_DIM), BF16),
                   jax.ShapeDtypeStruct((N_TOK, KV_DIM), BF16),
                   jax.ShapeDtypeStruct((N_TOK, KV_DIM), BF16)],
        grid=(N_TOK // tm,),
        in_specs=[_row_spec(tm, D_MODEL), _const_spec(D_MODEL, Q_DIM + 2 * KV_DIM),
                  _row_spec(tm, ROT_DIM), _const_spec(ROT_DIM, LANES), _const_spec(1, LANES)],
        out_specs=[_row_spec(tm, Q_DIM), _row_spec(tm, KV_DIM), _row_spec(tm, KV_DIM)],
        compiler_params=_cparams("parallel"),
        name="qkv_rope",
    )(h, w_qkv, rope_table, *_rope_placement())


ATTN_Q_TILE = 512
ATTN_SUB = ATTN_Q_TILE // BLOCK
ATTN_BAND = ATTN_Q_TILE + BLOCK
KV_PAIRS = N_KV_HEADS // 2

HEAD_ORDER = [h for c in range(KV_PAIRS) for g in range(Q_PER_KV)
              for h in (2 * c * Q_PER_KV + g, (2 * c + 1) * Q_PER_KV + g)]


def _band_bias():
    qi = jnp.arange(Q_PER_KV * BLOCK)[:, None] % BLOCK
    si = jnp.arange(2 * BLOCK)[None, :]
    rel = qi + BLOCK - si
    valid = (rel >= 0) & (rel < WINDOW)
    first = valid & (si >= BLOCK)
    neg = jnp.float32(-jnp.inf)
    return jnp.stack([jnp.where(valid, 0.0, neg), jnp.where(first, 0.0, neg)]).astype(F32)


def _attn_kernel(sink_ref, q_ref, kp_ref, kc_ref, vp_ref, vc_ref, bias_ref, h_ref, w_ref, lg_ref, lb_ref,
                 wr_ref, br_ref, tri_ref,
                 out_ref, meta_ref, counts_ref, dest_ref, xs_hbm,
                 ka_ref, kb_ref, va_ref, vb_ref, o_ref, *route_scratch):
    i = pl.program_id(1)
    lane = lax.broadcasted_iota(jnp.int32, (1, KV_DIM), 1)
    low = (lane % LANES) < HEAD_DIM
    zero = jnp.zeros((), BF16)
    for src, dst_a, dst_b in ((kp_ref, ka_ref, kb_ref), (vp_ref, va_ref, vb_ref)):
        t = src[...]
        dst_a[0:BLOCK, :] = jnp.where(low, t, zero)
        dst_b[0:BLOCK, :] = jnp.where(low, zero, t)
    for src, dst_a, dst_b in ((kc_ref, ka_ref, kb_ref), (vc_ref, va_ref, vb_ref)):
        t = src[...]
        dst_a[BLOCK:ATTN_BAND, :] = jnp.where(low, t, zero)
        dst_b[BLOCK:ATTN_BAND, :] = jnp.where(low, zero, t)

    lane1 = lax.broadcasted_iota(jnp.int32, (1, LANES), 1)
    row = lax.broadcasted_iota(jnp.int32, (Q_PER_KV * BLOCK, 1), 0)
    contract_last = (((1,), (1,)), ((), ()))
    for blk in range(ATTN_SUB):
        r0 = blk * BLOCK
        if blk == 0:
            bias = bias_ref[jnp.where(i == 0, 1, 0)]
        else:
            bias = bias_ref[0]
        for c in range(KV_PAIRS):
            cs = slice(c * LANES, (c + 1) * LANES)
            chunks = [c * Q_PER_KV + g for g in range(Q_PER_KV)]
            q4 = jnp.concatenate([q_ref[r0:r0 + BLOCK, m * LANES:(m + 1) * LANES] for m in chunks], axis=0)
            outs = []
            for half, (k_ref, v_ref) in enumerate(((ka_ref, va_ref), (kb_ref, vb_ref))):
                kband = k_ref[r0:r0 + 2 * BLOCK, cs]
                s = lax.dot_general(q4, kband, contract_last, preferred_element_type=F32) + bias
                sink = jnp.zeros((Q_PER_KV * BLOCK, 1), F32)
                for g, m in enumerate(chunks):
                    sink = jnp.where(row // BLOCK == g, sink_ref[HEAD_ORDER[2 * m + half]] * LOG2_E, sink)
                mx = jnp.maximum(jnp.max(s, axis=-1, keepdims=True), sink)
                p = jnp.exp2(s - mx)
                denom = jnp.sum(p, axis=-1, keepdims=True) + jnp.exp2(sink - mx)
                pv = jnp.dot(p.astype(BF16), v_ref[r0:r0 + 2 * BLOCK, cs], preferred_element_type=F32)
                outs.append((pv, 1.0 / denom))
            (pv_a, r_a), (pv_b, r_b) = outs
            o = (pv_a + pv_b) * jnp.where(lane1 < HEAD_DIM, r_a, r_b)
            for g, m in enumerate(chunks):
                o_ref[r0:r0 + BLOCK, m * LANES:(m + 1) * LANES] = o[g * BLOCK:(g + 1) * BLOCK].astype(BF16)

    mix = jnp.dot(o_ref[...], w_ref[...], preferred_element_type=F32)
    h_out = _layer_norm(DEEPNORM_ALPHA * h_ref[...] + mix, lg_ref[...], lb_ref[...])
    out_ref[...] = h_out
    _route_tile(pl.program_id(0) * ATTN_TILES + i, h_out, wr_ref, br_ref, tri_ref,
                meta_ref, counts_ref, dest_ref, xs_hbm, *route_scratch)


ATTN_TILES = SEQ // ATTN_Q_TILE


def _attention_ln_route(q, k, v, sinks, h, w_out, lg, lb, w_router, b_router):
    assert ATTN_Q_TILE == ROUTE_TILE
    tiles = ATTN_TILES
    cur = lambda b, i: (b * tiles + i, 0)
    prev = lambda b, i: (jnp.maximum((b * tiles + i) * ATTN_SUB - 1, 0), 0)
    const2 = lambda b, i: (0, 0)
    const3 = lambda b, i: (0, 0, 0)
    return pl.pallas_call(
        _attn_kernel,
        out_shape=[jax.ShapeDtypeStruct((N_TOK, D_MODEL), F32)] + ROUTE_OUT_SHAPES,
        grid=(BATCH, tiles),
        in_specs=[pl.BlockSpec(memory_space=pltpu.SMEM),
                  pl.BlockSpec((ATTN_Q_TILE, Q_DIM), cur),
                  pl.BlockSpec((BLOCK, KV_DIM), prev), pl.BlockSpec((ATTN_Q_TILE, KV_DIM), cur),
                  pl.BlockSpec((BLOCK, KV_DIM), prev), pl.BlockSpec((ATTN_Q_TILE, KV_DIM), cur),
                  pl.BlockSpec((2, Q_PER_KV * BLOCK, 2 * BLOCK), lambda b, i: (0, 0, 0)),
                  pl.BlockSpec((ATTN_Q_TILE, D_MODEL), cur),
                  pl.BlockSpec((Q_DIM, D_MODEL), const2),
                  pl.BlockSpec((1, D_MODEL), const2), pl.BlockSpec((1, D_MODEL), const2),
                  pl.BlockSpec((3, D_MODEL, LANES), const3), pl.BlockSpec((1, LANES), const2),
                  pl.BlockSpec((ROUTE_TILE, ROUTE_TILE), const2)],
        out_specs=[pl.BlockSpec((ATTN_Q_TILE, D_MODEL), cur),
                   pl.BlockSpec((ROUTE_TILE, LANES), cur), pl.BlockSpec((1, LANES), const2),
                   pl.BlockSpec((8, ROUTE_TILE), lambda b, i: (0, b * tiles + i)),
                   pl.BlockSpec(memory_space=pl.ANY)],
        scratch_shapes=[pltpu.VMEM((ATTN_BAND, KV_DIM), BF16)] * 4
                       + [pltpu.VMEM((ATTN_Q_TILE, Q_DIM), BF16)] + ROUTE_SCRATCH,
        compiler_params=_cparams("arbitrary", "arbitrary"),
        name="swa_attention_ln_route",
    )(sinks, q, k, k, v, v, _band_bias(), h, w_out, lg, lb, *_route_operands(w_router, b_router))


ROUTE_TILE = 512
EXPERT_TILE = 512
SLAB = D_MODEL // LANES
EXPERT_REGION = N_TOK
REGION_TILES = EXPERT_REGION // EXPERT_TILE
N_SORTED = N_EXPERTS * EXPERT_REGION
N_EXPERT_TILES = 2 * N_TOK // EXPERT_TILE + N_EXPERTS
META_E1, META_E2, META_R1, META_R2, META_W1, META_W2 = range(6)
META_D1, META_D2 = 8, 9
COMBINE_CHUNK = 64
N_ROUTE_TILES = N_TOK // ROUTE_TILE
ROUTER_BURST = 64


def _to_slabs(ref, val, rows):
    for c in range(SLAB):
        ref[pl.ds(c, rows, stride=SLAB), :] = val[:, c * LANES:(c + 1) * LANES]


def _from_slabs(ref, rows, first_row=0):
    return jnp.concatenate([ref[pl.ds(first_row * SLAB + c, rows, stride=SLAB), :] for c in range(SLAB)],
                           axis=1)


def _row_slab(ref, row):
    return ref.at[pl.ds(pl.multiple_of(row * SLAB, SLAB), SLAB)]


def _split_bf16(x):
    def top_bits(v):
        bits = lax.bitcast_convert_type(v, jnp.uint32) & jnp.uint32(0xFFFF0000)
        return lax.bitcast_convert_type(bits, F32)
    hi = top_bits(x)
    mid = top_bits(x - hi)
    lo = (x - hi) - mid
    return hi.astype(BF16), mid.astype(BF16), lo.astype(BF16)


def _route_tile(i, h, wr_ref, br_ref, tri_ref, meta_ref, counts_ref, dest_ref, xs_hbm,
                carry_ref, slab_a, slab_b, zero_ref, dest_v, dest_a, dest_b, count_v, count_s,
                sem, pad_sem, idx_sem):
    slabs = (slab_a, slab_b)
    dests = (dest_a, dest_b)

    @pl.when(i == 0)
    def _():
        carry_ref[...] = jnp.zeros_like(carry_ref)
        zero_ref[...] = jnp.zeros_like(zero_ref)

    def start_row(slot, t):
        for k in range(2):
            pltpu.make_async_copy(_row_slab(slabs[slot], t), _row_slab(xs_hbm, dests[slot][k, t]),
                                  sem.at[slot]).start(priority=k)

    def wait_scatter(slot):
        for _ in range(2):
            pltpu.make_async_copy(slabs[slot], xs_hbm.at[pl.ds(0, ROUTE_TILE * SLAB)], sem.at[slot]).wait()

    def route(slot, scatter_previous):
        pending = list(range(0, ROUTE_TILE, ROUTER_BURST)) if scatter_previous else []

        def burst():
            if pending:
                r0 = pending.pop(0)
                for t in range(r0, r0 + ROUTER_BURST):
                    start_row(1 - slot, t)

        lane = lax.broadcasted_iota(jnp.int32, (1, LANES), 1)
        lane_f = lane.astype(F32)
        h0, h1, h2 = _split_bf16(h)
        w0, w1, w2 = wr_ref[0], wr_ref[1], wr_ref[2]
        small = None
        for a, b in ((h2, w0), (h0, w2), (h1, w1)):
            part = jnp.dot(a, b, preferred_element_type=F32)
            small = part if small is None else small + part
            burst()
        mid = None
        for a, b in ((h1, w0), (h0, w1)):
            part = jnp.dot(a, b, preferred_element_type=F32)
            mid = part if mid is None else mid + part
            burst()
        logits = (small + mid) + jnp.dot(h0, w0, preferred_element_type=F32) + br_ref[...]
        burst()
        logits = jnp.where(lane < N_EXPERTS, logits, -jnp.inf)
        m1 = jnp.max(logits, axis=-1, keepdims=True)
        i1 = jnp.min(jnp.where(logits == m1, lane_f, float(LANES)), axis=-1, keepdims=True)
        rest = jnp.where(lane_f == i1, -jnp.inf, logits)
        m2 = jnp.max(rest, axis=-1, keepdims=True)
        i2 = jnp.min(jnp.where(rest == m2, lane_f, float(LANES)), axis=-1, keepdims=True)
        e2 = jnp.exp(m2 - m1)
        tot = 1.0 + e2
        burst()
        chosen = jnp.where((lane_f == i1) | (lane_f == i2), 1.0, 0.0)
        before = jnp.dot(tri_ref[...], chosen.astype(BF16), preferred_element_type=F32) + carry_ref[...]
        r1 = jnp.sum(jnp.where(lane_f == i1, before, 0.0), axis=-1, keepdims=True)
        r2 = jnp.sum(jnp.where(lane_f == i2, before, 0.0), axis=-1, keepdims=True)
        new_carry = carry_ref[...] + jnp.sum(chosen, axis=0, keepdims=True)
        carry_ref[...] = new_carry
        counts_ref[...] = new_carry.astype(jnp.int32)
        burst()
        record = jnp.zeros((ROUTE_TILE, LANES), F32)
        for lane_id, col in ((META_E1, i1), (META_E2, i2), (META_R1, r1), (META_R2, r2),
                             (META_W1, 1.0 / tot), (META_W2, e2 / tot),
                             (META_D1, i1 * EXPERT_REGION + r1), (META_D2, i2 * EXPERT_REGION + r2)):
            record = jnp.where(lane == lane_id, col, record)
        meta_ref[...] = record
        _to_slabs(slabs[slot], h, ROUTE_TILE)
        while pending:
            burst()
        dest_rows = record.T[META_D1:META_D1 + 8, :].astype(jnp.int32)
        dest_ref[...] = dest_rows
        dest_v[...] = dest_rows
        to_smem = pltpu.make_async_copy(dest_v, dests[slot], idx_sem)
        to_smem.start()
        to_smem.wait()

    @pl.when(i == 0)
    def _():
        route(0, False)

    for slot in range(2):
        @pl.when((i > 0) & (i % 2 == slot))
        def _():
            @pl.when(i >= 2)
            def _():
                wait_scatter(slot)
            route(slot, True)

    @pl.when(i == N_ROUTE_TILES - 1)
    def _():
        slot = (N_ROUTE_TILES - 1) % 2
        lax.fori_loop(0, ROUTE_TILE, lambda t, c: (start_row(slot, t), c)[1], 0, unroll=8)
        wait_scatter(1 - slot)
        wait_scatter(slot)
        count_v[...] = jnp.broadcast_to(carry_ref[...].astype(jnp.int32), count_v.shape)
        cp = pltpu.make_async_copy(count_v, count_s, idx_sem)
        cp.start()
        cp.wait()
        for e in range(N_EXPERTS):
            n = count_s[0, e]
            end = (n + EXPERT_TILE - 1) // EXPERT_TILE * EXPERT_TILE
            pad = lambda r: pltpu.make_async_copy(zero_ref, _row_slab(xs_hbm, e * EXPERT_REGION + r), pad_sem)
            lax.fori_loop(n, end, lambda r, c: (pad(r).start(), c)[1], 0)
            lax.fori_loop(n, end, lambda r, c: (pad(r).wait(), c)[1], 0)


def _route_operands(w_router, b_router):
    tri = (jnp.arange(ROUTE_TILE)[:, None] > jnp.arange(ROUTE_TILE)[None, :]).astype(BF16)
    return jnp.stack(_split_bf16(w_router)), b_router, tri


ROUTE_OUT_SHAPES = [jax.ShapeDtypeStruct((N_TOK, LANES), F32),
                    jax.ShapeDtypeStruct((1, LANES), jnp.int32),
                    jax.ShapeDtypeStruct((8, N_TOK), jnp.int32),
                    jax.ShapeDtypeStruct((N_SORTED * SLAB, LANES), F32)]
ROUTE_SCRATCH = [pltpu.VMEM((1, LANES), F32),
                 pltpu.VMEM((ROUTE_TILE * SLAB, LANES), F32), pltpu.VMEM((ROUTE_TILE * SLAB, LANES), F32),
                 pltpu.VMEM((SLAB, LANES), F32),
                 pltpu.VMEM((8, ROUTE_TILE), jnp.int32),
                 pltpu.SMEM((8, ROUTE_TILE), jnp.int32), pltpu.SMEM((8, ROUTE_TILE), jnp.int32),
                 pltpu.VMEM((8, LANES), jnp.int32), pltpu.SMEM((8, LANES), jnp.int32),
                 pltpu.SemaphoreType.DMA((2,)), pltpu.SemaphoreType.DMA, pltpu.SemaphoreType.DMA]


def _experts_kernel(tblk_ref, texp_ref, xs_ref, wg_ref, wu_ref, wd_ref, ys_ref, wb_ref):
    i = pl.program_id(0)

    @pl.when((i == 0) | (texp_ref[i] != texp_ref[jnp.maximum(i - 1, 0)]))
    def _():
        for j, w_ref in enumerate((wg_ref, wu_ref, wd_ref)):
            wb_ref[j] = w_ref[0].astype(BF16)

    @pl.when((i == 0) | (tblk_ref[i] != tblk_ref[jnp.maximum(i - 1, 0)]))
    def _():
        x = _from_slabs(xs_ref, EXPERT_TILE).astype(BF16)
        gate = jnp.dot(x, wb_ref[0], preferred_element_type=F32)
        up = jnp.dot(x, wb_ref[1], preferred_element_type=F32)
        act = (jax.nn.silu(gate) * up).astype(BF16)
        _to_slabs(ys_ref, jnp.dot(act, wb_ref[2], preferred_element_type=F32), EXPERT_TILE)


def _experts(tile_block, tile_expert, xs, wg, wu, wd):
    wspec = lambda a, b: pl.BlockSpec((1, a, b), lambda i, tblk, texp: (texp[i], 0, 0))
    rows = pl.BlockSpec((EXPERT_TILE * SLAB, LANES), lambda i, tblk, texp: (tblk[i], 0))
    return pl.pallas_call(
        _experts_kernel,
        out_shape=jax.ShapeDtypeStruct((N_SORTED * SLAB, LANES), F32),
        grid_spec=pltpu.PrefetchScalarGridSpec(
            num_scalar_prefetch=2,
            grid=(N_EXPERT_TILES,),
            in_specs=[rows, wspec(D_MODEL, EXPERT_DIM), wspec(D_MODEL, EXPERT_DIM),
                      wspec(EXPERT_DIM, D_MODEL)],
            out_specs=rows,
            scratch_shapes=[pltpu.VMEM((3, D_MODEL, EXPERT_DIM), BF16)]),
        compiler_params=_cparams("arbitrary"),
        name="moe_experts",
    )(tile_block, tile_expert, xs, wg, wu, wd)


def _combine_ln_kernel(dest_ref, meta_ref, h_ref, ys_hbm, lg_ref, lb_ref, o_ref, ya_ref, yb_ref, sem):
    i = pl.program_id(0)
    bufs = (ya_ref, yb_ref)

    def start_gather(tile, slot):
        def body(t, carry):
            for k in range(2):
                row = dest_ref[k * N_TOK + tile * ROUTE_TILE + t]
                pltpu.make_async_copy(_row_slab(ys_hbm, row), _row_slab(bufs[slot], k * ROUTE_TILE + t),
                                      sem.at[slot]).start(priority=k)
            return carry
        lax.fori_loop(0, ROUTE_TILE, body, 0, unroll=8)

    def wait_gather(slot):
        pltpu.make_async_copy(ys_hbm.at[pl.ds(0, 2 * ROUTE_TILE * SLAB)], bufs[slot], sem.at[slot]).wait()

    last = pl.num_programs(0) - 1
    lane = lax.broadcasted_iota(jnp.int32, (1, LANES), 1)

    def finish_and_prefetch(slot):
        nxt = jnp.minimum(i + 1, last) * ROUTE_TILE
        wait_gather(slot)
        for r0 in range(0, ROUTE_TILE, COMBINE_CHUNK):
            rows = slice(r0, r0 + COMBINE_CHUNK)
            meta = meta_ref[rows, :]
            w1 = jnp.sum(jnp.where(lane == META_W1, meta, 0.0), axis=-1, keepdims=True)
            w2 = jnp.sum(jnp.where(lane == META_W2, meta, 0.0), axis=-1, keepdims=True)
            ff = (w1 * _from_slabs(bufs[slot], COMBINE_CHUNK, first_row=r0)
                  + w2 * _from_slabs(bufs[slot], COMBINE_CHUNK, first_row=ROUTE_TILE + r0))
            o_ref[rows, :] = _layer_norm(DEEPNORM_ALPHA * h_ref[rows, :] + ff, lg_ref[...], lb_ref[...])
            for t in range(r0, r0 + COMBINE_CHUNK):
                for k in range(2):
                    row = dest_ref[k * N_TOK + nxt + t]
                    pltpu.make_async_copy(_row_slab(ys_hbm, row),
                                          _row_slab(bufs[1 - slot], k * ROUTE_TILE + t),
                                          sem.at[1 - slot]).start(priority=k)

        @pl.when(i == last)
        def _():
            wait_gather(1 - slot)

    @pl.when(i == 0)
    def _():
        start_gather(0, 0)

    for slot in range(2):
        @pl.when(i % 2 == slot)
        def _():
            finish_and_prefetch(slot)


def _combine_ln(dest, meta, h, ys, lg, lb):
    row = lambda width: pl.BlockSpec((ROUTE_TILE, width), lambda i, dest: (i, 0))
    const = pl.BlockSpec((1, D_MODEL), lambda i, dest: (0, 0))
    return pl.pallas_call(
        _combine_ln_kernel,
        out_shape=jax.ShapeDtypeStruct((N_TOK, D_MODEL), F32),
        grid_spec=pltpu.PrefetchScalarGridSpec(
            num_scalar_prefetch=1,
            grid=(N_TOK // ROUTE_TILE,),
            in_specs=[row(LANES), row(D_MODEL), pl.BlockSpec(memory_space=pl.ANY), const, const],
            out_specs=row(D_MODEL),
            scratch_shapes=[pltpu.VMEM((2 * ROUTE_TILE * SLAB, LANES), F32)] * 2
                           + [pltpu.SemaphoreType.DMA((2,))]),
        compiler_params=_cparams("arbitrary"),
        name="moe_combine_ln",
    )(dest, meta, h, ys, lg, lb)


def _moe_ln(h, meta, counts, dest_rows, xs, wg, wu, wd, lg, lb):
    expert = jnp.arange(N_EXPERTS)
    tiles = (counts[0, :N_EXPERTS] + EXPERT_TILE - 1) // EXPERT_TILE
    ends = jnp.sum(jnp.where(expert[None, :] <= expert[:, None], tiles[None, :], 0), axis=1)
    step = jnp.minimum(jnp.arange(N_EXPERT_TILES), ends[-1] - 1)
    tile_expert = jnp.sum(step[:, None] >= ends[None, :], axis=1)
    first = jnp.sum(jnp.where(tile_expert[:, None] == expert[None, :], (ends - tiles)[None, :], 0), axis=1)
    tile_block = tile_expert * REGION_TILES + (step - first)
    ys = _experts(tile_block.astype(jnp.int32), tile_expert.astype(jnp.int32), xs, wg, wu, wd)
    dest = dest_rows[:2].reshape(2 * N_TOK)
    return _combine_ln(dest, meta, h, ys, lg, lb)


def kernel(x, positions, ln_g, ln_b, ssm_lambda_re, ssm_lambda_im, ssm_log_step, ssm_b_re, ssm_b_im, ssm_c_re, ssm_c_im, ssm_d, ssm_w_glu, kv_w, attn_w_q, attn_sinks, attn_w_out, ffn_w_gate, ffn_w_up, ffn_w_down, moe_w_router, moe_b_router, moe_w_gate, moe_w_up, moe_w_down):
    ln = lambda layer, j: (ln_g[layer, j].reshape(1, D_MODEL).astype(F32),
                           ln_b[layer, j].reshape(1, D_MODEL).astype(F32))
    rope_table = _rope_tables(positions)
    x2 = x.reshape(N_TOK, D_MODEL)

    params = _ssm_params(ssm_lambda_re[0], ssm_lambda_im[0], ssm_log_step[0], ssm_b_re[0], ssm_b_im[0],
                         ssm_c_re[0], ssm_c_im[0])
    y = _ssm_mixer(x2, params)
    h = _glu_ln(y, x2, ssm_d[0].astype(F32).reshape(1, D_MODEL), ssm_w_glu[0].astype(BF16), *ln(0, 0))
    h = _ffn_ln(h, ffn_w_gate[0].astype(BF16), ffn_w_up[0].astype(BF16), ffn_w_down[0].astype(BF16),
                *ln(0, 1))

    order = jnp.array(HEAD_ORDER)
    w_q = attn_w_q[0].reshape(D_MODEL, N_HEADS, HEAD_DIM)[:, order].reshape(D_MODEL, Q_DIM)
    w_out = attn_w_out[0].reshape(N_HEADS, HEAD_DIM, D_MODEL)[order].reshape(Q_DIM, D_MODEL)
    w_qkv = jnp.concatenate([w_q, kv_w], axis=1).astype(BF16)
    q, k, v = _qkv(h, w_qkv, rope_table)
    w_router = jnp.pad(moe_w_router[0].astype(F32), ((0, 0), (0, LANES - N_EXPERTS)))
    b_router = jnp.pad(moe_b_router[0].astype(F32), (0, LANES - N_EXPERTS)).reshape(1, LANES)
    h, *routing = _attention_ln_route(q, k, v, attn_sinks[0].astype(F32), h, w_out.astype(BF16), *ln(1, 0),
                                      w_router, b_router)
    h = _moe_ln(h, *routing, moe_w_gate[0].astype(F32), moe_w_up[0].astype(F32),
                moe_w_down[0].astype(F32), *ln(1, 1))
    return h.reshape(BATCH, SEQ, D_MODEL)
```

```python
import math

import jax
import jax.numpy as jnp
import numpy as np
from jax import lax
from jax.experimental import pallas as pl
from jax.experimental.pallas import tpu as pltpu

F32 = jnp.float32
BF16 = jnp.bfloat16

D_MODEL = 1024
BATCH = 16
SEQ = 2048
N_TOK = BATCH * SEQ
DEPTH = 2

SSM_GROUP_CH = 16
SSM_GROUPS = D_MODEL // SSM_GROUP_CH
SSM_STATE = 64
SSM_CHUNK = 16
SSM_NCHUNK = SEQ // SSM_CHUNK
SSM_WIDTH = SSM_CHUNK * SSM_GROUP_CH
SSM_SCAN_STEPS = int(math.log2(SSM_NCHUNK))

N_HEADS = 16
HEAD_DIM = 64
N_KV_HEADS = 4
Q_PER_KV = N_HEADS // N_KV_HEADS
KV_DIM = N_KV_HEADS * HEAD_DIM
WINDOW = 128
BLOCK = 128
ROT_DIM = HEAD_DIM // 4
ROT_HALF = ROT_DIM // 2
ROPE_THETA = 500000.0

FFN_DIM = 2816
N_EXPERTS = 8
EXPERT_DIM = 1024

DEEPNORM_ALPHA = (2 * DEPTH) ** 0.25
LOG2_E = math.log2(math.e)
LN_EPS = 1e-5

LANES = 128
VMEM_LIMIT = 56 * 1024 * 1024


def _cparams(*sem):
    return pltpu.CompilerParams(dimension_semantics=sem, vmem_limit_bytes=VMEM_LIMIT)


def _layer_norm(r, g, b):
    mu = jnp.mean(r, axis=-1, keepdims=True)
    xc = r - mu
    var = jnp.mean(xc * xc, axis=-1, keepdims=True)
    return xc * lax.rsqrt(var + LN_EPS) * g + b


def _rope_kernel(invf_ref, pos_ref, cs_ref):
    pos = pos_ref[...]
    for f in range(ROT_HALF):
        ang = pos * invf_ref[f]
        cs_ref[f] = jnp.cos(ang)
        cs_ref[ROT_HALF + f] = jnp.sin(ang)


def _rope_tables(positions):
    inv_freq = ROPE_THETA ** (-jnp.arange(0, ROT_DIM, 2, dtype=F32) / ROT_DIM)
    pos = positions.astype(F32)
    cs = pl.pallas_call(
        _rope_kernel,
        out_shape=jax.ShapeDtypeStruct((ROT_DIM, BATCH, SEQ), F32),
        in_specs=[pl.BlockSpec(memory_space=pltpu.SMEM),
                  pl.BlockSpec(memory_space=pltpu.VMEM)],
        out_specs=pl.BlockSpec(memory_space=pltpu.VMEM),
        name="rope_tables",
    )(inv_freq, pos)
    return cs.reshape(ROT_DIM, N_TOK).T


def _rope_placement():
    f = np.arange(ROT_HALF)
    place = np.zeros((ROT_DIM, LANES), np.float32)
    place[f, f] = place[f, ROT_HALF + f] = 1.0
    place[ROT_HALF + f, HEAD_DIM + f] = -1.0
    place[ROT_HALF + f, HEAD_DIM + ROT_HALF + f] = 1.0
    lane = np.arange(LANES)
    bias = ((lane >= ROT_DIM) & (lane < HEAD_DIM)).astype(np.float32).reshape(1, LANES)
    return jnp.asarray(place, BF16), jnp.asarray(bias)


def _ssm_params(lam_re, lam_im, log_step, b_re, b_im, c_re, c_im):
    hp = lax.Precision.HIGHEST
    lr, li = lam_re.astype(F32), lam_im.astype(F32)
    dt = jnp.exp(log_step.astype(F32))[:, None]
    mag = jnp.exp(lr * dt)
    ar = mag * jnp.cos(li * dt)
    ai = mag * jnp.sin(li * dt)
    nr = ar - 1.0
    den = lr * lr + li * li
    kr = (nr * lr + ai * li) / den
    ki = (ai * lr - nr * li) / den
    br, bi = b_re.astype(F32), b_im.astype(F32)
    bbar_r = kr[..., None] * br - ki[..., None] * bi
    bbar_i = kr[..., None] * bi + ki[..., None] * br
    cr, ci = c_re.astype(F32), c_im.astype(F32)

    def powers(taus):
        t = taus.astype(F32)[:, None, None]
        m = jnp.exp(lr[None] * dt[None] * t)
        ang = li[None] * dt[None] * t
        return m * jnp.cos(ang), m * jnp.sin(ang)

    er, ei = powers(jnp.arange(SSM_CHUNK + 1))
    w_r = er[:, :, :, None] * bbar_r[None] - ei[:, :, :, None] * bbar_i[None]
    w_i = er[:, :, :, None] * bbar_i[None] + ei[:, :, :, None] * bbar_r[None]
    kern = (jnp.einsum('gcp,tgpd->gtcd', cr, w_r[:SSM_CHUNK], precision=hp)
            - jnp.einsum('gcp,tgpd->gtcd', ci, w_i[:SSM_CHUNK], precision=hp))
    s_idx = jnp.arange(SSM_CHUNK)[:, None, None]
    t_idx = jnp.arange(SSM_CHUNK)[None, :, None]
    lag = (t_idx - s_idx == jnp.arange(SSM_CHUNK)[None, None, :]).astype(F32)
    toep = jnp.einsum('stu,gucd->gtcsd', lag, kern, precision=hp)
    mt = toep.reshape(SSM_GROUPS, SSM_WIDTH, SSM_WIDTH)
    rev = SSM_CHUNK - 1 - jnp.arange(SSM_CHUNK)
    g_r = w_r[rev].transpose(1, 2, 0, 3).reshape(SSM_GROUPS, SSM_STATE, SSM_WIDTH)
    g_i = w_i[rev].transpose(1, 2, 0, 3).reshape(SSM_GROUPS, SSM_STATE, SSM_WIDTH)
    gt = jnp.concatenate([g_r, g_i], axis=1)
    e1r, e1i = er[1:], ei[1:]
    ce_r = cr[None] * e1r[:, :, None, :] - ci[None] * e1i[:, :, None, :]
    ce_i = cr[None] * e1i[:, :, None, :] + ci[None] * e1r[:, :, None, :]
    c_re = ce_r.transpose(1, 0, 2, 3).reshape(SSM_GROUPS, SSM_WIDTH, SSM_STATE)
    c_im = (-ce_i).transpose(1, 0, 2, 3).reshape(SSM_GROUPS, SSM_WIDTH, SSM_STATE)
    ct = jnp.concatenate([c_re, c_im], axis=-1)
    sr, si = powers(SSM_CHUNK * (2 ** jnp.arange(8)))
    coef = lambda t: jnp.concatenate([t, t], axis=-1).transpose(1, 0, 2)
    return mt.astype(BF16), gt.astype(BF16), ct.astype(BF16), coef(sr), coef(si)


SSM_LANE_GROUPS = LANES // SSM_GROUP_CH
SSM_LANE_CHUNKS = D_MODEL // LANES
SSM_BATCH_TILE = 4
SSM_COLS = SSM_BATCH_TILE * SSM_NCHUNK


def _ssm_kernel(x_ref, mt_ref, gt_ref, ct_ref, cr_ref, ci_ref, y_ref, v_ref, yt_ref):
    for bl in range(SSM_BATCH_TILE):
        cols = slice(bl * SSM_NCHUNK, (bl + 1) * SSM_NCHUNK)
        for s in range(SSM_CHUNK):
            a = x_ref[pl.ds(bl * SEQ + s, SSM_NCHUNK, stride=SSM_CHUNK), :]
            at = a.T.astype(BF16)
            for g in range(SSM_LANE_GROUPS):
                v_ref[g, s * SSM_GROUP_CH:(s + 1) * SSM_GROUP_CH, cols] = (
                    at[g * SSM_GROUP_CH:(g + 1) * SSM_GROUP_CH])

    row_j = lax.broadcasted_iota(jnp.int32, (SSM_NCHUNK, 1), 0)

    def group_body(g, carry):
        ut = v_ref[g]
        xt = jnp.dot(gt_ref[g], ut, preferred_element_type=F32)
        cr_all = cr_ref[g]
        ci_all = ci_ref[g]

        def seg(rows, b):
            return xt[rows, b * SSM_NCHUNK:(b + 1) * SSM_NCHUNK]

        def shifted(x, d):
            if d < 8:
                return jnp.where(row_j >= d, pltpu.roll(x, d, 0), 0.0)
            return jnp.concatenate([jnp.zeros((d, LANES), F32), x[:SSM_NCHUNK - d]], axis=0)

        re_rows, im_rows = slice(0, SSM_STATE), slice(SSM_STATE, 2 * SSM_STATE)
        sr, si = [], []
        for q in range(SSM_BATCH_TILE // 2):
            xr = jnp.concatenate([seg(re_rows, 2 * q), seg(re_rows, 2 * q + 1)], axis=0).T
            xi = jnp.concatenate([seg(im_rows, 2 * q), seg(im_rows, 2 * q + 1)], axis=0).T
            for k in range(SSM_SCAN_STEPS):
                cr = cr_all[k:k + 1, :]
                ci = ci_all[k:k + 1, :]
                rr = shifted(xr, 1 << k)
                ri = shifted(xi, 1 << k)
                xr, xi = xr + cr * rr - ci * ri, xi + cr * ri + ci * rr
            pr = shifted(xr, 1).T
            pi = shifted(xi, 1).T
            sr += [pr[:SSM_STATE], pr[SSM_STATE:]]
            si += [pi[:SSM_STATE], pi[SSM_STATE:]]
        sprev = jnp.concatenate([jnp.concatenate(sr, axis=1), jnp.concatenate(si, axis=1)],
                                axis=0).astype(BF16)
        yt_ref[g] = (jnp.dot(mt_ref[g], ut, preferred_element_type=F32)
                     + jnp.dot(ct_ref[g], sprev, preferred_element_type=F32))
        return carry

    lax.fori_loop(0, SSM_LANE_GROUPS, group_body, 0, unroll=2)

    for bl in range(SSM_BATCH_TILE):
        cols = slice(bl * SSM_NCHUNK, (bl + 1) * SSM_NCHUNK)
        for t in range(SSM_CHUNK):
            tile = jnp.concatenate(
                [yt_ref[g, t * SSM_GROUP_CH:(t + 1) * SSM_GROUP_CH, cols] for g in range(SSM_LANE_GROUPS)],
                axis=0)
            y_ref[pl.ds(bl * SEQ + t, SSM_NCHUNK, stride=SSM_CHUNK), :] = tile.T


def _ssm_mixer(x2, params):
    mt, gt, ct, coef_r, coef_i = params
    rows = SSM_BATCH_TILE * SEQ
    pspec = lambda a, b: pl.BlockSpec((SSM_LANE_GROUPS, a, b), lambda k, q: (k, 0, 0))
    return pl.pallas_call(
        _ssm_kernel,
        out_shape=jax.ShapeDtypeStruct((N_TOK, D_MODEL), F32),
        grid=(SSM_LANE_CHUNKS, BATCH // SSM_BATCH_TILE),
        in_specs=[pl.BlockSpec((rows, LANES), lambda k, q: (q, k)),
                  pspec(SSM_WIDTH, SSM_WIDTH), pspec(2 * SSM_STATE, SSM_WIDTH), pspec(SSM_WIDTH, 2 * SSM_STATE),
                  pspec(8, 2 * SSM_STATE), pspec(8, 2 * SSM_STATE)],
        out_specs=pl.BlockSpec((rows, LANES), lambda k, q: (q, k)),
        scratch_shapes=[pltpu.VMEM((SSM_LANE_GROUPS, SSM_WIDTH, SSM_COLS), BF16),
                        pltpu.VMEM((SSM_LANE_GROUPS, SSM_WIDTH, SSM_COLS), F32)],
        compiler_params=_cparams("parallel", "parallel"),
        name="ssm_mixer",
    )(x2, mt, gt, ct, coef_r, coef_i)


GLU_COLS = 256


def _glu_ln_kernel(y_ref, x_ref, d_ref, w_ref, lg_ref, lb_ref, o_ref):
    x = x_ref[...]
    act = jax.nn.gelu(y_ref[...] + d_ref[...] * x).astype(BF16)
    parts = []
    for c in range(0, D_MODEL, GLU_COLS):
        val = jnp.dot(act, w_ref[:, c:c + GLU_COLS], preferred_element_type=F32)
        gate = jnp.dot(act, w_ref[:, D_MODEL + c:D_MODEL + c + GLU_COLS], preferred_element_type=F32)
        parts.append(val * jax.nn.sigmoid(gate))
    mix = jnp.concatenate(parts, axis=1)
    o_ref[...] = _layer_norm(DEEPNORM_ALPHA * x + mix, lg_ref[...], lb_ref[...])


def _row_spec(tm, width):
    return pl.BlockSpec((tm, width), lambda i: (i, 0))


def _const_spec(*shape):
    return pl.BlockSpec(shape, lambda *_: (0,) * len(shape))


def _glu_ln(y, x2, d_skip, w_glu, lg, lb, tm=512):
    return pl.pallas_call(
        _glu_ln_kernel,
        out_shape=jax.ShapeDtypeStruct((N_TOK, D_MODEL), F32),
        grid=(N_TOK // tm,),
        in_specs=[_row_spec(tm, D_MODEL), _row_spec(tm, D_MODEL), _const_spec(1, D_MODEL),
                  _const_spec(D_MODEL, 2 * D_MODEL), _const_spec(1, D_MODEL), _const_spec(1, D_MODEL)],
        out_specs=_row_spec(tm, D_MODEL),
        compiler_params=_cparams("parallel"),
        name="glu_ln",
    )(y, x2, d_skip, w_glu, lg, lb)


FFN_SPLIT = 2
FFN_TILE = FFN_DIM // FFN_SPLIT


def _ffn_ln_kernel(h_ref, wg_ref, wu_ref, wd_ref, lg_ref, lb_ref, o_ref):
    h = h_ref[...]
    hb = h.astype(BF16)
    ff = None
    for f in range(FFN_SPLIT):
        cols = slice(f * FFN_TILE, (f + 1) * FFN_TILE)
        gate = jnp.dot(hb, wg_ref[:, cols], preferred_element_type=F32)
        up = jnp.dot(hb, wu_ref[:, cols], preferred_element_type=F32)
        act = (jax.nn.silu(gate) * up).astype(BF16)
        part = jnp.dot(act, wd_ref[cols, :], preferred_element_type=F32)
        ff = part if ff is None else ff + part
    o_ref[...] = _layer_norm(DEEPNORM_ALPHA * h + ff, lg_ref[...], lb_ref[...])


def _resident_spec(*shape):
    return pl.BlockSpec(shape, lambda *_: (0,) * len(shape), pipeline_mode=pl.Buffered(1))


def _ffn_ln(h, wg, wu, wd, lg, lb, tm=512):
    return pl.pallas_call(
        _ffn_ln_kernel,
        out_shape=jax.ShapeDtypeStruct((N_TOK, D_MODEL), F32),
        grid=(N_TOK // tm,),
        in_specs=[_row_spec(tm, D_MODEL),
                  _resident_spec(D_MODEL, FFN_DIM), _resident_spec(D_MODEL, FFN_DIM),
                  _resident_spec(FFN_DIM, D_MODEL),
                  _const_spec(1, D_MODEL), _const_spec(1, D_MODEL)],
        out_specs=_row_spec(tm, D_MODEL),
        compiler_params=_cparams("parallel"),
        name="ffn_ln",
    )(h, wg, wu, wd, lg, lb)


Q_DIM = N_HEADS * HEAD_DIM


def _qkv_kernel(h_ref, w_ref, cs_ref, place_ref, bias_ref, q_ref, k_ref, v_ref):
    hb = h_ref[...].astype(BF16)
    z = jnp.dot(hb, w_ref[...], preferred_element_type=F32)
    lane = lax.broadcasted_iota(jnp.int32, (1, LANES), 1)
    table = bias_ref[...]
    for term in _split_bf16(cs_ref[...]):
        table = table + jnp.dot(term, place_ref[...], preferred_element_type=F32)
    swapped = pltpu.roll(table, HEAD_DIM, 1)
    cos_t = jnp.where(lane < HEAD_DIM, table, swapped)
    sin_t = jnp.where(lane < HEAD_DIM, swapped, table)
    first_half = (lane % HEAD_DIM) < ROT_HALF

    def rope(t):
        partner = jnp.where(first_half, pltpu.roll(t, LANES - ROT_HALF, 1), pltpu.roll(t, ROT_HALF, 1))
        return t * cos_t + partner * sin_t

    scale = HEAD_DIM ** -0.5 * LOG2_E
    for c in range(Q_DIM // LANES):
        sl = slice(c * LANES, (c + 1) * LANES)
        q_ref[:, sl] = (rope(z[:, sl]) * scale).astype(BF16)
    for c in range(KV_DIM // LANES):
        sl = slice(c * LANES, (c + 1) * LANES)
        k_ref[:, sl] = rope(z[:, Q_DIM + c * LANES:Q_DIM + (c + 1) * LANES]).astype(BF16)
    v_ref[...] = z[:, Q_DIM + KV_DIM:].astype(BF16)


def _qkv(h, w_qkv, rope_table, tm=512):
    return pl.pallas_call(
        _qkv_kernel,
        out_shape=[jax.ShapeDtypeStruct((N_TOK, Q_DIM), BF16),
                   jax.ShapeDtypeStruct((N_TOK, KV_DIM), BF16),
                   jax.ShapeDtypeStruct((N_TOK, KV_DIM), BF16)],
        grid=(N_TOK // tm,),
        in_specs=[_row_spec(tm, D_MODEL), _const_spec(D_MODEL, Q_DIM + 2 * KV_DIM),
                  _row_spec(tm, ROT_DIM), _const_spec(ROT_DIM, LANES), _const_spec(1, LANES)],
        out_specs=[_row_spec(tm, Q_DIM), _row_spec(tm, KV_DIM), _row_spec(tm, KV_DIM)],
        compiler_params=_cparams("parallel"),
        name="qkv_rope",
    )(h, w_qkv, rope_table, *_rope_placement())


ATTN_Q_TILE = 512
ATTN_SUB = ATTN_Q_TILE // BLOCK
ATTN_BAND = ATTN_Q_TILE + BLOCK
KV_PAIRS = N_KV_HEADS // 2

HEAD_ORDER = [h for c in range(KV_PAIRS) for g in range(Q_PER_KV)
              for h in (2 * c * Q_PER_KV + g, (2 * c + 1) * Q_PER_KV + g)]


def _band_bias():
    qi = jnp.arange(Q_PER_KV * BLOCK)[:, None] % BLOCK
    si = jnp.arange(2 * BLOCK)[None, :]
    rel = qi + BLOCK - si
    valid = (rel >= 0) & (rel < WINDOW)
    first = valid & (si >= BLOCK)
    neg = jnp.float32(-jnp.inf)
    return jnp.stack([jnp.where(valid, 0.0, neg), jnp.where(first, 0.0, neg)]).astype(F32)


def _attn_kernel(sink_ref, q_ref, kp_ref, kc_ref, vp_ref, vc_ref, bias_ref, h_ref, w_ref, lg_ref, lb_ref,
                 wr_ref, br_ref, tri_ref,
                 out_ref, meta_ref, counts_ref, dest_ref, xs_hbm,
                 ka_ref, kb_ref, va_ref, vb_ref, o_ref, *route_scratch):
    i = pl.program_id(1)
    lane = lax.broadcasted_iota(jnp.int32, (1, KV_DIM), 1)
    low = (lane % LANES) < HEAD_DIM
    zero = jnp.zeros((), BF16)
    for src, dst_a, dst_b in ((kp_ref, ka_ref, kb_ref), (vp_ref, va_ref, vb_ref)):
        t = src[...]
        dst_a[0:BLOCK, :] = jnp.where(low, t, zero)
        dst_b[0:BLOCK, :] = jnp.where(low, zero, t)
    for src, dst_a, dst_b in ((kc_ref, ka_ref, kb_ref), (vc_ref, va_ref, vb_ref)):
        t = src[...]
        dst_a[BLOCK:ATTN_BAND, :] = jnp.where(low, t, zero)
        dst_b[BLOCK:ATTN_BAND, :] = jnp.where(low, zero, t)

    lane1 = lax.broadcasted_iota(jnp.int32, (1, LANES), 1)
    row = lax.broadcasted_iota(jnp.int32, (Q_PER_KV * BLOCK, 1), 0)
    contract_last = (((1,), (1,)), ((), ()))
    for blk in range(ATTN_SUB):
        r0 = blk * BLOCK
        if blk == 0:
            bias = bias_ref[jnp.where(i == 0, 1, 0)]
        else:
            bias = bias_ref[0]
        for c in range(KV_PAIRS):
            cs = slice(c * LANES, (c + 1) * LANES)
            chunks = [c * Q_PER_KV + g for g in range(Q_PER_KV)]
            q4 = jnp.concatenate([q_ref[r0:r0 + BLOCK, m * LANES:(m + 1) * LANES] for m in chunks], axis=0)
            outs = []
            for half, (k_ref, v_ref) in enumerate(((ka_ref, va_ref), (kb_ref, vb_ref))):
                kband = k_ref[r0:r0 + 2 * BLOCK, cs]
                s = lax.dot_general(q4, kband, contract_last, preferred_element_type=F32) + bias
                sink = jnp.zeros((Q_PER_KV * BLOCK, 1), F32)
                for g, m in enumerate(chunks):
                    sink = jnp.where(row // BLOCK == g, sink_ref[HEAD_ORDER[2 * m + half]] * LOG2_E, sink)
                mx = jnp.maximum(jnp.max(s, axis=-1, keepdims=True), sink)
                p = jnp.exp2(s - mx)
                denom = jnp.sum(p, axis=-1, keepdims=True) + jnp.exp2(sink - mx)
                pv = jnp.dot(p.astype(BF16), v_ref[r0:r0 + 2 * BLOCK, cs], preferred_element_type=F32)
                outs.append((pv, 1.0 / denom))
            (pv_a, r_a), (pv_b, r_b) = outs
            o = (pv_a + pv_b) * jnp.where(lane1 < HEAD_DIM, r_a, r_b)
            for g, m in enumerate(chunks):
                o_ref[r0:r0 + BLOCK, m * LANES:(m + 1) * LANES] = o[g * BLOCK:(g + 1) * BLOCK].astype(BF16)

    mix = jnp.dot(o_ref[...], w_ref[...], preferred_element_type=F32)
    h_out = _layer_norm(DEEPNORM_ALPHA * h_ref[...] + mix, lg_ref[...], lb_ref[...])
    out_ref[...] = h_out
    _route_tile(pl.program_id(0) * ATTN_TILES + i, h_out, wr_ref, br_ref, tri_ref,
                meta_ref, counts_ref, dest_ref, xs_hbm, *route_scratch)


ATTN_TILES = SEQ // ATTN_Q_TILE


def _attention_ln_route(q, k, v, sinks, h, w_out, lg, lb, w_router, b_router):
    assert ATTN_Q_TILE == ROUTE_TILE
    tiles = ATTN_TILES
    cur = lambda b, i: (b * tiles + i, 0)
    prev = lambda b, i: (jnp.maximum((b * tiles + i) * ATTN_SUB - 1, 0), 0)
    const2 = lambda b, i: (0, 0)
    const3 = lambda b, i: (0, 0, 0)
    return pl.pallas_call(
        _attn_kernel,
        out_shape=[jax.ShapeDtypeStruct((N_TOK, D_MODEL), F32)] + ROUTE_OUT_SHAPES,
        grid=(BATCH, tiles),
        in_specs=[pl.BlockSpec(memory_space=pltpu.SMEM),
                  pl.BlockSpec((ATTN_Q_TILE, Q_DIM), cur),
                  pl.BlockSpec((BLOCK, KV_DIM), prev), pl.BlockSpec((ATTN_Q_TILE, KV_DIM), cur),
                  pl.BlockSpec((BLOCK, KV_DIM), prev), pl.BlockSpec((ATTN_Q_TILE, KV_DIM), cur),
                  pl.BlockSpec((2, Q_PER_KV * BLOCK, 2 * BLOCK), lambda b, i: (0, 0, 0)),
                  pl.BlockSpec((ATTN_Q_TILE, D_MODEL), cur),
                  pl.BlockSpec((Q_DIM, D_MODEL), const2),
                  pl.BlockSpec((1, D_MODEL), const2), pl.BlockSpec((1, D_MODEL), const2),
                  pl.BlockSpec((3, D_MODEL, LANES), const3), pl.BlockSpec((1, LANES), const2),
                  pl.BlockSpec((ROUTE_TILE, ROUTE_TILE), const2)],
        out_specs=[pl.BlockSpec((ATTN_Q_TILE, D_MODEL), cur),
                   pl.BlockSpec((ROUTE_TILE, LANES), cur), pl.BlockSpec((1, LANES), const2),
                   pl.BlockSpec((8, ROUTE_TILE), lambda b, i: (0, b * tiles + i)),
                   pl.BlockSpec(memory_space=pl.ANY)],
        scratch_shapes=[pltpu.VMEM((ATTN_BAND, KV_DIM), BF16)] * 4
                       + [pltpu.VMEM((ATTN_Q_TILE, Q_DIM), BF16)] + ROUTE_SCRATCH,
        compiler_params=_cparams("arbitrary", "arbitrary"),
        name="swa_attention_ln_route",
    )(sinks, q, k, k, v, v, _band_bias(), h, w_out, lg, lb, *_route_operands(w_router, b_router))


ROUTE_TILE = 512
EXPERT_TILE = 512
SLAB = D_MODEL // LANES
EXPERT_REGION = N_TOK
REGION_TILES = EXPERT_REGION // EXPERT_TILE
N_SORTED = N_EXPERTS * EXPERT_REGION
N_EXPERT_TILES = 2 * N_TOK // EXPERT_TILE + N_EXPERTS
META_E1, META_E2, META_R1, META_R2, META_W1, META_W2 = range(6)
META_D1, META_D2 = 8, 9
COMBINE_CHUNK = 64
N_ROUTE_TILES = N_TOK // ROUTE_TILE
ROUTER_BURST = 64


def _to_slabs(ref, val, rows):
    for c in range(SLAB):
        ref[pl.ds(c, rows, stride=SLAB), :] = val[:, c * LANES:(c + 1) * LANES]


def _from_slabs(ref, rows, first_row=0):
    return jnp.concatenate([ref[pl.ds(first_row * SLAB + c, rows, stride=SLAB), :] for c in range(SLAB)],
                           axis=1)


def _row_slab(ref, row):
    return ref.at[pl.ds(pl.multiple_of(row * SLAB, SLAB), SLAB)]


def _split_bf16(x):
    def top_bits(v):
        bits = lax.bitcast_convert_type(v, jnp.uint32) & jnp.uint32(0xFFFF0000)
        return lax.bitcast_convert_type(bits, F32)
    hi = top_bits(x)
    mid = top_bits(x - hi)
    lo = (x - hi) - mid
    return hi.astype(BF16), mid.astype(BF16), lo.astype(BF16)


def _route_tile(i, h, wr_ref, br_ref, tri_ref, meta_ref, counts_ref, dest_ref, xs_hbm,
                carry_ref, slab_a, slab_b, zero_ref, dest_v, dest_a, dest_b, count_v, count_s,
                sem, pad_sem, idx_sem):
    slabs = (slab_a, slab_b)
    dests = (dest_a, dest_b)

    @pl.when(i == 0)
    def _():
        carry_ref[...] = jnp.zeros_like(carry_ref)
        zero_ref[...] = jnp.zeros_like(zero_ref)

    def start_row(slot, t):
        for k in range(2):
            pltpu.make_async_copy(_row_slab(slabs[slot], t), _row_slab(xs_hbm, dests[slot][k, t]),
                                  sem.at[slot]).start(priority=1)

    def wait_scatter(slot):
        for _ in range(2):
            pltpu.make_async_copy(slabs[slot], xs_hbm.at[pl.ds(0, ROUTE_TILE * SLAB)], sem.at[slot]).wait()

    def route(slot, scatter_previous):
        pending = list(range(0, ROUTE_TILE, ROUTER_BURST)) if scatter_previous else []

        def burst():
            if pending:
                r0 = pending.pop(0)
                for t in range(r0, r0 + ROUTER_BURST):
                    start_row(1 - slot, t)

        lane = lax.broadcasted_iota(jnp.int32, (1, LANES), 1)
        lane_f = lane.astype(F32)
        h0, h1, h2 = _split_bf16(h)
        w0, w1, w2 = wr_ref[0], wr_ref[1], wr_ref[2]
        small = None
        for a, b in ((h2, w0), (h0, w2), (h1, w1)):
            part = jnp.dot(a, b, preferred_element_type=F32)
            small = part if small is None else small + part
            burst()
        mid = None
        for a, b in ((h1, w0), (h0, w1)):
            part = jnp.dot(a, b, preferred_element_type=F32)
            mid = part if mid is None else mid + part
            burst()
        logits = (small + mid) + jnp.dot(h0, w0, preferred_element_type=F32) + br_ref[...]
        burst()
        logits = jnp.where(lane < N_EXPERTS, logits, -jnp.inf)
        m1 = jnp.max(logits, axis=-1, keepdims=True)
        i1 = jnp.min(jnp.where(logits == m1, lane_f, float(LANES)), axis=-1, keepdims=True)
        rest = jnp.where(lane_f == i1, -jnp.inf, logits)
        m2 = jnp.max(rest, axis=-1, keepdims=True)
        i2 = jnp.min(jnp.where(rest == m2, lane_f, float(LANES)), axis=-1, keepdims=True)
        e2 = jnp.exp(m2 - m1)
        tot = 1.0 + e2
        burst()
        chosen = jnp.where((lane_f == i1) | (lane_f == i2), 1.0, 0.0)
        before = jnp.dot(tri_ref[...], chosen.astype(BF16), preferred_element_type=F32) + carry_ref[...]
        r1 = jnp.sum(jnp.where(lane_f == i1, before, 0.0), axis=-1, keepdims=True)
        r2 = jnp.sum(jnp.where(lane_f == i2, before, 0.0), axis=-1, keepdims=True)
        new_carry = carry_ref[...] + jnp.sum(chosen, axis=0, keepdims=True)
        carry_ref[...] = new_carry
        counts_ref[...] = new_carry.astype(jnp.int32)
        burst()
        record = jnp.zeros((ROUTE_TILE, LANES), F32)
        for lane_id, col in ((META_E1, i1), (META_E2, i2), (META_R1, r1), (META_R2, r2),
                             (META_W1, 1.0 / tot), (META_W2, e2 / tot),
                             (META_D1, i1 * EXPERT_REGION + r1), (META_D2, i2 * EXPERT_REGION + r2)):
            record = jnp.where(lane == lane_id, col, record)
        meta_ref[...] = record
        _to_slabs(slabs[slot], h, ROUTE_TILE)
        while pending:
            burst()
        dest_rows = record.T[META_D1:META_D1 + 8, :].astype(jnp.int32)
        dest_ref[...] = dest_rows
        dest_v[...] = dest_rows
        to_smem = pltpu.make_async_copy(dest_v, dests[slot], idx_sem)
        to_smem.start()
        to_smem.wait()

    @pl.when(i == 0)
    def _():
        route(0, False)

    for slot in range(2):
        @pl.when((i > 0) & (i % 2 == slot))
        def _():
            @pl.when(i >= 2)
            def _():
                wait_scatter(slot)
            route(slot, True)

    @pl.when(i == N_ROUTE_TILES - 1)
    def _():
        slot = (N_ROUTE_TILES - 1) % 2
        lax.fori_loop(0, ROUTE_TILE, lambda t, c: (start_row(slot, t), c)[1], 0, unroll=8)
        wait_scatter(1 - slot)
        wait_scatter(slot)
        count_v[...] = jnp.broadcast_to(carry_ref[...].astype(jnp.int32), count_v.shape)
        cp = pltpu.make_async_copy(count_v, count_s, idx_sem)
        cp.start()
        cp.wait()
        for e in range(N_EXPERTS):
            n = count_s[0, e]
            end = (n + EXPERT_TILE - 1) // EXPERT_TILE * EXPERT_TILE
            pad = lambda r: pltpu.make_async_copy(zero_ref, _row_slab(xs_hbm, e * EXPERT_REGION + r), pad_sem)
            lax.fori_loop(n, end, lambda r, c: (pad(r).start(), c)[1], 0)
            lax.fori_loop(n, end, lambda r, c: (pad(r).wait(), c)[1], 0)


def _route_operands(w_router, b_router):
    tri = (jnp.arange(ROUTE_TILE)[:, None] > jnp.arange(ROUTE_TILE)[None, :]).astype(BF16)
    return jnp.stack(_split_bf16(w_router)), b_router, tri


ROUTE_OUT_SHAPES = [jax.ShapeDtypeStruct((N_TOK, LANES), F32),
                    jax.ShapeDtypeStruct((1, LANES), jnp.int32),
                    jax.ShapeDtypeStruct((8, N_TOK), jnp.int32),
                    jax.ShapeDtypeStruct((N_SORTED * SLAB, LANES), F32)]
ROUTE_SCRATCH = [pltpu.VMEM((1, LANES), F32),
                 pltpu.VMEM((ROUTE_TILE * SLAB, LANES), F32), pltpu.VMEM((ROUTE_TILE * SLAB, LANES), F32),
                 pltpu.VMEM((SLAB, LANES), F32),
                 pltpu.VMEM((8, ROUTE_TILE), jnp.int32),
                 pltpu.SMEM((8, ROUTE_TILE), jnp.int32), pltpu.SMEM((8, ROUTE_TILE), jnp.int32),
                 pltpu.VMEM((8, LANES), jnp.int32), pltpu.SMEM((8, LANES), jnp.int32),
                 pltpu.SemaphoreType.DMA((2,)), pltpu.SemaphoreType.DMA, pltpu.SemaphoreType.DMA]


def _experts_kernel(tblk_ref, texp_ref, xs_ref, wg_ref, wu_ref, wd_ref, ys_ref, wb_ref):
    i = pl.program_id(0)

    @pl.when((i == 0) | (texp_ref[i] != texp_ref[jnp.maximum(i - 1, 0)]))
    def _():
        for j, w_ref in enumerate((wg_ref, wu_ref, wd_ref)):
            wb_ref[j] = w_ref[0].astype(BF16)

    @pl.when((i == 0) | (tblk_ref[i] != tblk_ref[jnp.maximum(i - 1, 0)]))
    def _():
        x = _from_slabs(xs_ref, EXPERT_TILE).astype(BF16)
        gate = jnp.dot(x, wb_ref[0], preferred_element_type=F32)
        up = jnp.dot(x, wb_ref[1], preferred_element_type=F32)
        act = (jax.nn.silu(gate) * up).astype(BF16)
        _to_slabs(ys_ref, jnp.dot(act, wb_ref[2], preferred_element_type=F32), EXPERT_TILE)


def _experts(tile_block, tile_expert, xs, wg, wu, wd):
    wspec = lambda a, b: pl.BlockSpec((1, a, b), lambda i, tblk, texp: (texp[i], 0, 0))
    rows = pl.BlockSpec((EXPERT_TILE * SLAB, LANES), lambda i, tblk, texp: (tblk[i], 0))
    return pl.pallas_call(
        _experts_kernel,
        out_shape=jax.ShapeDtypeStruct((N_SORTED * SLAB, LANES), F32),
        grid_spec=pltpu.PrefetchScalarGridSpec(
            num_scalar_prefetch=2,
            grid=(N_EXPERT_TILES,),
            in_specs=[rows, wspec(D_MODEL, EXPERT_DIM), wspec(D_MODEL, EXPERT_DIM),
                      wspec(EXPERT_DIM, D_MODEL)],
            out_specs=rows,
            scratch_shapes=[pltpu.VMEM((3, D_MODEL, EXPERT_DIM), BF16)]),
        compiler_params=_cparams("arbitrary"),
        name="moe_experts",
    )(tile_block, tile_expert, xs, wg, wu, wd)


def _combine_ln_kernel(dest_ref, meta_ref, h_ref, ys_hbm, lg_ref, lb_ref, o_ref, ya_ref, yb_ref, sem):
    i = pl.program_id(0)
    bufs = (ya_ref, yb_ref)

    def start_gather(tile, slot):
        def body(t, carry):
            for k in range(2):
                row = dest_ref[k * N_TOK + tile * ROUTE_TILE + t]
                pltpu.make_async_copy(_row_slab(ys_hbm, row), _row_slab(bufs[slot], k * ROUTE_TILE + t),
                                      sem.at[slot]).start(priority=k)
            return carry
        lax.fori_loop(0, ROUTE_TILE, body, 0, unroll=8)

    def wait_gather(slot):
        pltpu.make_async_copy(ys_hbm.at[pl.ds(0, 2 * ROUTE_TILE * SLAB)], bufs[slot], sem.at[slot]).wait()

    last = pl.num_programs(0) - 1
    lane = lax.broadcasted_iota(jnp.int32, (1, LANES), 1)

    def finish_and_prefetch(slot):
        nxt = jnp.minimum(i + 1, last) * ROUTE_TILE
        wait_gather(slot)
        for r0 in range(0, ROUTE_TILE, COMBINE_CHUNK):
            rows = slice(r0, r0 + COMBINE_CHUNK)
            meta = meta_ref[rows, :]
            w1 = jnp.sum(jnp.where(lane == META_W1, meta, 0.0), axis=-1, keepdims=True)
            w2 = jnp.sum(jnp.where(lane == META_W2, meta, 0.0), axis=-1, keepdims=True)
            ff = (w1 * _from_slabs(bufs[slot], COMBINE_CHUNK, first_row=r0)
                  + w2 * _from_slabs(bufs[slot], COMBINE_CHUNK, first_row=ROUTE_TILE + r0))
            o_ref[rows, :] = _layer_norm(DEEPNORM_ALPHA * h_ref[rows, :] + ff, lg_ref[...], lb_ref[...])
            for t in range(r0, r0 + COMBINE_CHUNK):
                for k in range(2):
                    row = dest_ref[k * N_TOK + nxt + t]
                    pltpu.make_async_copy(_row_slab(ys_hbm, row),
                                          _row_slab(bufs[1 - slot], k * ROUTE_TILE + t),
                                          sem.at[1 - slot]).start(priority=k)

        @pl.when(i == last)
        def _():
            wait_gather(1 - slot)

    @pl.when(i == 0)
    def _():
        start_gather(0, 0)

    for slot in range(2):
        @pl.when(i % 2 == slot)
        def _():
            finish_and_prefetch(slot)


def _combine_ln(dest, meta, h, ys, lg, lb):
    row = lambda width: pl.BlockSpec((ROUTE_TILE, width), lambda i, dest: (i, 0))
    const = pl.BlockSpec((1, D_MODEL), lambda i, dest: (0, 0))
    return pl.pallas_call(
        _combine_ln_kernel,
        out_shape=jax.ShapeDtypeStruct((N_TOK, D_MODEL), F32),
        grid_spec=pltpu.PrefetchScalarGridSpec(
            num_scalar_prefetch=1,
            grid=(N_TOK // ROUTE_TILE,),
            in_specs=[row(LANES), row(D_MODEL), pl.BlockSpec(memory_space=pl.ANY), const, const],
            out_specs=row(D_MODEL),
            scratch_shapes=[pltpu.VMEM((2 * ROUTE_TILE * SLAB, LANES), F32)] * 2
                           + [pltpu.SemaphoreType.DMA((2,))]),
        compiler_params=_cparams("arbitrary"),
        name="moe_combine_ln",
    )(dest, meta, h, ys, lg, lb)


def _moe_ln(h, meta, counts, dest_rows, xs, wg, wu, wd, lg, lb):
    expert = jnp.arange(N_EXPERTS)
    tiles = (counts[0, :N_EXPERTS] + EXPERT_TILE - 1) // EXPERT_TILE
    ends = jnp.sum(jnp.where(expert[None, :] <= expert[:, None], tiles[None, :], 0), axis=1)
    step = jnp.minimum(jnp.arange(N_EXPERT_TILES), ends[-1] - 1)
    tile_expert = jnp.sum(step[:, None] >= ends[None, :], axis=1)
    first = jnp.sum(jnp.where(tile_expert[:, None] == expert[None, :], (ends - tiles)[None, :], 0), axis=1)
    tile_block = tile_expert * REGION_TILES + (step - first)
    ys = _experts(tile_block.astype(jnp.int32), tile_expert.astype(jnp.int32), xs, wg, wu, wd)
    dest = dest_rows[:2].reshape(2 * N_TOK)
    return _combine_ln(dest, meta, h, ys, lg, lb)


def kernel(x, positions, ln_g, ln_b, ssm_lambda_re, ssm_lambda_im, ssm_log_step, ssm_b_re, ssm_b_im, ssm_c_re, ssm_c_im, ssm_d, ssm_w_glu, kv_w, attn_w_q, attn_sinks, attn_w_out, ffn_w_gate, ffn_w_up, ffn_w_down, moe_w_router, moe_b_router, moe_w_gate, moe_w_up, moe_w_down):
    ln = lambda layer, j: (ln_g[layer, j].reshape(1, D_MODEL).astype(F32),
                           ln_b[layer, j].reshape(1, D_MODEL).astype(F32))
    rope_table = _rope_tables(positions)
    x2 = x.reshape(N_TOK, D_MODEL)

    params = _ssm_params(ssm_lambda_re[0], ssm_lambda_im[0], ssm_log_step[0], ssm_b_re[0], ssm_b_im[0],
                         ssm_c_re[0], ssm_c_im[0])
    y = _ssm_mixer(x2, params)
    h = _glu_ln(y, x2, ssm_d[0].astype(F32).reshape(1, D_MODEL), ssm_w_glu[0].astype(BF16), *ln(0, 0))
    h = _ffn_ln(h, ffn_w_gate[0].astype(BF16), ffn_w_up[0].astype(BF16), ffn_w_down[0].astype(BF16),
                *ln(0, 1))

    order = jnp.array(HEAD_ORDER)
    w_q = attn_w_q[0].reshape(D_MODEL, N_HEADS, HEAD_DIM)[:, order].reshape(D_MODEL, Q_DIM)
    w_out = attn_w_out[0].reshape(N_HEADS, HEAD_DIM, D_MODEL)[order].reshape(Q_DIM, D_MODEL)
    w_qkv = jnp.concatenate([w_q, kv_w], axis=1).astype(BF16)
    q, k, v = _qkv(h, w_qkv, rope_table)
    w_router = jnp.pad(moe_w_router[0].astype(F32), ((0, 0), (0, LANES - N_EXPERTS)))
    b_router = jnp.pad(moe_b_router[0].astype(F32), (0, LANES - N_EXPERTS)).reshape(1, LANES)
    h, *routing = _attention_ln_route(q, k, v, attn_sinks[0].astype(F32), h, w_out.astype(BF16), *ln(1, 0),
                                      w_router, b_router)
    h = _moe_ln(h, *routing, moe_w_gate[0].astype(F32), moe_w_up[0].astype(F32),
                moe_w_down[0].astype(F32), *ln(1, 1))
    return h.reshape(BATCH, SEQ, D_MODEL)
```

```python
import math

import jax
import jax.numpy as jnp
import numpy as np
from jax import lax
from jax.experimental import pallas as pl
from jax.experimental.pallas import tpu as pltpu

F32 = jnp.float32
BF16 = jnp.bfloat16

D_MODEL = 1024
BATCH = 16
SEQ = 2048
N_TOK = BATCH * SEQ
DEPTH = 2

SSM_GROUP_CH = 16
SSM_GROUPS = D_MODEL // SSM_GROUP_CH
SSM_STATE = 64
SSM_CHUNK = 16
SSM_NCHUNK = SEQ // SSM_CHUNK
SSM_WIDTH = SSM_CHUNK * SSM_GROUP_CH
SSM_SCAN_STEPS = int(math.log2(SSM_NCHUNK))

N_HEADS = 16
HEAD_DIM = 64
N_KV_HEADS = 4
Q_PER_KV = N_HEADS // N_KV_HEADS
KV_DIM = N_KV_HEADS * HEAD_DIM
WINDOW = 128
BLOCK = 128
ROT_DIM = HEAD_DIM // 4
ROT_HALF = ROT_DIM // 2
ROPE_THETA = 500000.0

FFN_DIM = 2816
N_EXPERTS = 8
EXPERT_DIM = 1024

DEEPNORM_ALPHA = (2 * DEPTH) ** 0.25
LOG2_E = math.log2(math.e)
LN_EPS = 1e-5

LANES = 128
VMEM_LIMIT = 56 * 1024 * 1024


def _cparams(*sem):
    return pltpu.CompilerParams(dimension_semantics=sem, vmem_limit_bytes=VMEM_LIMIT)


def _layer_norm(r, g, b):
    mu = jnp.mean(r, axis=-1, keepdims=True)
    xc = r - mu
    var = jnp.mean(xc * xc, axis=-1, keepdims=True)
    return xc * lax.rsqrt(var + LN_EPS) * g + b


def _rope_kernel(invf_ref, pos_ref, cs_ref):
    pos = pos_ref[...]
    for f in range(ROT_HALF):
        ang = pos * invf_ref[f]
        cs_ref[f] = jnp.cos(ang)
        cs_ref[ROT_HALF + f] = jnp.sin(ang)


def _rope_tables(positions):
    inv_freq = ROPE_THETA ** (-jnp.arange(0, ROT_DIM, 2, dtype=F32) / ROT_DIM)
    pos = positions.astype(F32)
    cs = pl.pallas_call(
        _rope_kernel,
        out_shape=jax.ShapeDtypeStruct((ROT_DIM, BATCH, SEQ), F32),
        in_specs=[pl.BlockSpec(memory_space=pltpu.SMEM),
                  pl.BlockSpec(memory_space=pltpu.VMEM)],
        out_specs=pl.BlockSpec(memory_space=pltpu.VMEM),
        name="rope_tables",
    )(inv_freq, pos)
    return cs.reshape(ROT_DIM, N_TOK).T


def _rope_placement():
    f = np.arange(ROT_HALF)
    place = np.zeros((ROT_DIM, LANES), np.float32)
    place[f, f] = place[f, ROT_HALF + f] = 1.0
    place[ROT_HALF + f, HEAD_DIM + f] = -1.0
    place[ROT_HALF + f, HEAD_DIM + ROT_HALF + f] = 1.0
    lane = np.arange(LANES)
    bias = ((lane >= ROT_DIM) & (lane < HEAD_DIM)).astype(np.float32).reshape(1, LANES)
    return jnp.asarray(place, BF16), jnp.asarray(bias)


def _ssm_params(lam_re, lam_im, log_step, b_re, b_im, c_re, c_im):
    hp = lax.Precision.HIGHEST
    lr, li = lam_re.astype(F32), lam_im.astype(F32)
    dt = jnp.exp(log_step.astype(F32))[:, None]
    mag = jnp.exp(lr * dt)
    ar = mag * jnp.cos(li * dt)
    ai = mag * jnp.sin(li * dt)
    nr = ar - 1.0
    den = lr * lr + li * li
    kr = (nr * lr + ai * li) / den
    ki = (ai * lr - nr * li) / den
    br, bi = b_re.astype(F32), b_im.astype(F32)
    bbar_r = kr[..., None] * br - ki[..., None] * bi
    bbar_i = kr[..., None] * bi + ki[..., None] * br
    cr, ci = c_re.astype(F32), c_im.astype(F32)

    def powers(taus):
        t = taus.astype(F32)[:, None, None]
        m = jnp.exp(lr[None] * dt[None] * t)
        ang = li[None] * dt[None] * t
        return m * jnp.cos(ang), m * jnp.sin(ang)

    er, ei = powers(jnp.arange(SSM_CHUNK + 1))
    w_r = er[:, :, :, None] * bbar_r[None] - ei[:, :, :, None] * bbar_i[None]
    w_i = er[:, :, :, None] * bbar_i[None] + ei[:, :, :, None] * bbar_r[None]
    kern = (jnp.einsum('gcp,tgpd->gtcd', cr, w_r[:SSM_CHUNK], precision=hp)
            - jnp.einsum('gcp,tgpd->gtcd', ci, w_i[:SSM_CHUNK], precision=hp))
    s_idx = jnp.arange(SSM_CHUNK)[:, None, None]
    t_idx = jnp.arange(SSM_CHUNK)[None, :, None]
    lag = (t_idx - s_idx == jnp.arange(SSM_CHUNK)[None, None, :]).astype(F32)
    toep = jnp.einsum('stu,gucd->gtcsd', lag, kern, precision=hp)
    mt = toep.reshape(SSM_GROUPS, SSM_WIDTH, SSM_WIDTH)
    rev = SSM_CHUNK - 1 - jnp.arange(SSM_CHUNK)
    g_r = w_r[rev].transpose(1, 2, 0, 3).reshape(SSM_GROUPS, SSM_STATE, SSM_WIDTH)
    g_i = w_i[rev].transpose(1, 2, 0, 3).reshape(SSM_GROUPS, SSM_STATE, SSM_WIDTH)
    gt = jnp.concatenate([g_r, g_i], axis=1)
    e1r, e1i = er[1:], ei[1:]
    ce_r = cr[None] * e1r[:, :, None, :] - ci[None] * e1i[:, :, None, :]
    ce_i = cr[None] * e1i[:, :, None, :] + ci[None] * e1r[:, :, None, :]
    c_re = ce_r.transpose(1, 0, 2, 3).reshape(SSM_GROUPS, SSM_WIDTH, SSM_STATE)
    c_im = (-ce_i).transpose(1, 0, 2, 3).reshape(SSM_GROUPS, SSM_WIDTH, SSM_STATE)
    ct = jnp.concatenate([c_re, c_im], axis=-1)
    sr, si = powers(SSM_CHUNK * (2 ** jnp.arange(8)))
    coef = lambda t: jnp.concatenate([t, t], axis=-1).transpose(1, 0, 2)
    return mt.astype(BF16), gt.astype(BF16), ct.astype(BF16), coef(sr), coef(si)


SSM_LANE_GROUPS = LANES // SSM_GROUP_CH
SSM_LANE_CHUNKS = D_MODEL // LANES
SSM_BATCH_TILE = 4
SSM_COLS = SSM_BATCH_TILE * SSM_NCHUNK


def _ssm_kernel(x_ref, mt_ref, gt_ref, ct_ref, cr_ref, ci_ref, y_ref, v_ref, yt_ref):
    for bl in range(SSM_BATCH_TILE):
        cols = slice(bl * SSM_NCHUNK, (bl + 1) * SSM_NCHUNK)
        for s in range(SSM_CHUNK):
            a = x_ref[pl.ds(bl * SEQ + s, SSM_NCHUNK, stride=SSM_CHUNK), :]
            at = a.T.astype(BF16)
            for g in range(SSM_LANE_GROUPS):
                v_ref[g, s * SSM_GROUP_CH:(s + 1) * SSM_GROUP_CH, cols] = (
                    at[g * SSM_GROUP_CH:(g + 1) * SSM_GROUP_CH])

    row_j = lax.broadcasted_iota(jnp.int32, (SSM_NCHUNK, 1), 0)

    def group_body(g, carry):
        ut = v_ref[g]
        xt = jnp.dot(gt_ref[g], ut, preferred_element_type=F32)
        cr_all = cr_ref[g]
        ci_all = ci_ref[g]

        def seg(rows, b):
            return xt[rows, b * SSM_NCHUNK:(b + 1) * SSM_NCHUNK]

        def shifted(x, d):
            if d < 8:
                return jnp.where(row_j >= d, pltpu.roll(x, d, 0), 0.0)
            return jnp.concatenate([jnp.zeros((d, LANES), F32), x[:SSM_NCHUNK - d]], axis=0)

        re_rows, im_rows = slice(0, SSM_STATE), slice(SSM_STATE, 2 * SSM_STATE)
        sr, si = [], []
        for q in range(SSM_BATCH_TILE // 2):
            xr = jnp.concatenate([seg(re_rows, 2 * q), seg(re_rows, 2 * q + 1)], axis=0).T
            xi = jnp.concatenate([seg(im_rows, 2 * q), seg(im_rows, 2 * q + 1)], axis=0).T
            for k in range(SSM_SCAN_STEPS):
                cr = cr_all[k:k + 1, :]
                ci = ci_all[k:k + 1, :]
                rr = shifted(xr, 1 << k)
                ri = shifted(xi, 1 << k)
                xr, xi = xr + cr * rr - ci * ri, xi + cr * ri + ci * rr
            pr = shifted(xr, 1).T
            pi = shifted(xi, 1).T
            sr += [pr[:SSM_STATE], pr[SSM_STATE:]]
            si += [pi[:SSM_STATE], pi[SSM_STATE:]]
        sprev = jnp.concatenate([jnp.concatenate(sr, axis=1), jnp.concatenate(si, axis=1)],
                                axis=0).astype(BF16)
        yt_ref[g] = (jnp.dot(mt_ref[g], ut, preferred_element_type=F32)
                     + jnp.dot(ct_ref[g], sprev, preferred_element_type=F32))
        return carry

    lax.fori_loop(0, SSM_LANE_GROUPS, group_body, 0, unroll=2)

    for bl in range(SSM_BATCH_TILE):
        cols = slice(bl * SSM_NCHUNK, (bl + 1) * SSM_NCHUNK)
        for t in range(SSM_CHUNK):
            tile = jnp.concatenate(
                [yt_ref[g, t * SSM_GROUP_CH:(t + 1) * SSM_GROUP_CH, cols] for g in range(SSM_LANE_GROUPS)],
                axis=0)
            y_ref[pl.ds(bl * SEQ + t, SSM_NCHUNK, stride=SSM_CHUNK), :] = tile.T


def _ssm_mixer(x2, params):
    mt, gt, ct, coef_r, coef_i = params
    rows = SSM_BATCH_TILE * SEQ
    pspec = lambda a, b: pl.BlockSpec((SSM_LANE_GROUPS, a, b), lambda k, q: (k, 0, 0))
    return pl.pallas_call(
        _ssm_kernel,
        out_shape=jax.ShapeDtypeStruct((N_TOK, D_MODEL), F32),
        grid=(SSM_LANE_CHUNKS, BATCH // SSM_BATCH_TILE),
        in_specs=[pl.BlockSpec((rows, LANES), lambda k, q: (q, k)),
                  pspec(SSM_WIDTH, SSM_WIDTH), pspec(2 * SSM_STATE, SSM_WIDTH), pspec(SSM_WIDTH, 2 * SSM_STATE),
                  pspec(8, 2 * SSM_STATE), pspec(8, 2 * SSM_STATE)],
        out_specs=pl.BlockSpec((rows, LANES), lambda k, q: (q, k)),
        scratch_shapes=[pltpu.VMEM((SSM_LANE_GROUPS, SSM_WIDTH, SSM_COLS), BF16),
                        pltpu.VMEM((SSM_LANE_GROUPS, SSM_WIDTH, SSM_COLS), F32)],
        compiler_params=_cparams("parallel", "parallel"),
        name="ssm_mixer",
    )(x2, mt, gt, ct, coef_r, coef_i)


GLU_COLS = 256


def _glu_ln_kernel(y_ref, x_ref, d_ref, w_ref, lg_ref, lb_ref, o_ref):
    x = x_ref[...]
    act = jax.nn.gelu(y_ref[...] + d_ref[...] * x).astype(BF16)
    parts = []
    for c in range(0, D_MODEL, GLU_COLS):
        val = jnp.dot(act, w_ref[:, c:c + GLU_COLS], preferred_element_type=F32)
        gate = jnp.dot(act, w_ref[:, D_MODEL + c:D_MODEL + c + GLU_COLS], preferred_element_type=F32)
        parts.append(val * jax.nn.sigmoid(gate))
    mix = jnp.concatenate(parts, axis=1)
    o_ref[...] = _layer_norm(DEEPNORM_ALPHA * x + mix, lg_ref[...], lb_ref[...])


def _row_spec(tm, width):
    return pl.BlockSpec((tm, width), lambda i: (i, 0))


def _const_spec(*shape):
    return pl.BlockSpec(shape, lambda *_: (0,) * len(shape))


def _glu_ln(y, x2, d_skip, w_glu, lg, lb, tm=512):
    return pl.pallas_call(
        _glu_ln_kernel,
        out_shape=jax.ShapeDtypeStruct((N_TOK, D_MODEL), F32),
        grid=(N_TOK // tm,),
        in_specs=[_row_spec(tm, D_MODEL), _row_spec(tm, D_MODEL), _const_spec(1, D_MODEL),
                  _const_spec(D_MODEL, 2 * D_MODEL), _const_spec(1, D_MODEL), _const_spec(1, D_MODEL)],
        out_specs=_row_spec(tm, D_MODEL),
        compiler_params=_cparams("parallel"),
        name="glu_ln",
    )(y, x2, d_skip, w_glu, lg, lb)


FFN_SPLIT = 2
FFN_TILE = FFN_DIM // FFN_SPLIT


def _ffn_ln_kernel(h_ref, wg_ref, wu_ref, wd_ref, lg_ref, lb_ref, o_ref):
    h = h_ref[...]
    hb = h.astype(BF16)
    ff = None
    for f in range(FFN_SPLIT):
        cols = slice(f * FFN_TILE, (f + 1) * FFN_TILE)
        gate = jnp.dot(hb, wg_ref[:, cols], preferred_element_type=F32)
        up = jnp.dot(hb, wu_ref[:, cols], preferred_element_type=F32)
        act = (jax.nn.silu(gate) * up).astype(BF16)
        part = jnp.dot(act, wd_ref[cols, :], preferred_element_type=F32)
        ff = part if ff is None else ff + part
    o_ref[...] = _layer_norm(DEEPNORM_ALPHA * h + ff, lg_ref[...], lb_ref[...])


def _resident_spec(*shape):
    return pl.BlockSpec(shape, lambda *_: (0,) * len(shape), pipeline_mode=pl.Buffered(1))


def _ffn_ln(h, wg, wu, wd, lg, lb, tm=512):
    return pl.pallas_call(
        _ffn_ln_kernel,
        out_shape=jax.ShapeDtypeStruct((N_TOK, D_MODEL), F32),
        grid=(N_TOK // tm,),
        in_specs=[_row_spec(tm, D_MODEL),
                  _resident_spec(D_MODEL, FFN_DIM), _resident_spec(D_MODEL, FFN_DIM),
                  _resident_spec(FFN_DIM, D_MODEL),
                  _const_spec(1, D_MODEL), _const_spec(1, D_MODEL)],
        out_specs=_row_spec(tm, D_MODEL),
        compiler_params=_cparams("parallel"),
        name="ffn_ln",
    )(h, wg, wu, wd, lg, lb)


Q_DIM = N_HEADS * HEAD_DIM


def _qkv_kernel(h_ref, w_ref, cs_ref, place_ref, bias_ref, q_ref, k_ref, v_ref):
    hb = h_ref[...].astype(BF16)
    z = jnp.dot(hb, w_ref[...], preferred_element_type=F32)
    lane = lax.broadcasted_iota(jnp.int32, (1, LANES), 1)
    table = bias_ref[...]
    for term in _split_bf16(cs_ref[...]):
        table = table + jnp.dot(term, place_ref[...], preferred_element_type=F32)
    swapped = pltpu.roll(table, HEAD_DIM, 1)
    cos_t = jnp.where(lane < HEAD_DIM, table, swapped)
    sin_t = jnp.where(lane < HEAD_DIM, swapped, table)
    first_half = (lane % HEAD_DIM) < ROT_HALF

    def rope(t):
        partner = jnp.where(first_half, pltpu.roll(t, LANES - ROT_HALF, 1), pltpu.roll(t, ROT_HALF, 1))
        return t * cos_t + partner * sin_t

    scale = HEAD_DIM ** -0.5 * LOG2_E
    for c in range(Q_DIM // LANES):
        sl = slice(c * LANES, (c + 1) * LANES)
        q_ref[:, sl] = (rope(z[:, sl]) * scale).astype(BF16)
    for c in range(KV_DIM // LANES):
        sl = slice(c * LANES, (c + 1) * LANES)
        k_ref[:, sl] = rope(z[:, Q_DIM + c * LANES:Q_DIM + (c + 1) * LANES]).astype(BF16)
    v_ref[...] = z[:, Q_DIM + KV_DIM:].astype(BF16)


def _qkv(h, w_qkv, rope_table, tm=512):
    return pl.pallas_call(
        _qkv_kernel,
        out_shape=[jax.ShapeDtypeStruct((N_TOK, Q_DIM), BF16),
                   jax.ShapeDtypeStruct((N_TOK, KV_DIM), BF16),
                   jax.ShapeDtypeStruct((N_TOK, KV_DIM), BF16)],
        grid=(N_TOK // tm,),
        in_specs=[_row_spec(tm, D_MODEL), _const_spec(D_MODEL, Q_DIM + 2 * KV_DIM),
                  _row_spec(tm, ROT_DIM), _const_spec(ROT_DIM, LANES), _const_spec(1, LANES)],
        out_specs=[_row_spec(tm, Q_DIM), _row_spec(tm, KV_DIM), _row_spec(tm, KV_DIM)],
        compiler_params=_cparams("parallel"),
        name="qkv_rope",
    )(h, w_qkv, rope_table, *_rope_placement())


ATTN_Q_TILE = 512
ATTN_SUB = ATTN_Q_TILE // BLOCK
ATTN_BAND = ATTN_Q_TILE + BLOCK
KV_PAIRS = N_KV_HEADS // 2

HEAD_ORDER = [h for c in range(KV_PAIRS) for g in range(Q_PER_KV)
              for h in (2 * c * Q_PER_KV + g, (2 * c + 1) * Q_PER_KV + g)]


def _band_bias():
    qi = jnp.arange(Q_PER_KV * BLOCK)[:, None] % BLOCK
    si = jnp.arange(2 * BLOCK)[None, :]
    rel = qi + BLOCK - si
    valid = (rel >= 0) & (rel < WINDOW)
    first = valid & (si >= BLOCK)
    neg = jnp.float32(-jnp.inf)
    return jnp.stack([jnp.where(valid, 0.0, neg), jnp.where(first, 0.0, neg)]).astype(F32)


def _attn_kernel(sink_ref, q_ref, kp_ref, kc_ref, vp_ref, vc_ref, bias_ref, h_ref, w_ref, lg_ref, lb_ref,
                 wr_ref, br_ref, tri_ref,
                 out_ref, meta_ref, counts_ref, dest_ref, xs_hbm,
                 ka_ref, kb_ref, va_ref, vb_ref, o_ref, *route_scratch):
    i = pl.program_id(1)
    tile = pl.program_id(0) * ATTN_TILES + i
    _route_begin(tile, xs_hbm, *route_scratch)
    lane = lax.broadcasted_iota(jnp.int32, (1, KV_DIM), 1)
    low = (lane % LANES) < HEAD_DIM
    zero = jnp.zeros((), BF16)
    for src, dst_a, dst_b in ((kp_ref, ka_ref, kb_ref), (vp_ref, va_ref, vb_ref)):
        t = src[...]
        dst_a[0:BLOCK, :] = jnp.where(low, t, zero)
        dst_b[0:BLOCK, :] = jnp.where(low, zero, t)
    for src, dst_a, dst_b in ((kc_ref, ka_ref, kb_ref), (vc_ref, va_ref, vb_ref)):
        t = src[...]
        dst_a[BLOCK:ATTN_BAND, :] = jnp.where(low, t, zero)
        dst_b[BLOCK:ATTN_BAND, :] = jnp.where(low, zero, t)

    lane1 = lax.broadcasted_iota(jnp.int32, (1, LANES), 1)
    row = lax.broadcasted_iota(jnp.int32, (Q_PER_KV * BLOCK, 1), 0)
    contract_last = (((1,), (1,)), ((), ()))
    for blk in range(ATTN_SUB):
        r0 = blk * BLOCK
        if blk == 0:
            bias = bias_ref[jnp.where(i == 0, 1, 0)]
        else:
            bias = bias_ref[0]
        for c in range(KV_PAIRS):
            cs = slice(c * LANES, (c + 1) * LANES)
            chunks = [c * Q_PER_KV + g for g in range(Q_PER_KV)]
            q4 = jnp.concatenate([q_ref[r0:r0 + BLOCK, m * LANES:(m + 1) * LANES] for m in chunks], axis=0)
            outs = []
            for half, (k_ref, v_ref) in enumerate(((ka_ref, va_ref), (kb_ref, vb_ref))):
                kband = k_ref[r0:r0 + 2 * BLOCK, cs]
                s = lax.dot_general(q4, kband, contract_last, preferred_element_type=F32) + bias
                sink = jnp.zeros((Q_PER_KV * BLOCK, 1), F32)
                for g, m in enumerate(chunks):
                    sink = jnp.where(row // BLOCK == g, sink_ref[HEAD_ORDER[2 * m + half]] * LOG2_E, sink)
                mx = jnp.maximum(jnp.max(s, axis=-1, keepdims=True), sink)
                p = jnp.exp2(s - mx)
                denom = jnp.sum(p, axis=-1, keepdims=True) + jnp.exp2(sink - mx)
                pv = jnp.dot(p.astype(BF16), v_ref[r0:r0 + 2 * BLOCK, cs], preferred_element_type=F32)
                outs.append((pv, 1.0 / denom))
            (pv_a, r_a), (pv_b, r_b) = outs
            o = (pv_a + pv_b) * jnp.where(lane1 < HEAD_DIM, r_a, r_b)
            for g, m in enumerate(chunks):
                o_ref[r0:r0 + BLOCK, m * LANES:(m + 1) * LANES] = o[g * BLOCK:(g + 1) * BLOCK].astype(BF16)
            _route_burst(tile, blk * KV_PAIRS + c, xs_hbm, *route_scratch)

    mix = jnp.dot(o_ref[...], w_ref[...], preferred_element_type=F32)
    h_out = _layer_norm(DEEPNORM_ALPHA * h_ref[...] + mix, lg_ref[...], lb_ref[...])
    out_ref[...] = h_out
    _route_finish(tile, h_out, wr_ref, br_ref, tri_ref, meta_ref, counts_ref, dest_ref, xs_hbm, *route_scratch)


ATTN_TILES = SEQ // ATTN_Q_TILE


def _attention_ln_route(q, k, v, sinks, h, w_out, lg, lb, w_router, b_router):
    assert ATTN_Q_TILE == ROUTE_TILE
    tiles = ATTN_TILES
    cur = lambda b, i: (b * tiles + i, 0)
    prev = lambda b, i: (jnp.maximum((b * tiles + i) * ATTN_SUB - 1, 0), 0)
    const2 = lambda b, i: (0, 0)
    const3 = lambda b, i: (0, 0, 0)
    return pl.pallas_call(
        _attn_kernel,
        out_shape=[jax.ShapeDtypeStruct((N_TOK, D_MODEL), F32)] + ROUTE_OUT_SHAPES,
        grid=(BATCH, tiles),
        in_specs=[pl.BlockSpec(memory_space=pltpu.SMEM),
                  pl.BlockSpec((ATTN_Q_TILE, Q_DIM), cur),
                  pl.BlockSpec((BLOCK, KV_DIM), prev), pl.BlockSpec((ATTN_Q_TILE, KV_DIM), cur),
                  pl.BlockSpec((BLOCK, KV_DIM), prev), pl.BlockSpec((ATTN_Q_TILE, KV_DIM), cur),
                  pl.BlockSpec((2, Q_PER_KV * BLOCK, 2 * BLOCK), lambda b, i: (0, 0, 0)),
                  pl.BlockSpec((ATTN_Q_TILE, D_MODEL), cur),
                  pl.BlockSpec((Q_DIM, D_MODEL), const2),
                  pl.BlockSpec((1, D_MODEL), const2), pl.BlockSpec((1, D_MODEL), const2),
                  pl.BlockSpec((3, D_MODEL, LANES), const3), pl.BlockSpec((1, LANES), const2),
                  pl.BlockSpec((ROUTE_TILE, ROUTE_TILE), const2)],
        out_specs=[pl.BlockSpec((ATTN_Q_TILE, D_MODEL), cur),
                   pl.BlockSpec((ROUTE_TILE, LANES), cur), pl.BlockSpec((1, LANES), const2),
                   pl.BlockSpec((8, ROUTE_TILE), lambda b, i: (0, b * tiles + i)),
                   pl.BlockSpec(memory_space=pl.ANY)],
        scratch_shapes=[pltpu.VMEM((ATTN_BAND, KV_DIM), BF16)] * 4
                       + [pltpu.VMEM((ATTN_Q_TILE, Q_DIM), BF16)] + ROUTE_SCRATCH,
        compiler_params=_cparams("arbitrary", "arbitrary"),
        name="swa_attention_ln_route",
    )(sinks, q, k, k, v, v, _band_bias(), h, w_out, lg, lb, *_route_operands(w_router, b_router))


ROUTE_TILE = 512
EXPERT_TILE = 512
SLAB = D_MODEL // LANES
EXPERT_REGION = N_TOK
REGION_TILES = EXPERT_REGION // EXPERT_TILE
N_SORTED = N_EXPERTS * EXPERT_REGION
N_EXPERT_TILES = 2 * N_TOK // EXPERT_TILE + N_EXPERTS
META_E1, META_E2, META_R1, META_R2, META_W1, META_W2 = range(6)
META_D1, META_D2 = 8, 9
COMBINE_CHUNK = 64
N_ROUTE_TILES = N_TOK // ROUTE_TILE
ROUTER_BURST = 64


def _to_slabs(ref, val, rows):
    for c in range(SLAB):
        ref[pl.ds(c, rows, stride=SLAB), :] = val[:, c * LANES:(c + 1) * LANES]


def _from_slabs(ref, rows, first_row=0):
    return jnp.concatenate([ref[pl.ds(first_row * SLAB + c, rows, stride=SLAB), :] for c in range(SLAB)],
                           axis=1)


def _row_slab(ref, row):
    return ref.at[pl.ds(pl.multiple_of(row * SLAB, SLAB), SLAB)]


def _split_bf16(x):
    def top_bits(v):
        bits = lax.bitcast_convert_type(v, jnp.uint32) & jnp.uint32(0xFFFF0000)
        return lax.bitcast_convert_type(bits, F32)
    hi = top_bits(x)
    mid = top_bits(x - hi)
    lo = (x - hi) - mid
    return hi.astype(BF16), mid.astype(BF16), lo.astype(BF16)


def _start_row(slot, t, slab_ref, dest_s, xs_hbm, sem):
    rows = pl.ds(t * SLAB, SLAB) if isinstance(t, int) else pl.ds(pl.multiple_of(t * SLAB, SLAB), SLAB)
    for k in range(2):
        pltpu.make_async_copy(slab_ref.at[slot, rows], _row_slab(xs_hbm, dest_s[slot, k, t]),
                              sem.at[slot]).start(priority=k)


def _start_rows(slot, first, count, slab_ref, dest_s, xs_hbm, sem):
    for t in range(first, first + count):
        _start_row(slot, t, slab_ref, dest_s, xs_hbm, sem)


def _wait_rows(slot, slab_ref, xs_hbm, sem):
    for _ in range(2):
        pltpu.make_async_copy(slab_ref.at[slot], xs_hbm.at[pl.ds(0, ROUTE_TILE * SLAB)], sem.at[slot]).wait()


def _route_begin(i, xs_hbm, carry_ref, slab_ref, zero_ref, dest_v, dest_s, count_v, count_s, sem, pad_sem,
                 idx_sem):
    @pl.when(i == 0)
    def _():
        carry_ref[...] = jnp.zeros_like(carry_ref)
        zero_ref[...] = jnp.zeros_like(zero_ref)
        slab_ref[1] = jnp.zeros(slab_ref.shape[1:], F32)
        spare = N_SORTED + lax.broadcasted_iota(jnp.int32, dest_v.shape, 1)
        dest_v[...] = spare + ROUTE_TILE * lax.broadcasted_iota(jnp.int32, dest_v.shape, 0)
        cp = pltpu.make_async_copy(dest_v, dest_s.at[1], idx_sem)
        cp.start()
        cp.wait()

    @pl.when(i > 0)
    def _():
        _wait_rows(i % 2, slab_ref, xs_hbm, sem)


def _route_burst(i, burst, xs_hbm, carry_ref, slab_ref, zero_ref, dest_v, dest_s, count_v, count_s, sem, pad_sem,
                 idx_sem):
    _start_rows(1 - i % 2, burst * ROUTER_BURST, ROUTER_BURST, slab_ref, dest_s, xs_hbm, sem)


def _route_finish(i, h, wr_ref, br_ref, tri_ref, meta_ref, counts_ref, dest_ref, xs_hbm,
                  carry_ref, slab_ref, zero_ref, dest_v, dest_s, count_v, count_s, sem, pad_sem, idx_sem):
    slot = i % 2

    def burst():
        pass

    def route():
        lane = lax.broadcasted_iota(jnp.int32, (1, LANES), 1)
        lane_f = lane.astype(F32)
        h0, h1, h2 = _split_bf16(h)
        w0, w1, w2 = wr_ref[0], wr_ref[1], wr_ref[2]
        small = None
        for a, b in ((h2, w0), (h0, w2), (h1, w1)):
            part = jnp.dot(a, b, preferred_element_type=F32)
            small = part if small is None else small + part
            burst()
        mid = None
        for a, b in ((h1, w0), (h0, w1)):
            part = jnp.dot(a, b, preferred_element_type=F32)
            mid = part if mid is None else mid + part
            burst()
        logits = (small + mid) + jnp.dot(h0, w0, preferred_element_type=F32) + br_ref[...]
        burst()
        logits = jnp.where(lane < N_EXPERTS, logits, -jnp.inf)
        m1 = jnp.max(logits, axis=-1, keepdims=True)
        i1 = jnp.min(jnp.where(logits == m1, lane_f, float(LANES)), axis=-1, keepdims=True)
        rest = jnp.where(lane_f == i1, -jnp.inf, logits)
        m2 = jnp.max(rest, axis=-1, keepdims=True)
        i2 = jnp.min(jnp.where(rest == m2, lane_f, float(LANES)), axis=-1, keepdims=True)
        e2 = jnp.exp(m2 - m1)
        tot = 1.0 + e2
        burst()
        chosen = jnp.where((lane_f == i1) | (lane_f == i2), 1.0, 0.0)
        before = jnp.dot(tri_ref[...], chosen.astype(BF16), preferred_element_type=F32) + carry_ref[...]
        r1 = jnp.sum(jnp.where(lane_f == i1, before, 0.0), axis=-1, keepdims=True)
        r2 = jnp.sum(jnp.where(lane_f == i2, before, 0.0), axis=-1, keepdims=True)
        new_carry = carry_ref[...] + jnp.sum(chosen, axis=0, keepdims=True)
        carry_ref[...] = new_carry
        counts_ref[...] = new_carry.astype(jnp.int32)
        burst()
        record = jnp.zeros((ROUTE_TILE, LANES), F32)
        for lane_id, col in ((META_E1, i1), (META_E2, i2), (META_R1, r1), (META_R2, r2),
                             (META_W1, 1.0 / tot), (META_W2, e2 / tot),
                             (META_D1, i1 * EXPERT_REGION + r1), (META_D2, i2 * EXPERT_REGION + r2)):
            record = jnp.where(lane == lane_id, col, record)
        meta_ref[...] = record
        dest_rows = record.T[META_D1:META_D1 + 8, :].astype(jnp.int32)
        dest_ref[...] = dest_rows
        dest_v[...] = dest_rows
        to_smem = pltpu.make_async_copy(dest_v, dest_s.at[slot], idx_sem)
        to_smem.start()
        to_smem.wait()

    route()
    for parity in range(2):
        @pl.when(slot == parity)
        def _():
            _to_slabs(slab_ref.at[parity], h, ROUTE_TILE)

    @pl.when(i == N_ROUTE_TILES - 1)
    def _():
        last_slot = (N_ROUTE_TILES - 1) % 2
        lax.fori_loop(0, ROUTE_TILE,
                      lambda t, c: (_start_row(last_slot, t, slab_ref, dest_s, xs_hbm, sem), c)[1], 0)
        _wait_rows(1 - last_slot, slab_ref, xs_hbm, sem)
        _wait_rows(last_slot, slab_ref, xs_hbm, sem)
        count_v[...] = jnp.broadcast_to(carry_ref[...].astype(jnp.int32), count_v.shape)
        cp = pltpu.make_async_copy(count_v, count_s, idx_sem)
        cp.start()
        cp.wait()
        for e in range(N_EXPERTS):
            n = count_s[0, e]
            end = (n + EXPERT_TILE - 1) // EXPERT_TILE * EXPERT_TILE
            pad = lambda r: pltpu.make_async_copy(zero_ref, _row_slab(xs_hbm, e * EXPERT_REGION + r), pad_sem)
            lax.fori_loop(n, end, lambda r, c: (pad(r).start(), c)[1], 0)
            lax.fori_loop(n, end, lambda r, c: (pad(r).wait(), c)[1], 0)


def _route_operands(w_router, b_router):
    tri = (jnp.arange(ROUTE_TILE)[:, None] > jnp.arange(ROUTE_TILE)[None, :]).astype(BF16)
    return jnp.stack(_split_bf16(w_router)), b_router, tri


ROUTE_OUT_SHAPES = [jax.ShapeDtypeStruct((N_TOK, LANES), F32),
                    jax.ShapeDtypeStruct((1, LANES), jnp.int32),
                    jax.ShapeDtypeStruct((8, N_TOK), jnp.int32),
                    jax.ShapeDtypeStruct(((N_SORTED + 2 * ROUTE_TILE) * SLAB, LANES), F32)]
ROUTE_SCRATCH = [pltpu.VMEM((1, LANES), F32),
                 pltpu.VMEM((2, ROUTE_TILE * SLAB, LANES), F32),
                 pltpu.VMEM((SLAB, LANES), F32),
                 pltpu.VMEM((8, ROUTE_TILE), jnp.int32),
                 pltpu.SMEM((2, 8, ROUTE_TILE), jnp.int32),
                 pltpu.VMEM((8, LANES), jnp.int32), pltpu.SMEM((8, LANES), jnp.int32),
                 pltpu.SemaphoreType.DMA((2,)), pltpu.SemaphoreType.DMA, pltpu.SemaphoreType.DMA]


def _experts_kernel(tblk_ref, texp_ref, xs_ref, wg_ref, wu_ref, wd_ref, ys_ref, wb_ref):
    i = pl.program_id(0)

    @pl.when((i == 0) | (texp_ref[i] != texp_ref[jnp.maximum(i - 1, 0)]))
    def _():
        for j, w_ref in enumerate((wg_ref, wu_ref, wd_ref)):
            wb_ref[j] = w_ref[0].astype(BF16)

    @pl.when((i == 0) | (tblk_ref[i] != tblk_ref[jnp.maximum(i - 1, 0)]))
    def _():
        x = _from_slabs(xs_ref, EXPERT_TILE).astype(BF16)
        gate = jnp.dot(x, wb_ref[0], preferred_element_type=F32)
        up = jnp.dot(x, wb_ref[1], preferred_element_type=F32)
        act = (jax.nn.silu(gate) * up).astype(BF16)
        _to_slabs(ys_ref, jnp.dot(act, wb_ref[2], preferred_element_type=F32), EXPERT_TILE)


def _experts(tile_block, tile_expert, xs, wg, wu, wd):
    wspec = lambda a, b: pl.BlockSpec((1, a, b), lambda i, tblk, texp: (texp[i], 0, 0))
    rows = pl.BlockSpec((EXPERT_TILE * SLAB, LANES), lambda i, tblk, texp: (tblk[i], 0))
    return pl.pallas_call(
        _experts_kernel,
        out_shape=jax.ShapeDtypeStruct((N_SORTED * SLAB, LANES), F32),
        grid_spec=pltpu.PrefetchScalarGridSpec(
            num_scalar_prefetch=2,
            grid=(N_EXPERT_TILES,),
            in_specs=[rows, wspec(D_MODEL, EXPERT_DIM), wspec(D_MODEL, EXPERT_DIM),
                      wspec(EXPERT_DIM, D_MODEL)],
            out_specs=rows,
            scratch_shapes=[pltpu.VMEM((3, D_MODEL, EXPERT_DIM), BF16)]),
        compiler_params=_cparams("arbitrary"),
        name="moe_experts",
    )(tile_block, tile_expert, xs, wg, wu, wd)


def _combine_ln_kernel(dest_ref, meta_ref, h_ref, ys_hbm, lg_ref, lb_ref, o_ref, ya_ref, yb_ref, sem):
    i = pl.program_id(0)
    bufs = (ya_ref, yb_ref)

    def start_gather(tile, slot):
        def body(t, carry):
            for k in range(2):
                row = dest_ref[k * N_TOK + tile * ROUTE_TILE + t]
                pltpu.make_async_copy(_row_slab(ys_hbm, row), _row_slab(bufs[slot], k * ROUTE_TILE + t),
                                      sem.at[slot]).start(priority=k)
            return carry
        lax.fori_loop(0, ROUTE_TILE, body, 0, unroll=8)

    def wait_gather(slot):
        pltpu.make_async_copy(ys_hbm.at[pl.ds(0, 2 * ROUTE_TILE * SLAB)], bufs[slot], sem.at[slot]).wait()

    last = pl.num_programs(0) - 1
    lane = lax.broadcasted_iota(jnp.int32, (1, LANES), 1)

    def finish_and_prefetch(slot):
        nxt = jnp.minimum(i + 1, last) * ROUTE_TILE
        wait_gather(slot)
        for r0 in range(0, ROUTE_TILE, COMBINE_CHUNK):
            rows = slice(r0, r0 + COMBINE_CHUNK)
            meta = meta_ref[rows, :]
            w1 = jnp.sum(jnp.where(lane == META_W1, meta, 0.0), axis=-1, keepdims=True)
            w2 = jnp.sum(jnp.where(lane == META_W2, meta, 0.0), axis=-1, keepdims=True)
            ff = (w1 * _from_slabs(bufs[slot], COMBINE_CHUNK, first_row=r0)
                  + w2 * _from_slabs(bufs[slot], COMBINE_CHUNK, first_row=ROUTE_TILE + r0))
            o_ref[rows, :] = _layer_norm(DEEPNORM_ALPHA * h_ref[rows, :] + ff, lg_ref[...], lb_ref[...])
            for t in range(r0, r0 + COMBINE_CHUNK):
                for k in range(2):
                    row = dest_ref[k * N_TOK + nxt + t]
                    pltpu.make_async_copy(_row_slab(ys_hbm, row),
                                          _row_slab(bufs[1 - slot], k * ROUTE_TILE + t),
                                          sem.at[1 - slot]).start(priority=k)

        @pl.when(i == last)
        def _():
            wait_gather(1 - slot)

    @pl.when(i == 0)
    def _():
        start_gather(0, 0)

    for slot in range(2):
        @pl.when(i % 2 == slot)
        def _():
            finish_and_prefetch(slot)


def _combine_ln(dest, meta, h, ys, lg, lb):
    row = lambda width: pl.BlockSpec((ROUTE_TILE, width), lambda i, dest: (i, 0))
    const = pl.BlockSpec((1, D_MODEL), lambda i, dest: (0, 0))
    return pl.pallas_call(
        _combine_ln_kernel,
        out_shape=jax.ShapeDtypeStruct((N_TOK, D_MODEL), F32),
        grid_spec=pltpu.PrefetchScalarGridSpec(
            num_scalar_prefetch=1,
            grid=(N_TOK // ROUTE_TILE,),
            in_specs=[row(LANES), row(D_MODEL), pl.BlockSpec(memory_space=pl.ANY), const, const],
            out_specs=row(D_MODEL),
            scratch_shapes=[pltpu.VMEM((2 * ROUTE_TILE * SLAB, LANES), F32)] * 2
                           + [pltpu.SemaphoreType.DMA((2,))]),
        compiler_params=_cparams("arbitrary"),
        name="moe_combine_ln",
    )(dest, meta, h, ys, lg, lb)


def _moe_ln(h, meta, counts, dest_rows, xs, wg, wu, wd, lg, lb):
    expert = jnp.arange(N_EXPERTS)
    tiles = (counts[0, :N_EXPERTS] + EXPERT_TILE - 1) // EXPERT_TILE
    ends = jnp.sum(jnp.where(expert[None, :] <= expert[:, None], tiles[None, :], 0), axis=1)
    step = jnp.minimum(jnp.arange(N_EXPERT_TILES), ends[-1] - 1)
    tile_expert = jnp.sum(step[:, None] >= ends[None, :], axis=1)
    first = jnp.sum(jnp.where(tile_expert[:, None] == expert[None, :], (ends - tiles)[None, :], 0), axis=1)
    tile_block = tile_expert * REGION_TILES + (step - first)
    ys = _experts(tile_block.astype(jnp.int32), tile_expert.astype(jnp.int32), xs, wg, wu, wd)
    dest = dest_rows[:2].reshape(2 * N_TOK)
    return _combine_ln(dest, meta, h, ys, lg, lb)


def kernel(x, positions, ln_g, ln_b, ssm_lambda_re, ssm_lambda_im, ssm_log_step, ssm_b_re, ssm_b_im, ssm_c_re, ssm_c_im, ssm_d, ssm_w_glu, kv_w, attn_w_q, attn_sinks, attn_w_out, ffn_w_gate, ffn_w_up, ffn_w_down, moe_w_router, moe_b_router, moe_w_gate, moe_w_up, moe_w_down):
    ln = lambda layer, j: (ln_g[layer, j].reshape(1, D_MODEL).astype(F32),
                           ln_b[layer, j].reshape(1, D_MODEL).astype(F32))
    rope_table = _rope_tables(positions)
    x2 = x.reshape(N_TOK, D_MODEL)

    params = _ssm_params(ssm_lambda_re[0], ssm_lambda_im[0], ssm_log_step[0], ssm_b_re[0], ssm_b_im[0],
                         ssm_c_re[0], ssm_c_im[0])
    y = _ssm_mixer(x2, params)
    h = _glu_ln(y, x2, ssm_d[0].astype(F32).reshape(1, D_MODEL), ssm_w_glu[0].astype(BF16), *ln(0, 0))
    h = _ffn_ln(h, ffn_w_gate[0].astype(BF16), ffn_w_up[0].astype(BF16), ffn_w_down[0].astype(BF16),
                *ln(0, 1))

    order = jnp.array(HEAD_ORDER)
    w_q = attn_w_q[0].reshape(D_MODEL, N_HEADS, HEAD_DIM)[:, order].reshape(D_MODEL, Q_DIM)
    w_out = attn_w_out[0].reshape(N_HEADS, HEAD_DIM, D_MODEL)[order].reshape(Q_DIM, D_MODEL)
    w_qkv = jnp.concatenate([w_q, kv_w], axis=1).astype(BF16)
    q, k, v = _qkv(h, w_qkv, rope_table)
    w_router = jnp.pad(moe_w_router[0].astype(F32), ((0, 0), (0, LANES - N_EXPERTS)))
    b_router = jnp.pad(moe_b_router[0].astype(F32), (0, LANES - N_EXPERTS)).reshape(1, LANES)
    h, *routing = _attention_ln_route(q, k, v, attn_sinks[0].astype(F32), h, w_out.astype(BF16), *ln(1, 0),
                                      w_router, b_router)
    h = _moe_ln(h, *routing, moe_w_gate[0].astype(F32), moe_w_up[0].astype(F32),
                moe_w_down[0].astype(F32), *ln(1, 1))
    return h.reshape(BATCH, SEQ, D_MODEL)
```

```python
import math

import jax
import jax.numpy as jnp
import numpy as np
from jax import lax
from jax.experimental import pallas as pl
from jax.experimental.pallas import tpu as pltpu

F32 = jnp.float32
BF16 = jnp.bfloat16

D_MODEL = 1024
BATCH = 16
SEQ = 2048
N_TOK = BATCH * SEQ
DEPTH = 2

SSM_GROUP_CH = 16
SSM_GROUPS = D_MODEL // SSM_GROUP_CH
SSM_STATE = 64
SSM_CHUNK = 16
SSM_NCHUNK = SEQ // SSM_CHUNK
SSM_WIDTH = SSM_CHUNK * SSM_GROUP_CH
SSM_SCAN_STEPS = int(math.log2(SSM_NCHUNK))

N_HEADS = 16
HEAD_DIM = 64
N_KV_HEADS = 4
Q_PER_KV = N_HEADS // N_KV_HEADS
KV_DIM = N_KV_HEADS * HEAD_DIM
WINDOW = 128
BLOCK = 128
ROT_DIM = HEAD_DIM // 4
ROT_HALF = ROT_DIM // 2
ROPE_THETA = 500000.0

FFN_DIM = 2816
N_EXPERTS = 8
EXPERT_DIM = 1024

DEEPNORM_ALPHA = (2 * DEPTH) ** 0.25
LOG2_E = math.log2(math.e)
LN_EPS = 1e-5

LANES = 128
VMEM_LIMIT = 56 * 1024 * 1024


def _cparams(*sem):
    return pltpu.CompilerParams(dimension_semantics=sem, vmem_limit_bytes=VMEM_LIMIT)


def _layer_norm(r, g, b):
    mu = jnp.mean(r, axis=-1, keepdims=True)
    xc = r - mu
    var = jnp.mean(xc * xc, axis=-1, keepdims=True)
    return xc * lax.rsqrt(var + LN_EPS) * g + b


def _rope_kernel(invf_ref, pos_ref, cs_ref):
    pos = pos_ref[...]
    for f in range(ROT_HALF):
        ang = pos * invf_ref[f]
        cs_ref[f] = jnp.cos(ang)
        cs_ref[ROT_HALF + f] = jnp.sin(ang)


def _rope_tables(positions):
    inv_freq = ROPE_THETA ** (-jnp.arange(0, ROT_DIM, 2, dtype=F32) / ROT_DIM)
    pos = positions.astype(F32)
    cs = pl.pallas_call(
        _rope_kernel,
        out_shape=jax.ShapeDtypeStruct((ROT_DIM, BATCH, SEQ), F32),
        in_specs=[pl.BlockSpec(memory_space=pltpu.SMEM),
                  pl.BlockSpec(memory_space=pltpu.VMEM)],
        out_specs=pl.BlockSpec(memory_space=pltpu.VMEM),
        name="rope_tables",
    )(inv_freq, pos)
    return cs.reshape(ROT_DIM, N_TOK).T


def _rope_placement():
    f = np.arange(ROT_HALF)
    place = np.zeros((ROT_DIM, LANES), np.float32)
    place[f, f] = place[f, ROT_HALF + f] = 1.0
    place[ROT_HALF + f, HEAD_DIM + f] = -1.0
    place[ROT_HALF + f, HEAD_DIM + ROT_HALF + f] = 1.0
    lane = np.arange(LANES)
    bias = ((lane >= ROT_DIM) & (lane < HEAD_DIM)).astype(np.float32).reshape(1, LANES)
    return jnp.asarray(place, BF16), jnp.asarray(bias)


def _ssm_params(lam_re, lam_im, log_step, b_re, b_im, c_re, c_im):
    hp = lax.Precision.HIGHEST
    lr, li = lam_re.astype(F32), lam_im.astype(F32)
    dt = jnp.exp(log_step.astype(F32))[:, None]
    mag = jnp.exp(lr * dt)
    ar = mag * jnp.cos(li * dt)
    ai = mag * jnp.sin(li * dt)
    nr = ar - 1.0
    den = lr * lr + li * li
    kr = (nr * lr + ai * li) / den
    ki = (ai * lr - nr * li) / den
    br, bi = b_re.astype(F32), b_im.astype(F32)
    bbar_r = kr[..., None] * br - ki[..., None] * bi
    bbar_i = kr[..., None] * bi + ki[..., None] * br
    cr, ci = c_re.astype(F32), c_im.astype(F32)

    def powers(taus):
        t = taus.astype(F32)[:, None, None]
        m = jnp.exp(lr[None] * dt[None] * t)
        ang = li[None] * dt[None] * t
        return m * jnp.cos(ang), m * jnp.sin(ang)

    er, ei = powers(jnp.arange(SSM_CHUNK + 1))
    w_r = er[:, :, :, None] * bbar_r[None] - ei[:, :, :, None] * bbar_i[None]
    w_i = er[:, :, :, None] * bbar_i[None] + ei[:, :, :, None] * bbar_r[None]
    kern = (jnp.einsum('gcp,tgpd->gtcd', cr, w_r[:SSM_CHUNK], precision=hp)
            - jnp.einsum('gcp,tgpd->gtcd', ci, w_i[:SSM_CHUNK], precision=hp))
    s_idx = jnp.arange(SSM_CHUNK)[:, None, None]
    t_idx = jnp.arange(SSM_CHUNK)[None, :, None]
    lag = (t_idx - s_idx == jnp.arange(SSM_CHUNK)[None, None, :]).astype(BF16)
    toep = jnp.einsum('stu,gucd->gtcsd', lag, kern.astype(BF16), preferred_element_type=BF16)
    mt = toep.reshape(SSM_GROUPS, SSM_WIDTH, SSM_WIDTH)
    rev = SSM_CHUNK - 1 - jnp.arange(SSM_CHUNK)
    g_r = w_r[rev].transpose(1, 2, 0, 3).reshape(SSM_GROUPS, SSM_STATE, SSM_WIDTH)
    g_i = w_i[rev].transpose(1, 2, 0, 3).reshape(SSM_GROUPS, SSM_STATE, SSM_WIDTH)
    gt = jnp.concatenate([g_r, g_i], axis=1)
    e1r, e1i = er[1:], ei[1:]
    ce_r = cr[None] * e1r[:, :, None, :] - ci[None] * e1i[:, :, None, :]
    ce_i = cr[None] * e1i[:, :, None, :] + ci[None] * e1r[:, :, None, :]
    c_re = ce_r.transpose(1, 0, 2, 3).reshape(SSM_GROUPS, SSM_WIDTH, SSM_STATE)
    c_im = (-ce_i).transpose(1, 0, 2, 3).reshape(SSM_GROUPS, SSM_WIDTH, SSM_STATE)
    ct = jnp.concatenate([c_re, c_im], axis=-1)
    sr, si = powers(SSM_CHUNK * (2 ** jnp.arange(8)))
    coef = lambda t: jnp.concatenate([t, t], axis=-1).transpose(1, 0, 2)
    return mt.astype(BF16), gt.astype(BF16), ct.astype(BF16), coef(sr), coef(si)


SSM_LANE_GROUPS = LANES // SSM_GROUP_CH
SSM_LANE_CHUNKS = D_MODEL // LANES
SSM_BATCH_TILE = 4
SSM_COLS = SSM_BATCH_TILE * SSM_NCHUNK


def _ssm_kernel(x_ref, mt_ref, gt_ref, ct_ref, cr_ref, ci_ref, y_ref, v_ref, yt_ref):
    for bl in range(SSM_BATCH_TILE):
        cols = slice(bl * SSM_NCHUNK, (bl + 1) * SSM_NCHUNK)
        for s in range(SSM_CHUNK):
            a = x_ref[pl.ds(bl * SEQ + s, SSM_NCHUNK, stride=SSM_CHUNK), :]
            at = a.T.astype(BF16)
            for g in range(SSM_LANE_GROUPS):
                v_ref[g, s * SSM_GROUP_CH:(s + 1) * SSM_GROUP_CH, cols] = (
                    at[g * SSM_GROUP_CH:(g + 1) * SSM_GROUP_CH])

    row_j = lax.broadcasted_iota(jnp.int32, (SSM_NCHUNK, 1), 0)

    def group_body(g, carry):
        ut = v_ref[g]
        xt = jnp.dot(gt_ref[g], ut, preferred_element_type=F32)
        cr_all = cr_ref[g]
        ci_all = ci_ref[g]

        def seg(rows, b):
            return xt[rows, b * SSM_NCHUNK:(b + 1) * SSM_NCHUNK]

        def shifted(x, d):
            if d < 8:
                return jnp.where(row_j >= d, pltpu.roll(x, d, 0), 0.0)
            return jnp.concatenate([jnp.zeros((d, LANES), F32), x[:SSM_NCHUNK - d]], axis=0)

        re_rows, im_rows = slice(0, SSM_STATE), slice(SSM_STATE, 2 * SSM_STATE)
        sr, si = [], []
        for q in range(SSM_BATCH_TILE // 2):
            xr = jnp.concatenate([seg(re_rows, 2 * q), seg(re_rows, 2 * q + 1)], axis=0).T
            xi = jnp.concatenate([seg(im_rows, 2 * q), seg(im_rows, 2 * q + 1)], axis=0).T
            for k in range(SSM_SCAN_STEPS):
                cr = cr_all[k:k + 1, :]
                ci = ci_all[k:k + 1, :]
                rr = shifted(xr, 1 << k)
                ri = shifted(xi, 1 << k)
                xr, xi = xr + cr * rr - ci * ri, xi + cr * ri + ci * rr
            pr = shifted(xr, 1).T
            pi = shifted(xi, 1).T
            sr += [pr[:SSM_STATE], pr[SSM_STATE:]]
            si += [pi[:SSM_STATE], pi[SSM_STATE:]]
        sprev = jnp.concatenate([jnp.concatenate(sr, axis=1), jnp.concatenate(si, axis=1)],
                                axis=0).astype(BF16)
        yt_ref[g] = (jnp.dot(mt_ref[g], ut, preferred_element_type=F32)
                     + jnp.dot(ct_ref[g], sprev, preferred_element_type=F32))
        return carry

    lax.fori_loop(0, SSM_LANE_GROUPS, group_body, 0, unroll=4)

    for bl in range(SSM_BATCH_TILE):
        cols = slice(bl * SSM_NCHUNK, (bl + 1) * SSM_NCHUNK)
        for t in range(SSM_CHUNK):
            tile = jnp.concatenate(
                [yt_ref[g, t * SSM_GROUP_CH:(t + 1) * SSM_GROUP_CH, cols] for g in range(SSM_LANE_GROUPS)],
                axis=0)
            y_ref[pl.ds(bl * SEQ + t, SSM_NCHUNK, stride=SSM_CHUNK), :] = tile.T


def _ssm_mixer(x2, params):
    mt, gt, ct, coef_r, coef_i = params
    rows = SSM_BATCH_TILE * SEQ
    pspec = lambda a, b: pl.BlockSpec((SSM_LANE_GROUPS, a, b), lambda k, q: (k, 0, 0))
    return pl.pallas_call(
        _ssm_kernel,
        out_shape=jax.ShapeDtypeStruct((N_TOK, D_MODEL), F32),
        grid=(SSM_LANE_CHUNKS, BATCH // SSM_BATCH_TILE),
        in_specs=[pl.BlockSpec((rows, LANES), lambda k, q: (q, k)),
                  pspec(SSM_WIDTH, SSM_WIDTH), pspec(2 * SSM_STATE, SSM_WIDTH), pspec(SSM_WIDTH, 2 * SSM_STATE),
                  pspec(8, 2 * SSM_STATE), pspec(8, 2 * SSM_STATE)],
        out_specs=pl.BlockSpec((rows, LANES), lambda k, q: (q, k)),
        scratch_shapes=[pltpu.VMEM((SSM_LANE_GROUPS, SSM_WIDTH, SSM_COLS), BF16),
                        pltpu.VMEM((SSM_LANE_GROUPS, SSM_WIDTH, SSM_COLS), F32)],
        compiler_params=_cparams("parallel", "parallel"),
        name="ssm_mixer",
    )(x2, mt, gt, ct, coef_r, coef_i)


GLU_COLS = 256


def _glu_ln_kernel(y_ref, x_ref, d_ref, w_ref, lg_ref, lb_ref, o_ref):
    x = x_ref[...]
    act = jax.nn.gelu(y_ref[...] + d_ref[...] * x).astype(BF16)
    parts = []
    for c in range(0, D_MODEL, GLU_COLS):
        val = jnp.dot(act, w_ref[:, c:c + GLU_COLS], preferred_element_type=F32)
        gate = jnp.dot(act, w_ref[:, D_MODEL + c:D_MODEL + c + GLU_COLS], preferred_element_type=F32)
        parts.append(val * jax.nn.sigmoid(gate))
    mix = jnp.concatenate(parts, axis=1)
    o_ref[...] = _layer_norm(DEEPNORM_ALPHA * x + mix, lg_ref[...], lb_ref[...])


def _row_spec(tm, width):
    return pl.BlockSpec((tm, width), lambda i: (i, 0))


def _const_spec(*shape):
    return pl.BlockSpec(shape, lambda *_: (0,) * len(shape))


def _glu_ln(y, x2, d_skip, w_glu, lg, lb, tm=1024):
    return pl.pallas_call(
        _glu_ln_kernel,
        out_shape=jax.ShapeDtypeStruct((N_TOK, D_MODEL), F32),
        grid=(N_TOK // tm,),
        in_specs=[_row_spec(tm, D_MODEL), _row_spec(tm, D_MODEL), _const_spec(1, D_MODEL),
                  _const_spec(D_MODEL, 2 * D_MODEL), _const_spec(1, D_MODEL), _const_spec(1, D_MODEL)],
        out_specs=_row_spec(tm, D_MODEL),
        compiler_params=_cparams("parallel"),
        name="glu_ln",
    )(y, x2, d_skip, w_glu, lg, lb)


FFN_SPLIT = 2
FFN_TILE = FFN_DIM // FFN_SPLIT


def _ffn_ln_kernel(h_ref, wg_ref, wu_ref, wd_ref, lg_ref, lb_ref, o_ref):
    h = h_ref[...]
    hb = h.astype(BF16)
    ff = None
    for f in range(FFN_SPLIT):
        cols = slice(f * FFN_TILE, (f + 1) * FFN_TILE)
        gate = jnp.dot(hb, wg_ref[:, cols], preferred_element_type=F32)
        up = jnp.dot(hb, wu_ref[:, cols], preferred_element_type=F32)
        act = (jax.nn.silu(gate) * up).astype(BF16)
        part = jnp.dot(act, wd_ref[cols, :], preferred_element_type=F32)
        ff = part if ff is None else ff + part
    o_ref[...] = _layer_norm(DEEPNORM_ALPHA * h + ff, lg_ref[...], lb_ref[...])


def _resident_spec(*shape):
    return pl.BlockSpec(shape, lambda *_: (0,) * len(shape), pipeline_mode=pl.Buffered(1))


def _ffn_ln(h, wg, wu, wd, lg, lb, tm=512):
    return pl.pallas_call(
        _ffn_ln_kernel,
        out_shape=jax.ShapeDtypeStruct((N_TOK, D_MODEL), F32),
        grid=(N_TOK // tm,),
        in_specs=[_row_spec(tm, D_MODEL),
                  _resident_spec(D_MODEL, FFN_DIM), _resident_spec(D_MODEL, FFN_DIM),
                  _resident_spec(FFN_DIM, D_MODEL),
                  _const_spec(1, D_MODEL), _const_spec(1, D_MODEL)],
        out_specs=_row_spec(tm, D_MODEL),
        compiler_params=_cparams("parallel"),
        name="ffn_ln",
    )(h, wg, wu, wd, lg, lb)


Q_DIM = N_HEADS * HEAD_DIM


def _qkv_kernel(h_ref, w_ref, cs_ref, place_ref, bias_ref, q_ref, k_ref, v_ref):
    hb = h_ref[...].astype(BF16)
    z = jnp.dot(hb, w_ref[...], preferred_element_type=F32)
    lane = lax.broadcasted_iota(jnp.int32, (1, LANES), 1)
    table = bias_ref[...]
    for term in _split_bf16(cs_ref[...]):
        table = table + jnp.dot(term, place_ref[...], preferred_element_type=F32)
    swapped = pltpu.roll(table, HEAD_DIM, 1)
    cos_t = jnp.where(lane < HEAD_DIM, table, swapped)
    sin_t = jnp.where(lane < HEAD_DIM, swapped, table)
    first_half = (lane % HEAD_DIM) < ROT_HALF

    def rope(t):
        partner = jnp.where(first_half, pltpu.roll(t, LANES - ROT_HALF, 1), pltpu.roll(t, ROT_HALF, 1))
        return t * cos_t + partner * sin_t

    scale = HEAD_DIM ** -0.5 * LOG2_E
    for c in range(Q_DIM // LANES):
        sl = slice(c * LANES, (c + 1) * LANES)
        q_ref[:, sl] = (rope(z[:, sl]) * scale).astype(BF16)
    for c in range(KV_DIM // LANES):
        sl = slice(c * LANES, (c + 1) * LANES)
        k_ref[:, sl] = rope(z[:, Q_DIM + c * LANES:Q_DIM + (c + 1) * LANES]).astype(BF16)
    v_ref[...] = z[:, Q_DIM + KV_DIM:].astype(BF16)


def _qkv(h, w_qkv, rope_table, tm=512):
    return pl.pallas_call(
        _qkv_kernel,
        out_shape=[jax.ShapeDtypeStruct((N_TOK, Q_DIM), BF16),
                   jax.ShapeDtypeStruct((N_TOK, KV_DIM), BF16),
                   jax.ShapeDtypeStruct((N_TOK, KV_DIM), BF16)],
        grid=(N_TOK // tm,),
        in_specs=[_row_spec(tm, D_MODEL), _const_spec(D_MODEL, Q_DIM + 2 * KV_DIM),
                  _row_spec(tm, ROT_DIM), _const_spec(ROT_DIM, LANES), _const_spec(1, LANES)],
        out_specs=[_row_spec(tm, Q_DIM), _row_spec(tm, KV_DIM), _row_spec(tm, KV_DIM)],
        compiler_params=_cparams("parallel"),
        name="qkv_rope",
    )(h, w_qkv, rope_table, *_rope_placement())


ATTN_Q_TILE = 512
ATTN_SUB = ATTN_Q_TILE // BLOCK
ATTN_BAND = ATTN_Q_TILE + BLOCK
KV_PAIRS = N_KV_HEADS // 2

HEAD_ORDER = [h for c in range(KV_PAIRS) for g in range(Q_PER_KV)
              for h in (2 * c * Q_PER_KV + g, (2 * c + 1) * Q_PER_KV + g)]


def _band_bias():
    qi = jnp.arange(Q_PER_KV * BLOCK)[:, None] % BLOCK
    si = jnp.arange(2 * BLOCK)[None, :]
    rel = qi + BLOCK - si
    valid = (rel >= 0) & (rel < WINDOW)
    first = valid & (si >= BLOCK)
    neg = jnp.float32(-jnp.inf)
    return jnp.stack([jnp.where(valid, 0.0, neg), jnp.where(first, 0.0, neg)]).astype(F32)


def _attn_kernel(sink_ref, q_ref, kp_ref, kc_ref, vp_ref, vc_ref, bias_ref, h_ref, w_ref, lg_ref, lb_ref,
                 out_ref, ka_ref, kb_ref, va_ref, vb_ref, o_ref):
    i = pl.program_id(1)
    lane = lax.broadcasted_iota(jnp.int32, (1, KV_DIM), 1)
    low = (lane % LANES) < HEAD_DIM
    zero = jnp.zeros((), BF16)
    for src, dst_a, dst_b in ((kp_ref, ka_ref, kb_ref), (vp_ref, va_ref, vb_ref)):
        t = src[...]
        dst_a[0:BLOCK, :] = jnp.where(low, t, zero)
        dst_b[0:BLOCK, :] = jnp.where(low, zero, t)
    for src, dst_a, dst_b in ((kc_ref, ka_ref, kb_ref), (vc_ref, va_ref, vb_ref)):
        t = src[...]
        dst_a[BLOCK:ATTN_BAND, :] = jnp.where(low, t, zero)
        dst_b[BLOCK:ATTN_BAND, :] = jnp.where(low, zero, t)

    lane1 = lax.broadcasted_iota(jnp.int32, (1, LANES), 1)
    row = lax.broadcasted_iota(jnp.int32, (Q_PER_KV * BLOCK, 1), 0)
    contract_last = (((1,), (1,)), ((), ()))
    for blk in range(ATTN_SUB):
        r0 = blk * BLOCK
        if blk == 0:
            bias = bias_ref[jnp.where(i == 0, 1, 0)]
        else:
            bias = bias_ref[0]
        for c in range(KV_PAIRS):
            cs = slice(c * LANES, (c + 1) * LANES)
            chunks = [c * Q_PER_KV + g for g in range(Q_PER_KV)]
            q4 = jnp.concatenate([q_ref[r0:r0 + BLOCK, m * LANES:(m + 1) * LANES] for m in chunks], axis=0)
            outs = []
            for half, (k_ref, v_ref) in enumerate(((ka_ref, va_ref), (kb_ref, vb_ref))):
                kband = k_ref[r0:r0 + 2 * BLOCK, cs]
                s = lax.dot_general(q4, kband, contract_last, preferred_element_type=F32) + bias
                sink = jnp.zeros((Q_PER_KV * BLOCK, 1), F32)
                for g, m in enumerate(chunks):
                    sink = jnp.where(row // BLOCK == g, sink_ref[HEAD_ORDER[2 * m + half]] * LOG2_E, sink)
                mx = jnp.maximum(jnp.max(s, axis=-1, keepdims=True), sink)
                p = jnp.exp2(s - mx)
                denom = jnp.sum(p, axis=-1, keepdims=True) + jnp.exp2(sink - mx)
                pv = jnp.dot(p.astype(BF16), v_ref[r0:r0 + 2 * BLOCK, cs], preferred_element_type=F32)
                outs.append((pv, 1.0 / denom))
            (pv_a, r_a), (pv_b, r_b) = outs
            o = (pv_a + pv_b) * jnp.where(lane1 < HEAD_DIM, r_a, r_b)
            for g, m in enumerate(chunks):
                o_ref[r0:r0 + BLOCK, m * LANES:(m + 1) * LANES] = o[g * BLOCK:(g + 1) * BLOCK].astype(BF16)

    mix = jnp.dot(o_ref[...], w_ref[...], preferred_element_type=F32)
    out_ref[...] = _layer_norm(DEEPNORM_ALPHA * h_ref[...] + mix, lg_ref[...], lb_ref[...])


def _attention_ln(q, k, v, sinks, h, w_out, lg, lb):
    tiles = SEQ // ATTN_Q_TILE
    cur = lambda b, i: (b * tiles + i, 0)
    prev = lambda b, i: (jnp.maximum((b * tiles + i) * ATTN_SUB - 1, 0), 0)
    const2 = lambda b, i: (0, 0)
    return pl.pallas_call(
        _attn_kernel,
        out_shape=jax.ShapeDtypeStruct((N_TOK, D_MODEL), F32),
        grid=(BATCH, tiles),
        in_specs=[pl.BlockSpec(memory_space=pltpu.SMEM),
                  pl.BlockSpec((ATTN_Q_TILE, Q_DIM), cur),
                  pl.BlockSpec((BLOCK, KV_DIM), prev), pl.BlockSpec((ATTN_Q_TILE, KV_DIM), cur),
                  pl.BlockSpec((BLOCK, KV_DIM), prev), pl.BlockSpec((ATTN_Q_TILE, KV_DIM), cur),
                  pl.BlockSpec((2, Q_PER_KV * BLOCK, 2 * BLOCK), lambda b, i: (0, 0, 0)),
                  pl.BlockSpec((ATTN_Q_TILE, D_MODEL), cur),
                  pl.BlockSpec((Q_DIM, D_MODEL), const2),
                  pl.BlockSpec((1, D_MODEL), const2), pl.BlockSpec((1, D_MODEL), const2)],
        out_specs=pl.BlockSpec((ATTN_Q_TILE, D_MODEL), cur),
        scratch_shapes=[pltpu.VMEM((ATTN_BAND, KV_DIM), BF16)] * 4
                       + [pltpu.VMEM((ATTN_Q_TILE, Q_DIM), BF16)],
        compiler_params=_cparams("parallel", "parallel"),
        name="swa_attention_ln",
    )(sinks, q, k, k, v, v, _band_bias(), h, w_out, lg, lb)


ROUTE_TILE = 512
EXPERT_TILE = 512
SLAB = D_MODEL // LANES
EXPERT_REGION = N_TOK
REGION_TILES = EXPERT_REGION // EXPERT_TILE
N_SORTED = N_EXPERTS * EXPERT_REGION
N_EXPERT_TILES = 2 * N_TOK // EXPERT_TILE + N_EXPERTS
META_E1, META_E2, META_R1, META_R2, META_W1, META_W2 = range(6)
META_D1, META_D2 = 8, 9
COMBINE_CHUNK = 64
N_ROUTE_TILES = N_TOK // ROUTE_TILE
ROUTER_BURST = 64


def _to_slabs(ref, val, rows):
    for c in range(SLAB):
        ref[pl.ds(c, rows, stride=SLAB), :] = val[:, c * LANES:(c + 1) * LANES]


def _from_slabs(ref, rows, first_row=0):
    return jnp.concatenate([ref[pl.ds(first_row * SLAB + c, rows, stride=SLAB), :] for c in range(SLAB)],
                           axis=1)


def _row_slab(ref, row):
    return ref.at[pl.ds(pl.multiple_of(row * SLAB, SLAB), SLAB)]


def _split_bf16(x):
    def top_bits(v):
        bits = lax.bitcast_convert_type(v, jnp.uint32) & jnp.uint32(0xFFFF0000)
        return lax.bitcast_convert_type(bits, F32)
    hi = top_bits(x)
    mid = top_bits(x - hi)
    lo = (x - hi) - mid
    return hi.astype(BF16), mid.astype(BF16), lo.astype(BF16)


def _router_kernel(h_ref, wr_ref, br_ref, tri_ref, meta_ref, counts_ref, dest_ref, xs_hbm,
                   carry_ref, slab_a, slab_b, zero_ref, dest_v, dest_a, dest_b, count_v, count_s,
                   sem, pad_sem, idx_sem):
    i = pl.program_id(0)
    slabs = (slab_a, slab_b)
    dests = (dest_a, dest_b)

    @pl.when(i == 0)
    def _():
        carry_ref[...] = jnp.zeros_like(carry_ref)
        zero_ref[...] = jnp.zeros_like(zero_ref)

    def start_row(slot, t):
        for k in range(2):
            pltpu.make_async_copy(_row_slab(slabs[slot], t), _row_slab(xs_hbm, dests[slot][k, t]),
                                  sem.at[slot]).start(priority=k)

    def wait_scatter(slot):
        for _ in range(2):
            pltpu.make_async_copy(slabs[slot], xs_hbm.at[pl.ds(0, ROUTE_TILE * SLAB)], sem.at[slot]).wait()

    def route(slot, scatter_previous):
        pending = list(range(0, ROUTE_TILE, ROUTER_BURST)) if scatter_previous else []

        def burst():
            if pending:
                r0 = pending.pop(0)
                for t in range(r0, r0 + ROUTER_BURST):
                    start_row(1 - slot, t)

        lane = lax.broadcasted_iota(jnp.int32, (1, LANES), 1)
        lane_f = lane.astype(F32)
        h = h_ref[...]
        h0, h1, h2 = _split_bf16(h)
        w0, w1, w2 = wr_ref[0], wr_ref[1], wr_ref[2]
        small = None
        for a, b in ((h2, w0), (h0, w2), (h1, w1)):
            part = jnp.dot(a, b, preferred_element_type=F32)
            small = part if small is None else small + part
            burst()
        mid = None
        for a, b in ((h1, w0), (h0, w1)):
            part = jnp.dot(a, b, preferred_element_type=F32)
            mid = part if mid is None else mid + part
            burst()
        logits = (small + mid) + jnp.dot(h0, w0, preferred_element_type=F32) + br_ref[...]
        burst()
        logits = jnp.where(lane < N_EXPERTS, logits, -jnp.inf)
        m1 = jnp.max(logits, axis=-1, keepdims=True)
        i1 = jnp.min(jnp.where(logits == m1, lane_f, float(LANES)), axis=-1, keepdims=True)
        rest = jnp.where(lane_f == i1, -jnp.inf, logits)
        m2 = jnp.max(rest, axis=-1, keepdims=True)
        i2 = jnp.min(jnp.where(rest == m2, lane_f, float(LANES)), axis=-1, keepdims=True)
        e2 = jnp.exp(m2 - m1)
        tot = 1.0 + e2
        burst()
        chosen = jnp.where((lane_f == i1) | (lane_f == i2), 1.0, 0.0)
        before = jnp.dot(tri_ref[...], chosen.astype(BF16), preferred_element_type=F32) + carry_ref[...]
        r1 = jnp.sum(jnp.where(lane_f == i1, before, 0.0), axis=-1, keepdims=True)
        r2 = jnp.sum(jnp.where(lane_f == i2, before, 0.0), axis=-1, keepdims=True)
        new_carry = carry_ref[...] + jnp.sum(chosen, axis=0, keepdims=True)
        carry_ref[...] = new_carry
        counts_ref[...] = new_carry.astype(jnp.int32)
        burst()
        record = jnp.zeros((ROUTE_TILE, LANES), F32)
        for lane_id, col in ((META_E1, i1), (META_E2, i2), (META_R1, r1), (META_R2, r2),
                             (META_W1, 1.0 / tot), (META_W2, e2 / tot),
                             (META_D1, i1 * EXPERT_REGION + r1), (META_D2, i2 * EXPERT_REGION + r2)):
            record = jnp.where(lane == lane_id, col, record)
        meta_ref[...] = record
        _to_slabs(slabs[slot], h, ROUTE_TILE)
        while pending:
            burst()
        dest_rows = record.T[META_D1:META_D1 + 8, :].astype(jnp.int32)
        dest_ref[...] = dest_rows
        dest_v[...] = dest_rows
        to_smem = pltpu.make_async_copy(dest_v, dests[slot], idx_sem)
        to_smem.start()
        to_smem.wait()

    @pl.when(i == 0)
    def _():
        route(0, False)

    for slot in range(2):
        @pl.when((i > 0) & (i % 2 == slot))
        def _():
            @pl.when(i >= 2)
            def _():
                wait_scatter(slot)
            route(slot, True)

    @pl.when(i == N_ROUTE_TILES - 1)
    def _():
        slot = (N_ROUTE_TILES - 1) % 2
        lax.fori_loop(0, ROUTE_TILE, lambda t, c: (start_row(slot, t), c)[1], 0, unroll=8)
        wait_scatter(1 - slot)
        wait_scatter(slot)
        count_v[...] = jnp.broadcast_to(carry_ref[...].astype(jnp.int32), count_v.shape)
        cp = pltpu.make_async_copy(count_v, count_s, idx_sem)
        cp.start()
        cp.wait()
        for e in range(N_EXPERTS):
            n = count_s[0, e]
            end = (n + EXPERT_TILE - 1) // EXPERT_TILE * EXPERT_TILE
            pad = lambda r: pltpu.make_async_copy(zero_ref, _row_slab(xs_hbm, e * EXPERT_REGION + r), pad_sem)
            lax.fori_loop(n, end, lambda r, c: (pad(r).start(), c)[1], 0)
            lax.fori_loop(n, end, lambda r, c: (pad(r).wait(), c)[1], 0)


def _router(h, w_router, b_router):
    tri = (jnp.arange(ROUTE_TILE)[:, None] > jnp.arange(ROUTE_TILE)[None, :]).astype(BF16)
    w_split = jnp.stack(_split_bf16(w_router))
    return pl.pallas_call(
        _router_kernel,
        out_shape=[jax.ShapeDtypeStruct((N_TOK, LANES), F32),
                   jax.ShapeDtypeStruct((1, LANES), jnp.int32),
                   jax.ShapeDtypeStruct((8, N_TOK), jnp.int32),
                   jax.ShapeDtypeStruct((N_SORTED * SLAB, LANES), F32)],
        grid=(N_TOK // ROUTE_TILE,),
        in_specs=[_row_spec(ROUTE_TILE, D_MODEL), _const_spec(3, D_MODEL, LANES),
                  _const_spec(1, LANES), _const_spec(ROUTE_TILE, ROUTE_TILE)],
        out_specs=[_row_spec(ROUTE_TILE, LANES), _const_spec(1, LANES),
                   pl.BlockSpec((8, ROUTE_TILE), lambda i: (0, i)), pl.BlockSpec(memory_space=pl.ANY)],
        scratch_shapes=[pltpu.VMEM((1, LANES), F32),
                        pltpu.VMEM((ROUTE_TILE * SLAB, LANES), F32), pltpu.VMEM((ROUTE_TILE * SLAB, LANES), F32),
                        pltpu.VMEM((SLAB, LANES), F32),
                        pltpu.VMEM((8, ROUTE_TILE), jnp.int32),
                        pltpu.SMEM((8, ROUTE_TILE), jnp.int32), pltpu.SMEM((8, ROUTE_TILE), jnp.int32),
                        pltpu.VMEM((8, LANES), jnp.int32), pltpu.SMEM((8, LANES), jnp.int32),
                        pltpu.SemaphoreType.DMA((2,)), pltpu.SemaphoreType.DMA, pltpu.SemaphoreType.DMA],
        compiler_params=_cparams("arbitrary"),
        name="moe_router",
    )(h, w_split, b_router, tri)


def _experts_kernel(tblk_ref, texp_ref, xs_ref, wg_ref, wu_ref, wd_ref, ys_ref, wb_ref):
    i = pl.program_id(0)

    @pl.when((i == 0) | (texp_ref[i] != texp_ref[jnp.maximum(i - 1, 0)]))
    def _():
        for j, w_ref in enumerate((wg_ref, wu_ref, wd_ref)):
            wb_ref[j] = w_ref[0].astype(BF16)

    @pl.when((i == 0) | (tblk_ref[i] != tblk_ref[jnp.maximum(i - 1, 0)]))
    def _():
        x = _from_slabs(xs_ref, EXPERT_TILE).astype(BF16)
        gate = jnp.dot(x, wb_ref[0], preferred_element_type=F32)
        up = jnp.dot(x, wb_ref[1], preferred_element_type=F32)
        act = (jax.nn.silu(gate) * up).astype(BF16)
        _to_slabs(ys_ref, jnp.dot(act, wb_ref[2], preferred_element_type=F32), EXPERT_TILE)


def _experts(tile_block, tile_expert, xs, wg, wu, wd):
    wspec = lambda a, b: pl.BlockSpec((1, a, b), lambda i, tblk, texp: (texp[i], 0, 0))
    rows = pl.BlockSpec((EXPERT_TILE * SLAB, LANES), lambda i, tblk, texp: (tblk[i], 0))
    return pl.pallas_call(
        _experts_kernel,
        out_shape=jax.ShapeDtypeStruct((N_SORTED * SLAB, LANES), F32),
        grid_spec=pltpu.PrefetchScalarGridSpec(
            num_scalar_prefetch=2,
            grid=(N_EXPERT_TILES,),
            in_specs=[rows, wspec(D_MODEL, EXPERT_DIM), wspec(D_MODEL, EXPERT_DIM),
                      wspec(EXPERT_DIM, D_MODEL)],
            out_specs=rows,
            scratch_shapes=[pltpu.VMEM((3, D_MODEL, EXPERT_DIM), BF16)]),
        compiler_params=_cparams("arbitrary"),
        name="moe_experts",
    )(tile_block, tile_expert, xs, wg, wu, wd)


def _combine_ln_kernel(dest_ref, meta_ref, h_ref, ys_hbm, lg_ref, lb_ref, o_ref, ya_ref, yb_ref, sem):
    i = pl.program_id(0)
    bufs = (ya_ref, yb_ref)

    def start_gather(tile, slot):
        def body(t, carry):
            for k in range(2):
                row = dest_ref[k * N_TOK + tile * ROUTE_TILE + t]
                pltpu.make_async_copy(_row_slab(ys_hbm, row), _row_slab(bufs[slot], k * ROUTE_TILE + t),
                                      sem.at[slot]).start(priority=k)
            return carry
        lax.fori_loop(0, ROUTE_TILE, body, 0, unroll=8)

    def wait_gather(slot):
        pltpu.make_async_copy(ys_hbm.at[pl.ds(0, 2 * ROUTE_TILE * SLAB)], bufs[slot], sem.at[slot]).wait()

    last = pl.num_programs(0) - 1
    lane = lax.broadcasted_iota(jnp.int32, (1, LANES), 1)

    def finish_and_prefetch(slot):
        nxt = jnp.minimum(i + 1, last) * ROUTE_TILE
        wait_gather(slot)
        for r0 in range(0, ROUTE_TILE, COMBINE_CHUNK):
            rows = slice(r0, r0 + COMBINE_CHUNK)
            meta = meta_ref[rows, :]
            w1 = jnp.sum(jnp.where(lane == META_W1, meta, 0.0), axis=-1, keepdims=True)
            w2 = jnp.sum(jnp.where(lane == META_W2, meta, 0.0), axis=-1, keepdims=True)
            ff = (w1 * _from_slabs(bufs[slot], COMBINE_CHUNK, first_row=r0)
                  + w2 * _from_slabs(bufs[slot], COMBINE_CHUNK, first_row=ROUTE_TILE + r0))
            o_ref[rows, :] = _layer_norm(DEEPNORM_ALPHA * h_ref[rows, :] + ff, lg_ref[...], lb_ref[...])
            for t in range(r0, r0 + COMBINE_CHUNK):
                for k in range(2):
                    row = dest_ref[k * N_TOK + nxt + t]
                    pltpu.make_async_copy(_row_slab(ys_hbm, row),
                                          _row_slab(bufs[1 - slot], k * ROUTE_TILE + t),
                                          sem.at[1 - slot]).start(priority=k)

        @pl.when(i == last)
        def _():
            wait_gather(1 - slot)

    @pl.when(i == 0)
    def _():
        start_gather(0, 0)

    for slot in range(2):
        @pl.when(i % 2 == slot)
        def _():
            finish_and_prefetch(slot)


def _combine_ln(dest, meta, h, ys, lg, lb):
    row = lambda width: pl.BlockSpec((ROUTE_TILE, width), lambda i, dest: (i, 0))
    const = pl.BlockSpec((1, D_MODEL), lambda i, dest: (0, 0))
    return pl.pallas_call(
        _combine_ln_kernel,
        out_shape=jax.ShapeDtypeStruct((N_TOK, D_MODEL), F32),
        grid_spec=pltpu.PrefetchScalarGridSpec(
            num_scalar_prefetch=1,
            grid=(N_TOK // ROUTE_TILE,),
            in_specs=[row(LANES), row(D_MODEL), pl.BlockSpec(memory_space=pl.ANY), const, const],
            out_specs=row(D_MODEL),
            scratch_shapes=[pltpu.VMEM((2 * ROUTE_TILE * SLAB, LANES), F32)] * 2
                           + [pltpu.SemaphoreType.DMA((2,))]),
        compiler_params=_cparams("arbitrary"),
        name="moe_combine_ln",
    )(dest, meta, h, ys, lg, lb)


def _moe_ln(h, w_router, b_router, wg, wu, wd, lg, lb):
    meta, counts, dest_rows, xs = _router(h, w_router, b_router)
    expert = jnp.arange(N_EXPERTS)
    tiles = (counts[0, :N_EXPERTS] + EXPERT_TILE - 1) // EXPERT_TILE
    ends = jnp.sum(jnp.where(expert[None, :] <= expert[:, None], tiles[None, :], 0), axis=1)
    step = jnp.minimum(jnp.arange(N_EXPERT_TILES), ends[-1] - 1)
    tile_expert = jnp.sum(step[:, None] >= ends[None, :], axis=1)
    first = jnp.sum(jnp.where(tile_expert[:, None] == expert[None, :], (ends - tiles)[None, :], 0), axis=1)
    tile_block = tile_expert * REGION_TILES + (step - first)
    ys = _experts(tile_block.astype(jnp.int32), tile_expert.astype(jnp.int32), xs, wg, wu, wd)
    dest = dest_rows[:2].reshape(2 * N_TOK)
    return _combine_ln(dest, meta, h, ys, lg, lb)


def kernel(x, positions, ln_g, ln_b, ssm_lambda_re, ssm_lambda_im, ssm_log_step, ssm_b_re, ssm_b_im, ssm_c_re, ssm_c_im, ssm_d, ssm_w_glu, kv_w, attn_w_q, attn_sinks, attn_w_out, ffn_w_gate, ffn_w_up, ffn_w_down, moe_w_router, moe_b_router, moe_w_gate, moe_w_up, moe_w_down):
    ln = lambda layer, j: (ln_g[layer, j].reshape(1, D_MODEL).astype(F32),
                           ln_b[layer, j].reshape(1, D_MODEL).astype(F32))
    rope_table = _rope_tables(positions)
    x2 = x.reshape(N_TOK, D_MODEL)

    params = _ssm_params(ssm_lambda_re[0], ssm_lambda_im[0], ssm_log_step[0], ssm_b_re[0], ssm_b_im[0],
                         ssm_c_re[0], ssm_c_im[0])
    y = _ssm_mixer(x2, params)
    h = _glu_ln(y, x2, ssm_d[0].astype(F32).reshape(1, D_MODEL), ssm_w_glu[0].astype(BF16), *ln(0, 0))
    h = _ffn_ln(h, ffn_w_gate[0].astype(BF16), ffn_w_up[0].astype(BF16), ffn_w_down[0].astype(BF16),
                *ln(0, 1))

    order = jnp.array(HEAD_ORDER)
    w_q = attn_w_q[0].reshape(D_MODEL, N_HEADS, HEAD_DIM)[:, order].reshape(D_MODEL, Q_DIM)
    w_out = attn_w_out[0].reshape(N_HEADS, HEAD_DIM, D_MODEL)[order].reshape(Q_DIM, D_MODEL)
    w_qkv = jnp.concatenate([w_q, kv_w], axis=1).astype(BF16)
    q, k, v = _qkv(h, w_qkv, rope_table)
    h = _attention_ln(q, k, v, attn_sinks[0].astype(F32), h, w_out.astype(BF16), *ln(1, 0))
    w_router = jnp.pad(moe_w_router[0].astype(F32), ((0, 0), (0, LANES - N_EXPERTS)))
    b_router = jnp.pad(moe_b_router[0].astype(F32), (0, LANES - N_EXPERTS)).reshape(1, LANES)
    h = _moe_ln(h, w_router, b_router, moe_w_gate[0].astype(F32), moe_w_up[0].astype(F32),
                moe_w_down[0].astype(F32), *ln(1, 1))
    return h.reshape(BATCH, SEQ, D_MODEL)
```

```python
import math

import jax
import jax.numpy as jnp
import numpy as np
from jax import lax
from jax.experimental import pallas as pl
from jax.experimental.pallas import tpu as pltpu

F32 = jnp.float32
BF16 = jnp.bfloat16

D_MODEL = 1024
BATCH = 16
SEQ = 2048
N_TOK = BATCH * SEQ
DEPTH = 2

SSM_GROUP_CH = 16
SSM_GROUPS = D_MODEL // SSM_GROUP_CH
SSM_STATE = 64
SSM_CHUNK = 16
SSM_NCHUNK = SEQ // SSM_CHUNK
SSM_WIDTH = SSM_CHUNK * SSM_GROUP_CH
SSM_SCAN_STEPS = int(math.log2(SSM_NCHUNK))

N_HEADS = 16
HEAD_DIM = 64
N_KV_HEADS = 4
Q_PER_KV = N_HEADS // N_KV_HEADS
KV_DIM = N_KV_HEADS * HEAD_DIM
WINDOW = 128
BLOCK = 128
ROT_DIM = HEAD_DIM // 4
ROT_HALF = ROT_DIM // 2
ROPE_THETA = 500000.0

FFN_DIM = 2816
N_EXPERTS = 8
EXPERT_DIM = 1024

DEEPNORM_ALPHA = (2 * DEPTH) ** 0.25
LOG2_E = math.log2(math.e)
LN_EPS = 1e-5

LANES = 128
VMEM_LIMIT = 56 * 1024 * 1024


def _cparams(*sem):
    return pltpu.CompilerParams(dimension_semantics=sem, vmem_limit_bytes=VMEM_LIMIT)


def _layer_norm(r, g, b):
    mu = jnp.mean(r, axis=-1, keepdims=True)
    xc = r - mu
    var = jnp.mean(xc * xc, axis=-1, keepdims=True)
    return xc * lax.rsqrt(var + LN_EPS) * g + b


def _rope_kernel(invf_ref, pos_ref, cs_ref):
    pos = pos_ref[...]
    for f in range(ROT_HALF):
        ang = pos * invf_ref[f]
        cs_ref[f] = jnp.cos(ang)
        cs_ref[ROT_HALF + f] = jnp.sin(ang)


def _rope_tables(positions):
    inv_freq = ROPE_THETA ** (-jnp.arange(0, ROT_DIM, 2, dtype=F32) / ROT_DIM)
    pos = positions.astype(F32)
    cs = pl.pallas_call(
        _rope_kernel,
        out_shape=jax.ShapeDtypeStruct((ROT_DIM, BATCH, SEQ), F32),
        in_specs=[pl.BlockSpec(memory_space=pltpu.SMEM),
                  pl.BlockSpec(memory_space=pltpu.VMEM)],
        out_specs=pl.BlockSpec(memory_space=pltpu.VMEM),
        name="rope_tables",
    )(inv_freq, pos)
    return cs.reshape(ROT_DIM, N_TOK).T


def _rope_placement():
    f = np.arange(ROT_HALF)
    place = np.zeros((ROT_DIM, LANES), np.float32)
    place[f, f] = place[f, ROT_HALF + f] = 1.0
    place[ROT_HALF + f, HEAD_DIM + f] = -1.0
    place[ROT_HALF + f, HEAD_DIM + ROT_HALF + f] = 1.0
    lane = np.arange(LANES)
    bias = ((lane >= ROT_DIM) & (lane < HEAD_DIM)).astype(np.float32).reshape(1, LANES)
    return jnp.asarray(place, BF16), jnp.asarray(bias)


def _ssm_params(lam_re, lam_im, log_step, b_re, b_im, c_re, c_im):
    hp = lax.Precision.HIGHEST
    lr, li = lam_re.astype(F32), lam_im.astype(F32)
    dt = jnp.exp(log_step.astype(F32))[:, None]
    mag = jnp.exp(lr * dt)
    ar = mag * jnp.cos(li * dt)
    ai = mag * jnp.sin(li * dt)
    nr = ar - 1.0
    den = lr * lr + li * li
    kr = (nr * lr + ai * li) / den
    ki = (ai * lr - nr * li) / den
    br, bi = b_re.astype(F32), b_im.astype(F32)
    bbar_r = kr[..., None] * br - ki[..., None] * bi
    bbar_i = kr[..., None] * bi + ki[..., None] * br
    cr, ci = c_re.astype(F32), c_im.astype(F32)

    def powers(taus):
        t = taus.astype(F32)[:, None, None]
        m = jnp.exp(lr[None] * dt[None] * t)
        ang = li[None] * dt[None] * t
        return m * jnp.cos(ang), m * jnp.sin(ang)

    er, ei = powers(jnp.arange(SSM_CHUNK + 1))
    w_r = er[:, :, :, None] * bbar_r[None] - ei[:, :, :, None] * bbar_i[None]
    w_i = er[:, :, :, None] * bbar_i[None] + ei[:, :, :, None] * bbar_r[None]
    kern = (jnp.einsum('gcp,tgpd->gtcd', cr, w_r[:SSM_CHUNK], precision=hp)
            - jnp.einsum('gcp,tgpd->gtcd', ci, w_i[:SSM_CHUNK], precision=hp))
    s_idx = jnp.arange(SSM_CHUNK)[:, None, None]
    t_idx = jnp.arange(SSM_CHUNK)[None, :, None]
    lag = (t_idx - s_idx == jnp.arange(SSM_CHUNK)[None, None, :]).astype(BF16)
    toep = jnp.einsum('stu,gucd->gtcsd', lag, kern.astype(BF16), preferred_element_type=BF16)
    mt = toep.reshape(SSM_GROUPS, SSM_WIDTH, SSM_WIDTH)
    rev = SSM_CHUNK - 1 - jnp.arange(SSM_CHUNK)
    g_r = w_r[rev].transpose(1, 2, 0, 3).reshape(SSM_GROUPS, SSM_STATE, SSM_WIDTH)
    g_i = w_i[rev].transpose(1, 2, 0, 3).reshape(SSM_GROUPS, SSM_STATE, SSM_WIDTH)
    gt = jnp.concatenate([g_r, g_i], axis=1)
    e1r, e1i = er[1:], ei[1:]
    ce_r = cr[None] * e1r[:, :, None, :] - ci[None] * e1i[:, :, None, :]
    ce_i = cr[None] * e1i[:, :, None, :] + ci[None] * e1r[:, :, None, :]
    c_re = ce_r.transpose(1, 0, 2, 3).reshape(SSM_GROUPS, SSM_WIDTH, SSM_STATE)
    c_im = (-ce_i).transpose(1, 0, 2, 3).reshape(SSM_GROUPS, SSM_WIDTH, SSM_STATE)
    ct = jnp.concatenate([c_re, c_im], axis=-1)
    sr, si = powers(SSM_CHUNK * (2 ** jnp.arange(8)))
    coef = lambda t: jnp.concatenate([t, t], axis=-1).transpose(1, 0, 2)
    return mt.astype(BF16), gt.astype(BF16), ct.astype(BF16), coef(sr), coef(si)


SSM_LANE_GROUPS = LANES // SSM_GROUP_CH
SSM_LANE_CHUNKS = D_MODEL // LANES
SSM_BATCH_TILE = 4
SSM_COLS = SSM_BATCH_TILE * SSM_NCHUNK


def _ssm_kernel(x_ref, mt_ref, gt_ref, ct_ref, cr_ref, ci_ref, y_ref, v_ref, yt_ref):
    for bl in range(SSM_BATCH_TILE):
        cols = slice(bl * SSM_NCHUNK, (bl + 1) * SSM_NCHUNK)
        for s in range(SSM_CHUNK):
            a = x_ref[pl.ds(bl * SEQ + s, SSM_NCHUNK, stride=SSM_CHUNK), :]
            at = a.T.astype(BF16)
            for g in range(SSM_LANE_GROUPS):
                v_ref[g, s * SSM_GROUP_CH:(s + 1) * SSM_GROUP_CH, cols] = (
                    at[g * SSM_GROUP_CH:(g + 1) * SSM_GROUP_CH])

    row_j = lax.broadcasted_iota(jnp.int32, (SSM_NCHUNK, 1), 0)

    def group_body(g, carry):
        ut = v_ref[g]
        xt = jnp.dot(gt_ref[g], ut, preferred_element_type=F32)
        cr_all = cr_ref[g]
        ci_all = ci_ref[g]

        def seg(rows, b):
            return xt[rows, b * SSM_NCHUNK:(b + 1) * SSM_NCHUNK]

        def shifted(x, d):
            if d < 8:
                return jnp.where(row_j >= d, pltpu.roll(x, d, 0), 0.0)
            return jnp.concatenate([jnp.zeros((d, LANES), F32), x[:SSM_NCHUNK - d]], axis=0)

        re_rows, im_rows = slice(0, SSM_STATE), slice(SSM_STATE, 2 * SSM_STATE)
        sr, si = [], []
        for q in range(SSM_BATCH_TILE // 2):
            xr = jnp.concatenate([seg(re_rows, 2 * q), seg(re_rows, 2 * q + 1)], axis=0).T
            xi = jnp.concatenate([seg(im_rows, 2 * q), seg(im_rows, 2 * q + 1)], axis=0).T
            for k in range(SSM_SCAN_STEPS):
                cr = cr_all[k:k + 1, :]
                ci = ci_all[k:k + 1, :]
                rr = shifted(xr, 1 << k)
                ri = shifted(xi, 1 << k)
                xr, xi = xr + cr * rr - ci * ri, xi + cr * ri + ci * rr
            pr = shifted(xr, 1).T
            pi = shifted(xi, 1).T
            sr += [pr[:SSM_STATE], pr[SSM_STATE:]]
            si += [pi[:SSM_STATE], pi[SSM_STATE:]]
        sprev = jnp.concatenate([jnp.concatenate(sr, axis=1), jnp.concatenate(si, axis=1)],
                                axis=0).astype(BF16)
        yt_ref[g] = (jnp.dot(mt_ref[g], ut, preferred_element_type=F32)
                     + jnp.dot(ct_ref[g], sprev, preferred_element_type=F32))
        return carry

    lax.fori_loop(0, SSM_LANE_GROUPS, group_body, 0, unroll=4)

    for bl in range(SSM_BATCH_TILE):
        cols = slice(bl * SSM_NCHUNK, (bl + 1) * SSM_NCHUNK)
        for t in range(SSM_CHUNK):
            tile = jnp.concatenate(
                [yt_ref[g, t * SSM_GROUP_CH:(t + 1) * SSM_GROUP_CH, cols] for g in range(SSM_LANE_GROUPS)],
                axis=0)
            y_ref[pl.ds(bl * SEQ + t, SSM_NCHUNK, stride=SSM_CHUNK), :] = tile.T


def _ssm_mixer(x2, params):
    mt, gt, ct, coef_r, coef_i = params
    rows = SSM_BATCH_TILE * SEQ
    pspec = lambda a, b: pl.BlockSpec((SSM_LANE_GROUPS, a, b), lambda k, q: (k, 0, 0))
    return pl.pallas_call(
        _ssm_kernel,
        out_shape=jax.ShapeDtypeStruct((N_TOK, D_MODEL), F32),
        grid=(SSM_LANE_CHUNKS, BATCH // SSM_BATCH_TILE),
        in_specs=[pl.BlockSpec((rows, LANES), lambda k, q: (q, k)),
                  pspec(SSM_WIDTH, SSM_WIDTH), pspec(2 * SSM_STATE, SSM_WIDTH), pspec(SSM_WIDTH, 2 * SSM_STATE),
                  pspec(8, 2 * SSM_STATE), pspec(8, 2 * SSM_STATE)],
        out_specs=pl.BlockSpec((rows, LANES), lambda k, q: (q, k)),
        scratch_shapes=[pltpu.VMEM((SSM_LANE_GROUPS, SSM_WIDTH, SSM_COLS), BF16),
                        pltpu.VMEM((SSM_LANE_GROUPS, SSM_WIDTH, SSM_COLS), F32)],
        compiler_params=_cparams("parallel", "parallel"),
        name="ssm_mixer",
    )(x2, mt, gt, ct, coef_r, coef_i)


GLU_COLS = 256


def _glu_ln_kernel(y_ref, x_ref, d_ref, w_ref, lg_ref, lb_ref, o_ref):
    x = x_ref[...]
    act = jax.nn.gelu(y_ref[...] + d_ref[...] * x).astype(BF16)
    parts = []
    for c in range(0, D_MODEL, GLU_COLS):
        val = jnp.dot(act, w_ref[:, c:c + GLU_COLS], preferred_element_type=F32)
        gate = jnp.dot(act, w_ref[:, D_MODEL + c:D_MODEL + c + GLU_COLS], preferred_element_type=F32)
        parts.append(val * jax.nn.sigmoid(gate))
    mix = jnp.concatenate(parts, axis=1)
    o_ref[...] = _layer_norm(DEEPNORM_ALPHA * x + mix, lg_ref[...], lb_ref[...])


def _row_spec(tm, width):
    return pl.BlockSpec((tm, width), lambda i: (i, 0))


def _const_spec(*shape):
    return pl.BlockSpec(shape, lambda *_: (0,) * len(shape))


def _glu_ln(y, x2, d_skip, w_glu, lg, lb, tm=1024):
    return pl.pallas_call(
        _glu_ln_kernel,
        out_shape=jax.ShapeDtypeStruct((N_TOK, D_MODEL), F32),
        grid=(N_TOK // tm,),
        in_specs=[_row_spec(tm, D_MODEL), _row_spec(tm, D_MODEL), _const_spec(1, D_MODEL),
                  _const_spec(D_MODEL, 2 * D_MODEL), _const_spec(1, D_MODEL), _const_spec(1, D_MODEL)],
        out_specs=_row_spec(tm, D_MODEL),
        compiler_params=_cparams("parallel"),
        name="glu_ln",
    )(y, x2, d_skip, w_glu, lg, lb)


FFN_SPLIT = 2
FFN_TILE = FFN_DIM // FFN_SPLIT


def _ffn_ln_kernel(h_ref, wg_ref, wu_ref, wd_ref, lg_ref, lb_ref, o_ref):
    h = h_ref[...]
    hb = h.astype(BF16)
    ff = None
    for f in range(FFN_SPLIT):
        cols = slice(f * FFN_TILE, (f + 1) * FFN_TILE)
        gate = jnp.dot(hb, wg_ref[:, cols], preferred_element_type=F32)
        up = jnp.dot(hb, wu_ref[:, cols], preferred_element_type=F32)
        act = (jax.nn.silu(gate) * up).astype(BF16)
        part = jnp.dot(act, wd_ref[cols, :], preferred_element_type=F32)
        ff = part if ff is None else ff + part
    o_ref[...] = _layer_norm(DEEPNORM_ALPHA * h + ff, lg_ref[...], lb_ref[...])


def _resident_spec(*shape):
    return pl.BlockSpec(shape, lambda *_: (0,) * len(shape), pipeline_mode=pl.Buffered(1))


def _ffn_ln(h, wg, wu, wd, lg, lb, tm=512):
    return pl.pallas_call(
        _ffn_ln_kernel,
        out_shape=jax.ShapeDtypeStruct((N_TOK, D_MODEL), F32),
        grid=(N_TOK // tm,),
        in_specs=[_row_spec(tm, D_MODEL),
                  _resident_spec(D_MODEL, FFN_DIM), _resident_spec(D_MODEL, FFN_DIM),
                  _resident_spec(FFN_DIM, D_MODEL),
                  _const_spec(1, D_MODEL), _const_spec(1, D_MODEL)],
        out_specs=_row_spec(tm, D_MODEL),
        compiler_params=_cparams("parallel"),
        name="ffn_ln",
    )(h, wg, wu, wd, lg, lb)


Q_DIM = N_HEADS * HEAD_DIM


def _qkv_kernel(h_ref, w_ref, cs_ref, place_ref, bias_ref, q_ref, k_ref, v_ref):
    hb = h_ref[...].astype(BF16)
    z = jnp.dot(hb, w_ref[...], preferred_element_type=F32)
    lane = lax.broadcasted_iota(jnp.int32, (1, LANES), 1)
    table = bias_ref[...]
    for term in _split_bf16(cs_ref[...]):
        table = table + jnp.dot(term, place_ref[...], preferred_element_type=F32)
    swapped = pltpu.roll(table, HEAD_DIM, 1)
    cos_t = jnp.where(lane < HEAD_DIM, table, swapped)
    sin_t = jnp.where(lane < HEAD_DIM, swapped, table)
    first_half = (lane % HEAD_DIM) < ROT_HALF

    def rope(t):
        partner = jnp.where(first_half, pltpu.roll(t, LANES - ROT_HALF, 1), pltpu.roll(t, ROT_HALF, 1))
        return t * cos_t + partner * sin_t

    scale = HEAD_DIM ** -0.5 * LOG2_E
    for c in range(Q_DIM // LANES):
        sl = slice(c * LANES, (c + 1) * LANES)
        q_ref[:, sl] = (rope(z[:, sl]) * scale).astype(BF16)
    for c in range(KV_DIM // LANES):
        sl = slice(c * LANES, (c + 1) * LANES)
        k_ref[:, sl] = rope(z[:, Q_DIM + c * LANES:Q_DIM + (c + 1) * LANES]).astype(BF16)
    v_ref[...] = z[:, Q_DIM + KV_DIM:].astype(BF16)


def _qkv(h, w_qkv, rope_table, tm=1024):
    return pl.pallas_call(
        _qkv_kernel,
        out_shape=[jax.ShapeDtypeStruct((N_TOK, Q_DIM), BF16),
                   jax.ShapeDtypeStruct((N_TOK, KV_DIM), BF16),
                   jax.ShapeDtypeStruct((N_TOK, KV_DIM), BF16)],
        grid=(N_TOK // tm,),
        in_specs=[_row_spec(tm, D_MODEL), _const_spec(D_MODEL, Q_DIM + 2 * KV_DIM),
                  _row_spec(tm, ROT_DIM), _const_spec(ROT_DIM, LANES), _const_spec(1, LANES)],
        out_specs=[_row_spec(tm, Q_DIM), _row_spec(tm, KV_DIM), _row_spec(tm, KV_DIM)],
        compiler_params=_cparams("parallel"),
        name="qkv_rope",
    )(h, w_qkv, rope_table, *_rope_placement())


ATTN_Q_TILE = 512
ATTN_SUB = ATTN_Q_TILE // BLOCK
ATTN_BAND = ATTN_Q_TILE + BLOCK
KV_PAIRS = N_KV_HEADS // 2

HEAD_ORDER = [h for c in range(KV_PAIRS) for g in range(Q_PER_KV)
              for h in (2 * c * Q_PER_KV + g, (2 * c + 1) * Q_PER_KV + g)]


def _band_bias():
    qi = jnp.arange(Q_PER_KV * BLOCK)[:, None] % BLOCK
    si = jnp.arange(2 * BLOCK)[None, :]
    rel = qi + BLOCK - si
    valid = (rel >= 0) & (rel < WINDOW)
    first = valid & (si >= BLOCK)
    neg = jnp.float32(-jnp.inf)
    return jnp.stack([jnp.where(valid, 0.0, neg), jnp.where(first, 0.0, neg)]).astype(F32)


def _attn_kernel(sink_ref, q_ref, kp_ref, kc_ref, vp_ref, vc_ref, bias_ref, h_ref, w_ref, lg_ref, lb_ref,
                 out_ref, ka_ref, kb_ref, va_ref, vb_ref, o_ref):
    i = pl.program_id(1)
    lane = lax.broadcasted_iota(jnp.int32, (1, KV_DIM), 1)
    low = (lane % LANES) < HEAD_DIM
    zero = jnp.zeros((), BF16)
    for src, dst_a, dst_b in ((kp_ref, ka_ref, kb_ref), (vp_ref, va_ref, vb_ref)):
        t = src[...]
        dst_a[0:BLOCK, :] = jnp.where(low, t, zero)
        dst_b[0:BLOCK, :] = jnp.where(low, zero, t)
    for src, dst_a, dst_b in ((kc_ref, ka_ref, kb_ref), (vc_ref, va_ref, vb_ref)):
        t = src[...]
        dst_a[BLOCK:ATTN_BAND, :] = jnp.where(low, t, zero)
        dst_b[BLOCK:ATTN_BAND, :] = jnp.where(low, zero, t)

    lane1 = lax.broadcasted_iota(jnp.int32, (1, LANES), 1)
    row = lax.broadcasted_iota(jnp.int32, (Q_PER_KV * BLOCK, 1), 0)
    contract_last = (((1,), (1,)), ((), ()))
    for blk in range(ATTN_SUB):
        r0 = blk * BLOCK
        if blk == 0:
            bias = bias_ref[jnp.where(i == 0, 1, 0)]
        else:
            bias = bias_ref[0]
        for c in range(KV_PAIRS):
            cs = slice(c * LANES, (c + 1) * LANES)
            chunks = [c * Q_PER_KV + g for g in range(Q_PER_KV)]
            q4 = jnp.concatenate([q_ref[r0:r0 + BLOCK, m * LANES:(m + 1) * LANES] for m in chunks], axis=0)
            outs = []
            for half, (k_ref, v_ref) in enumerate(((ka_ref, va_ref), (kb_ref, vb_ref))):
                kband = k_ref[r0:r0 + 2 * BLOCK, cs]
                s = lax.dot_general(q4, kband, contract_last, preferred_element_type=F32) + bias
                sink = jnp.zeros((Q_PER_KV * BLOCK, 1), F32)
                for g, m in enumerate(chunks):
                    sink = jnp.where(row // BLOCK == g, sink_ref[HEAD_ORDER[2 * m + half]] * LOG2_E, sink)
                mx = jnp.maximum(jnp.max(s, axis=-1, keepdims=True), sink)
                p = jnp.exp2(s - mx)
                denom = jnp.sum(p, axis=-1, keepdims=True) + jnp.exp2(sink - mx)
                pv = jnp.dot(p.astype(BF16), v_ref[r0:r0 + 2 * BLOCK, cs], preferred_element_type=F32)
                outs.append((pv, 1.0 / denom))
            (pv_a, r_a), (pv_b, r_b) = outs
            o = (pv_a + pv_b) * jnp.where(lane1 < HEAD_DIM, r_a, r_b)
            for g, m in enumerate(chunks):
                o_ref[r0:r0 + BLOCK, m * LANES:(m + 1) * LANES] = o[g * BLOCK:(g + 1) * BLOCK].astype(BF16)

    mix = jnp.dot(o_ref[...], w_ref[...], preferred_element_type=F32)
    out_ref[...] = _layer_norm(DEEPNORM_ALPHA * h_ref[...] + mix, lg_ref[...], lb_ref[...])


def _attention_ln(q, k, v, sinks, h, w_out, lg, lb):
    tiles = SEQ // ATTN_Q_TILE
    cur = lambda b, i: (b * tiles + i, 0)
    prev = lambda b, i: (jnp.maximum((b * tiles + i) * ATTN_SUB - 1, 0), 0)
    const2 = lambda b, i: (0, 0)
    return pl.pallas_call(
        _attn_kernel,
        out_shape=jax.ShapeDtypeStruct((N_TOK, D_MODEL), F32),
        grid=(BATCH, tiles),
        in_specs=[pl.BlockSpec(memory_space=pltpu.SMEM),
                  pl.BlockSpec((ATTN_Q_TILE, Q_DIM), cur),
                  pl.BlockSpec((BLOCK, KV_DIM), prev), pl.BlockSpec((ATTN_Q_TILE, KV_DIM), cur),
                  pl.BlockSpec((BLOCK, KV_DIM), prev), pl.BlockSpec((ATTN_Q_TILE, KV_DIM), cur),
                  pl.BlockSpec((2, Q_PER_KV * BLOCK, 2 * BLOCK), lambda b, i: (0, 0, 0)),
                  pl.BlockSpec((ATTN_Q_TILE, D_MODEL), cur),
                  pl.BlockSpec((Q_DIM, D_MODEL), const2),
                  pl.BlockSpec((1, D_MODEL), const2), pl.BlockSpec((1, D_MODEL), const2)],
        out_specs=pl.BlockSpec((ATTN_Q_TILE, D_MODEL), cur),
        scratch_shapes=[pltpu.VMEM((ATTN_BAND, KV_DIM), BF16)] * 4
                       + [pltpu.VMEM((ATTN_Q_TILE, Q_DIM), BF16)],
        compiler_params=_cparams("parallel", "parallel"),
        name="swa_attention_ln",
    )(sinks, q, k, k, v, v, _band_bias(), h, w_out, lg, lb)


ROUTE_TILE = 512
EXPERT_TILE = 512
SLAB = D_MODEL // LANES
EXPERT_REGION = N_TOK
REGION_TILES = EXPERT_REGION // EXPERT_TILE
N_SORTED = N_EXPERTS * EXPERT_REGION
N_EXPERT_TILES = 2 * N_TOK // EXPERT_TILE + N_EXPERTS
META_E1, META_E2, META_R1, META_R2, META_W1, META_W2 = range(6)
META_D1, META_D2 = 8, 9
COMBINE_CHUNK = 64
N_ROUTE_TILES = N_TOK // ROUTE_TILE
ROUTER_BURST = 64


def _to_slabs(ref, val, rows):
    for c in range(SLAB):
        ref[pl.ds(c, rows, stride=SLAB), :] = val[:, c * LANES:(c + 1) * LANES]


def _from_slabs(ref, rows, first_row=0):
    return jnp.concatenate([ref[pl.ds(first_row * SLAB + c, rows, stride=SLAB), :] for c in range(SLAB)],
                           axis=1)


def _row_slab(ref, row):
    return ref.at[pl.ds(pl.multiple_of(row * SLAB, SLAB), SLAB)]


def _split_bf16(x):
    def top_bits(v):
        bits = lax.bitcast_convert_type(v, jnp.uint32) & jnp.uint32(0xFFFF0000)
        return lax.bitcast_convert_type(bits, F32)
    hi = top_bits(x)
    mid = top_bits(x - hi)
    lo = (x - hi) - mid
    return hi.astype(BF16), mid.astype(BF16), lo.astype(BF16)


def _router_kernel(h_ref, wr_ref, br_ref, tri_ref, meta_ref, counts_ref, dest_ref, xs_hbm,
                   carry_ref, slab_a, slab_b, zero_ref, dest_v, dest_a, dest_b, count_v, count_s,
                   sem, pad_sem, idx_sem):
    i = pl.program_id(0)
    slabs = (slab_a, slab_b)
    dests = (dest_a, dest_b)

    @pl.when(i == 0)
    def _():
        carry_ref[...] = jnp.zeros_like(carry_ref)
        zero_ref[...] = jnp.zeros_like(zero_ref)

    def start_row(slot, t):
        for k in range(2):
            pltpu.make_async_copy(_row_slab(slabs[slot], t), _row_slab(xs_hbm, dests[slot][k, t]),
                                  sem.at[slot]).start(priority=k)

    def wait_scatter(slot):
        for _ in range(2):
            pltpu.make_async_copy(slabs[slot], xs_hbm.at[pl.ds(0, ROUTE_TILE * SLAB)], sem.at[slot]).wait()

    def route(slot, scatter_previous):
        pending = list(range(0, ROUTE_TILE, ROUTER_BURST)) if scatter_previous else []

        def burst():
            if pending:
                r0 = pending.pop(0)
                for t in range(r0, r0 + ROUTER_BURST):
                    start_row(1 - slot, t)

        lane = lax.broadcasted_iota(jnp.int32, (1, LANES), 1)
        lane_f = lane.astype(F32)
        h = h_ref[...]
        h0, h1, h2 = _split_bf16(h)
        w0, w1, w2 = wr_ref[0], wr_ref[1], wr_ref[2]
        small = None
        for a, b in ((h2, w0), (h0, w2), (h1, w1)):
            part = jnp.dot(a, b, preferred_element_type=F32)
            small = part if small is None else small + part
            burst()
        mid = None
        for a, b in ((h1, w0), (h0, w1)):
            part = jnp.dot(a, b, preferred_element_type=F32)
            mid = part if mid is None else mid + part
            burst()
        logits = (small + mid) + jnp.dot(h0, w0, preferred_element_type=F32) + br_ref[...]
        burst()
        logits = jnp.where(lane < N_EXPERTS, logits, -jnp.inf)
        m1 = jnp.max(logits, axis=-1, keepdims=True)
        i1 = jnp.min(jnp.where(logits == m1, lane_f, float(LANES)), axis=-1, keepdims=True)
        rest = jnp.where(lane_f == i1, -jnp.inf, logits)
        m2 = jnp.max(rest, axis=-1, keepdims=True)
        i2 = jnp.min(jnp.where(rest == m2, lane_f, float(LANES)), axis=-1, keepdims=True)
        e2 = jnp.exp(m2 - m1)
        tot = 1.0 + e2
        burst()
        chosen = jnp.where((lane_f == i1) | (lane_f == i2), 1.0, 0.0)
        before = jnp.dot(tri_ref[...], chosen.astype(BF16), preferred_element_type=F32) + carry_ref[...]
        r1 = jnp.sum(jnp.where(lane_f == i1, before, 0.0), axis=-1, keepdims=True)
        r2 = jnp.sum(jnp.where(lane_f == i2, before, 0.0), axis=-1, keepdims=True)
        new_carry = carry_ref[...] + jnp.sum(chosen, axis=0, keepdims=True)
        carry_ref[...] = new_carry
        counts_ref[...] = new_carry.astype(jnp.int32)
        burst()
        record = jnp.zeros((ROUTE_TILE, LANES), F32)
        for lane_id, col in ((META_E1, i1), (META_E2, i2), (META_R1, r1), (META_R2, r2),
                             (META_W1, 1.0 / tot), (META_W2, e2 / tot),
                             (META_D1, i1 * EXPERT_REGION + r1), (META_D2, i2 * EXPERT_REGION + r2)):
            record = jnp.where(lane == lane_id, col, record)
        meta_ref[...] = record
        _to_slabs(slabs[slot], h, ROUTE_TILE)
        while pending:
            burst()
        dest_rows = record.T[META_D1:META_D1 + 8, :].astype(jnp.int32)
        dest_ref[...] = dest_rows
        dest_v[...] = dest_rows
        to_smem = pltpu.make_async_copy(dest_v, dests[slot], idx_sem)
        to_smem.start()
        to_smem.wait()

    @pl.when(i == 0)
    def _():
        route(0, False)

    for slot in range(2):
        @pl.when((i > 0) & (i % 2 == slot))
        def _():
            @pl.when(i >= 2)
            def _():
                wait_scatter(slot)
            route(slot, True)

    @pl.when(i == N_ROUTE_TILES - 1)
    def _():
        slot = (N_ROUTE_TILES - 1) % 2
        lax.fori_loop(0, ROUTE_TILE, lambda t, c: (start_row(slot, t), c)[1], 0, unroll=8)
        wait_scatter(1 - slot)
        wait_scatter(slot)
        count_v[...] = jnp.broadcast_to(carry_ref[...].astype(jnp.int32), count_v.shape)
        cp = pltpu.make_async_copy(count_v, count_s, idx_sem)
        cp.start()
        cp.wait()
        for e in range(N_EXPERTS):
            n = count_s[0, e]
            end = (n + EXPERT_TILE - 1) // EXPERT_TILE * EXPERT_TILE
            pad = lambda r: pltpu.make_async_copy(zero_ref, _row_slab(xs_hbm, e * EXPERT_REGION + r), pad_sem)
            lax.fori_loop(n, end, lambda r, c: (pad(r).start(), c)[1], 0)
            lax.fori_loop(n, end, lambda r, c: (pad(r).wait(), c)[1], 0)


def _router(h, w_router, b_router):
    tri = (jnp.arange(ROUTE_TILE)[:, None] > jnp.arange(ROUTE_TILE)[None, :]).astype(BF16)
    w_split = jnp.stack(_split_bf16(w_router))
    return pl.pallas_call(
        _router_kernel,
        out_shape=[jax.ShapeDtypeStruct((N_TOK, LANES), F32),
                   jax.ShapeDtypeStruct((1, LANES), jnp.int32),
                   jax.ShapeDtypeStruct((8, N_TOK), jnp.int32),
                   jax.ShapeDtypeStruct((N_SORTED * SLAB, LANES), F32)],
        grid=(N_TOK // ROUTE_TILE,),
        in_specs=[_row_spec(ROUTE_TILE, D_MODEL), _const_spec(3, D_MODEL, LANES),
                  _const_spec(1, LANES), _const_spec(ROUTE_TILE, ROUTE_TILE)],
        out_specs=[_row_spec(ROUTE_TILE, LANES), _const_spec(1, LANES),
                   pl.BlockSpec((8, ROUTE_TILE), lambda i: (0, i)), pl.BlockSpec(memory_space=pl.ANY)],
        scratch_shapes=[pltpu.VMEM((1, LANES), F32),
                        pltpu.VMEM((ROUTE_TILE * SLAB, LANES), F32), pltpu.VMEM((ROUTE_TILE * SLAB, LANES), F32),
                        pltpu.VMEM((SLAB, LANES), F32),
                        pltpu.VMEM((8, ROUTE_TILE), jnp.int32),
                        pltpu.SMEM((8, ROUTE_TILE), jnp.int32), pltpu.SMEM((8, ROUTE_TILE), jnp.int32),
                        pltpu.VMEM((8, LANES), jnp.int32), pltpu.SMEM((8, LANES), jnp.int32),
                        pltpu.SemaphoreType.DMA((2,)), pltpu.SemaphoreType.DMA, pltpu.SemaphoreType.DMA],
        compiler_params=_cparams("arbitrary"),
        name="moe_router",
    )(h, w_split, b_router, tri)


def _experts_kernel(tblk_ref, texp_ref, xs_ref, wg_ref, wu_ref, wd_ref, ys_ref, wb_ref):
    i = pl.program_id(0)

    @pl.when((i == 0) | (texp_ref[i] != texp_ref[jnp.maximum(i - 1, 0)]))
    def _():
        for j, w_ref in enumerate((wg_ref, wu_ref, wd_ref)):
            wb_ref[j] = w_ref[0].astype(BF16)

    @pl.when((i == 0) | (tblk_ref[i] != tblk_ref[jnp.maximum(i - 1, 0)]))
    def _():
        x = _from_slabs(xs_ref, EXPERT_TILE).astype(BF16)
        gate = jnp.dot(x, wb_ref[0], preferred_element_type=F32)
        up = jnp.dot(x, wb_ref[1], preferred_element_type=F32)
        act = (jax.nn.silu(gate) * up).astype(BF16)
        _to_slabs(ys_ref, jnp.dot(act, wb_ref[2], preferred_element_type=F32), EXPERT_TILE)


def _experts(tile_block, tile_expert, xs, wg, wu, wd):
    wspec = lambda a, b: pl.BlockSpec((1, a, b), lambda i, tblk, texp: (texp[i], 0, 0))
    rows = pl.BlockSpec((EXPERT_TILE * SLAB, LANES), lambda i, tblk, texp: (tblk[i], 0))
    return pl.pallas_call(
        _experts_kernel,
        out_shape=jax.ShapeDtypeStruct((N_SORTED * SLAB, LANES), F32),
        grid_spec=pltpu.PrefetchScalarGridSpec(
            num_scalar_prefetch=2,
            grid=(N_EXPERT_TILES,),
            in_specs=[rows, wspec(D_MODEL, EXPERT_DIM), wspec(D_MODEL, EXPERT_DIM),
                      wspec(EXPERT_DIM, D_MODEL)],
            out_specs=rows,
            scratch_shapes=[pltpu.VMEM((3, D_MODEL, EXPERT_DIM), BF16)]),
        compiler_params=_cparams("arbitrary"),
        name="moe_experts",
    )(tile_block, tile_expert, xs, wg, wu, wd)


def _combine_ln_kernel(dest_ref, meta_ref, h_ref, ys_hbm, lg_ref, lb_ref, o_ref, ya_ref, yb_ref, sem):
    i = pl.program_id(0)
    bufs = (ya_ref, yb_ref)

    def start_gather(tile, slot):
        def body(t, carry):
            for k in range(2):
                row = dest_ref[k * N_TOK + tile * ROUTE_TILE + t]
                pltpu.make_async_copy(_row_slab(ys_hbm, row), _row_slab(bufs[slot], k * ROUTE_TILE + t),
                                      sem.at[slot]).start(priority=k)
            return carry
        lax.fori_loop(0, ROUTE_TILE, body, 0, unroll=8)

    def wait_gather(slot):
        pltpu.make_async_copy(ys_hbm.at[pl.ds(0, 2 * ROUTE_TILE * SLAB)], bufs[slot], sem.at[slot]).wait()

    last = pl.num_programs(0) - 1
    lane = lax.broadcasted_iota(jnp.int32, (1, LANES), 1)

    def finish_and_prefetch(slot):
        nxt = jnp.minimum(i + 1, last) * ROUTE_TILE
        wait_gather(slot)
        for r0 in range(0, ROUTE_TILE, COMBINE_CHUNK):
            rows = slice(r0, r0 + COMBINE_CHUNK)
            meta = meta_ref[rows, :]
            w1 = jnp.sum(jnp.where(lane == META_W1, meta, 0.0), axis=-1, keepdims=True)
            w2 = jnp.sum(jnp.where(lane == META_W2, meta, 0.0), axis=-1, keepdims=True)
            ff = (w1 * _from_slabs(bufs[slot], COMBINE_CHUNK, first_row=r0)
                  + w2 * _from_slabs(bufs[slot], COMBINE_CHUNK, first_row=ROUTE_TILE + r0))
            o_ref[rows, :] = _layer_norm(DEEPNORM_ALPHA * h_ref[rows, :] + ff, lg_ref[...], lb_ref[...])
            for t in range(r0, r0 + COMBINE_CHUNK):
                for k in range(2):
                    row = dest_ref[k * N_TOK + nxt + t]
                    pltpu.make_async_copy(_row_slab(ys_hbm, row),
                                          _row_slab(bufs[1 - slot], k * ROUTE_TILE + t),
                                          sem.at[1 - slot]).start(priority=k)

        @pl.when(i == last)
        def _():
            wait_gather(1 - slot)

    @pl.when(i == 0)
    def _():
        start_gather(0, 0)

    for slot in range(2):
        @pl.when(i % 2 == slot)
        def _():
            finish_and_prefetch(slot)


def _combine_ln(dest, meta, h, ys, lg, lb):
    row = lambda width: pl.BlockSpec((ROUTE_TILE, width), lambda i, dest: (i, 0))
    const = pl.BlockSpec((1, D_MODEL), lambda i, dest: (0, 0))
    return pl.pallas_call(
        _combine_ln_kernel,
        out_shape=jax.ShapeDtypeStruct((N_TOK, D_MODEL), F32),
        grid_spec=pltpu.PrefetchScalarGridSpec(
            num_scalar_prefetch=1,
            grid=(N_TOK // ROUTE_TILE,),
            in_specs=[row(LANES), row(D_MODEL), pl.BlockSpec(memory_space=pl.ANY), const, const],
            out_specs=row(D_MODEL),
            scratch_shapes=[pltpu.VMEM((2 * ROUTE_TILE * SLAB, LANES), F32)] * 2
                           + [pltpu.SemaphoreType.DMA((2,))]),
        compiler_params=_cparams("arbitrary"),
        name="moe_combine_ln",
    )(dest, meta, h, ys, lg, lb)


def _moe_ln(h, w_router, b_router, wg, wu, wd, lg, lb):
    meta, counts, dest_rows, xs = _router(h, w_router, b_router)
    expert = jnp.arange(N_EXPERTS)
    tiles = (counts[0, :N_EXPERTS] + EXPERT_TILE - 1) // EXPERT_TILE
    ends = jnp.sum(jnp.where(expert[None, :] <= expert[:, None], tiles[None, :], 0), axis=1)
    step = jnp.minimum(jnp.arange(N_EXPERT_TILES), ends[-1] - 1)
    tile_expert = jnp.sum(step[:, None] >= ends[None, :], axis=1)
    first = jnp.sum(jnp.where(tile_expert[:, None] == expert[None, :], (ends - tiles)[None, :], 0), axis=1)
    tile_block = tile_expert * REGION_TILES + (step - first)
    ys = _experts(tile_block.astype(jnp.int32), tile_expert.astype(jnp.int32), xs, wg, wu, wd)
    dest = dest_rows[:2].reshape(2 * N_TOK)
    return _combine_ln(dest, meta, h, ys, lg, lb)


def kernel(x, positions, ln_g, ln_b, ssm_lambda_re, ssm_lambda_im, ssm_log_step, ssm_b_re, ssm_b_im, ssm_c_re, ssm_c_im, ssm_d, ssm_w_glu, kv_w, attn_w_q, attn_sinks, attn_w_out, ffn_w_gate, ffn_w_up, ffn_w_down, moe_w_router, moe_b_router, moe_w_gate, moe_w_up, moe_w_down):
    ln = lambda layer, j: (ln_g[layer, j].reshape(1, D_MODEL).astype(F32),
                           ln_b[layer, j].reshape(1, D_MODEL).astype(F32))
    rope_table = _rope_tables(positions)
    x2 = x.reshape(N_TOK, D_MODEL)

    params = _ssm_params(ssm_lambda_re[0], ssm_lambda_im[0], ssm_log_step[0], ssm_b_re[0], ssm_b_im[0],
                         ssm_c_re[0], ssm_c_im[0])
    y = _ssm_mixer(x2, params)
    h = _glu_ln(y, x2, ssm_d[0].astype(F32).reshape(1, D_MODEL), ssm_w_glu[0].astype(BF16), *ln(0, 0))
    h = _ffn_ln(h, ffn_w_gate[0].astype(BF16), ffn_w_up[0].astype(BF16), ffn_w_down[0].astype(BF16),
                *ln(0, 1))

    order = jnp.array(HEAD_ORDER)
    w_q = attn_w_q[0].reshape(D_MODEL, N_HEADS, HEAD_DIM)[:, order].reshape(D_MODEL, Q_DIM)
    w_out = attn_w_out[0].reshape(N_HEADS, HEAD_DIM, D_MODEL)[order].reshape(Q_DIM, D_MODEL)
    w_qkv = jnp.concatenate([w_q, kv_w], axis=1).astype(BF16)
    q, k, v = _qkv(h, w_qkv, rope_table)
    h = _attention_ln(q, k, v, attn_sinks[0].astype(F32), h, w_out.astype(BF16), *ln(1, 0))
    w_router = jnp.pad(moe_w_router[0].astype(F32), ((0, 0), (0, LANES - N_EXPERTS)))
    b_router = jnp.pad(moe_b_router[0].astype(F32), (0, LANES - N_EXPERTS)).reshape(1, LANES)
    h = _moe_ln(h, w_router, b_router, moe_w_gate[0].astype(F32), moe_w_up[0].astype(F32),
                moe_w_down[0].astype(F32), *ln(1, 1))
    return h.reshape(BATCH, SEQ, D_MODEL)
```

```python
import math

import jax
import jax.numpy as jnp
import numpy as np
from jax import lax
from jax.experimental import pallas as pl
from jax.experimental.pallas import tpu as pltpu

F32 = jnp.float32
BF16 = jnp.bfloat16

D_MODEL = 1024
BATCH = 16
SEQ = 2048
N_TOK = BATCH * SEQ
DEPTH = 2

SSM_GROUP_CH = 16
SSM_GROUPS = D_MODEL // SSM_GROUP_CH
SSM_STATE = 64
SSM_CHUNK = 16
SSM_NCHUNK = SEQ // SSM_CHUNK
SSM_WIDTH = SSM_CHUNK * SSM_GROUP_CH
SSM_SCAN_STEPS = int(math.log2(SSM_NCHUNK))

N_HEADS = 16
HEAD_DIM = 64
N_KV_HEADS = 4
Q_PER_KV = N_HEADS // N_KV_HEADS
KV_DIM = N_KV_HEADS * HEAD_DIM
WINDOW = 128
BLOCK = 128
ROT_DIM = HEAD_DIM // 4
ROT_HALF = ROT_DIM // 2
ROPE_THETA = 500000.0

FFN_DIM = 2816
N_EXPERTS = 8
EXPERT_DIM = 1024

DEEPNORM_ALPHA = (2 * DEPTH) ** 0.25
LOG2_E = math.log2(math.e)
LN_EPS = 1e-5

LANES = 128
VMEM_LIMIT = 56 * 1024 * 1024


def _cparams(*sem):
    return pltpu.CompilerParams(dimension_semantics=sem, vmem_limit_bytes=VMEM_LIMIT)


def _layer_norm(r, g, b):
    mu = jnp.mean(r, axis=-1, keepdims=True)
    xc = r - mu
    var = jnp.mean(xc * xc, axis=-1, keepdims=True)
    return xc * lax.rsqrt(var + LN_EPS) * g + b


def _rope_kernel(invf_ref, pos_ref, cs_ref):
    pos = pos_ref[...]
    for f in range(ROT_HALF):
        ang = pos * invf_ref[f]
        cs_ref[f] = jnp.cos(ang)
        cs_ref[ROT_HALF + f] = jnp.sin(ang)


def _rope_tables(positions):
    inv_freq = ROPE_THETA ** (-jnp.arange(0, ROT_DIM, 2, dtype=F32) / ROT_DIM)
    pos = positions.astype(F32)
    cs = pl.pallas_call(
        _rope_kernel,
        out_shape=jax.ShapeDtypeStruct((ROT_DIM, BATCH, SEQ), F32),
        in_specs=[pl.BlockSpec(memory_space=pltpu.SMEM),
                  pl.BlockSpec(memory_space=pltpu.VMEM)],
        out_specs=pl.BlockSpec(memory_space=pltpu.VMEM),
        name="rope_tables",
    )(inv_freq, pos)
    return cs.reshape(ROT_DIM, N_TOK).T


def _rope_placement():
    f = np.arange(ROT_HALF)
    place = np.zeros((ROT_DIM, LANES), np.float32)
    place[f, f] = place[f, ROT_HALF + f] = 1.0
    place[ROT_HALF + f, HEAD_DIM + f] = -1.0
    place[ROT_HALF + f, HEAD_DIM + ROT_HALF + f] = 1.0
    lane = np.arange(LANES)
    bias = ((lane >= ROT_DIM) & (lane < HEAD_DIM)).astype(np.float32).reshape(1, LANES)
    return jnp.asarray(place, BF16), jnp.asarray(bias)


def _ssm_params(lam_re, lam_im, log_step, b_re, b_im, c_re, c_im):
    hp = lax.Precision.HIGHEST
    lr, li = lam_re.astype(F32), lam_im.astype(F32)
    dt = jnp.exp(log_step.astype(F32))[:, None]
    mag = jnp.exp(lr * dt)
    ar = mag * jnp.cos(li * dt)
    ai = mag * jnp.sin(li * dt)
    nr = ar - 1.0
    den = lr * lr + li * li
    kr = (nr * lr + ai * li) / den
    ki = (ai * lr - nr * li) / den
    br, bi = b_re.astype(F32), b_im.astype(F32)
    bbar_r = kr[..., None] * br - ki[..., None] * bi
    bbar_i = kr[..., None] * bi + ki[..., None] * br
    cr, ci = c_re.astype(F32), c_im.astype(F32)

    def powers(taus):
        t = taus.astype(F32)[:, None, None]
        m = jnp.exp(lr[None] * dt[None] * t)
        ang = li[None] * dt[None] * t
        return m * jnp.cos(ang), m * jnp.sin(ang)

    er, ei = powers(jnp.arange(SSM_CHUNK + 1))
    w_r = er[:, :, :, None] * bbar_r[None] - ei[:, :, :, None] * bbar_i[None]
    w_i = er[:, :, :, None] * bbar_i[None] + ei[:, :, :, None] * bbar_r[None]
    kern = (jnp.einsum('gcp,tgpd->gtcd', cr, w_r[:SSM_CHUNK], precision=hp)
            - jnp.einsum('gcp,tgpd->gtcd', ci, w_i[:SSM_CHUNK], precision=hp))
    s_idx = jnp.arange(SSM_CHUNK)[:, None, None]
    t_idx = jnp.arange(SSM_CHUNK)[None, :, None]
    lag = (t_idx - s_idx == jnp.arange(SSM_CHUNK)[None, None, :]).astype(BF16)
    toep = jnp.einsum('stu,gucd->gtcsd', lag, kern.astype(BF16), preferred_element_type=BF16)
    mt = toep.reshape(SSM_GROUPS, SSM_WIDTH, SSM_WIDTH)
    rev = SSM_CHUNK - 1 - jnp.arange(SSM_CHUNK)
    g_r = w_r[rev].transpose(1, 2, 0, 3).reshape(SSM_GROUPS, SSM_STATE, SSM_WIDTH)
    g_i = w_i[rev].transpose(1, 2, 0, 3).reshape(SSM_GROUPS, SSM_STATE, SSM_WIDTH)
    gt = jnp.concatenate([g_r, g_i], axis=1)
    e1r, e1i = er[1:], ei[1:]
    ce_r = cr[None] * e1r[:, :, None, :] - ci[None] * e1i[:, :, None, :]
    ce_i = cr[None] * e1i[:, :, None, :] + ci[None] * e1r[:, :, None, :]
    c_re = ce_r.transpose(1, 0, 2, 3).reshape(SSM_GROUPS, SSM_WIDTH, SSM_STATE)
    c_im = (-ce_i).transpose(1, 0, 2, 3).reshape(SSM_GROUPS, SSM_WIDTH, SSM_STATE)
    ct = jnp.concatenate([c_re, c_im], axis=-1)
    sr, si = powers(SSM_CHUNK * (2 ** jnp.arange(8)))
    coef = lambda t: jnp.concatenate([t, t], axis=-1).transpose(1, 0, 2)
    return mt.astype(BF16), gt.astype(BF16), ct.astype(BF16), coef(sr), coef(si)


SSM_LANE_GROUPS = LANES // SSM_GROUP_CH
SSM_LANE_CHUNKS = D_MODEL // LANES
SSM_BATCH_TILE = 4
SSM_COLS = SSM_BATCH_TILE * SSM_NCHUNK


def _ssm_kernel(x_ref, mt_ref, gt_ref, ct_ref, cr_ref, ci_ref, y_ref, v_ref, yt_ref):
    for bl in range(SSM_BATCH_TILE):
        cols = slice(bl * SSM_NCHUNK, (bl + 1) * SSM_NCHUNK)
        for s in range(SSM_CHUNK):
            a = x_ref[pl.ds(bl * SEQ + s, SSM_NCHUNK, stride=SSM_CHUNK), :]
            at = a.T.astype(BF16)
            for g in range(SSM_LANE_GROUPS):
                v_ref[g, s * SSM_GROUP_CH:(s + 1) * SSM_GROUP_CH, cols] = (
                    at[g * SSM_GROUP_CH:(g + 1) * SSM_GROUP_CH])

    row_j = lax.broadcasted_iota(jnp.int32, (SSM_NCHUNK, 1), 0)

    def group_body(g, carry):
        ut = v_ref[g]
        xt = jnp.dot(gt_ref[g], ut, preferred_element_type=F32)
        cr_all = cr_ref[g]
        ci_all = ci_ref[g]

        def seg(rows, b):
            return xt[rows, b * SSM_NCHUNK:(b + 1) * SSM_NCHUNK]

        def shifted(x, d):
            if d < 8:
                return jnp.where(row_j >= d, pltpu.roll(x, d, 0), 0.0)
            return jnp.concatenate([jnp.zeros((d, LANES), F32), x[:SSM_NCHUNK - d]], axis=0)

        re_rows, im_rows = slice(0, SSM_STATE), slice(SSM_STATE, 2 * SSM_STATE)
        sr, si = [], []
        for q in range(SSM_BATCH_TILE // 2):
            xr = jnp.concatenate([seg(re_rows, 2 * q), seg(re_rows, 2 * q + 1)], axis=0).T
            xi = jnp.concatenate([seg(im_rows, 2 * q), seg(im_rows, 2 * q + 1)], axis=0).T
            for k in range(SSM_SCAN_STEPS):
                cr = cr_all[k:k + 1, :]
                ci = ci_all[k:k + 1, :]
                rr = shifted(xr, 1 << k)
                ri = shifted(xi, 1 << k)
                xr, xi = xr + cr * rr - ci * ri, xi + cr * ri + ci * rr
            pr = shifted(xr, 1).T
            pi = shifted(xi, 1).T
            sr += [pr[:SSM_STATE], pr[SSM_STATE:]]
            si += [pi[:SSM_STATE], pi[SSM_STATE:]]
        sprev = jnp.concatenate([jnp.concatenate(sr, axis=1), jnp.concatenate(si, axis=1)],
                                axis=0).astype(BF16)
        yt_ref[g] = (jnp.dot(mt_ref[g], ut, preferred_element_type=F32)
                     + jnp.dot(ct_ref[g], sprev, preferred_element_type=F32))
        return carry

    lax.fori_loop(0, SSM_LANE_GROUPS, group_body, 0, unroll=4)

    for bl in range(SSM_BATCH_TILE):
        cols = slice(bl * SSM_NCHUNK, (bl + 1) * SSM_NCHUNK)
        for t in range(SSM_CHUNK):
            tile = jnp.concatenate(
                [yt_ref[g, t * SSM_GROUP_CH:(t + 1) * SSM_GROUP_CH, cols] for g in range(SSM_LANE_GROUPS)],
                axis=0)
            y_ref[pl.ds(bl * SEQ + t, SSM_NCHUNK, stride=SSM_CHUNK), :] = tile.T


def _ssm_mixer(x2, params):
    mt, gt, ct, coef_r, coef_i = params
    rows = SSM_BATCH_TILE * SEQ
    pspec = lambda a, b: pl.BlockSpec((SSM_LANE_GROUPS, a, b), lambda k, q: (k, 0, 0))
    return pl.pallas_call(
        _ssm_kernel,
        out_shape=jax.ShapeDtypeStruct((N_TOK, D_MODEL), F32),
        grid=(SSM_LANE_CHUNKS, BATCH // SSM_BATCH_TILE),
        in_specs=[pl.BlockSpec((rows, LANES), lambda k, q: (q, k)),
                  pspec(SSM_WIDTH, SSM_WIDTH), pspec(2 * SSM_STATE, SSM_WIDTH), pspec(SSM_WIDTH, 2 * SSM_STATE),
                  pspec(8, 2 * SSM_STATE), pspec(8, 2 * SSM_STATE)],
        out_specs=pl.BlockSpec((rows, LANES), lambda k, q: (q, k)),
        scratch_shapes=[pltpu.VMEM((SSM_LANE_GROUPS, SSM_WIDTH, SSM_COLS), BF16),
                        pltpu.VMEM((SSM_LANE_GROUPS, SSM_WIDTH, SSM_COLS), F32)],
        compiler_params=_cparams("parallel", "parallel"),
        name="ssm_mixer",
    )(x2, mt, gt, ct, coef_r, coef_i)


GLU_COLS = 256


def _glu_ln_kernel(y_ref, x_ref, d_ref, w_ref, lg_ref, lb_ref, o_ref):
    x = x_ref[...]
    act = jax.nn.gelu(y_ref[...] + d_ref[...] * x).astype(BF16)
    parts = []
    for c in range(0, D_MODEL, GLU_COLS):
        val = jnp.dot(act, w_ref[:, c:c + GLU_COLS], preferred_element_type=F32)
        gate = jnp.dot(act, w_ref[:, D_MODEL + c:D_MODEL + c + GLU_COLS], preferred_element_type=F32)
        parts.append(val * jax.nn.sigmoid(gate))
    mix = jnp.concatenate(parts, axis=1)
    o_ref[...] = _layer_norm(DEEPNORM_ALPHA * x + mix, lg_ref[...], lb_ref[...])


def _row_spec(tm, width):
    return pl.BlockSpec((tm, width), lambda i: (i, 0))


def _const_spec(*shape):
    return pl.BlockSpec(shape, lambda *_: (0,) * len(shape))


def _glu_ln(y, x2, d_skip, w_glu, lg, lb, tm=1024):
    return pl.pallas_call(
        _glu_ln_kernel,
        out_shape=jax.ShapeDtypeStruct((N_TOK, D_MODEL), F32),
        grid=(N_TOK // tm,),
        in_specs=[_row_spec(tm, D_MODEL), _row_spec(tm, D_MODEL), _const_spec(1, D_MODEL),
                  _const_spec(D_MODEL, 2 * D_MODEL), _const_spec(1, D_MODEL), _const_spec(1, D_MODEL)],
        out_specs=_row_spec(tm, D_MODEL),
        compiler_params=_cparams("parallel"),
        name="glu_ln",
    )(y, x2, d_skip, w_glu, lg, lb)


FFN_SPLIT = 1
FFN_TILE = FFN_DIM // FFN_SPLIT


def _ffn_ln_kernel(h_ref, wg_ref, wu_ref, wd_ref, lg_ref, lb_ref, o_ref):
    h = h_ref[...]
    hb = h.astype(BF16)
    ff = None
    for f in range(FFN_SPLIT):
        cols = slice(f * FFN_TILE, (f + 1) * FFN_TILE)
        gate = jnp.dot(hb, wg_ref[:, cols], preferred_element_type=F32)
        up = jnp.dot(hb, wu_ref[:, cols], preferred_element_type=F32)
        act = (jax.nn.silu(gate) * up).astype(BF16)
        part = jnp.dot(act, wd_ref[cols, :], preferred_element_type=F32)
        ff = part if ff is None else ff + part
    o_ref[...] = _layer_norm(DEEPNORM_ALPHA * h + ff, lg_ref[...], lb_ref[...])


def _resident_spec(*shape):
    return pl.BlockSpec(shape, lambda *_: (0,) * len(shape), pipeline_mode=pl.Buffered(1))


def _ffn_ln(h, wg, wu, wd, lg, lb, tm=512):
    return pl.pallas_call(
        _ffn_ln_kernel,
        out_shape=jax.ShapeDtypeStruct((N_TOK, D_MODEL), F32),
        grid=(N_TOK // tm,),
        in_specs=[_row_spec(tm, D_MODEL),
                  _resident_spec(D_MODEL, FFN_DIM), _resident_spec(D_MODEL, FFN_DIM),
                  _resident_spec(FFN_DIM, D_MODEL),
                  _const_spec(1, D_MODEL), _const_spec(1, D_MODEL)],
        out_specs=_row_spec(tm, D_MODEL),
        compiler_params=_cparams("parallel"),
        name="ffn_ln",
    )(h, wg, wu, wd, lg, lb)


Q_DIM = N_HEADS * HEAD_DIM


def _qkv_kernel(h_ref, w_ref, cs_ref, place_ref, bias_ref, q_ref, k_ref, v_ref):
    hb = h_ref[...].astype(BF16)
    z = jnp.dot(hb, w_ref[...], preferred_element_type=F32)
    lane = lax.broadcasted_iota(jnp.int32, (1, LANES), 1)
    table = bias_ref[...]
    for term in _split_bf16(cs_ref[...]):
        table = table + jnp.dot(term, place_ref[...], preferred_element_type=F32)
    swapped = pltpu.roll(table, HEAD_DIM, 1)
    cos_t = jnp.where(lane < HEAD_DIM, table, swapped)
    sin_t = jnp.where(lane < HEAD_DIM, swapped, table)
    first_half = (lane % HEAD_DIM) < ROT_HALF

    def rope(t):
        partner = jnp.where(first_half, pltpu.roll(t, LANES - ROT_HALF, 1), pltpu.roll(t, ROT_HALF, 1))
        return t * cos_t + partner * sin_t

    scale = HEAD_DIM ** -0.5 * LOG2_E
    for c in range(Q_DIM // LANES):
        sl = slice(c * LANES, (c + 1) * LANES)
        q_ref[:, sl] = (rope(z[:, sl]) * scale).astype(BF16)
    for c in range(KV_DIM // LANES):
        sl = slice(c * LANES, (c + 1) * LANES)
        k_ref[:, sl] = rope(z[:, Q_DIM + c * LANES:Q_DIM + (c + 1) * LANES]).astype(BF16)
    v_ref[...] = z[:, Q_DIM + KV_DIM:].astype(BF16)


def _qkv(h, w_qkv, rope_table, tm=1024):
    return pl.pallas_call(
        _qkv_kernel,
        out_shape=[jax.ShapeDtypeStruct((N_TOK, Q_DIM), BF16),
                   jax.ShapeDtypeStruct((N_TOK, KV_DIM), BF16),
                   jax.ShapeDtypeStruct((N_TOK, KV_DIM), BF16)],
        grid=(N_TOK // tm,),
        in_specs=[_row_spec(tm, D_MODEL), _const_spec(D_MODEL, Q_DIM + 2 * KV_DIM),
                  _row_spec(tm, ROT_DIM), _const_spec(ROT_DIM, LANES), _const_spec(1, LANES)],
        out_specs=[_row_spec(tm, Q_DIM), _row_spec(tm, KV_DIM), _row_spec(tm, KV_DIM)],
        compiler_params=_cparams("parallel"),
        name="qkv_rope",
    )(h, w_qkv, rope_table, *_rope_placement())


ATTN_Q_TILE = 512
ATTN_SUB = ATTN_Q_TILE // BLOCK
ATTN_BAND = ATTN_Q_TILE + BLOCK
KV_PAIRS = N_KV_HEADS // 2

HEAD_ORDER = [h for c in range(KV_PAIRS) for g in range(Q_PER_KV)
              for h in (2 * c * Q_PER_KV + g, (2 * c + 1) * Q_PER_KV + g)]


def _band_bias():
    qi = jnp.arange(Q_PER_KV * BLOCK)[:, None] % BLOCK
    si = jnp.arange(2 * BLOCK)[None, :]
    rel = qi + BLOCK - si
    valid = (rel >= 0) & (rel < WINDOW)
    first = valid & (si >= BLOCK)
    neg = jnp.float32(-jnp.inf)
    return jnp.stack([jnp.where(valid, 0.0, neg), jnp.where(first, 0.0, neg)]).astype(F32)


def _attn_kernel(sink_ref, q_ref, kp_ref, kc_ref, vp_ref, vc_ref, bias_ref, h_ref, w_ref, lg_ref, lb_ref,
                 out_ref, ka_ref, kb_ref, va_ref, vb_ref, o_ref):
    i = pl.program_id(1)
    lane = lax.broadcasted_iota(jnp.int32, (1, KV_DIM), 1)
    low = (lane % LANES) < HEAD_DIM
    zero = jnp.zeros((), BF16)
    for src, dst_a, dst_b in ((kp_ref, ka_ref, kb_ref), (vp_ref, va_ref, vb_ref)):
        t = src[...]
        dst_a[0:BLOCK, :] = jnp.where(low, t, zero)
        dst_b[0:BLOCK, :] = jnp.where(low, zero, t)
    for src, dst_a, dst_b in ((kc_ref, ka_ref, kb_ref), (vc_ref, va_ref, vb_ref)):
        t = src[...]
        dst_a[BLOCK:ATTN_BAND, :] = jnp.where(low, t, zero)
        dst_b[BLOCK:ATTN_BAND, :] = jnp.where(low, zero, t)

    lane1 = lax.broadcasted_iota(jnp.int32, (1, LANES), 1)
    row = lax.broadcasted_iota(jnp.int32, (Q_PER_KV * BLOCK, 1), 0)
    contract_last = (((1,), (1,)), ((), ()))
    for blk in range(ATTN_SUB):
        r0 = blk * BLOCK
        if blk == 0:
            bias = bias_ref[jnp.where(i == 0, 1, 0)]
        else:
            bias = bias_ref[0]
        for c in range(KV_PAIRS):
            cs = slice(c * LANES, (c + 1) * LANES)
            chunks = [c * Q_PER_KV + g for g in range(Q_PER_KV)]
            q4 = jnp.concatenate([q_ref[r0:r0 + BLOCK, m * LANES:(m + 1) * LANES] for m in chunks], axis=0)
            outs = []
            for half, (k_ref, v_ref) in enumerate(((ka_ref, va_ref), (kb_ref, vb_ref))):
                kband = k_ref[r0:r0 + 2 * BLOCK, cs]
                s = lax.dot_general(q4, kband, contract_last, preferred_element_type=F32) + bias
                sink = jnp.zeros((Q_PER_KV * BLOCK, 1), F32)
                for g, m in enumerate(chunks):
                    sink = jnp.where(row // BLOCK == g, sink_ref[HEAD_ORDER[2 * m + half]] * LOG2_E, sink)
                mx = jnp.maximum(jnp.max(s, axis=-1, keepdims=True), sink)
                p = jnp.exp2(s - mx)
                denom = jnp.sum(p, axis=-1, keepdims=True) + jnp.exp2(sink - mx)
                pv = jnp.dot(p.astype(BF16), v_ref[r0:r0 + 2 * BLOCK, cs], preferred_element_type=F32)
                outs.append((pv, 1.0 / denom))
            (pv_a, r_a), (pv_b, r_b) = outs
            o = (pv_a + pv_b) * jnp.where(lane1 < HEAD_DIM, r_a, r_b)
            for g, m in enumerate(chunks):
                o_ref[r0:r0 + BLOCK, m * LANES:(m + 1) * LANES] = o[g * BLOCK:(g + 1) * BLOCK].astype(BF16)

    mix = jnp.dot(o_ref[...], w_ref[...], preferred_element_type=F32)
    out_ref[...] = _layer_norm(DEEPNORM_ALPHA * h_ref[...] + mix, lg_ref[...], lb_ref[...])


def _attention_ln(q, k, v, sinks, h, w_out, lg, lb):
    tiles = SEQ // ATTN_Q_TILE
    cur = lambda b, i: (b * tiles + i, 0)
    prev = lambda b, i: (jnp.maximum((b * tiles + i) * ATTN_SUB - 1, 0), 0)
    const2 = lambda b, i: (0, 0)
    return pl.pallas_call(
        _attn_kernel,
        out_shape=jax.ShapeDtypeStruct((N_TOK, D_MODEL), F32),
        grid=(BATCH, tiles),
        in_specs=[pl.BlockSpec(memory_space=pltpu.SMEM),
                  pl.BlockSpec((ATTN_Q_TILE, Q_DIM), cur),
                  pl.BlockSpec((BLOCK, KV_DIM), prev), pl.BlockSpec((ATTN_Q_TILE, KV_DIM), cur),
                  pl.BlockSpec((BLOCK, KV_DIM), prev), pl.BlockSpec((ATTN_Q_TILE, KV_DIM), cur),
                  pl.BlockSpec((2, Q_PER_KV * BLOCK, 2 * BLOCK), lambda b, i: (0, 0, 0)),
                  pl.BlockSpec((ATTN_Q_TILE, D_MODEL), cur),
                  pl.BlockSpec((Q_DIM, D_MODEL), const2),
                  pl.BlockSpec((1, D_MODEL), const2), pl.BlockSpec((1, D_MODEL), const2)],
        out_specs=pl.BlockSpec((ATTN_Q_TILE, D_MODEL), cur),
        scratch_shapes=[pltpu.VMEM((ATTN_BAND, KV_DIM), BF16)] * 4
                       + [pltpu.VMEM((ATTN_Q_TILE, Q_DIM), BF16)],
        compiler_params=_cparams("parallel", "parallel"),
        name="swa_attention_ln",
    )(sinks, q, k, k, v, v, _band_bias(), h, w_out, lg, lb)


ROUTE_TILE = 512
EXPERT_TILE = 512
SLAB = D_MODEL // LANES
EXPERT_REGION = N_TOK
REGION_TILES = EXPERT_REGION // EXPERT_TILE
N_SORTED = N_EXPERTS * EXPERT_REGION
N_EXPERT_TILES = 2 * N_TOK // EXPERT_TILE + N_EXPERTS
META_E1, META_E2, META_R1, META_R2, META_W1, META_W2 = range(6)
META_D1, META_D2 = 8, 9
COMBINE_CHUNK = 64
N_ROUTE_TILES = N_TOK // ROUTE_TILE
ROUTER_BURST = 64


def _to_slabs(ref, val, rows):
    for c in range(SLAB):
        ref[pl.ds(c, rows, stride=SLAB), :] = val[:, c * LANES:(c + 1) * LANES]


def _from_slabs(ref, rows, first_row=0):
    return jnp.concatenate([ref[pl.ds(first_row * SLAB + c, rows, stride=SLAB), :] for c in range(SLAB)],
                           axis=1)


def _row_slab(ref, row):
    return ref.at[pl.ds(pl.multiple_of(row * SLAB, SLAB), SLAB)]


def _split_bf16(x):
    def top_bits(v):
        bits = lax.bitcast_convert_type(v, jnp.uint32) & jnp.uint32(0xFFFF0000)
        return lax.bitcast_convert_type(bits, F32)
    hi = top_bits(x)
    mid = top_bits(x - hi)
    lo = (x - hi) - mid
    return hi.astype(BF16), mid.astype(BF16), lo.astype(BF16)


def _router_kernel(h_ref, wr_ref, br_ref, tri_ref, meta_ref, counts_ref, dest_ref, xs_hbm,
                   carry_ref, slab_a, slab_b, zero_ref, dest_v, dest_a, dest_b, count_v, count_s,
                   sem, pad_sem, idx_sem):
    i = pl.program_id(0)
    slabs = (slab_a, slab_b)
    dests = (dest_a, dest_b)

    @pl.when(i == 0)
    def _():
        carry_ref[...] = jnp.zeros_like(carry_ref)
        zero_ref[...] = jnp.zeros_like(zero_ref)

    def start_row(slot, t):
        for k in range(2):
            pltpu.make_async_copy(_row_slab(slabs[slot], t), _row_slab(xs_hbm, dests[slot][k, t]),
                                  sem.at[slot]).start(priority=k)

    def wait_scatter(slot):
        for _ in range(2):
            pltpu.make_async_copy(slabs[slot], xs_hbm.at[pl.ds(0, ROUTE_TILE * SLAB)], sem.at[slot]).wait()

    def route(slot, scatter_previous):
        pending = list(range(0, ROUTE_TILE, ROUTER_BURST)) if scatter_previous else []

        def burst():
            if pending:
                r0 = pending.pop(0)
                for t in range(r0, r0 + ROUTER_BURST):
                    start_row(1 - slot, t)

        lane = lax.broadcasted_iota(jnp.int32, (1, LANES), 1)
        lane_f = lane.astype(F32)
        h = h_ref[...]
        h0, h1, h2 = _split_bf16(h)
        w0, w1, w2 = wr_ref[0], wr_ref[1], wr_ref[2]
        small = None
        for a, b in ((h2, w0), (h0, w2), (h1, w1)):
            part = jnp.dot(a, b, preferred_element_type=F32)
            small = part if small is None else small + part
            burst()
        mid = None
        for a, b in ((h1, w0), (h0, w1)):
            part = jnp.dot(a, b, preferred_element_type=F32)
            mid = part if mid is None else mid + part
            burst()
        logits = (small + mid) + jnp.dot(h0, w0, preferred_element_type=F32) + br_ref[...]
        burst()
        logits = jnp.where(lane < N_EXPERTS, logits, -jnp.inf)
        m1 = jnp.max(logits, axis=-1, keepdims=True)
        i1 = jnp.min(jnp.where(logits == m1, lane_f, float(LANES)), axis=-1, keepdims=True)
        rest = jnp.where(lane_f == i1, -jnp.inf, logits)
        m2 = jnp.max(rest, axis=-1, keepdims=True)
        i2 = jnp.min(jnp.where(rest == m2, lane_f, float(LANES)), axis=-1, keepdims=True)
        e2 = jnp.exp(m2 - m1)
        tot = 1.0 + e2
        burst()
        chosen = jnp.where((lane_f == i1) | (lane_f == i2), 1.0, 0.0)
        before = jnp.dot(tri_ref[...], chosen.astype(BF16), preferred_element_type=F32) + carry_ref[...]
        r1 = jnp.sum(jnp.where(lane_f == i1, before, 0.0), axis=-1, keepdims=True)
        r2 = jnp.sum(jnp.where(lane_f == i2, before, 0.0), axis=-1, keepdims=True)
        new_carry = carry_ref[...] + jnp.sum(chosen, axis=0, keepdims=True)
        carry_ref[...] = new_carry
        counts_ref[...] = new_carry.astype(jnp.int32)
        burst()
        record = jnp.zeros((ROUTE_TILE, LANES), F32)
        for lane_id, col in ((META_E1, i1), (META_E2, i2), (META_R1, r1), (META_R2, r2),
                             (META_W1, 1.0 / tot), (META_W2, e2 / tot),
                             (META_D1, i1 * EXPERT_REGION + r1), (META_D2, i2 * EXPERT_REGION + r2)):
            record = jnp.where(lane == lane_id, col, record)
        meta_ref[...] = record
        _to_slabs(slabs[slot], h, ROUTE_TILE)
        while pending:
            burst()
        dest_rows = record.T[META_D1:META_D1 + 8, :].astype(jnp.int32)
        dest_ref[...] = dest_rows
        dest_v[...] = dest_rows
        to_smem = pltpu.make_async_copy(dest_v, dests[slot], idx_sem)
        to_smem.start()
        to_smem.wait()

    @pl.when(i == 0)
    def _():
        route(0, False)

    for slot in range(2):
        @pl.when((i > 0) & (i % 2 == slot))
        def _():
            @pl.when(i >= 2)
            def _():
                wait_scatter(slot)
            route(slot, True)

    @pl.when(i == N_ROUTE_TILES - 1)
    def _():
        slot = (N_ROUTE_TILES - 1) % 2
        lax.fori_loop(0, ROUTE_TILE, lambda t, c: (start_row(slot, t), c)[1], 0, unroll=8)
        wait_scatter(1 - slot)
        wait_scatter(slot)
        count_v[...] = jnp.broadcast_to(carry_ref[...].astype(jnp.int32), count_v.shape)
        cp = pltpu.make_async_copy(count_v, count_s, idx_sem)
        cp.start()
        cp.wait()
        for e in range(N_EXPERTS):
            n = count_s[0, e]
            end = (n + EXPERT_TILE - 1) // EXPERT_TILE * EXPERT_TILE
            pad = lambda r: pltpu.make_async_copy(zero_ref, _row_slab(xs_hbm, e * EXPERT_REGION + r), pad_sem)
            lax.fori_loop(n, end, lambda r, c: (pad(r).start(), c)[1], 0)
            lax.fori_loop(n, end, lambda r, c: (pad(r).wait(), c)[1], 0)


def _router(h, w_router, b_router):
    tri = (jnp.arange(ROUTE_TILE)[:, None] > jnp.arange(ROUTE_TILE)[None, :]).astype(BF16)
    w_split = jnp.stack(_split_bf16(w_router))
    return pl.pallas_call(
        _router_kernel,
        out_shape=[jax.ShapeDtypeStruct((N_TOK, LANES), F32),
                   jax.ShapeDtypeStruct((1, LANES), jnp.int32),
                   jax.ShapeDtypeStruct((8, N_TOK), jnp.int32),
                   jax.ShapeDtypeStruct((N_SORTED * SLAB, LANES), F32)],
        grid=(N_TOK // ROUTE_TILE,),
        in_specs=[_row_spec(ROUTE_TILE, D_MODEL), _const_spec(3, D_MODEL, LANES),
                  _const_spec(1, LANES), _const_spec(ROUTE_TILE, ROUTE_TILE)],
        out_specs=[_row_spec(ROUTE_TILE, LANES), _const_spec(1, LANES),
                   pl.BlockSpec((8, ROUTE_TILE), lambda i: (0, i)), pl.BlockSpec(memory_space=pl.ANY)],
        scratch_shapes=[pltpu.VMEM((1, LANES), F32),
                        pltpu.VMEM((ROUTE_TILE * SLAB, LANES), F32), pltpu.VMEM((ROUTE_TILE * SLAB, LANES), F32),
                        pltpu.VMEM((SLAB, LANES), F32),
                        pltpu.VMEM((8, ROUTE_TILE), jnp.int32),
                        pltpu.SMEM((8, ROUTE_TILE), jnp.int32), pltpu.SMEM((8, ROUTE_TILE), jnp.int32),
                        pltpu.VMEM((8, LANES), jnp.int32), pltpu.SMEM((8, LANES), jnp.int32),
                        pltpu.SemaphoreType.DMA((2,)), pltpu.SemaphoreType.DMA, pltpu.SemaphoreType.DMA],
        compiler_params=_cparams("arbitrary"),
        name="moe_router",
    )(h, w_split, b_router, tri)


def _experts_kernel(tblk_ref, texp_ref, xs_ref, wg_ref, wu_ref, wd_ref, ys_ref, wb_ref):
    i = pl.program_id(0)

    @pl.when((i == 0) | (texp_ref[i] != texp_ref[jnp.maximum(i - 1, 0)]))
    def _():
        for j, w_ref in enumerate((wg_ref, wu_ref, wd_ref)):
            wb_ref[j] = w_ref[0].astype(BF16)

    @pl.when((i == 0) | (tblk_ref[i] != tblk_ref[jnp.maximum(i - 1, 0)]))
    def _():
        x = _from_slabs(xs_ref, EXPERT_TILE).astype(BF16)
        gate = jnp.dot(x, wb_ref[0], preferred_element_type=F32)
        up = jnp.dot(x, wb_ref[1], preferred_element_type=F32)
        act = (jax.nn.silu(gate) * up).astype(BF16)
        _to_slabs(ys_ref, jnp.dot(act, wb_ref[2], preferred_element_type=F32), EXPERT_TILE)


def _experts(tile_block, tile_expert, xs, wg, wu, wd):
    wspec = lambda a, b: pl.BlockSpec((1, a, b), lambda i, tblk, texp: (texp[i], 0, 0))
    rows = pl.BlockSpec((EXPERT_TILE * SLAB, LANES), lambda i, tblk, texp: (tblk[i], 0))
    return pl.pallas_call(
        _experts_kernel,
        out_shape=jax.ShapeDtypeStruct((N_SORTED * SLAB, LANES), F32),
        grid_spec=pltpu.PrefetchScalarGridSpec(
            num_scalar_prefetch=2,
            grid=(N_EXPERT_TILES,),
            in_specs=[rows, wspec(D_MODEL, EXPERT_DIM), wspec(D_MODEL, EXPERT_DIM),
                      wspec(EXPERT_DIM, D_MODEL)],
            out_specs=rows,
            scratch_shapes=[pltpu.VMEM((3, D_MODEL, EXPERT_DIM), BF16)]),
        compiler_params=_cparams("arbitrary"),
        name="moe_experts",
    )(tile_block, tile_expert, xs, wg, wu, wd)


def _combine_ln_kernel(dest_ref, meta_ref, h_ref, ys_hbm, lg_ref, lb_ref, o_ref, ya_ref, yb_ref, sem):
    i = pl.program_id(0)
    bufs = (ya_ref, yb_ref)

    def start_gather(tile, slot):
        def body(t, carry):
            for k in range(2):
                row = dest_ref[k * N_TOK + tile * ROUTE_TILE + t]
                pltpu.make_async_copy(_row_slab(ys_hbm, row), _row_slab(bufs[slot], k * ROUTE_TILE + t),
                                      sem.at[slot]).start(priority=k)
            return carry
        lax.fori_loop(0, ROUTE_TILE, body, 0, unroll=8)

    def wait_gather(slot):
        pltpu.make_async_copy(ys_hbm.at[pl.ds(0, 2 * ROUTE_TILE * SLAB)], bufs[slot], sem.at[slot]).wait()

    last = pl.num_programs(0) - 1
    lane = lax.broadcasted_iota(jnp.int32, (1, LANES), 1)

    def finish_and_prefetch(slot):
        nxt = jnp.minimum(i + 1, last) * ROUTE_TILE
        wait_gather(slot)
        for r0 in range(0, ROUTE_TILE, COMBINE_CHUNK):
            rows = slice(r0, r0 + COMBINE_CHUNK)
            meta = meta_ref[rows, :]
            w1 = jnp.sum(jnp.where(lane == META_W1, meta, 0.0), axis=-1, keepdims=True)
            w2 = jnp.sum(jnp.where(lane == META_W2, meta, 0.0), axis=-1, keepdims=True)
            ff = (w1 * _from_slabs(bufs[slot], COMBINE_CHUNK, first_row=r0)
                  + w2 * _from_slabs(bufs[slot], COMBINE_CHUNK, first_row=ROUTE_TILE + r0))
            o_ref[rows, :] = _layer_norm(DEEPNORM_ALPHA * h_ref[rows, :] + ff, lg_ref[...], lb_ref[...])
            for t in range(r0, r0 + COMBINE_CHUNK):
                for k in range(2):
                    row = dest_ref[k * N_TOK + nxt + t]
                    pltpu.make_async_copy(_row_slab(ys_hbm, row),
                                          _row_slab(bufs[1 - slot], k * ROUTE_TILE + t),
                                          sem.at[1 - slot]).start(priority=k)

        @pl.when(i == last)
        def _():
            wait_gather(1 - slot)

    @pl.when(i == 0)
    def _():
        start_gather(0, 0)

    for slot in range(2):
        @pl.when(i % 2 == slot)
        def _():
            finish_and_prefetch(slot)


def _combine_ln(dest, meta, h, ys, lg, lb):
    row = lambda width: pl.BlockSpec((ROUTE_TILE, width), lambda i, dest: (i, 0))
    const = pl.BlockSpec((1, D_MODEL), lambda i, dest: (0, 0))
    return pl.pallas_call(
        _combine_ln_kernel,
        out_shape=jax.ShapeDtypeStruct((N_TOK, D_MODEL), F32),
        grid_spec=pltpu.PrefetchScalarGridSpec(
            num_scalar_prefetch=1,
            grid=(N_TOK // ROUTE_TILE,),
            in_specs=[row(LANES), row(D_MODEL), pl.BlockSpec(memory_space=pl.ANY), const, const],
            out_specs=row(D_MODEL),
            scratch_shapes=[pltpu.VMEM((2 * ROUTE_TILE * SLAB, LANES), F32)] * 2
                           + [pltpu.SemaphoreType.DMA((2,))]),
        compiler_params=_cparams("arbitrary"),
        name="moe_combine_ln",
    )(dest, meta, h, ys, lg, lb)


def _moe_ln(h, w_router, b_router, wg, wu, wd, lg, lb):
    meta, counts, dest_rows, xs = _router(h, w_router, b_router)
    expert = jnp.arange(N_EXPERTS)
    tiles = (counts[0, :N_EXPERTS] + EXPERT_TILE - 1) // EXPERT_TILE
    ends = jnp.sum(jnp.where(expert[None, :] <= expert[:, None], tiles[None, :], 0), axis=1)
    step = jnp.minimum(jnp.arange(N_EXPERT_TILES), ends[-1] - 1)
    tile_expert = jnp.sum(step[:, None] >= ends[None, :], axis=1)
    first = jnp.sum(jnp.where(tile_expert[:, None] == expert[None, :], (ends - tiles)[None, :], 0), axis=1)
    tile_block = tile_expert * REGION_TILES + (step - first)
    ys = _experts(tile_block.astype(jnp.int32), tile_expert.astype(jnp.int32), xs, wg, wu, wd)
    dest = dest_rows[:2].reshape(2 * N_TOK)
    return _combine_ln(dest, meta, h, ys, lg, lb)


def kernel(x, positions, ln_g, ln_b, ssm_lambda_re, ssm_lambda_im, ssm_log_step, ssm_b_re, ssm_b_im, ssm_c_re, ssm_c_im, ssm_d, ssm_w_glu, kv_w, attn_w_q, attn_sinks, attn_w_out, ffn_w_gate, ffn_w_up, ffn_w_down, moe_w_router, moe_b_router, moe_w_gate, moe_w_up, moe_w_down):
    ln = lambda layer, j: (ln_g[layer, j].reshape(1, D_MODEL).astype(F32),
                           ln_b[layer, j].reshape(1, D_MODEL).astype(F32))
    rope_table = _rope_tables(positions)
    x2 = x.reshape(N_TOK, D_MODEL)

    params = _ssm_params(ssm_lambda_re[0], ssm_lambda_im[0], ssm_log_step[0], ssm_b_re[0], ssm_b_im[0],
                         ssm_c_re[0], ssm_c_im[0])
    y = _ssm_mixer(x2, params)
    h = _glu_ln(y, x2, ssm_d[0].astype(F32).reshape(1, D_MODEL), ssm_w_glu[0].astype(BF16), *ln(0, 0))
    h = _ffn_ln(h, ffn_w_gate[0].astype(BF16), ffn_w_up[0].astype(BF16), ffn_w_down[0].astype(BF16),
                *ln(0, 1))

    order = jnp.array(HEAD_ORDER)
    w_q = attn_w_q[0].reshape(D_MODEL, N_HEADS, HEAD_DIM)[:, order].reshape(D_MODEL, Q_DIM)
    w_out = attn_w_out[0].reshape(N_HEADS, HEAD_DIM, D_MODEL)[order].reshape(Q_DIM, D_MODEL)
    w_qkv = jnp.concatenate([w_q, kv_w], axis=1).astype(BF16)
    q, k, v = _qkv(h, w_qkv, rope_table)
    h = _attention_ln(q, k, v, attn_sinks[0].astype(F32), h, w_out.astype(BF16), *ln(1, 0))
    w_router = jnp.pad(moe_w_router[0].astype(F32), ((0, 0), (0, LANES - N_EXPERTS)))
    b_router = jnp.pad(moe_b_router[0].astype(F32), (0, LANES - N_EXPERTS)).reshape(1, LANES)
    h = _moe_ln(h, w_router, b_router, moe_w_gate[0].astype(F32), moe_w_up[0].astype(F32),
                moe_w_down[0].astype(F32), *ln(1, 1))
    return h.reshape(BATCH, SEQ, D_MODEL)
```
